```python
import math
import jax, jax.numpy as jnp
from jax import lax
import numpy as np

D_MODEL = 1024
BATCH = 2
SEQ = 8192
DEPTH = 1
DEC_BATCH = 128
DEC_SEQ = 8
PAST_LEN = 8192
PAGE_SIZE = 128

D_PLE = 256
D_FF = 2816
SSM_GROUP = 16
SSM_STATE = 64
D_SSM = 512
N_SSM_GROUPS = D_SSM // SSM_GROUP
HEAD_DIM = 64
HEADS_PER_GROUP = 4
WINDOWS = (128, 512, 2048)
DILATIONS = (1, 4, 16)
N_ATTN_GROUPS = 3
D_ATTN_GROUP = HEADS_PER_GROUP * HEAD_DIM
D_ATTN = N_ATTN_GROUPS * D_ATTN_GROUP
N_STEPS = WINDOWS[0] // DILATIONS[0]
Q_BLOCK = 128
D_IN = D_SSM + 3 * D_ATTN + 2 * D_MODEL
ATTN_SCALE = HEAD_DIM ** -0.5
EPS = 1e-6

kernel_name = 'hybrid_s5_dilated_attn_decoder_step'


def _rmsnorm(x, g):
    xf = x.astype(jnp.float32)
    y = xf * lax.rsqrt(jnp.mean(xf * xf, axis=-1, keepdims=True) + EPS) * g.astype(jnp.float32)
    return y.astype(x.dtype)


def _swiglu(h, w_gate, w_up, w_down):
    return (jax.nn.silu(h @ w_gate) * (h @ w_up)) @ w_down


def _cplx_affine_combine(e1, e2):
    a1r, a1i, b1r, b1i = e1
    a2r, a2i, b2r, b2i = e2
    return (a2r * a1r - a2i * a1i, a2r * a1i + a2i * a1r,
            a2r * b1r - a2i * b1i + b2r, a2r * b1i + a2i * b1r + b2i)


def _ssm_branch(u, h0, a_re, a_im, log_dt, b_re, b_im, c_re, c_im, d_skip, w_glu):
    f32 = jnp.float32
    n, l, _ = u.shape
    a_re, a_im = a_re.astype(f32), a_im.astype(f32)
    dt = jnp.exp(log_dt.astype(f32))[:, None]
    mag = jnp.exp(a_re * dt)
    ab_re, ab_im = mag * jnp.cos(a_im * dt), mag * jnp.sin(a_im * dt)
    den = a_re * a_re + a_im * a_im
    nr, ni = ab_re - 1.0, ab_im
    f_re = (nr * a_re + ni * a_im) / den
    f_im = (ni * a_re - nr * a_im) / den
    br, bi = b_re.astype(f32), b_im.astype(f32)
    bb_re = f_re[..., None] * br - f_im[..., None] * bi
    bb_im = f_re[..., None] * bi + f_im[..., None] * br
    ug = u.astype(f32).reshape(n, l, N_SSM_GROUPS, SSM_GROUP)
    bu_re = jnp.einsum('nlgc,gpc->nlgp', ug, bb_re)
    bu_im = jnp.einsum('nlgc,gpc->nlgp', ug, bb_im)
    h0r, h0i = h0[..., 0].astype(f32), h0[..., 1].astype(f32)
    bu_re = bu_re.at[:, 0].add(ab_re * h0r - ab_im * h0i)
    bu_im = bu_im.at[:, 0].add(ab_re * h0i + ab_im * h0r)
    a_r = jnp.broadcast_to(ab_re, bu_re.shape)
    a_i = jnp.broadcast_to(ab_im, bu_im.shape)
    _, _, h_re, h_im = lax.associative_scan(_cplx_affine_combine, (a_r, a_i, bu_re, bu_im), axis=1)
    y = (jnp.einsum('nlgp,gcp->nlgc', h_re, c_re.astype(f32))
         - jnp.einsum('nlgp,gcp->nlgc', h_im, c_im.astype(f32))
         + d_skip.astype(f32).reshape(N_SSM_GROUPS, SSM_GROUP) * ug)
    y = jax.nn.gelu(y.reshape(n, l, D_SSM)).astype(u.dtype)
    out = y * jax.nn.sigmoid(y @ w_glu)
    h_last = jnp.stack([h_re[:, -1], h_im[:, -1]], axis=-1)
    return out, h_last


def _dilated_attn_prompt(q, k, v, dil):
    n, s, h, e = q.shape
    blk = dil * Q_BLOCK
    s_pad = -(-s // blk) * blk
    m_len = s_pad // dil
    nb = m_len // Q_BLOCK

    def to_res(a):
        a = jnp.pad(a, ((0, 0), (0, s_pad - s), (0, 0), (0, 0)))
        return a.reshape(n, m_len, dil, h, e).transpose(0, 2, 1, 3, 4)

    def band(a):
        ap = jnp.pad(a, ((0, 0), (0, 0), (Q_BLOCK, 0), (0, 0), (0, 0)))
        prev = ap[:, :, :m_len].reshape(n, dil, nb, Q_BLOCK, h, e)
        cur = a.reshape(n, dil, nb, Q_BLOCK, h, e)
        return jnp.concatenate([prev, cur], axis=3)

    qb = to_res(q).reshape(n, dil, nb, Q_BLOCK, h, e)
    kb, vb = band(to_res(k)), band(to_res(v))
    sc = jnp.einsum('ndbqhe,ndbkhe->ndbhqk', qb, kb).astype(jnp.float32) * ATTN_SCALE
    qi = jnp.arange(Q_BLOCK)[:, None]
    ki = jnp.arange(2 * Q_BLOCK)[None, :]
    diff = Q_BLOCK + qi - ki
    key_m = jnp.arange(nb)[:, None, None] * Q_BLOCK - Q_BLOCK + ki
    valid = (diff >= 0) & (diff <= N_STEPS) & (key_m >= 0)
    sc = jnp.where(valid[:, None], sc, -jnp.inf)
    mx = jnp.max(sc, axis=-1, keepdims=True)
    pe = jnp.exp(sc - mx)
    den = jnp.sum(pe, axis=-1, keepdims=True)
    o = jnp.einsum('ndbhqk,ndbkhe->ndbqhe', (pe / den).astype(v.dtype), vb)
    lse = (mx + jnp.log(den))[..., 0]
    o = o.reshape(n, dil, m_len, h, e).transpose(0, 2, 1, 3, 4).reshape(n, s_pad, h, e)[:, :s]
    lse = lse.transpose(0, 2, 4, 1, 3).reshape(n, s_pad, h)[:, :s]
    return o, lse


def _dilated_attn_sample(q, k, v, buf, dil):
    n, l, h, e = q.shape
    wb = buf.shape[1]
    k_all = jnp.concatenate([buf[:, :, 0].astype(k.dtype), k], axis=1)
    v_all = jnp.concatenate([buf[:, :, 1].astype(v.dtype), v], axis=1)
    idx = wb + jnp.arange(l)[:, None] - dil * jnp.arange(N_STEPS + 1)[None, :]
    valid = idx >= 0
    idx = jnp.maximum(idx, 0)
    kg, vg = k_all[:, idx], v_all[:, idx]
    sc = jnp.einsum('nlhe,nlkhe->nlhk', q, kg).astype(jnp.float32) * ATTN_SCALE
    sc = jnp.where(valid[:, None, :], sc, -jnp.inf)
    mx = jnp.max(sc, axis=-1, keepdims=True)
    pe = jnp.exp(sc - mx)
    den = jnp.sum(pe, axis=-1, keepdims=True)
    o = jnp.einsum('nlhk,nlkhe->nlhe', (pe / den).astype(v.dtype), vg)
    return o, (mx + jnp.log(den))[..., 0]


def _layer(x, p, h0, bufs, lw):
    n, l, _ = x.shape
    x = x + 0.5 * _swiglu(_rmsnorm(x, lw['g_ffn1']), lw['ffn1_w_gate'], lw['ffn1_w_up'], lw['ffn1_w_down'])
    h = _rmsnorm(x, lw['g_mix'])
    z = h @ lw['w_in']
    u = z[..., :D_SSM]
    qkv = z[..., D_SSM:D_SSM + 3 * D_ATTN].reshape(n, l, 3, N_ATTN_GROUPS, HEADS_PER_GROUP, HEAD_DIM)
    gate_a = z[..., D_SSM + 3 * D_ATTN:D_SSM + 3 * D_ATTN + D_MODEL]
    gate_b = z[..., D_SSM + 3 * D_ATTN + D_MODEL:]
    y_ssm, h_new = _ssm_branch(u, h0, lw['ssm_a_re'], lw['ssm_a_im'], lw['ssm_log_dt'], lw['ssm_b_re'],
                               lw['ssm_b_im'], lw['ssm_c_re'], lw['ssm_c_im'], lw['ssm_d'], lw['ssm_w_glu'])
    outs, lses, kv_new = [], [], []
    for g in range(N_ATTN_GROUPS):
        qg, kg, vg = qkv[:, :, 0, g], qkv[:, :, 1, g], qkv[:, :, 2, g]
        if bufs is None:
            o, lse = _dilated_attn_prompt(qg, kg, vg, DILATIONS[g])
            keep = min(WINDOWS[g], l)
            kv_new.append(jnp.stack([kg[:, l - keep:], vg[:, l - keep:]], axis=2))
        else:
            o, lse = _dilated_attn_sample(qg, kg, vg, bufs[g], DILATIONS[g])
            kv_new.append(jnp.stack([kg, vg], axis=2))
        outs.append(o)
        lses.append(lse)
    w_grp = jax.nn.softmax(jnp.stack(lses, axis=0), axis=0)
    y_attn = jnp.einsum('gnlh,gnlhe->nlhe', w_grp.astype(x.dtype), jnp.stack(outs, axis=0)).reshape(n, l, D_ATTN_GROUP)
    merged = (jax.nn.sigmoid(gate_a) * (y_ssm @ lw['w_br_ssm'])
              + jax.nn.sigmoid(gate_b) * (y_attn @ lw['w_br_attn']))
    x = x + merged @ lw['w_out']
    x = x + 0.5 * _swiglu(_rmsnorm(x, lw['g_ffn2']), lw['ffn2_w_gate'], lw['ffn2_w_up'], lw['ffn2_w_down'])
    x = x + jax.nn.sigmoid(_rmsnorm(x, lw['g_ple']) @ lw['w_ple_gate']) * (p @ lw['w_ple_proj'])
    return x, kv_new, h_new


def setup_inputs(seed: int = 0) -> dict:
    key = jax.random.key(seed)
    ks = list(jax.random.split(key, 40))
    f32 = jnp.float32

    def nrm(k, shape, scale):
        return jax.random.normal(k, shape, f32) * scale

    def gain(k, shape):
        return 1.0 + 0.01 * jax.random.normal(k, shape, f32)

    G, P = N_SSM_GROUPS, SSM_STATE
    kv_shape = lambda w: (DEPTH, DEC_BATCH, min(w, PAST_LEN), 2, HEADS_PER_GROUP, HEAD_DIM)
    a_im0 = math.pi * jnp.arange(P, dtype=f32)
    return {
        'x_prompt': nrm(ks[0], (BATCH, SEQ, D_MODEL), 1.0),
        'x_sample': nrm(ks[1], (DEC_BATCH, DEC_SEQ, D_MODEL), 1.0),
        'p_prompt': nrm(ks[2], (DEPTH, BATCH, SEQ, D_PLE), 1.0),
        'p_sample': nrm(ks[3], (DEPTH, DEC_BATCH, DEC_SEQ, D_PLE), 1.0),
        'cache_kv_w128': nrm(ks[4], kv_shape(WINDOWS[0]), 1.0),
        'cache_kv_w512': nrm(ks[5], kv_shape(WINDOWS[1]), 1.0),
        'cache_kv_w2048': nrm(ks[6], kv_shape(WINDOWS[2]), 1.0),
        'state_ssm': nrm(ks[7], (DEPTH, DEC_BATCH, G, P, 2), 0.1),
        'g_ffn1': gain(ks[8], (DEPTH, D_MODEL)),
        'ffn1_w_gate': nrm(ks[9], (DEPTH, D_MODEL, D_FF), D_MODEL ** -0.5),
        'ffn1_w_up': nrm(ks[10], (DEPTH, D_MODEL, D_FF), D_MODEL ** -0.5),
        'ffn1_w_down': nrm(ks[11], (DEPTH, D_FF, D_MODEL), D_FF ** -0.5),
        'g_mix': gain(ks[12], (DEPTH, D_MODEL)),
        'w_in': nrm(ks[13], (DEPTH, D_MODEL, D_IN), D_MODEL ** -0.5),
        'ssm_a_re': -0.5 + 0.01 * jax.random.normal(ks[14], (DEPTH, G, P), f32),
        'ssm_a_im': a_im0 + 0.01 * jax.random.normal(ks[15], (DEPTH, G, P), f32),
        'ssm_log_dt': jax.random.uniform(ks[16], (DEPTH, G), f32, math.log(1e-3), math.log(1e-1)),
        'ssm_b_re': nrm(ks[17], (DEPTH, G, P, SSM_GROUP), (2 * SSM_GROUP) ** -0.5),
        'ssm_b_im': nrm(ks[18], (DEPTH, G, P, SSM_GROUP), (2 * SSM_GROUP) ** -0.5),
        'ssm_c_re': nrm(ks[19], (DEPTH, G, SSM_GROUP, P), P ** -0.5),
        'ssm_c_im': nrm(ks[20], (DEPTH, G, SSM_GROUP, P), P ** -0.5),
        'ssm_d': nrm(ks[21], (DEPTH, D_SSM), 1.0),
        'ssm_w_glu': nrm(ks[22], (DEPTH, D_SSM, D_SSM), D_SSM ** -0.5),
        'w_br_ssm': nrm(ks[23], (DEPTH, D_SSM, D_MODEL), D_SSM ** -0.5),
        'w_br_attn': nrm(ks[24], (DEPTH, D_ATTN_GROUP, D_MODEL), D_ATTN_GROUP ** -0.5),
        'w_out': nrm(ks[25], (DEPTH, D_MODEL, D_MODEL), D_MODEL ** -0.5),
        'g_ffn2': gain(ks[26], (DEPTH, D_MODEL)),
        'ffn2_w_gate': nrm(ks[27], (DEPTH, D_MODEL, D_FF), D_MODEL ** -0.5),
        'ffn2_w_up': nrm(ks[28], (DEPTH, D_MODEL, D_FF), D_MODEL ** -0.5),
        'ffn2_w_down': nrm(ks[29], (DEPTH, D_FF, D_MODEL), D_FF ** -0.5),
        'g_ple': gain(ks[30], (DEPTH, D_MODEL)),
        'w_ple_gate': nrm(ks[31], (DEPTH, D_MODEL, D_MODEL), D_MODEL ** -0.5),
        'w_ple_proj': nrm(ks[32], (DEPTH, D_PLE, D_MODEL), D_PLE ** -0.5),
        'g_final': gain(ks[33], (D_MODEL,)),
    }


def reference(x_prompt, x_sample, p_prompt, p_sample, cache_kv_w128, cache_kv_w512, cache_kv_w2048, state_ssm,
              g_ffn1, ffn1_w_gate, ffn1_w_up, ffn1_w_down, g_mix, w_in, ssm_a_re, ssm_a_im, ssm_log_dt,
              ssm_b_re, ssm_b_im, ssm_c_re, ssm_c_im, ssm_d, ssm_w_glu, w_br_ssm, w_br_attn, w_out,
              g_ffn2, ffn2_w_gate, ffn2_w_up, ffn2_w_down, g_ple, w_ple_gate, w_ple_proj, g_final):
    xp, xs = x_prompt, x_sample
    kvp = ([], [], [])
    kvs = ([], [], [])
    hp_all, hs_all = [], []
    for i in range(DEPTH):
        lw = {
            'g_ffn1': g_ffn1[i], 'ffn1_w_gate': ffn1_w_gate[i], 'ffn1_w_up': ffn1_w_up[i], 'ffn1_w_down': ffn1_w_down[i],
            'g_mix': g_mix[i], 'w_in': w_in[i],
            'ssm_a_re': ssm_a_re[i], 'ssm_a_im': ssm_a_im[i], 'ssm_log_dt': ssm_log_dt[i],
            'ssm_b_re': ssm_b_re[i], 'ssm_b_im': ssm_b_im[i], 'ssm_c_re': ssm_c_re[i], 'ssm_c_im': ssm_c_im[i],
            'ssm_d': ssm_d[i], 'ssm_w_glu': ssm_w_glu[i],
            'w_br_ssm': w_br_ssm[i], 'w_br_attn': w_br_attn[i], 'w_out': w_out[i],
            'g_ffn2': g_ffn2[i], 'ffn2_w_gate': ffn2_w_gate[i], 'ffn2_w_up': ffn2_w_up[i], 'ffn2_w_down': ffn2_w_down[i],
            'g_ple': g_ple[i], 'w_ple_gate': w_ple_gate[i], 'w_ple_proj': w_ple_proj[i],
        }
        h0p = jnp.zeros((xp.shape[0], N_SSM_GROUPS, SSM_STATE, 2), jnp.float32)
        xp, kv_p, hp = _layer(xp, p_prompt[i], h0p, None, lw)
        xs, kv_s, hs = _layer(xs, p_sample[i], state_ssm[i], (cache_kv_w128[i], cache_kv_w512[i], cache_kv_w2048[i]), lw)
        for g in range(N_ATTN_GROUPS):
            kvp[g].append(kv_p[g])
            kvs[g].append(kv_s[g])
        hp_all.append(hp)
        hs_all.append(hs)
    y_prompt = _rmsnorm(xp, g_final)
    y_sample = _rmsnorm(xs, g_final)
    kv_w128_prompt = jnp.stack(kvp[0], axis=0)
    kv_w512_prompt = jnp.stack(kvp[1], axis=0)
    kv_w2048_prompt = jnp.stack(kvp[2], axis=0)
    ssm_prompt = jnp.stack(hp_all, axis=0)
    kv_w128_sample = jnp.stack(kvs[0], axis=0)
    kv_w512_sample = jnp.stack(kvs[1], axis=0)
    kv_w2048_sample = jnp.stack(kvs[2], axis=0)
    ssm_sample = jnp.stack(hs_all, axis=0)
    return (y_prompt, y_sample, kv_w128_prompt, kv_w512_prompt, kv_w2048_prompt, ssm_prompt,
            kv_w128_sample, kv_w512_sample, kv_w2048_sample, ssm_sample)
```

```python
import functools
import math

import jax
import jax.numpy as jnp
from jax import lax
from jax.experimental import pallas as pl
from jax.experimental.pallas import tpu as pltpu

D_MODEL = 1024
D_PLE = 256
D_FF = 2816
SSM_GROUP = 16
SSM_STATE = 64
D_SSM = 512
N_SSM_GROUPS = D_SSM // SSM_GROUP
HEAD_DIM = 64
HEADS = 4
WINDOWS = (128, 512, 2048)
DILATIONS = (1, 4, 16)
N_GROUPS = 3
D_HEADS = HEADS * HEAD_DIM
N_STEPS = 128
ATTN_SCALE = HEAD_DIM ** -0.5
EPS = 1e-6

CHUNK = 16
CHUNK_W = CHUNK * SSM_GROUP
STATE_W = 2 * SSM_STATE
Q_BLOCK = 128

Z_U = (0, D_SSM)
Z_Q = (D_SSM, D_SSM + N_GROUPS * D_HEADS)
Z_KV = (Z_Q[1], Z_Q[1] + 2 * N_GROUPS * D_HEADS)
Z_GA = (Z_KV[1], Z_KV[1] + D_MODEL)
Z_GB = (Z_GA[1], Z_GA[1] + D_MODEL)
D_IN = Z_GB[1]

V7X_VMEM_LIMIT_BYTES = 56 * 1024 * 1024

BF16 = jnp.bfloat16
F32 = jnp.float32


def _dot(a, b):
    return jnp.dot(a, b, preferred_element_type=F32)


def _dot_nt(a, b):
    return lax.dot_general(a, b, (((1,), (1,)), ((), ())), preferred_element_type=F32)


def _rms(x, g):
    return x * lax.rsqrt(jnp.mean(x * x, axis=-1, keepdims=True) + EPS) * g


def _swiglu(xn, wg_ref, wu_ref, wd_ref):
    gate = _dot(xn, wg_ref[...])
    up = _dot(xn, wu_ref[...])
    act = (gate * jax.nn.sigmoid(gate) * up).astype(BF16)
    return _dot(act, wd_ref[...])


def _const_spec(shape):
    nd = len(shape)
    return pl.BlockSpec(shape, lambda *_: (0,) * nd, pipeline_mode=pl.Buffered(1))


def _params(n_grid_dims):
    return pltpu.CompilerParams(dimension_semantics=("arbitrary",) * n_grid_dims,
                                vmem_limit_bytes=V7X_VMEM_LIMIT_BYTES)


def _stage1_kernel(x_ref, g1_ref, wg_ref, wu_ref, wd_ref, gm_ref, win_ref,
                   x1_ref, u_ref, q_ref, kv_ref, ga_ref, gb_ref):
    x = x_ref[...]
    xn = _rms(x, g1_ref[...]).astype(BF16)
    x1 = x + 0.5 * _swiglu(xn, wg_ref, wu_ref, wd_ref)
    x1_ref[...] = x1
    h = _rms(x1, gm_ref[...]).astype(BF16)
    z = _dot(h, win_ref[...])
    u_ref[...] = z[:, Z_U[0]:Z_U[1]].astype(BF16)
    q_ref[...] = z[:, Z_Q[0]:Z_Q[1]].astype(q_ref.dtype)
    kv_ref[...] = z[:, Z_KV[0]:Z_KV[1]]
    ga_ref[...] = jax.nn.sigmoid(z[:, Z_GA[0]:Z_GA[1]])
    gb_ref[...] = jax.nn.sigmoid(z[:, Z_GB[0]:Z_GB[1]])


def _stage1(x, g1, wg, wu, wd, gm, win, tm, q_dtype):
    t = x.shape[0]
    tok = lambda w: pl.BlockSpec((tm, w), lambda i: (i, 0))
    widths = (D_MODEL, D_SSM, N_GROUPS * D_HEADS, 2 * N_GROUPS * D_HEADS, D_MODEL, D_MODEL)
    dtypes = (F32, BF16, q_dtype, F32, F32, F32)
    return pl.pallas_call(
        _stage1_kernel,
        grid=(t // tm,),
        in_specs=[tok(D_MODEL), _const_spec(g1.shape), _const_spec(wg.shape), _const_spec(wu.shape),
                  _const_spec(wd.shape), _const_spec(gm.shape), _const_spec(win.shape)],
        out_specs=[tok(w) for w in widths],
        out_shape=[jax.ShapeDtypeStruct((t, w), dt) for w, dt in zip(widths, dtypes)],
        compiler_params=_params(1),
        name="stage1",
    )(x, g1, wg, wu, wd, gm, win)


def _cmul_packed(x, a1, a2):
    return x * a1 + pltpu.roll(x, SSM_STATE, axis=1) * a2


def _shift_rows(x, sh):
    row = lax.broadcasted_iota(jnp.int32, x.shape, 0)
    return jnp.where(row >= sh, pltpu.roll(x, sh, axis=0), 0.0)


def _ssm_prompt_kernel(u_ref, m_ref, w_ref, v_ref, a1_ref, a2_ref, y_ref, hl_ref, *, n_chunks):
    u = u_ref[0, 0]
    h = _dot(u, w_ref[0])
    n_levels = int(math.log2(n_chunks))
    for lvl in range(n_levels):
        hs = _shift_rows(h, 1 << lvl)
        h = h + _cmul_packed(hs, a1_ref[0, lvl:lvl + 1, :], a2_ref[0, lvl:lvl + 1, :])
    hprev = _shift_rows(h, 1).astype(BF16)
    y_ref[0, 0] = _dot(u, m_ref[0]) + _dot(hprev, v_ref[0])
    hl_ref[0, 0] = h[n_chunks - 1:n_chunks, :]


def _ssm_prompt(u_t, m_tab, w_tab, v_tab, a1, a2):
    n, g, n_chunks, _ = u_t.shape
    grp = lambda s: pl.BlockSpec((1,) + s, lambda i, j: (j, 0, 0))
    return pl.pallas_call(
        functools.partial(_ssm_prompt_kernel, n_chunks=n_chunks),
        grid=(n, g),
        in_specs=[pl.BlockSpec((1, 1, n_chunks, CHUNK_W), lambda i, j: (i, j, 0, 0)),
                  grp((CHUNK_W, CHUNK_W)), grp((CHUNK_W, STATE_W)), grp((STATE_W, CHUNK_W)),
                  grp(a1.shape[1:]), grp(a2.shape[1:])],
        out_specs=[pl.BlockSpec((1, 1, n_chunks, CHUNK_W), lambda i, j: (i, j, 0, 0)),
                   pl.BlockSpec((1, 1, 1, STATE_W), lambda i, j: (i, j, 0, 0))],
        out_shape=[jax.ShapeDtypeStruct((n, g, n_chunks, CHUNK_W), F32),
                   jax.ShapeDtypeStruct((n, g, 1, STATE_W), F32)],
        compiler_params=_params(2),
        name="ssm_prompt",
    )(u_t, m_tab, w_tab, v_tab, a1, a2)


def _ssm_sample_kernel(u_ref, h0_ref, m_ref, w_ref, v_ref, a1_ref, a2_ref, y_ref, hl_ref, *, width):
    u = u_ref[0]
    h0 = h0_ref[0]
    y_ref[0] = _dot(u, m_ref[0, :width, :width]) + _dot(h0.astype(BF16), v_ref[0, :, :width])
    hl_ref[0] = _cmul_packed(h0, a1_ref[0], a2_ref[0]) + _dot(u, w_ref[0, CHUNK_W - width:, :])


def _ssm_sample(u_s, h0, m_tab, w_tab, v_tab, a1, a2):
    g, n_seq, width = u_s.shape
    grp = lambda s: pl.BlockSpec((1,) + s, lambda j: (j, 0, 0))
    return pl.pallas_call(
        functools.partial(_ssm_sample_kernel, width=width),
        grid=(g,),
        in_specs=[grp((n_seq, width)), grp((n_seq, STATE_W)),
                  grp((CHUNK_W, CHUNK_W)), grp((CHUNK_W, STATE_W)), grp((STATE_W, CHUNK_W)),
                  grp((1, STATE_W)), grp((1, STATE_W))],
        out_specs=[grp((n_seq, width)), grp((n_seq, STATE_W))],
        out_shape=[jax.ShapeDtypeStruct((g, n_seq, width), F32),
                   jax.ShapeDtypeStruct((g, n_seq, STATE_W), F32)],
        compiler_params=_params(1),
        name="ssm_sample",
    )(u_s, h0, m_tab, w_tab, v_tab, a1, a2)


def _ssm_tables(a_re, a_im, log_dt, b_re, b_im, c_re, c_im, d_skip, n_scan_levels):
    hi = lax.Precision.HIGHEST
    g, p = a_re.shape
    dt = jnp.exp(log_dt)[:, None]
    mag = jnp.exp(a_re * dt)
    ab_re, ab_im = mag * jnp.cos(a_im * dt), mag * jnp.sin(a_im * dt)
    den = a_re * a_re + a_im * a_im
    nr, ni = ab_re - 1.0, ab_im
    f_re = (nr * a_re + ni * a_im) / den
    f_im = (ni * a_re - nr * a_im) / den
    bb_re = f_re[..., None] * b_re - f_im[..., None] * b_im
    bb_im = f_re[..., None] * b_im + f_im[..., None] * b_re
    pw_re, pw_im = [jnp.ones_like(ab_re)], [jnp.zeros_like(ab_im)]
    for _ in range(CHUNK):
        r, i = pw_re[-1], pw_im[-1]
        pw_re.append(r * ab_re - i * ab_im)
        pw_im.append(r * ab_im + i * ab_re)
    pw_re, pw_im = jnp.stack(pw_re), jnp.stack(pw_im)
    e_re = c_re[None] * pw_re[:, :, None, :] - c_im[None] * pw_im[:, :, None, :]
    e_im = c_re[None] * pw_im[:, :, None, :] + c_im[None] * pw_re[:, :, None, :]
    k_lag = (jnp.einsum('kgcp,gpd->kgdc', e_re, bb_re, precision=hi)
             - jnp.einsum('kgcp,gpd->kgdc', e_im, bb_im, precision=hi))
    k_lag = k_lag.at[0].add(jnp.eye(SSM_GROUP, dtype=F32)[None] * d_skip.reshape(g, 1, SSM_GROUP))
    lag = jnp.arange(CHUNK)[None, :] - jnp.arange(CHUNK)[:, None]
    m_tab = jnp.where((lag >= 0)[:, :, None, None, None], k_lag[jnp.maximum(lag, 0)], 0.0)
    m_tab = m_tab.transpose(2, 0, 3, 1, 4).reshape(g, CHUNK_W, CHUNK_W)
    rev_re, rev_im = pw_re[CHUNK - 1::-1][:CHUNK], pw_im[CHUNK - 1::-1][:CHUNK]
    w_re = rev_re[:, :, :, None] * bb_re[None] - rev_im[:, :, :, None] * bb_im[None]
    w_im = rev_re[:, :, :, None] * bb_im[None] + rev_im[:, :, :, None] * bb_re[None]
    w_tab = jnp.concatenate([w_re, w_im], axis=2).transpose(1, 0, 3, 2).reshape(g, CHUNK_W, STATE_W)
    v_tab = jnp.concatenate([e_re[1:], -e_im[1:]], axis=3)
    v_tab = v_tab.transpose(1, 3, 0, 2).reshape(g, STATE_W, CHUNK_W)

    def packed(r, i):
        return jnp.concatenate([r, r], -1), jnp.concatenate([-i, i], -1)

    lv_re, lv_im = [pw_re[CHUNK]], [pw_im[CHUNK]]
    for _ in range(n_scan_levels - 1):
        r, i = lv_re[-1], lv_im[-1]
        lv_re.append(r * r - i * i)
        lv_im.append(2.0 * r * i)
    a1, a2 = packed(jnp.stack(lv_re, 1), jnp.stack(lv_im, 1))
    return m_tab.astype(BF16), w_tab.astype(BF16), v_tab.astype(BF16), a1, a2, pw_re, pw_im, packed


def _head_of_lane(width):
    return lax.broadcasted_iota(jnp.int32, (1, width), 1) // HEAD_DIM


def _stack_heads(q):
    head = _head_of_lane(q.shape[1])
    q = q.astype(F32)
    return jnp.concatenate([jnp.where(head == h, q, 0.0) for h in range(HEADS)], axis=0).astype(BF16)


def _unstack_heads(x, rows):
    head = _head_of_lane(x.shape[1])
    out = jnp.zeros((rows, x.shape[1]), x.dtype)
    for h in range(HEADS):
        out = out + jnp.where(head == h, x[h * rows:(h + 1) * rows], 0.0)
    return out


def _softmax_pv(s, valid, v):
    s = jnp.where(valid, s * ATTN_SCALE, -jnp.inf)
    mx = jnp.max(s, axis=-1, keepdims=True)
    pe = jnp.exp(s - mx)
    den = jnp.sum(pe, axis=-1, keepdims=True)
    o = _dot(pe.astype(BF16), v) / den
    return o, mx + jnp.log(den)


def _attn_prompt_kernel(q_ref, kp_ref, kc_ref, vp_ref, vc_ref, o_ref, lse_ref):
    qb = Q_BLOCK
    b = pl.program_id(2)
    qs = _stack_heads(q_ref[0])
    k = jnp.concatenate([kp_ref[0], kc_ref[0]], axis=0).astype(BF16)
    v = jnp.concatenate([vp_ref[0], vc_ref[0]], axis=0).astype(BF16)
    s = _dot_nt(qs, k)
    qi = lax.broadcasted_iota(jnp.int32, s.shape, 0) % qb
    ki = lax.broadcasted_iota(jnp.int32, s.shape, 1)
    diff = qb + qi - ki
    valid = (diff >= 0) & (diff <= N_STEPS) & (ki >= jnp.where(b > 0, 0, qb))
    o, lse = _softmax_pv(s, valid, v)
    o_ref[0] = _unstack_heads(o, qb)
    lse_ref[0] = _unstack_heads(jnp.broadcast_to(lse, o.shape), qb)


def _attn_prompt(q_all, kv_all, n, seq, grp):
    d = DILATIONS[grp]
    nb = seq // (d * Q_BLOCK)
    qv = q_all.reshape(n, seq // d, d * N_GROUPS * D_HEADS)
    kvv = kv_all.reshape(n, seq // d, d * 2 * N_GROUPS * D_HEADS)
    blk = (1, Q_BLOCK, D_HEADS)
    kcol = lambda r: r * 2 * N_GROUPS + 2 * grp
    o, lse = pl.pallas_call(
        _attn_prompt_kernel,
        grid=(n, d, nb),
        in_specs=[pl.BlockSpec(blk, lambda i, r, b: (i, b, r * N_GROUPS + grp)),
                  pl.BlockSpec(blk, lambda i, r, b: (i, jnp.maximum(b - 1, 0), kcol(r))),
                  pl.BlockSpec(blk, lambda i, r, b: (i, b, kcol(r))),
                  pl.BlockSpec(blk, lambda i, r, b: (i, jnp.maximum(b - 1, 0), kcol(r) + 1)),
                  pl.BlockSpec(blk, lambda i, r, b: (i, b, kcol(r) + 1))],
        out_specs=[pl.BlockSpec(blk, lambda i, r, b: (i, b, r))] * 2,
        out_shape=[jax.ShapeDtypeStruct((n, seq // d, d * D_HEADS), F32)] * 2,
        compiler_params=_params(3),
        name=f"attn_prompt_d{d}",
    )(qv, kvv, kvv, kvv, kvv)
    return o.reshape(n * seq, D_HEADS), lse.reshape(n * seq, D_HEADS)


def _attn_sample_kernel(q_ref, kn_ref, vn_ref, c_ref, o_ref, lse_ref, *, d, n_res, n_new, seq_blk):
    m_rows = c_ref.shape[1]
    n_rows = HEADS * n_new
    n_keys = (n_res + 1) * m_rows
    l_q = lax.broadcasted_iota(jnp.int32, (n_rows, n_keys), 0) % n_new
    col = lax.broadcasted_iota(jnp.int32, (n_rows, n_keys), 1)
    r_k, m_k = col // m_rows, col % m_rows
    shift = int(math.log2(d))
    dl = l_q - r_k
    valid_buf = (dl >= 0) & ((dl & (d - 1)) == 0) & (m_k >= (dl >> shift))
    dn = l_q - m_k
    valid_new = (m_k < n_new) & (dn >= 0) & ((dn & (d - 1)) == 0)
    is_buf = r_k < n_res
    valid = (is_buf & valid_buf) | (~is_buf & valid_new)
    pad = jnp.zeros((m_rows - n_new, D_HEADS), F32)
    kv_w = 2 * D_HEADS
    for s in range(seq_blk):
        qs = _stack_heads(q_ref[s])
        ks = [c_ref[s, :, r * kv_w:r * kv_w + D_HEADS] for r in range(n_res)]
        vs = [c_ref[s, :, r * kv_w + D_HEADS:(r + 1) * kv_w] for r in range(n_res)]
        k = jnp.concatenate(ks + [kn_ref[s], pad], axis=0).astype(BF16)
        v = jnp.concatenate(vs + [vn_ref[s], pad], axis=0).astype(BF16)
        o, lse = _softmax_pv(_dot_nt(qs, k), valid, v)
        o_ref[s] = _unstack_heads(o, n_new)
        lse_ref[s] = _unstack_heads(jnp.broadcast_to(lse, o.shape), n_new)


def _attn_sample(q_all, kv_all, cache, n_seq, n_new, grp, seq_blk):
    d = DILATIONS[grp]
    wb = cache.shape[1]
    assert wb == N_STEPS * d and n_new <= wb // d
    n_res = min(d, n_new)
    m_rows = wb // d
    qv = q_all.reshape(n_seq, n_new, N_GROUPS * D_HEADS)
    kvv = kv_all.reshape(n_seq, n_new, 2 * N_GROUPS * D_HEADS)
    cv = cache.reshape(n_seq, m_rows, d * 2 * D_HEADS)
    blk = (seq_blk, n_new, D_HEADS)
    o, lse = pl.pallas_call(
        functools.partial(_attn_sample_kernel, d=d, n_res=n_res, n_new=n_new, seq_blk=seq_blk),
        grid=(n_seq // seq_blk,),
        in_specs=[pl.BlockSpec(blk, lambda i: (i, 0, grp)),
                  pl.BlockSpec(blk, lambda i: (i, 0, 2 * grp)),
                  pl.BlockSpec(blk, lambda i: (i, 0, 2 * grp + 1)),
                  pl.BlockSpec((seq_blk, m_rows, n_res * 2 * D_HEADS), lambda i: (i, 0, 0))],
        out_specs=[pl.BlockSpec(blk, lambda i: (i, 0, 0))] * 2,
        out_shape=[jax.ShapeDtypeStruct((n_seq, n_new, D_HEADS), F32)] * 2,
        compiler_params=_params(1),
        name=f"attn_sample_d{d}",
    )(qv, kvv, kvv, cv)
    return o.reshape(n_seq * n_new, D_HEADS), lse.reshape(n_seq * n_new, D_HEADS)


def _gelu_tanh(x):
    return 0.5 * x * (1.0 + jnp.tanh(math.sqrt(2.0 / math.pi) * (x + 0.044715 * (x * x * x))))


def _stage3_kernel(x1_ref, yp_ref, o0_ref, o1_ref, o2_ref, l0_ref, l1_ref, l2_ref, ga_ref, gb_ref, p_ref,
                   wglu_ref, wbs_ref, wba_ref, wout_ref, g2_ref, wg_ref, wu_ref, wd_ref,
                   gp_ref, wpg_ref, wpp_ref, gf_ref, y_ref):
    y = _gelu_tanh(yp_ref[...])
    glu = y * jax.nn.sigmoid(_dot(y.astype(BF16), wglu_ref[...]))
    l0, l1, l2 = l0_ref[...], l1_ref[...], l2_ref[...]
    mx = jnp.maximum(jnp.maximum(l0, l1), l2)
    e0, e1, e2 = jnp.exp(l0 - mx), jnp.exp(l1 - mx), jnp.exp(l2 - mx)
    y_attn = (e0 * o0_ref[...] + e1 * o1_ref[...] + e2 * o2_ref[...]) / (e0 + e1 + e2)
    merged = (ga_ref[...] * _dot(glu.astype(BF16), wbs_ref[...])
              + gb_ref[...] * _dot(y_attn.astype(BF16), wba_ref[...]))
    x = x1_ref[...] + _dot(merged.astype(BF16), wout_ref[...])
    x = x + 0.5 * _swiglu(_rms(x, g2_ref[...]).astype(BF16), wg_ref, wu_ref, wd_ref)
    gate = jax.nn.sigmoid(_dot(_rms(x, gp_ref[...]).astype(BF16), wpg_ref[...]))
    x = x + gate * _dot(p_ref[...].astype(BF16), wpp_ref[...])
    y_ref[...] = _rms(x, gf_ref[...])


def _stage3(tok_inputs, weights, tm):
    t = tok_inputs[0].shape[0]
    tok = lambda a: pl.BlockSpec((tm, a.shape[1]), lambda i: (i, 0))
    return pl.pallas_call(
        _stage3_kernel,
        grid=(t // tm,),
        in_specs=[tok(a) for a in tok_inputs] + [_const_spec(w.shape) for w in weights],
        out_specs=pl.BlockSpec((tm, D_MODEL), lambda i: (i, 0)),
        out_shape=jax.ShapeDtypeStruct((t, D_MODEL), F32),
        compiler_params=_params(1),
        name="stage3",
    )(*tok_inputs, *weights)


def kernel(x_prompt, x_sample, p_prompt, p_sample, cache_kv_w128, cache_kv_w512, cache_kv_w2048, state_ssm,
           g_ffn1, ffn1_w_gate, ffn1_w_up, ffn1_w_down, g_mix, w_in, ssm_a_re, ssm_a_im, ssm_log_dt,
           ssm_b_re, ssm_b_im, ssm_c_re, ssm_c_im, ssm_d, ssm_w_glu, w_br_ssm, w_br_attn, w_out,
           g_ffn2, ffn2_w_gate, ffn2_w_up, ffn2_w_down, g_ple, w_ple_gate, w_ple_proj, g_final):
    assert x_prompt.shape[-1] == D_MODEL and g_ffn1.shape[0] == 1
    n_p, seq, _ = x_prompt.shape
    n_s, n_new, _ = x_sample.shape
    caches = (cache_kv_w128, cache_kv_w512, cache_kv_w2048)
    row = lambda g: g.reshape(1, -1)
    bf = lambda w: w[0].astype(BF16)

    wi = w_in[0]
    qkv0 = D_SSM
    kv_cols = wi[:, qkv0 + N_GROUPS * D_HEADS:qkv0 + 3 * N_GROUPS * D_HEADS]
    kv_cols = kv_cols.reshape(D_MODEL, 2, N_GROUPS, D_HEADS).transpose(0, 2, 1, 3).reshape(D_MODEL, -1)
    win = jnp.concatenate([wi[:, :qkv0 + N_GROUPS * D_HEADS], kv_cols,
                           wi[:, qkv0 + 3 * N_GROUPS * D_HEADS:]], axis=1).astype(BF16)
    s1_w = (row(g_ffn1[0]), bf(ffn1_w_gate), bf(ffn1_w_up), bf(ffn1_w_down), row(g_mix[0]), win)
    s3_w = (bf(ssm_w_glu), bf(w_br_ssm), bf(w_br_attn), bf(w_out), row(g_ffn2[0]),
            bf(ffn2_w_gate), bf(ffn2_w_up), bf(ffn2_w_down), row(g_ple[0]), bf(w_ple_gate),
            bf(w_ple_proj), row(g_final))

    n_chunks = seq // CHUNK
    m_tab, w_tab, v_tab, a1, a2, pw_re, pw_im, packed = _ssm_tables(
        ssm_a_re[0], ssm_a_im[0], ssm_log_dt[0], ssm_b_re[0], ssm_b_im[0], ssm_c_re[0], ssm_c_im[0],
        ssm_d[0], int(math.log2(n_chunks)))
    G = N_SSM_GROUPS

    x1, u, q, kv, ga, gb = _stage1(x_prompt.reshape(n_p * seq, D_MODEL), *s1_w, tm=256, q_dtype=BF16)
    u_t = u.reshape(n_p, n_chunks, CHUNK, G, SSM_GROUP).transpose(0, 3, 1, 2, 4).reshape(n_p, G, n_chunks, CHUNK_W)
    y_t, h_last = _ssm_prompt(u_t, m_tab, w_tab, v_tab, a1, a2)
    y_pre = y_t.reshape(n_p, G, n_chunks, CHUNK, SSM_GROUP).transpose(0, 2, 3, 1, 4).reshape(n_p * seq, D_SSM)
    attn = [_attn_prompt(q, kv, n_p, seq, grp) for grp in range(N_GROUPS)]
    tok = (x1, y_pre, attn[0][0], attn[1][0], attn[2][0], attn[0][1], attn[1][1], attn[2][1], ga, gb,
           p_prompt[0].reshape(n_p * seq, D_PLE))
    y_prompt = _stage3(tok, s3_w, tm=256).reshape(n_p, seq, D_MODEL)
    kv_p = kv.reshape(n_p, seq, N_GROUPS, 2, HEADS, HEAD_DIM)
    kv_prompt = [kv_p[:, seq - min(WINDOWS[grp], seq):, grp][None] for grp in range(N_GROUPS)]
    ssm_prompt = h_last.reshape(n_p, G, 2, SSM_STATE).transpose(0, 1, 3, 2)[None]

    x1, u, q, kv, ga, gb = _stage1(x_sample.reshape(n_s * n_new, D_MODEL), *s1_w, tm=256, q_dtype=F32)
    width = n_new * SSM_GROUP
    u_s = u.reshape(n_s, n_new, G, SSM_GROUP).transpose(2, 0, 1, 3).reshape(G, n_s, width)
    h0 = state_ssm[0].transpose(1, 0, 3, 2).reshape(G, n_s, STATE_W)
    a1s, a2s = packed(pw_re[n_new][:, None, :], pw_im[n_new][:, None, :])
    y_s, h_new = _ssm_sample(u_s, h0, m_tab, w_tab, v_tab, a1s, a2s)
    y_pre = y_s.reshape(G, n_s, n_new, SSM_GROUP).transpose(1, 2, 0, 3).reshape(n_s * n_new, D_SSM)
    attn = [_attn_sample(q, kv, caches[grp][0].reshape(n_s, -1, 2 * D_HEADS), n_s, n_new, grp, seq_blk=4)
            for grp in range(N_GROUPS)]
    tok = (x1, y_pre, attn[0][0], attn[1][0], attn[2][0], attn[0][1], attn[1][1], attn[2][1], ga, gb,
           p_sample[0].reshape(n_s * n_new, D_PLE))
    y_sample = _stage3(tok, s3_w, tm=256).reshape(n_s, n_new, D_MODEL)
    kv_s = kv.reshape(n_s, n_new, N_GROUPS, 2, HEADS, HEAD_DIM)
    kv_sample = [kv_s[:, :, grp][None] for grp in range(N_GROUPS)]
    ssm_sample = h_new.reshape(G, n_s, 2, SSM_STATE).transpose(1, 0, 3, 2)[None]

    return (y_prompt, y_sample, kv_prompt[0], kv_prompt[1], kv_prompt[2], ssm_prompt,
            kv_sample[0], kv_sample[1], kv_sample[2], ssm_sample)
```

```python
import functools
import math

import jax
import jax.numpy as jnp
from jax import lax
from jax.experimental import pallas as pl
from jax.experimental.pallas import tpu as pltpu

D_MODEL = 1024
D_PLE = 256
D_FF = 2816
SSM_GROUP = 16
SSM_STATE = 64
D_SSM = 512
N_SSM_GROUPS = D_SSM // SSM_GROUP
HEAD_DIM = 64
HEADS = 4
WINDOWS = (128, 512, 2048)
DILATIONS = (1, 4, 16)
N_GROUPS = 3
D_HEADS = HEADS * HEAD_DIM
N_STEPS = 128
ATTN_SCALE = HEAD_DIM ** -0.5
EPS = 1e-6

LANE = 128
V7X_VMEM_LIMIT_BYTES = 56 * 1024 * 1024
STAGE_TM = 256

CHUNK = 16
N_PAIRS = CHUNK // 2
PAIR_W = 2 * LANE
SSM_TILE_GROUPS = LANE // SSM_GROUP
N_SSM_TILES = N_SSM_GROUPS // SSM_TILE_GROUPS
TILE_STATE_W = SSM_TILE_GROUPS * 2 * SSM_STATE
Q_BLOCK = 128

Z_U = (0, D_SSM)
Z_Q = (D_SSM, D_SSM + N_GROUPS * D_HEADS)
Z_KV = (Z_Q[1], Z_Q[1] + 2 * N_GROUPS * D_HEADS)
Z_GA = (Z_KV[1], Z_KV[1] + D_MODEL)
Z_GB = (Z_GA[1], Z_GA[1] + D_MODEL)

BF16 = jnp.bfloat16
F32 = jnp.float32


def _dot(a, b):
    return jnp.dot(a, b, preferred_element_type=F32)


def _dot_nt(a, b):
    return lax.dot_general(a, b, (((1,), (1,)), ((), ())), preferred_element_type=F32)


def _rms(x, g):
    return x * lax.rsqrt(jnp.mean(x * x, axis=-1, keepdims=True) + EPS) * g


def _swiglu(xn, wg_ref, wu_ref, wd_ref):
    gate = _dot(xn, wg_ref[...])
    up = _dot(xn, wu_ref[...])
    act = (gate * jax.nn.sigmoid(gate) * up).astype(BF16)
    return _dot(act, wd_ref[...])


def _const_spec(shape):
    nd = len(shape)
    return pl.BlockSpec(shape, lambda *_: (0,) * nd, pipeline_mode=pl.Buffered(1))


def _params(n_grid_dims):
    return pltpu.CompilerParams(dimension_semantics=("arbitrary",) * n_grid_dims,
                                vmem_limit_bytes=V7X_VMEM_LIMIT_BYTES)


def _stage1_kernel(x_ref, g1_ref, wg_ref, wu_ref, wd_ref, gm_ref, win_ref,
                   x1_ref, u_ref, q_ref, kv_ref, ga_ref, gb_ref):
    x = x_ref[...]
    xn = _rms(x, g1_ref[...]).astype(BF16)
    x1 = x + 0.5 * _swiglu(xn, wg_ref, wu_ref, wd_ref)
    x1_ref[...] = x1
    h = _rms(x1, gm_ref[...]).astype(BF16)
    z = _dot(h, win_ref[...])
    u_ref[...] = z[:, Z_U[0]:Z_U[1]]
    q_ref[...] = z[:, Z_Q[0]:Z_Q[1]]
    kv_ref[...] = z[:, Z_KV[0]:Z_KV[1]]
    ga_ref[...] = jax.nn.sigmoid(z[:, Z_GA[0]:Z_GA[1]])
    gb_ref[...] = jax.nn.sigmoid(z[:, Z_GB[0]:Z_GB[1]])


def _stage1(x, g1, wg, wu, wd, gm, win):
    t, tm = x.shape[0], STAGE_TM
    tok = lambda w: pl.BlockSpec((tm, w), lambda i: (i, 0))
    widths = (D_MODEL, D_SSM, N_GROUPS * D_HEADS, 2 * N_GROUPS * D_HEADS, D_MODEL, D_MODEL)
    return pl.pallas_call(
        _stage1_kernel,
        grid=(t // tm,),
        in_specs=[tok(D_MODEL), _const_spec(g1.shape), _const_spec(wg.shape), _const_spec(wu.shape),
                  _const_spec(wd.shape), _const_spec(gm.shape), _const_spec(win.shape)],
        out_specs=[tok(w) for w in widths],
        out_shape=[jax.ShapeDtypeStruct((t, w), F32) for w in widths],
        compiler_params=_params(1),
        name="stage1",
    )(x, g1, wg, wu, wd, gm, win)


def _swap_halves(x):
    half = x.shape[1] // 2
    return jnp.concatenate([x[:, half:], x[:, :half]], axis=1)


def _cmul_split(x, a1, a2):
    return x * a1 + _swap_halves(x) * a2


def _shift_rows(x, sh):
    if sh % 8 == 0:
        return jnp.concatenate([jnp.zeros((sh, x.shape[1]), x.dtype), x[:x.shape[0] - sh]], axis=0)
    row = lax.broadcasted_iota(jnp.int32, x.shape, 0)
    return jnp.where(row >= sh, pltpu.roll(x, sh, axis=0), 0.0)


def _ssm_kernel(*refs, n_t, n_rows, has_h0):
    if has_h0:
        u_ref, h0_ref, kp_ref, wp_ref, vp_ref, a1_ref, a2_ref, y_ref, hl_ref = refs
    else:
        u_ref, kp_ref, wp_ref, vp_ref, a1_ref, a2_ref, y_ref, hl_ref = refs
    n_pairs = n_t // 2
    tok = lambda t: pl.ds(t, n_rows, stride=n_t)
    up = [jnp.concatenate([u_ref[tok(2 * a), :], u_ref[tok(2 * a + 1), :]], axis=1).astype(BF16)
          for a in range(n_pairs)]
    w_row0 = (CHUNK - n_t) * LANE
    s = _dot(jnp.concatenate(up, axis=1), wp_ref[0, w_row0:, :])
    if has_h0:
        hprev = h0_ref[...]
        h_last = _cmul_split(hprev, a1_ref[0], a2_ref[0]) + s
        hl_ref[...] = h_last
    else:
        h = s
        for lvl in range(int(math.log2(n_rows))):
            h = h + _cmul_split(_shift_rows(h, 1 << lvl), a1_ref[0, lvl:lvl + 1, :], a2_ref[0, lvl:lvl + 1, :])
        hprev = _shift_rows(h, 1)
        hl_ref[0, 0] = h[n_rows - 1:n_rows, :]
    hpb = hprev.astype(BF16)
    for b in range(n_pairs):
        acc = _dot(hpb, vp_ref[0, b])
        for a in range(b + 1):
            acc = acc + _dot(up[a], kp_ref[0, b - a])
        y_ref[tok(2 * b), :] = acc[:, :LANE]
        y_ref[tok(2 * b + 1), :] = acc[:, LANE:]


def _ssm_table_specs(tabs, idx):
    kp, wp, vp, a1, a2 = tabs
    return [pl.BlockSpec((1,) + kp.shape[1:], lambda *g: (idx(*g), 0, 0, 0)),
            pl.BlockSpec((1,) + wp.shape[1:], lambda *g: (idx(*g), 0, 0)),
            pl.BlockSpec((1,) + vp.shape[1:], lambda *g: (idx(*g), 0, 0, 0)),
            pl.BlockSpec((1,) + a1.shape[1:], lambda *g: (idx(*g), 0, 0)),
            pl.BlockSpec((1,) + a2.shape[1:], lambda *g: (idx(*g), 0, 0))]


def _ssm_prompt(u, tabs, n, seq):
    n_rows = seq // CHUNK
    tile = pl.BlockSpec((seq, LANE), lambda j, i: (i, j))
    return pl.pallas_call(
        functools.partial(_ssm_kernel, n_t=CHUNK, n_rows=n_rows, has_h0=False),
        grid=(N_SSM_TILES, n),
        in_specs=[tile] + _ssm_table_specs(tabs, lambda j, i: j),
        out_specs=[tile, pl.BlockSpec((1, 1, 1, TILE_STATE_W), lambda j, i: (i, j, 0, 0))],
        out_shape=[jax.ShapeDtypeStruct((n * seq, D_SSM), F32),
                   jax.ShapeDtypeStruct((n, N_SSM_TILES, 1, TILE_STATE_W), F32)],
        compiler_params=_params(2),
        name="ssm_prompt",
    )(u, *tabs)


def _ssm_sample(u, h0, tabs, n_seq, n_new):
    tile = pl.BlockSpec((n_seq * n_new, LANE), lambda j: (0, j))
    state = pl.BlockSpec((n_seq, TILE_STATE_W), lambda j: (0, j))
    return pl.pallas_call(
        functools.partial(_ssm_kernel, n_t=n_new, n_rows=n_seq, has_h0=True),
        grid=(N_SSM_TILES,),
        in_specs=[tile, state] + _ssm_table_specs(tabs, lambda j: j),
        out_specs=[tile, state],
        out_shape=[jax.ShapeDtypeStruct((n_seq * n_new, D_SSM), F32),
                   jax.ShapeDtypeStruct((n_seq, N_SSM_TILES * TILE_STATE_W), F32)],
        compiler_params=_params(1),
        name="ssm_sample",
    )(u, h0, *tabs)


def _ssm_tables(a_re, a_im, log_dt, b_re, b_im, c_re, c_im, d_skip, n_scan_levels, n_new):
    hi = lax.Precision.HIGHEST
    g, p = a_re.shape
    nj, tg = N_SSM_TILES, SSM_TILE_GROUPS
    eye = jnp.eye(tg, dtype=F32)
    dt = jnp.exp(log_dt)[:, None]
    mag = jnp.exp(a_re * dt)
    ab_re, ab_im = mag * jnp.cos(a_im * dt), mag * jnp.sin(a_im * dt)
    den = a_re * a_re + a_im * a_im
    nr, ni = ab_re - 1.0, ab_im
    f_re = (nr * a_re + ni * a_im) / den
    f_im = (ni * a_re - nr * a_im) / den
    bb_re = f_re[..., None] * b_re - f_im[..., None] * b_im
    bb_im = f_re[..., None] * b_im + f_im[..., None] * b_re
    pw_re, pw_im = [jnp.ones_like(ab_re)], [jnp.zeros_like(ab_im)]
    for _ in range(CHUNK):
        r, i = pw_re[-1], pw_im[-1]
        pw_re.append(r * ab_re - i * ab_im)
        pw_im.append(r * ab_im + i * ab_re)
    rev_re, rev_im = jnp.stack(pw_re[CHUNK - 1::-1]), jnp.stack(pw_im[CHUNK - 1::-1])
    pw_re, pw_im = jnp.stack(pw_re), jnp.stack(pw_im)
    e_re = c_re[None] * pw_re[:, :, None, :] - c_im[None] * pw_im[:, :, None, :]
    e_im = c_re[None] * pw_im[:, :, None, :] + c_im[None] * pw_re[:, :, None, :]
    k_lag = (jnp.einsum('kgcp,gpd->kgdc', e_re, bb_re, precision=hi)
             - jnp.einsum('kgcp,gpd->kgdc', e_im, bb_im, precision=hi))
    k_lag = k_lag.at[0].add(jnp.eye(SSM_GROUP, dtype=F32)[None] * d_skip.reshape(g, 1, SSM_GROUP))
    two = jnp.arange(2)
    lag = 2 * jnp.arange(N_PAIRS)[:, None, None] + two[None, None, :] - two[None, :, None]
    kp = jnp.where((lag >= 0)[..., None, None, None], k_lag[jnp.maximum(lag, 0)], 0.0)
    kp = kp.reshape(N_PAIRS, 2, 2, nj, tg, SSM_GROUP, SSM_GROUP)
    kp = kp.transpose(3, 0, 1, 4, 5, 2, 6)[..., None, :] * eye.reshape(1, 1, 1, tg, 1, 1, tg, 1)
    kp = kp.reshape(nj, N_PAIRS, PAIR_W, PAIR_W)
    w_re = rev_re[:, :, :, None] * bb_re[None] - rev_im[:, :, :, None] * bb_im[None]
    w_im = rev_re[:, :, :, None] * bb_im[None] + rev_im[:, :, :, None] * bb_re[None]
    wt = jnp.stack([w_re, w_im], axis=2).reshape(CHUNK, nj, tg, 2, p, SSM_GROUP)
    wp = wt.transpose(1, 0, 2, 5, 3, 4)[:, :, :, :, :, None, :] * eye.reshape(1, 1, tg, 1, 1, tg, 1)
    wp = wp.reshape(nj, CHUNK * LANE, TILE_STATE_W)
    vt = jnp.stack([e_re[1:], -e_im[1:]], axis=2).reshape(N_PAIRS, 2, nj, tg, 2, SSM_GROUP, p)
    vp = vt.transpose(2, 0, 4, 3, 6, 1, 5)[..., None, :] * eye.reshape(1, 1, 1, tg, 1, 1, tg, 1)
    vp = vp.reshape(nj, N_PAIRS, TILE_STATE_W, PAIR_W)

    def split_pack(r, i):
        r, i = r.reshape(-1, nj, tg * p), i.reshape(-1, nj, tg * p)
        return (jnp.concatenate([r, r], -1).transpose(1, 0, 2), jnp.concatenate([-i, i], -1).transpose(1, 0, 2))

    lv_re, lv_im = [pw_re[CHUNK]], [pw_im[CHUNK]]
    for _ in range(n_scan_levels - 1):
        r, i = lv_re[-1], lv_im[-1]
        lv_re.append(r * r - i * i)
        lv_im.append(2.0 * r * i)
    a1, a2 = split_pack(jnp.stack(lv_re), jnp.stack(lv_im))
    a1s, a2s = split_pack(pw_re[n_new][None], pw_im[n_new][None])
    mats = (kp.astype(BF16), wp.astype(BF16), vp.astype(BF16))
    return mats + (a1, a2), mats + (a1s, a2s)


def _head_of_lane(width):
    return lax.broadcasted_iota(jnp.int32, (1, width), 1) // HEAD_DIM


def _stack_heads(q):
    head = _head_of_lane(q.shape[1])
    q = q.astype(F32)
    return jnp.concatenate([jnp.where(head == h, q, 0.0) for h in range(HEADS)], axis=0).astype(BF16)


def _unstack_heads(x, rows):
    head = _head_of_lane(x.shape[1])
    out = jnp.zeros((rows, x.shape[1]), x.dtype)
    for h in range(HEADS):
        out = out + jnp.where(head == h, x[h * rows:(h + 1) * rows], 0.0)
    return out


def _softmax_pv(s, valid, v):
    s = jnp.where(valid, s * ATTN_SCALE, -jnp.inf)
    mx = jnp.max(s, axis=-1, keepdims=True)
    pe = jnp.exp(s - mx)
    den = jnp.sum(pe, axis=-1, keepdims=True)
    o = _dot(pe.astype(BF16), v) / den
    return o, mx + jnp.log(den)


def _attn_prompt_kernel(q0, q1, kp0, kp1, kc0, kc1, vp0, vp1, vc0, vc1, o0, o1, l0, l1, *, d):
    qb = Q_BLOCK
    b = pl.program_id(1)
    qi = lax.broadcasted_iota(jnp.int32, (HEADS * qb, 2 * qb), 0) % qb
    ki = lax.broadcasted_iota(jnp.int32, (HEADS * qb, 2 * qb), 1)
    diff = qb + qi - ki
    valid = (diff >= 0) & (diff <= N_STEPS) & (ki >= jnp.where(b > 0, 0, qb))

    def one_class(r, carry):
        rows = pl.ds(r, qb, stride=d) if d > 1 else pl.ds(0, qb)
        both = lambda lo, hi: jnp.concatenate([lo[rows, :], hi[rows, :]], axis=1)
        qs = _stack_heads(both(q0, q1))
        k = jnp.concatenate([both(kp0, kp1), both(kc0, kc1)], axis=0).astype(BF16)
        v = jnp.concatenate([both(vp0, vp1), both(vc0, vc1)], axis=0).astype(BF16)
        o, lse = _softmax_pv(_dot_nt(qs, k), valid, v)
        o = _unstack_heads(o, qb)
        lse = _unstack_heads(jnp.broadcast_to(lse, (HEADS * qb, D_HEADS)), qb)
        o0[rows, :], o1[rows, :] = o[:, :LANE], o[:, LANE:]
        l0[rows, :], l1[rows, :] = lse[:, :LANE], lse[:, LANE:]
        return carry

    if d > 1:
        lax.fori_loop(0, d, one_class, 0)
    else:
        one_class(0, 0)


def _attn_prompt(q_all, kv_all, n, seq, grp):
    d = DILATIONS[grp]
    tb = Q_BLOCK * d
    nb = seq // tb
    cur = lambda c: pl.BlockSpec((tb, LANE), lambda i, b: (i * nb + b, c))
    prev = lambda c: pl.BlockSpec((tb, LANE), lambda i, b: (i * nb + jnp.maximum(b - 1, 0), c))
    qc, kc, vc = 2 * grp, 4 * grp, 4 * grp + 2
    o0, o1, l0, l1 = pl.pallas_call(
        functools.partial(_attn_prompt_kernel, d=d),
        grid=(n, nb),
        in_specs=[cur(qc), cur(qc + 1), prev(kc), prev(kc + 1), cur(kc), cur(kc + 1),
                  prev(vc), prev(vc + 1), cur(vc), cur(vc + 1)],
        out_specs=[pl.BlockSpec((tb, LANE), lambda i, b: (i * nb + b, 0))] * 4,
        out_shape=[jax.ShapeDtypeStruct((n * seq, LANE), F32)] * 4,
        compiler_params=_params(2),
        name=f"attn_prompt_d{d}",
    )(q_all, q_all, kv_all, kv_all, kv_all, kv_all, kv_all, kv_all, kv_all, kv_all)
    return o0, o1, l0, l1


def _attn_sample_kernel(q_ref, kn_ref, vn_ref, c_ref, o_ref, lse_ref, *, d, n_res, n_new, seq_blk):
    m_rows = c_ref.shape[1]
    n_rows = HEADS * n_new
    n_keys = (n_res + 1) * m_rows
    l_q = lax.broadcasted_iota(jnp.int32, (n_rows, n_keys), 0) % n_new
    col = lax.broadcasted_iota(jnp.int32, (n_rows, n_keys), 1)
    r_k, m_k = col // m_rows, col % m_rows
    shift = int(math.log2(d))
    dl = l_q - r_k
    valid_buf = (dl >= 0) & ((dl & (d - 1)) == 0) & (m_k >= (dl >> shift))
    dn = l_q - m_k
    valid_new = (m_k < n_new) & (dn >= 0) & ((dn & (d - 1)) == 0)
    is_buf = r_k < n_res
    valid = (is_buf & valid_buf) | (~is_buf & valid_new)
    pad = jnp.zeros((m_rows - n_new, D_HEADS), F32)
    kv_w = 2 * D_HEADS
    for s in range(seq_blk):
        qs = _stack_heads(q_ref[s])
        ks = [c_ref[s, :, r * kv_w:r * kv_w + D_HEADS] for r in range(n_res)]
        vs = [c_ref[s, :, r * kv_w + D_HEADS:(r + 1) * kv_w] for r in range(n_res)]
        k = jnp.concatenate(ks + [kn_ref[s], pad], axis=0).astype(BF16)
        v = jnp.concatenate(vs + [vn_ref[s], pad], axis=0).astype(BF16)
        o, lse = _softmax_pv(_dot_nt(qs, k), valid, v)
        o_ref[s] = _unstack_heads(o, n_new)
        lse_ref[s] = _unstack_heads(jnp.broadcast_to(lse, o.shape), n_new)


def _attn_sample(q_all, kv_all, cache, n_seq, n_new, grp, seq_blk):
    d = DILATIONS[grp]
    wb = cache.shape[1]
    assert wb == N_STEPS * d and n_new <= wb // d
    n_res = min(d, n_new)
    m_rows = wb // d
    qv = q_all.reshape(n_seq, n_new, N_GROUPS * D_HEADS)
    kvv = kv_all.reshape(n_seq, n_new, 2 * N_GROUPS * D_HEADS)
    cv = cache.reshape(n_seq, m_rows, d * 2 * D_HEADS)
    blk = (seq_blk, n_new, D_HEADS)
    o, lse = pl.pallas_call(
        functools.partial(_attn_sample_kernel, d=d, n_res=n_res, n_new=n_new, seq_blk=seq_blk),
        grid=(n_seq // seq_blk,),
        in_specs=[pl.BlockSpec(blk, lambda i: (i, 0, grp)),
                  pl.BlockSpec(blk, lambda i: (i, 0, 2 * grp)),
                  pl.BlockSpec(blk, lambda i: (i, 0, 2 * grp + 1)),
                  pl.BlockSpec((seq_blk, m_rows, n_res * 2 * D_HEADS), lambda i: (i, 0, 0))],
        out_specs=[pl.BlockSpec(blk, lambda i: (i, 0, 0))] * 2,
        out_shape=[jax.ShapeDtypeStruct((n_seq, n_new, D_HEADS), F32)] * 2,
        compiler_params=_params(1),
        name=f"attn_sample_d{d}",
    )(qv, kvv, kvv, cv)
    o, lse = o.reshape(n_seq * n_new, D_HEADS), lse.reshape(n_seq * n_new, D_HEADS)
    return o[:, :LANE], o[:, LANE:], lse[:, :LANE], lse[:, LANE:]


def _gelu_tanh(x):
    return 0.5 * x * (1.0 + jnp.tanh(math.sqrt(2.0 / math.pi) * (x + 0.044715 * (x * x * x))))


def _stage3_kernel(*refs):
    x1_ref, yp_ref = refs[:2]
    attn = refs[2:2 + 4 * N_GROUPS]
    (ga_ref, gb_ref, p_ref, wglu_ref, wbs_ref, wba_ref, wout_ref, g2_ref, wg_ref, wu_ref, wd_ref,
     gp_ref, wpg_ref, wpp_ref, gf_ref, y_ref) = refs[2 + 4 * N_GROUPS:]
    y = _gelu_tanh(yp_ref[...])
    glu = y * jax.nn.sigmoid(_dot(y.astype(BF16), wglu_ref[...]))
    full = lambda lo, hi: jnp.concatenate([lo[...], hi[...]], axis=1)
    outs = [full(attn[4 * g], attn[4 * g + 1]) for g in range(N_GROUPS)]
    lses = [full(attn[4 * g + 2], attn[4 * g + 3]) for g in range(N_GROUPS)]
    mx = jnp.maximum(jnp.maximum(lses[0], lses[1]), lses[2])
    es = [jnp.exp(l - mx) for l in lses]
    y_attn = (es[0] * outs[0] + es[1] * outs[1] + es[2] * outs[2]) / (es[0] + es[1] + es[2])
    merged = (ga_ref[...] * _dot(glu.astype(BF16), wbs_ref[...])
              + gb_ref[...] * _dot(y_attn.astype(BF16), wba_ref[...]))
    x = x1_ref[...] + _dot(merged.astype(BF16), wout_ref[...])
    x = x + 0.5 * _swiglu(_rms(x, g2_ref[...]).astype(BF16), wg_ref, wu_ref, wd_ref)
    gate = jax.nn.sigmoid(_dot(_rms(x, gp_ref[...]).astype(BF16), wpg_ref[...]))
    x = x + gate * _dot(p_ref[...].astype(BF16), wpp_ref[...])
    y_ref[...] = _rms(x, gf_ref[...])


def _stage3(tok_inputs, weights):
    t, tm = tok_inputs[0].shape[0], STAGE_TM
    tok = lambda a: pl.BlockSpec((tm, a.shape[1]), lambda i: (i, 0))
    return pl.pallas_call(
        _stage3_kernel,
        grid=(t // tm,),
        in_specs=[tok(a) for a in tok_inputs] + [_const_spec(w.shape) for w in weights],
        out_specs=pl.BlockSpec((tm, D_MODEL), lambda i: (i, 0)),
        out_shape=jax.ShapeDtypeStruct((t, D_MODEL), F32),
        compiler_params=_params(1),
        name="stage3",
    )(*tok_inputs, *weights)


def kernel(x_prompt, x_sample, p_prompt, p_sample, cache_kv_w128, cache_kv_w512, cache_kv_w2048, state_ssm,
           g_ffn1, ffn1_w_gate, ffn1_w_up, ffn1_w_down, g_mix, w_in, ssm_a_re, ssm_a_im, ssm_log_dt,
           ssm_b_re, ssm_b_im, ssm_c_re, ssm_c_im, ssm_d, ssm_w_glu, w_br_ssm, w_br_attn, w_out,
           g_ffn2, ffn2_w_gate, ffn2_w_up, ffn2_w_down, g_ple, w_ple_gate, w_ple_proj, g_final):
    assert x_prompt.shape[-1] == D_MODEL and g_ffn1.shape[0] == 1
    n_p, seq, _ = x_prompt.shape
    n_s, n_new, _ = x_sample.shape
    caches = (cache_kv_w128, cache_kv_w512, cache_kv_w2048)
    row = lambda g: g.reshape(1, -1)
    bf = lambda w: w[0].astype(BF16)

    wi = w_in[0]
    qkv0 = D_SSM
    kv_cols = wi[:, qkv0 + N_GROUPS * D_HEADS:qkv0 + 3 * N_GROUPS * D_HEADS]
    kv_cols = kv_cols.reshape(D_MODEL, 2, N_GROUPS, D_HEADS).transpose(0, 2, 1, 3).reshape(D_MODEL, -1)
    win = jnp.concatenate([wi[:, :qkv0 + N_GROUPS * D_HEADS], kv_cols,
                           wi[:, qkv0 + 3 * N_GROUPS * D_HEADS:]], axis=1).astype(BF16)
    s1_w = (row(g_ffn1[0]), bf(ffn1_w_gate), bf(ffn1_w_up), bf(ffn1_w_down), row(g_mix[0]), win)
    s3_w = (bf(ssm_w_glu), bf(w_br_ssm), bf(w_br_attn), bf(w_out), row(g_ffn2[0]),
            bf(ffn2_w_gate), bf(ffn2_w_up), bf(ffn2_w_down), row(g_ple[0]), bf(w_ple_gate),
            bf(w_ple_proj), row(g_final))

    n_chunks = seq // CHUNK
    tabs_p, tabs_s = _ssm_tables(
        ssm_a_re[0], ssm_a_im[0], ssm_log_dt[0], ssm_b_re[0], ssm_b_im[0], ssm_c_re[0], ssm_c_im[0],
        ssm_d[0], int(math.log2(n_chunks)), n_new)
    tile_state = (-1, N_SSM_TILES, 2, SSM_TILE_GROUPS, SSM_STATE)
    from_tiles = lambda h: h.reshape(tile_state).transpose(0, 1, 3, 4, 2).reshape(1, -1, N_SSM_GROUPS, SSM_STATE, 2)

    x1, u, q, kv, ga, gb = _stage1(x_prompt.reshape(n_p * seq, D_MODEL), *s1_w)
    y_pre, h_last = _ssm_prompt(u, tabs_p, n_p, seq)
    attn = [a for grp in range(N_GROUPS) for a in _attn_prompt(q, kv, n_p, seq, grp)]
    tok = (x1, y_pre, *attn, ga, gb, p_prompt[0].reshape(n_p * seq, D_PLE))
    y_prompt = _stage3(tok, s3_w).reshape(n_p, seq, D_MODEL)
    kv_p = kv.reshape(n_p, seq, N_GROUPS, 2, HEADS, HEAD_DIM)
    kv_prompt = [kv_p[:, seq - min(WINDOWS[grp], seq):, grp][None] for grp in range(N_GROUPS)]
    ssm_prompt = from_tiles(h_last)

    x1, u, q, kv, ga, gb = _stage1(x_sample.reshape(n_s * n_new, D_MODEL), *s1_w)
    h0 = state_ssm[0].reshape(n_s, N_SSM_TILES, SSM_TILE_GROUPS, SSM_STATE, 2)
    h0 = h0.transpose(0, 1, 4, 2, 3).reshape(n_s, N_SSM_TILES * TILE_STATE_W)
    y_pre, h_new = _ssm_sample(u, h0, tabs_s, n_s, n_new)
    attn = [a for grp in range(N_GROUPS)
            for a in _attn_sample(q, kv, caches[grp][0].reshape(n_s, -1, 2 * D_HEADS), n_s, n_new, grp, seq_blk=4)]
    tok = (x1, y_pre, *attn, ga, gb, p_sample[0].reshape(n_s * n_new, D_PLE))
    y_sample = _stage3(tok, s3_w).reshape(n_s, n_new, D_MODEL)
    kv_s = kv.reshape(n_s, n_new, N_GROUPS, 2, HEADS, HEAD_DIM)
    kv_sample = [kv_s[:, :, grp][None] for grp in range(N_GROUPS)]
    ssm_sample = from_tiles(h_new)

    return (y_prompt, y_sample, kv_prompt[0], kv_prompt[1], kv_prompt[2], ssm_prompt,
            kv_sample[0], kv_sample[1], kv_sample[2], ssm_sample)
```

```python
import functools
import math

import jax
import jax.numpy as jnp
from jax import lax
from jax.experimental import pallas as pl
from jax.experimental.pallas import tpu as pltpu

D_MODEL = 1024
D_PLE = 256
D_FF = 2816
SSM_GROUP = 16
SSM_STATE = 64
D_SSM = 512
N_SSM_GROUPS = D_SSM // SSM_GROUP
HEAD_DIM = 64
HEADS = 4
WINDOWS = (128, 512, 2048)
DILATIONS = (1, 4, 16)
N_GROUPS = 3
D_HEADS = HEADS * HEAD_DIM
N_STEPS = 128
ATTN_SCALE = HEAD_DIM ** -0.5
EPS = 1e-6

LANE = 128
V7X_VMEM_LIMIT_BYTES = 56 * 1024 * 1024
STAGE_TM = 256

CHUNK = 16
N_PAIRS = CHUNK // 2
PAIR_W = 2 * LANE
SSM_TILE_GROUPS = LANE // SSM_GROUP
N_SSM_TILES = N_SSM_GROUPS // SSM_TILE_GROUPS
TILE_STATE_W = SSM_TILE_GROUPS * 2 * SSM_STATE
Q_BLOCK = 128

Z_U = (0, D_SSM)
Z_Q = (D_SSM, D_SSM + N_GROUPS * D_HEADS)
Z_KV = (Z_Q[1], Z_Q[1] + 2 * N_GROUPS * D_HEADS)
Z_GA = (Z_KV[1], Z_KV[1] + D_MODEL)
Z_GB = (Z_GA[1], Z_GA[1] + D_MODEL)

BF16 = jnp.bfloat16
F32 = jnp.float32


def _dot(a, b):
    return jnp.dot(a, b, preferred_element_type=F32)


def _dot_nt(a, b):
    return lax.dot_general(a, b, (((1,), (1,)), ((), ())), preferred_element_type=F32)


def _rms(x, g):
    return x * lax.rsqrt(jnp.mean(x * x, axis=-1, keepdims=True) + EPS) * g


def _swiglu(xn, wg_ref, wu_ref, wd_ref):
    gate = _dot(xn, wg_ref[...])
    up = _dot(xn, wu_ref[...])
    act = (gate * jax.nn.sigmoid(gate) * up).astype(BF16)
    return _dot(act, wd_ref[...])


def _const_spec(shape):
    nd = len(shape)
    return pl.BlockSpec(shape, lambda *_: (0,) * nd, pipeline_mode=pl.Buffered(1))


def _params(n_grid_dims):
    return pltpu.CompilerParams(dimension_semantics=("arbitrary",) * n_grid_dims,
                                vmem_limit_bytes=V7X_VMEM_LIMIT_BYTES)


def _stage1_kernel(x_ref, g1_ref, wg_ref, wu_ref, wd_ref, gm_ref, win_ref,
                   x1_ref, u_ref, q_ref, kv_ref, ga_ref, gb_ref):
    x = x_ref[...]
    xn = _rms(x, g1_ref[...]).astype(BF16)
    x1 = x + 0.5 * _swiglu(xn, wg_ref, wu_ref, wd_ref)
    x1_ref[...] = x1
    h = _rms(x1, gm_ref[...]).astype(BF16)
    z = _dot(h, win_ref[...])
    u_ref[...] = z[:, Z_U[0]:Z_U[1]]
    q_ref[...] = z[:, Z_Q[0]:Z_Q[1]]
    kv_ref[...] = z[:, Z_KV[0]:Z_KV[1]]
    ga_ref[...] = jax.nn.sigmoid(z[:, Z_GA[0]:Z_GA[1]])
    gb_ref[...] = jax.nn.sigmoid(z[:, Z_GB[0]:Z_GB[1]])


def _stage1(x, g1, wg, wu, wd, gm, win):
    t, tm = x.shape[0], STAGE_TM
    tok = lambda w: pl.BlockSpec((tm, w), lambda i: (i, 0))
    widths = (D_MODEL, D_SSM, N_GROUPS * D_HEADS, 2 * N_GROUPS * D_HEADS, D_MODEL, D_MODEL)
    return pl.pallas_call(
        _stage1_kernel,
        grid=(t // tm,),
        in_specs=[tok(D_MODEL), _const_spec(g1.shape), _const_spec(wg.shape), _const_spec(wu.shape),
                  _const_spec(wd.shape), _const_spec(gm.shape), _const_spec(win.shape)],
        out_specs=[tok(w) for w in widths],
        out_shape=[jax.ShapeDtypeStruct((t, w), F32) for w in widths],
        compiler_params=_params(1),
        name="stage1",
    )(x, g1, wg, wu, wd, gm, win)


def _swap_halves(x):
    half = x.shape[1] // 2
    return jnp.concatenate([x[:, half:], x[:, :half]], axis=1)


def _cmul_split(x, a1, a2):
    return x * a1 + _swap_halves(x) * a2


def _shift_rows(x, sh):
    if sh % 8 == 0:
        return jnp.concatenate([jnp.zeros((sh, x.shape[1]), x.dtype), x[:x.shape[0] - sh]], axis=0)
    row = lax.broadcasted_iota(jnp.int32, x.shape, 0)
    return jnp.where(row >= sh, pltpu.roll(x, sh, axis=0), 0.0)


def _ssm_kernel(*refs, n_t, n_rows, has_h0):
    if has_h0:
        u_ref, h0_ref, kp_ref, wp_ref, vpt_ref, a1_ref, a2_ref, y_ref, hl_ref = refs
    else:
        u_ref, kp_ref, wp_ref, vpt_ref, a1_ref, a2_ref, y_ref, hl_ref = refs
    n_pairs = n_t // 2
    tok = lambda t: pl.ds(t, n_rows, stride=n_t)
    up = [jnp.concatenate([u_ref[tok(2 * a), :], u_ref[tok(2 * a + 1), :]], axis=1).astype(BF16)
          for a in range(n_pairs)]
    w_row0 = (CHUNK - n_t) * LANE
    s = _dot(jnp.concatenate(up, axis=1), wp_ref[0, w_row0:, :])
    if has_h0:
        hprev = h0_ref[...]
        h_last = _cmul_split(hprev, a1_ref[0], a2_ref[0]) + s
        hl_ref[...] = h_last
    else:
        h = s
        for lvl in range(int(math.log2(n_rows))):
            h = h + _cmul_split(_shift_rows(h, 1 << lvl), a1_ref[0, lvl:lvl + 1, :], a2_ref[0, lvl:lvl + 1, :])
        hprev = _shift_rows(h, 1)
        hl_ref[0, 0] = h[n_rows - 1:n_rows, :]
    hpb = hprev.astype(BF16)
    for b in range(n_pairs):
        acc = _dot_nt(hpb, vpt_ref[0, b])
        for a in range(b + 1):
            acc = acc + _dot(up[a], kp_ref[0, b - a])
        y_ref[tok(2 * b), :] = acc[:, :LANE]
        y_ref[tok(2 * b + 1), :] = acc[:, LANE:]


def _ssm_table_specs(tabs, idx):
    kp, wp, vp, a1, a2 = tabs
    return [pl.BlockSpec((1,) + kp.shape[1:], lambda *g: (idx(*g), 0, 0, 0)),
            pl.BlockSpec((1,) + wp.shape[1:], lambda *g: (idx(*g), 0, 0)),
            pl.BlockSpec((1,) + vp.shape[1:], lambda *g: (idx(*g), 0, 0, 0)),
            pl.BlockSpec((1,) + a1.shape[1:], lambda *g: (idx(*g), 0, 0)),
            pl.BlockSpec((1,) + a2.shape[1:], lambda *g: (idx(*g), 0, 0))]


def _ssm_prompt(u, tabs, n, seq):
    n_rows = seq // CHUNK
    tile = pl.BlockSpec((seq, LANE), lambda j, i: (i, j))
    return pl.pallas_call(
        functools.partial(_ssm_kernel, n_t=CHUNK, n_rows=n_rows, has_h0=False),
        grid=(N_SSM_TILES, n),
        in_specs=[tile] + _ssm_table_specs(tabs, lambda j, i: j),
        out_specs=[tile, pl.BlockSpec((1, 1, 1, TILE_STATE_W), lambda j, i: (i, j, 0, 0))],
        out_shape=[jax.ShapeDtypeStruct((n * seq, D_SSM), F32),
                   jax.ShapeDtypeStruct((n, N_SSM_TILES, 1, TILE_STATE_W), F32)],
        compiler_params=_params(2),
        name="ssm_prompt",
    )(u, *tabs)


def _ssm_sample(u, h0, tabs, n_seq, n_new):
    tile = pl.BlockSpec((n_seq * n_new, LANE), lambda j: (0, j))
    state = pl.BlockSpec((n_seq, TILE_STATE_W), lambda j: (0, j))
    return pl.pallas_call(
        functools.partial(_ssm_kernel, n_t=n_new, n_rows=n_seq, has_h0=True),
        grid=(N_SSM_TILES,),
        in_specs=[tile, state] + _ssm_table_specs(tabs, lambda j: j),
        out_specs=[tile, state],
        out_shape=[jax.ShapeDtypeStruct((n_seq * n_new, D_SSM), F32),
                   jax.ShapeDtypeStruct((n_seq, N_SSM_TILES * TILE_STATE_W), F32)],
        compiler_params=_params(1),
        name="ssm_sample",
    )(u, h0, *tabs)


def _ssm_prep_kernel(are_ref, aim_ref, ldt_ref, bre_ref, bim_ref, cre_ref, cim_ref, d_ref,
                     kp_ref, wp_ref, vpt_ref, a1_ref, a2_ref, a1s_ref, a2s_ref, *, n_levels, n_new):
    a_re, a_im = are_ref[0], aim_ref[0]
    dt = jnp.exp(ldt_ref[0])
    mag = jnp.exp(a_re * dt)
    ab_re, ab_im = mag * jnp.cos(a_im * dt), mag * jnp.sin(a_im * dt)
    den = a_re * a_re + a_im * a_im
    nr, ni = ab_re - 1.0, ab_im
    f_re = (nr * a_re + ni * a_im) / den
    f_im = (ni * a_re - nr * a_im) / den
    shape = bre_ref.shape[1:]
    same_group = (lax.broadcasted_iota(jnp.int32, shape, 0) // SSM_GROUP
                  == lax.broadcasted_iota(jnp.int32, shape, 1) // SSM_STATE)
    b_re, b_im = bre_ref[0], bim_ref[0]
    x_re = jnp.where(same_group, f_re * b_re - f_im * b_im, 0.0)
    x_im = jnp.where(same_group, f_re * b_im + f_im * b_re, 0.0)
    c_re = jnp.where(same_group, cre_ref[0], 0.0)
    c_im = jnp.where(same_group, cim_ref[0], 0.0)
    pw = [(jnp.ones_like(ab_re), jnp.zeros_like(ab_im))]
    for _ in range(CHUNK):
        r, i = pw[-1]
        pw.append((r * ab_re - i * ab_im, r * ab_im + i * ab_re))
    cat = lambda r, i: jnp.concatenate([r, i], axis=1)
    e_pack = [cat(c_re * r - c_im * i, -(c_re * i + c_im * r)) for r, i in pw]
    x_pack = cat(x_re, x_im)
    k_lag = [lax.dot_general(x_pack, e_pack[k], (((1,), (1,)), ((), ())), precision=lax.Precision.HIGHEST,
                             preferred_element_type=F32) for k in range(CHUNK)]
    diag = (lax.broadcasted_iota(jnp.int32, (LANE, LANE), 0) == lax.broadcasted_iota(jnp.int32, (LANE, LANE), 1))
    k_lag[0] = k_lag[0] + jnp.where(diag, d_ref[0], 0.0)
    zero = jnp.zeros((LANE, LANE), F32)
    for dl in range(N_PAIRS):
        top = cat(k_lag[2 * dl], k_lag[2 * dl + 1])
        bot = cat(k_lag[2 * dl - 1] if dl > 0 else zero, k_lag[2 * dl])
        kp_ref[0, dl] = jnp.concatenate([top, bot], axis=0).astype(BF16)
    for t in range(CHUNK):
        r, i = pw[CHUNK - 1 - t]
        wp_ref[0, t * LANE:(t + 1) * LANE, :] = cat(x_re * r - x_im * i, x_re * i + x_im * r).astype(BF16)
    for t in range(CHUNK):
        vpt_ref[0, t // 2, (t % 2) * LANE:(t % 2 + 1) * LANE, :] = e_pack[t + 1].astype(BF16)
    r, i = pw[CHUNK]
    for lvl in range(n_levels):
        a1_ref[0, lvl:lvl + 1, :], a2_ref[0, lvl:lvl + 1, :] = cat(r, r), cat(-i, i)
        r, i = r * r - i * i, 2.0 * r * i
    r, i = pw[n_new]
    a1s_ref[0], a2s_ref[0] = cat(r, r), cat(-i, i)


def _ssm_tables(a_re, a_im, log_dt, b_re, b_im, c_re, c_im, d_skip, n_levels, n_new):
    nj, tg = N_SSM_TILES, SSM_TILE_GROUPS
    half = tg * SSM_STATE
    rowv = lambda v: v.reshape(nj, 1, half)
    tiled = lambda m: jnp.tile(m.reshape(nj, LANE, SSM_STATE), (1, 1, tg))
    ins = (rowv(a_re), rowv(a_im), rowv(jnp.repeat(log_dt, SSM_STATE)),
           tiled(b_re.transpose(0, 2, 1)), tiled(b_im.transpose(0, 2, 1)), tiled(c_re), tiled(c_im),
           d_skip.reshape(nj, 1, LANE))
    shapes = ((nj, N_PAIRS, PAIR_W, PAIR_W), (nj, CHUNK * LANE, TILE_STATE_W), (nj, N_PAIRS, PAIR_W, TILE_STATE_W),
              (nj, n_levels, TILE_STATE_W), (nj, n_levels, TILE_STATE_W), (nj, 1, TILE_STATE_W), (nj, 1, TILE_STATE_W))
    dtypes = (BF16, BF16, BF16, F32, F32, F32, F32)
    per_tile = lambda s: pl.BlockSpec((1,) + tuple(s[1:]), lambda j: (j,) + (0,) * (len(s) - 1))
    kp, wp, vpt, a1, a2, a1s, a2s = pl.pallas_call(
        functools.partial(_ssm_prep_kernel, n_levels=n_levels, n_new=n_new),
        grid=(nj,),
        in_specs=[per_tile(x.shape) for x in ins],
        out_specs=[per_tile(s) for s in shapes],
        out_shape=[jax.ShapeDtypeStruct(s, dt) for s, dt in zip(shapes, dtypes)],
        compiler_params=_params(1),
        name="ssm_prep",
    )(*ins)
    return (kp, wp, vpt, a1, a2), (kp, wp, vpt, a1s, a2s)


def _head_of_lane(width):
    return lax.broadcasted_iota(jnp.int32, (1, width), 1) // HEAD_DIM


def _stack_heads(q):
    head = _head_of_lane(q.shape[1])
    q = q.astype(F32)
    return jnp.concatenate([jnp.where(head == h, q, 0.0) for h in range(HEADS)], axis=0).astype(BF16)


def _unstack_heads(x, rows):
    head = _head_of_lane(x.shape[1])
    out = jnp.zeros((rows, x.shape[1]), x.dtype)
    for h in range(HEADS):
        out = out + jnp.where(head == h, x[h * rows:(h + 1) * rows], 0.0)
    return out


def _softmax_pv(s, valid, v):
    s = jnp.where(valid, s * ATTN_SCALE, -jnp.inf)
    mx = jnp.max(s, axis=-1, keepdims=True)
    pe = jnp.exp(s - mx)
    den = jnp.sum(pe, axis=-1, keepdims=True)
    o = _dot(pe.astype(BF16), v) / den
    return o, mx + jnp.log(den)


def _attn_prompt_kernel(q0, q1, kp0, kp1, kc0, kc1, vp0, vp1, vc0, vc1, o0, o1, l0, l1, *, d):
    qb = Q_BLOCK
    b = pl.program_id(1)
    qi = lax.broadcasted_iota(jnp.int32, (HEADS * qb, 2 * qb), 0) % qb
    ki = lax.broadcasted_iota(jnp.int32, (HEADS * qb, 2 * qb), 1)
    diff = qb + qi - ki
    valid = (diff >= 0) & (diff <= N_STEPS) & (ki >= jnp.where(b > 0, 0, qb))

    def one_class(r, carry):
        rows = pl.ds(r, qb, stride=d) if d > 1 else pl.ds(0, qb)
        both = lambda lo, hi: jnp.concatenate([lo[rows, :], hi[rows, :]], axis=1)
        qs = _stack_heads(both(q0, q1))
        k = jnp.concatenate([both(kp0, kp1), both(kc0, kc1)], axis=0).astype(BF16)
        v = jnp.concatenate([both(vp0, vp1), both(vc0, vc1)], axis=0).astype(BF16)
        o, lse = _softmax_pv(_dot_nt(qs, k), valid, v)
        o = _unstack_heads(o, qb)
        lse = _unstack_heads(jnp.broadcast_to(lse, (HEADS * qb, D_HEADS)), qb)
        o0[rows, :], o1[rows, :] = o[:, :LANE], o[:, LANE:]
        l0[rows, :], l1[rows, :] = lse[:, :LANE], lse[:, LANE:]
        return carry

    if d > 1:
        lax.fori_loop(0, d, one_class, 0)
    else:
        one_class(0, 0)


def _attn_prompt(q_all, kv_all, n, seq, grp):
    d = DILATIONS[grp]
    tb = Q_BLOCK * d
    nb = seq // tb
    cur = lambda c: pl.BlockSpec((tb, LANE), lambda i, b: (i * nb + b, c))
    prev = lambda c: pl.BlockSpec((tb, LANE), lambda i, b: (i * nb + jnp.maximum(b - 1, 0), c))
    qc, kc, vc = 2 * grp, 4 * grp, 4 * grp + 2
    o0, o1, l0, l1 = pl.pallas_call(
        functools.partial(_attn_prompt_kernel, d=d),
        grid=(n, nb),
        in_specs=[cur(qc), cur(qc + 1), prev(kc), prev(kc + 1), cur(kc), cur(kc + 1),
                  prev(vc), prev(vc + 1), cur(vc), cur(vc + 1)],
        out_specs=[pl.BlockSpec((tb, LANE), lambda i, b: (i * nb + b, 0))] * 4,
        out_shape=[jax.ShapeDtypeStruct((n * seq, LANE), F32)] * 4,
        compiler_params=_params(2),
        name=f"attn_prompt_d{d}",
    )(q_all, q_all, kv_all, kv_all, kv_all, kv_all, kv_all, kv_all, kv_all, kv_all)
    return o0, o1, l0, l1


def _attn_sample_kernel(q_ref, kn_ref, vn_ref, c_ref, o_ref, lse_ref, *, d, n_new, seq_blk):
    w = c_ref.shape[-1]
    n_rows = HEADS * n_new
    l_b = lax.broadcasted_iota(jnp.int32, (n_rows, w), 0) % n_new
    pos = lax.broadcasted_iota(jnp.int32, (n_rows, w), 1)
    valid_buf = (((w + l_b - pos) & (d - 1)) == 0) & (pos >= l_b)
    dn = (lax.broadcasted_iota(jnp.int32, (n_rows, LANE), 0) % n_new) - lax.broadcasted_iota(jnp.int32, (n_rows, LANE), 1)
    valid_new = (dn >= 0) & ((dn & (d - 1)) == 0)
    pad = jnp.zeros((LANE - n_new, D_HEADS), F32)

    def one_seq(s, carry):
        qs = _stack_heads(q_ref[s])
        k_t, v_t = c_ref[s, 0].astype(BF16), c_ref[s, 1].astype(BF16)
        k_n = jnp.concatenate([kn_ref[s], pad], axis=0).astype(BF16)
        v_n = jnp.concatenate([vn_ref[s], pad], axis=0).astype(BF16)
        s_b = jnp.where(valid_buf, _dot(qs, k_t) * ATTN_SCALE, -jnp.inf)
        s_n = jnp.where(valid_new, _dot_nt(qs, k_n) * ATTN_SCALE, -jnp.inf)
        mx = jnp.maximum(jnp.max(s_b, axis=-1, keepdims=True), jnp.max(s_n, axis=-1, keepdims=True))
        p_b, p_n = jnp.exp(s_b - mx), jnp.exp(s_n - mx)
        den = jnp.sum(p_b, axis=-1, keepdims=True) + jnp.sum(p_n, axis=-1, keepdims=True)
        o = (_dot_nt(p_b.astype(BF16), v_t) + _dot(p_n.astype(BF16), v_n)) / den
        o_ref[s] = _unstack_heads(o, n_new)
        lse_ref[s] = _unstack_heads(jnp.broadcast_to(mx + jnp.log(den), o.shape), n_new)
        return carry

    lax.fori_loop(0, seq_blk, one_seq, 0)


def _attn_sample(q_all, kv_all, cache, n_seq, n_new, grp):
    d = DILATIONS[grp]
    w = cache.shape[1]
    assert w == N_STEPS * d and n_new <= LANE
    seq_blk = max(1, min(16, (2 * 1024 * 1024) // (w * D_HEADS * 4)))
    qv = q_all.reshape(n_seq, n_new, N_GROUPS * D_HEADS)
    kvv = kv_all.reshape(n_seq, n_new, 2 * N_GROUPS * D_HEADS)
    c_t = cache.transpose(0, 2, 3, 4, 1).reshape(n_seq, 2, D_HEADS, w)
    blk = (seq_blk, n_new, D_HEADS)
    o, lse = pl.pallas_call(
        functools.partial(_attn_sample_kernel, d=d, n_new=n_new, seq_blk=seq_blk),
        grid=(n_seq // seq_blk,),
        in_specs=[pl.BlockSpec(blk, lambda i: (i, 0, grp)),
                  pl.BlockSpec(blk, lambda i: (i, 0, 2 * grp)),
                  pl.BlockSpec(blk, lambda i: (i, 0, 2 * grp + 1)),
                  pl.BlockSpec((seq_blk, 2, D_HEADS, w), lambda i: (i, 0, 0, 0))],
        out_specs=[pl.BlockSpec(blk, lambda i: (i, 0, 0))] * 2,
        out_shape=[jax.ShapeDtypeStruct((n_seq, n_new, D_HEADS), F32)] * 2,
        compiler_params=_params(1),
        name=f"attn_sample_d{d}",
    )(qv, kvv, kvv, c_t)
    o, lse = o.reshape(n_seq * n_new, D_HEADS), lse.reshape(n_seq * n_new, D_HEADS)
    return o[:, :LANE], o[:, LANE:], lse[:, :LANE], lse[:, LANE:]


def _kv_tail_kernel(kv_ref, o_ref):
    o_ref[0] = kv_ref[...].T


def _kv_tail(kv_all, n, seq, grp):
    keep = min(WINDOWS[grp], seq)
    tt = min(keep, 512)
    assert keep % tt == 0 and (seq - keep) % tt == 0
    first, width = (seq - keep) // tt, 2 * D_HEADS
    out = pl.pallas_call(
        _kv_tail_kernel,
        grid=(n, keep // tt),
        in_specs=[pl.BlockSpec((tt, width), lambda i, t: (i * (seq // tt) + first + t, grp))],
        out_specs=pl.BlockSpec((1, width, tt), lambda i, t: (i, 0, t)),
        out_shape=jax.ShapeDtypeStruct((n, width, keep), F32),
        compiler_params=_params(2),
        name=f"kv_tail_w{keep}",
    )(kv_all)
    return out.reshape(1, n, 2, HEADS, HEAD_DIM, keep).transpose(0, 1, 5, 2, 3, 4)


def _gelu_tanh(x):
    return 0.5 * x * (1.0 + jnp.tanh(math.sqrt(2.0 / math.pi) * (x + 0.044715 * (x * x * x))))


def _stage3_kernel(*refs):
    x1_ref, yp_ref = refs[:2]
    attn = refs[2:2 + 4 * N_GROUPS]
    (ga_ref, gb_ref, p_ref, wglu_ref, wbs_ref, wba_ref, wout_ref, g2_ref, wg_ref, wu_ref, wd_ref,
     gp_ref, wpg_ref, wpp_ref, gf_ref, y_ref) = refs[2 + 4 * N_GROUPS:]
    y = _gelu_tanh(yp_ref[...])
    glu = y * jax.nn.sigmoid(_dot(y.astype(BF16), wglu_ref[...]))
    full = lambda lo, hi: jnp.concatenate([lo[...], hi[...]], axis=1)
    outs = [full(attn[4 * g], attn[4 * g + 1]) for g in range(N_GROUPS)]
    lses = [full(attn[4 * g + 2], attn[4 * g + 3]) for g in range(N_GROUPS)]
    mx = jnp.maximum(jnp.maximum(lses[0], lses[1]), lses[2])
    es = [jnp.exp(l - mx) for l in lses]
    y_attn = (es[0] * outs[0] + es[1] * outs[1] + es[2] * outs[2]) / (es[0] + es[1] + es[2])
    merged = (ga_ref[...] * _dot(glu.astype(BF16), wbs_ref[...])
              + gb_ref[...] * _dot(y_attn.astype(BF16), wba_ref[...]))
    x = x1_ref[...] + _dot(merged.astype(BF16), wout_ref[...])
    x = x + 0.5 * _swiglu(_rms(x, g2_ref[...]).astype(BF16), wg_ref, wu_ref, wd_ref)
    gate = jax.nn.sigmoid(_dot(_rms(x, gp_ref[...]).astype(BF16), wpg_ref[...]))
    x = x + gate * _dot(p_ref[...].astype(BF16), wpp_ref[...])
    y_ref[...] = _rms(x, gf_ref[...])


def _stage3(tok_inputs, weights):
    t, tm = tok_inputs[0].shape[0], STAGE_TM
    tok = lambda a: pl.BlockSpec((tm, a.shape[1]), lambda i: (i, 0))
    return pl.pallas_call(
        _stage3_kernel,
        grid=(t // tm,),
        in_specs=[tok(a) for a in tok_inputs] + [_const_spec(w.shape) for w in weights],
        out_specs=pl.BlockSpec((tm, D_MODEL), lambda i: (i, 0)),
        out_shape=jax.ShapeDtypeStruct((t, D_MODEL), F32),
        compiler_params=_params(1),
        name="stage3",
    )(*tok_inputs, *weights)


def kernel(x_prompt, x_sample, p_prompt, p_sample, cache_kv_w128, cache_kv_w512, cache_kv_w2048, state_ssm,
           g_ffn1, ffn1_w_gate, ffn1_w_up, ffn1_w_down, g_mix, w_in, ssm_a_re, ssm_a_im, ssm_log_dt,
           ssm_b_re, ssm_b_im, ssm_c_re, ssm_c_im, ssm_d, ssm_w_glu, w_br_ssm, w_br_attn, w_out,
           g_ffn2, ffn2_w_gate, ffn2_w_up, ffn2_w_down, g_ple, w_ple_gate, w_ple_proj, g_final):
    assert x_prompt.shape[-1] == D_MODEL and g_ffn1.shape[0] == 1
    n_p, seq, _ = x_prompt.shape
    n_s, n_new, _ = x_sample.shape
    caches = (cache_kv_w128, cache_kv_w512, cache_kv_w2048)
    row = lambda g: g.reshape(1, -1)
    bf = lambda w: w[0].astype(BF16)

    wi = w_in[0]
    qkv0 = D_SSM
    kv_cols = wi[:, qkv0 + N_GROUPS * D_HEADS:qkv0 + 3 * N_GROUPS * D_HEADS]
    kv_cols = kv_cols.reshape(D_MODEL, 2, N_GROUPS, D_HEADS).transpose(0, 2, 1, 3).reshape(D_MODEL, -1)
    win = jnp.concatenate([wi[:, :qkv0 + N_GROUPS * D_HEADS], kv_cols,
                           wi[:, qkv0 + 3 * N_GROUPS * D_HEADS:]], axis=1).astype(BF16)
    s1_w = (row(g_ffn1[0]), bf(ffn1_w_gate), bf(ffn1_w_up), bf(ffn1_w_down), row(g_mix[0]), win)
    s3_w = (bf(ssm_w_glu), bf(w_br_ssm), bf(w_br_attn), bf(w_out), row(g_ffn2[0]),
            bf(ffn2_w_gate), bf(ffn2_w_up), bf(ffn2_w_down), row(g_ple[0]), bf(w_ple_gate),
            bf(w_ple_proj), row(g_final))

    n_chunks = seq // CHUNK
    tabs_p, tabs_s = _ssm_tables(
        ssm_a_re[0], ssm_a_im[0], ssm_log_dt[0], ssm_b_re[0], ssm_b_im[0], ssm_c_re[0], ssm_c_im[0],
        ssm_d[0], int(math.log2(n_chunks)), n_new)
    tile_state = (-1, N_SSM_TILES, 2, SSM_TILE_GROUPS, SSM_STATE)
    from_tiles = lambda h: h.reshape(tile_state).transpose(0, 1, 3, 4, 2).reshape(1, -1, N_SSM_GROUPS, SSM_STATE, 2)

    x1, u, q, kv, ga, gb = _stage1(x_prompt.reshape(n_p * seq, D_MODEL), *s1_w)
    y_pre, h_last = _ssm_prompt(u, tabs_p, n_p, seq)
    attn = [a for grp in range(N_GROUPS) for a in _attn_prompt(q, kv, n_p, seq, grp)]
    tok = (x1, y_pre, *attn, ga, gb, p_prompt[0].reshape(n_p * seq, D_PLE))
    y_prompt = _stage3(tok, s3_w).reshape(n_p, seq, D_MODEL)
    kv_prompt = [_kv_tail(kv, n_p, seq, grp) for grp in range(N_GROUPS)]
    ssm_prompt = from_tiles(h_last)

    x1, u, q, kv, ga, gb = _stage1(x_sample.reshape(n_s * n_new, D_MODEL), *s1_w)
    h0 = state_ssm[0].reshape(n_s, N_SSM_TILES, SSM_TILE_GROUPS, SSM_STATE, 2)
    h0 = h0.transpose(0, 1, 4, 2, 3).reshape(n_s, N_SSM_TILES * TILE_STATE_W)
    y_pre, h_new = _ssm_sample(u, h0, tabs_s, n_s, n_new)
    attn = [a for grp in range(N_GROUPS)
            for a in _attn_sample(q, kv, caches[grp][0], n_s, n_new, grp)]
    tok = (x1, y_pre, *attn, ga, gb, p_sample[0].reshape(n_s * n_new, D_PLE))
    y_sample = _stage3(tok, s3_w).reshape(n_s, n_new, D_MODEL)
    kv_s = kv.reshape(n_s, n_new, N_GROUPS, 2, HEADS, HEAD_DIM)
    kv_sample = [kv_s[:, :, grp][None] for grp in range(N_GROUPS)]
    ssm_sample = from_tiles(h_new)

    return (y_prompt, y_sample, kv_prompt[0], kv_prompt[1], kv_prompt[2], ssm_prompt,
            kv_sample[0], kv_sample[1], kv_sample[2], ssm_sample)
```

```python
import functools
import math

import jax
import jax.numpy as jnp
import numpy as np
from jax import lax
from jax.experimental import pallas as pl
from jax.experimental.pallas import tpu as pltpu

D_MODEL = 1024
D_PLE = 256
D_FF = 2816
SSM_GROUP = 16
SSM_STATE = 64
D_SSM = 512
N_SSM_GROUPS = D_SSM // SSM_GROUP
HEAD_DIM = 64
HEADS = 4
WINDOWS = (128, 512, 2048)
DILATIONS = (1, 4, 16)
N_GROUPS = 3
D_HEADS = HEADS * HEAD_DIM
N_STEPS = 128
ATTN_SCALE = HEAD_DIM ** -0.5
EPS = 1e-6

LANE = 128
V7X_VMEM_LIMIT_BYTES = 56 * 1024 * 1024
STAGE_TM = 256

CHUNK = 16
N_PAIRS = CHUNK // 2
PAIR_W = 2 * LANE
SSM_TILE_GROUPS = LANE // SSM_GROUP
N_SSM_TILES = N_SSM_GROUPS // SSM_TILE_GROUPS
TILE_STATE_W = SSM_TILE_GROUPS * 2 * SSM_STATE
Q_BLOCK = 128
ATTN_TB = 2048

Z_U = (0, D_SSM)
Z_Q = (D_SSM, D_SSM + N_GROUPS * D_HEADS)
Z_KV = (Z_Q[1], Z_Q[1] + 2 * N_GROUPS * D_HEADS)
Z_GA = (Z_KV[1], Z_KV[1] + D_MODEL)
Z_GB = (Z_GA[1], Z_GA[1] + D_MODEL)

BF16 = jnp.bfloat16
F32 = jnp.float32


def _dot(a, b):
    return jnp.dot(a, b, preferred_element_type=F32)


def _dot_nt(a, b):
    return lax.dot_general(a, b, (((1,), (1,)), ((), ())), preferred_element_type=F32)


def _rms(x, g):
    return x * lax.rsqrt(jnp.mean(x * x, axis=-1, keepdims=True) + EPS) * g


def _swiglu(xn, wg_ref, wu_ref, wd_ref):
    gate = _dot(xn, wg_ref[...])
    up = _dot(xn, wu_ref[...])
    act = (gate * jax.nn.sigmoid(gate) * up).astype(BF16)
    return _dot(act, wd_ref[...])


def _const_spec(shape):
    nd = len(shape)
    return pl.BlockSpec(shape, lambda *_: (0,) * nd, pipeline_mode=pl.Buffered(1))


def _params(n_grid_dims):
    return pltpu.CompilerParams(dimension_semantics=("arbitrary",) * n_grid_dims,
                                vmem_limit_bytes=V7X_VMEM_LIMIT_BYTES)


def _stage1_kernel(x_ref, g1_ref, wg_ref, wu_ref, wd_ref, gm_ref, win_ref,
                   x1_ref, u_ref, q_ref, kv_ref, ga_ref, gb_ref):
    x = x_ref[...]
    xn = _rms(x, g1_ref[...]).astype(BF16)
    x1 = x + 0.5 * _swiglu(xn, wg_ref, wu_ref, wd_ref)
    x1_ref[...] = x1
    h = _rms(x1, gm_ref[...]).astype(BF16)
    z = _dot(h, win_ref[...])
    u_ref[...] = z[:, Z_U[0]:Z_U[1]]
    q_ref[...] = z[:, Z_Q[0]:Z_Q[1]]
    kv_ref[...] = z[:, Z_KV[0]:Z_KV[1]]
    ga_ref[...] = jax.nn.sigmoid(z[:, Z_GA[0]:Z_GA[1]])
    gb_ref[...] = jax.nn.sigmoid(z[:, Z_GB[0]:Z_GB[1]])


def _stage1(x, g1, wg, wu, wd, gm, win):
    t, tm = x.shape[0], STAGE_TM
    tok = lambda w: pl.BlockSpec((tm, w), lambda i: (i, 0))
    widths = (D_MODEL, D_SSM, N_GROUPS * D_HEADS, 2 * N_GROUPS * D_HEADS, D_MODEL, D_MODEL)
    return pl.pallas_call(
        _stage1_kernel,
        grid=(t // tm,),
        in_specs=[tok(D_MODEL), _const_spec(g1.shape), _const_spec(wg.shape), _const_spec(wu.shape),
                  _const_spec(wd.shape), _const_spec(gm.shape), _const_spec(win.shape)],
        out_specs=[tok(w) for w in widths],
        out_shape=[jax.ShapeDtypeStruct((t, w), F32) for w in widths],
        compiler_params=_params(1),
        name="stage1",
    )(x, g1, wg, wu, wd, gm, win)


def _swap_halves(x):
    half = x.shape[1] // 2
    return jnp.concatenate([x[:, half:], x[:, :half]], axis=1)


def _cmul_split(x, a1, a2):
    return x * a1 + _swap_halves(x) * a2


def _shift_rows(x, sh):
    if sh % 8 == 0:
        return jnp.concatenate([jnp.zeros((sh, x.shape[1]), x.dtype), x[:x.shape[0] - sh]], axis=0)
    row = lax.broadcasted_iota(jnp.int32, x.shape, 0)
    return jnp.where(row >= sh, pltpu.roll(x, sh, axis=0), 0.0)


def _ssm_kernel(*refs, n_t, n_rows, has_h0):
    if has_h0:
        u_ref, h0_ref, kp_ref, wp_ref, vpt_ref, a1_ref, a2_ref, y_ref, hl_ref = refs
    else:
        u_ref, kp_ref, wp_ref, vpt_ref, a1_ref, a2_ref, y_ref, hl_ref = refs
    n_pairs = n_t // 2
    tok = lambda t: pl.ds(t, n_rows, stride=n_t)
    up = [jnp.concatenate([u_ref[tok(2 * a), :], u_ref[tok(2 * a + 1), :]], axis=1).astype(BF16)
          for a in range(n_pairs)]
    w_row0 = (CHUNK - n_t) * LANE
    s = _dot(jnp.concatenate(up, axis=1), wp_ref[0, w_row0:, :])
    if has_h0:
        hprev = h0_ref[...]
        h_last = _cmul_split(hprev, a1_ref[0], a2_ref[0]) + s
        hl_ref[...] = h_last
    else:
        h = s
        for lvl in range(int(math.log2(n_rows))):
            h = h + _cmul_split(_shift_rows(h, 1 << lvl), a1_ref[0, lvl:lvl + 1, :], a2_ref[0, lvl:lvl + 1, :])
        hprev = _shift_rows(h, 1)
        hl_ref[0, 0] = h[n_rows - 1:n_rows, :]
    hpb = hprev.astype(BF16)
    for b in range(n_pairs):
        acc = _dot_nt(hpb, vpt_ref[0, b])
        for a in range(b + 1):
            acc = acc + _dot(up[a], kp_ref[0, b - a])
        y_ref[tok(2 * b), :] = acc[:, :LANE]
        y_ref[tok(2 * b + 1), :] = acc[:, LANE:]


def _ssm_table_specs(tabs, idx):
    kp, wp, vp, a1, a2 = tabs
    return [pl.BlockSpec((1,) + kp.shape[1:], lambda *g: (idx(*g), 0, 0, 0)),
            pl.BlockSpec((1,) + wp.shape[1:], lambda *g: (idx(*g), 0, 0)),
            pl.BlockSpec((1,) + vp.shape[1:], lambda *g: (idx(*g), 0, 0, 0)),
            pl.BlockSpec((1,) + a1.shape[1:], lambda *g: (idx(*g), 0, 0)),
            pl.BlockSpec((1,) + a2.shape[1:], lambda *g: (idx(*g), 0, 0))]


def _ssm_prompt(u, tabs, n, seq):
    n_rows = seq // CHUNK
    tile = pl.BlockSpec((seq, LANE), lambda j, i: (i, j))
    return pl.pallas_call(
        functools.partial(_ssm_kernel, n_t=CHUNK, n_rows=n_rows, has_h0=False),
        grid=(N_SSM_TILES, n),
        in_specs=[tile] + _ssm_table_specs(tabs, lambda j, i: j),
        out_specs=[tile, pl.BlockSpec((1, 1, 1, TILE_STATE_W), lambda j, i: (i, j, 0, 0))],
        out_shape=[jax.ShapeDtypeStruct((n * seq, D_SSM), F32),
                   jax.ShapeDtypeStruct((n, N_SSM_TILES, 1, TILE_STATE_W), F32)],
        compiler_params=_params(2),
        name="ssm_prompt",
    )(u, *tabs)


def _ssm_sample(u, h0, tabs, n_seq, n_new):
    tile = pl.BlockSpec((n_seq * n_new, LANE), lambda j: (0, j))
    state = pl.BlockSpec((n_seq, TILE_STATE_W), lambda j: (0, j))
    return pl.pallas_call(
        functools.partial(_ssm_kernel, n_t=n_new, n_rows=n_seq, has_h0=True),
        grid=(N_SSM_TILES,),
        in_specs=[tile, state] + _ssm_table_specs(tabs, lambda j: j),
        out_specs=[tile, state],
        out_shape=[jax.ShapeDtypeStruct((n_seq * n_new, D_SSM), F32),
                   jax.ShapeDtypeStruct((n_seq, N_SSM_TILES * TILE_STATE_W), F32)],
        compiler_params=_params(1),
        name="ssm_sample",
    )(u, h0, *tabs)


def _ssm_prep_kernel(are_ref, aim_ref, ldt_ref, bre_ref, bim_ref, cre_ref, cim_ref, d_ref,
                     kp_ref, wp_ref, vpt_ref, a1_ref, a2_ref, a1s_ref, a2s_ref, *, n_levels, n_new):
    a_re, a_im = are_ref[0], aim_ref[0]
    dt = jnp.exp(ldt_ref[0])
    mag = jnp.exp(a_re * dt)
    ab_re, ab_im = mag * jnp.cos(a_im * dt), mag * jnp.sin(a_im * dt)
    den = a_re * a_re + a_im * a_im
    nr, ni = ab_re - 1.0, ab_im
    f_re = (nr * a_re + ni * a_im) / den
    f_im = (ni * a_re - nr * a_im) / den
    shape = bre_ref.shape[1:]
    same_group = (lax.broadcasted_iota(jnp.int32, shape, 0) // SSM_GROUP
                  == lax.broadcasted_iota(jnp.int32, shape, 1) // SSM_STATE)
    b_re, b_im = bre_ref[0], bim_ref[0]
    x_re = jnp.where(same_group, f_re * b_re - f_im * b_im, 0.0)
    x_im = jnp.where(same_group, f_re * b_im + f_im * b_re, 0.0)
    c_re = jnp.where(same_group, cre_ref[0], 0.0)
    c_im = jnp.where(same_group, cim_ref[0], 0.0)
    pw = [(jnp.ones_like(ab_re), jnp.zeros_like(ab_im))]
    for _ in range(CHUNK):
        r, i = pw[-1]
        pw.append((r * ab_re - i * ab_im, r * ab_im + i * ab_re))
    cat = lambda r, i: jnp.concatenate([r, i], axis=1)
    e_pack = [cat(c_re * r - c_im * i, -(c_re * i + c_im * r)) for r, i in pw]
    x_pack = cat(x_re, x_im)
    k_lag = [lax.dot_general(x_pack, e_pack[k], (((1,), (1,)), ((), ())), precision=lax.Precision.HIGHEST,
                             preferred_element_type=F32) for k in range(CHUNK)]
    diag = (lax.broadcasted_iota(jnp.int32, (LANE, LANE), 0) == lax.broadcasted_iota(jnp.int32, (LANE, LANE), 1))
    k_lag[0] = k_lag[0] + jnp.where(diag, d_ref[0], 0.0)
    zero = jnp.zeros((LANE, LANE), F32)
    for dl in range(N_PAIRS):
        top = cat(k_lag[2 * dl], k_lag[2 * dl + 1])
        bot = cat(k_lag[2 * dl - 1] if dl > 0 else zero, k_lag[2 * dl])
        kp_ref[0, dl] = jnp.concatenate([top, bot], axis=0).astype(BF16)
    for t in range(CHUNK):
        r, i = pw[CHUNK - 1 - t]
        wp_ref[0, t * LANE:(t + 1) * LANE, :] = cat(x_re * r - x_im * i, x_re * i + x_im * r).astype(BF16)
    for t in range(CHUNK):
        vpt_ref[0, t // 2, (t % 2) * LANE:(t % 2 + 1) * LANE, :] = e_pack[t + 1].astype(BF16)
    r, i = pw[CHUNK]
    for lvl in range(n_levels):
        a1_ref[0, lvl:lvl + 1, :], a2_ref[0, lvl:lvl + 1, :] = cat(r, r), cat(-i, i)
        r, i = r * r - i * i, 2.0 * r * i
    r, i = pw[n_new]
    a1s_ref[0], a2s_ref[0] = cat(r, r), cat(-i, i)


def _ssm_tables(a_re, a_im, log_dt, b_re, b_im, c_re, c_im, d_skip, n_levels, n_new):
    nj, tg = N_SSM_TILES, SSM_TILE_GROUPS
    half = tg * SSM_STATE
    rowv = lambda v: v.reshape(nj, 1, half)
    tiled = lambda m: jnp.tile(m.reshape(nj, LANE, SSM_STATE), (1, 1, tg))
    ins = (rowv(a_re), rowv(a_im), rowv(jnp.repeat(log_dt, SSM_STATE)),
           tiled(b_re.transpose(0, 2, 1)), tiled(b_im.transpose(0, 2, 1)), tiled(c_re), tiled(c_im),
           d_skip.reshape(nj, 1, LANE))
    shapes = ((nj, N_PAIRS, PAIR_W, PAIR_W), (nj, CHUNK * LANE, TILE_STATE_W), (nj, N_PAIRS, PAIR_W, TILE_STATE_W),
              (nj, n_levels, TILE_STATE_W), (nj, n_levels, TILE_STATE_W), (nj, 1, TILE_STATE_W), (nj, 1, TILE_STATE_W))
    dtypes = (BF16, BF16, BF16, F32, F32, F32, F32)
    per_tile = lambda s: pl.BlockSpec((1,) + tuple(s[1:]), lambda j: (j,) + (0,) * (len(s) - 1))
    kp, wp, vpt, a1, a2, a1s, a2s = pl.pallas_call(
        functools.partial(_ssm_prep_kernel, n_levels=n_levels, n_new=n_new),
        grid=(nj,),
        in_specs=[per_tile(x.shape) for x in ins],
        out_specs=[per_tile(s) for s in shapes],
        out_shape=[jax.ShapeDtypeStruct(s, dt) for s, dt in zip(shapes, dtypes)],
        compiler_params=_params(1),
        name="ssm_prep",
    )(*ins)
    return (kp, wp, vpt, a1, a2), (kp, wp, vpt, a1s, a2s)


def _head_of_lane(width):
    return lax.broadcasted_iota(jnp.int32, (1, width), 1) // HEAD_DIM


def _stack_heads(q):
    head = _head_of_lane(q.shape[1])
    q = q.astype(F32)
    return jnp.concatenate([jnp.where(head == h, q, 0.0) for h in range(HEADS)], axis=0).astype(BF16)


def _unstack_heads(x, rows):
    head = _head_of_lane(x.shape[1])
    out = jnp.zeros((rows, x.shape[1]), x.dtype)
    for h in range(HEADS):
        out = out + jnp.where(head == h, x[h * rows:(h + 1) * rows], 0.0)
    return out


def _attn_prompt_kernel(bias_ref, q0, q1, kp0, kp1, kc0, kc1, vp0, vp1, vc0, vc1, o0, o1, l0, l1, *, d, m):
    qb = Q_BLOCK
    b = pl.program_id(1)

    def rows_at(start):
        return pl.ds(start, qb, stride=d) if d > 1 else pl.ds(pl.multiple_of(start, qb), qb)

    def attend(q_rows, prev_rows, prev_refs, bias):
        both = lambda lo, hi, rows: jnp.concatenate([lo[rows, :], hi[rows, :]], axis=1)
        kp_lo, kp_hi, vp_lo, vp_hi = prev_refs
        qs = _stack_heads(both(q0, q1, q_rows) * ATTN_SCALE)
        k = jnp.concatenate([both(kp_lo, kp_hi, prev_rows), both(kc0, kc1, q_rows)], axis=0).astype(BF16)
        v = jnp.concatenate([both(vp_lo, vp_hi, prev_rows), both(vc0, vc1, q_rows)], axis=0).astype(BF16)
        s = _dot_nt(qs, k) + bias
        mx = jnp.max(s, axis=-1, keepdims=True)
        pe = jnp.exp(s - mx)
        den = jnp.sum(pe, axis=-1, keepdims=True)
        o = _unstack_heads(_dot(pe.astype(BF16), v) / den, qb)
        lse = _unstack_heads(jnp.broadcast_to(mx + jnp.log(den), (HEADS * qb, D_HEADS)), qb)
        o0[q_rows, :], o1[q_rows, :] = o[:, :LANE], o[:, LANE:]
        l0[q_rows, :], l1[q_rows, :] = lse[:, :LANE], lse[:, LANE:]

    def first_block(r, carry):
        attend(rows_at(r), rows_at(r), (kp0, kp1, vp0, vp1), bias_ref[jnp.minimum(b, 1)])
        return carry

    def later_block(idx, carry):
        r, j = idx % d, idx // d + 1
        attend(rows_at(r + j * qb * d), rows_at(r + (j - 1) * qb * d), (kc0, kc1, vc0, vc1), bias_ref[1])
        return carry

    lax.fori_loop(0, d, first_block, 0, unroll=min(d, 2))
    if m > 1:
        lax.fori_loop(0, d * (m - 1), later_block, 0, unroll=2)


def _attn_bias():
    qi = np.arange(HEADS * Q_BLOCK)[:, None] % Q_BLOCK
    ki = np.arange(2 * Q_BLOCK)[None, :]
    diff = Q_BLOCK + qi - ki
    valid = (diff >= 0) & (diff <= N_STEPS)
    return jnp.asarray(np.where(np.stack([valid & (ki >= Q_BLOCK), valid]), 0.0, -np.inf), F32)


def _attn_prompt(q_all, kv_all, n, seq, grp):
    d = DILATIONS[grp]
    tb = min(ATTN_TB, seq)
    hb = Q_BLOCK * d
    assert tb % hb == 0 and seq % tb == 0
    m, nb = tb // hb, seq // tb
    cur = lambda c: pl.BlockSpec((tb, LANE), lambda i, b: (i * nb + b, c))
    prev = lambda c: pl.BlockSpec((hb, LANE), lambda i, b: (i * nb * m + jnp.maximum(b * m - 1, 0), c))
    qc, kc, vc = 2 * grp, 4 * grp, 4 * grp + 2
    bias = _attn_bias()
    return pl.pallas_call(
        functools.partial(_attn_prompt_kernel, d=d, m=m),
        grid=(n, nb),
        in_specs=[_const_spec(bias.shape), cur(qc), cur(qc + 1), prev(kc), prev(kc + 1), cur(kc), cur(kc + 1),
                  prev(vc), prev(vc + 1), cur(vc), cur(vc + 1)],
        out_specs=[pl.BlockSpec((tb, LANE), lambda i, b: (i * nb + b, 0))] * 4,
        out_shape=[jax.ShapeDtypeStruct((n * seq, LANE), F32)] * 4,
        compiler_params=_params(2),
        name=f"attn_prompt_d{d}",
    )(bias, q_all, q_all, kv_all, kv_all, kv_all, kv_all, kv_all, kv_all, kv_all, kv_all)


def _attn_sample_kernel(q_ref, kn_ref, vn_ref, c_ref, o_ref, lse_ref, *, d, n_new, seq_blk):
    w = c_ref.shape[-1]
    n_rows = HEADS * n_new
    l_b = lax.broadcasted_iota(jnp.int32, (n_rows, w), 0) % n_new
    pos = lax.broadcasted_iota(jnp.int32, (n_rows, w), 1)
    valid_buf = (((w + l_b - pos) & (d - 1)) == 0) & (pos >= l_b)
    dn = (lax.broadcasted_iota(jnp.int32, (n_rows, LANE), 0) % n_new) - lax.broadcasted_iota(jnp.int32, (n_rows, LANE), 1)
    valid_new = (dn >= 0) & ((dn & (d - 1)) == 0)
    pad = jnp.zeros((LANE - n_new, D_HEADS), F32)

    def one_seq(s, carry):
        qs = _stack_heads(q_ref[s])
        k_t, v_t = c_ref[s, 0].astype(BF16), c_ref[s, 1].astype(BF16)
        k_n = jnp.concatenate([kn_ref[s], pad], axis=0).astype(BF16)
        v_n = jnp.concatenate([vn_ref[s], pad], axis=0).astype(BF16)
        s_b = jnp.where(valid_buf, _dot(qs, k_t) * ATTN_SCALE, -jnp.inf)
        s_n = jnp.where(valid_new, _dot_nt(qs, k_n) * ATTN_SCALE, -jnp.inf)
        mx = jnp.maximum(jnp.max(s_b, axis=-1, keepdims=True), jnp.max(s_n, axis=-1, keepdims=True))
        p_b, p_n = jnp.exp(s_b - mx), jnp.exp(s_n - mx)
        den = jnp.sum(p_b, axis=-1, keepdims=True) + jnp.sum(p_n, axis=-1, keepdims=True)
        o = (_dot_nt(p_b.astype(BF16), v_t) + _dot(p_n.astype(BF16), v_n)) / den
        o_ref[s] = _unstack_heads(o, n_new)
        lse_ref[s] = _unstack_heads(jnp.broadcast_to(mx + jnp.log(den), o.shape), n_new)
        return carry

    lax.fori_loop(0, seq_blk, one_seq, 0, unroll=min(seq_blk, 2))


def _attn_sample(q_all, kv_all, cache, n_seq, n_new, grp):
    d = DILATIONS[grp]
    w = cache.shape[1]
    assert w == N_STEPS * d and n_new <= LANE
    seq_blk = max(2, min(16, (2 * 1024 * 1024) // (w * D_HEADS * 4)))
    qv = q_all.reshape(n_seq, n_new, N_GROUPS * D_HEADS)
    kvv = kv_all.reshape(n_seq, n_new, 2 * N_GROUPS * D_HEADS)
    c_t = cache.transpose(0, 2, 3, 4, 1).reshape(n_seq, 2, D_HEADS, w)
    blk = (seq_blk, n_new, D_HEADS)
    o, lse = pl.pallas_call(
        functools.partial(_attn_sample_kernel, d=d, n_new=n_new, seq_blk=seq_blk),
        grid=(n_seq // seq_blk,),
        in_specs=[pl.BlockSpec(blk, lambda i: (i, 0, grp)),
                  pl.BlockSpec(blk, lambda i: (i, 0, 2 * grp)),
                  pl.BlockSpec(blk, lambda i: (i, 0, 2 * grp + 1)),
                  pl.BlockSpec((seq_blk, 2, D_HEADS, w), lambda i: (i, 0, 0, 0))],
        out_specs=[pl.BlockSpec(blk, lambda i: (i, 0, 0))] * 2,
        out_shape=[jax.ShapeDtypeStruct((n_seq, n_new, D_HEADS), F32)] * 2,
        compiler_params=_params(1),
        name=f"attn_sample_d{d}",
    )(qv, kvv, kvv, c_t)
    o, lse = o.reshape(n_seq * n_new, D_HEADS), lse.reshape(n_seq * n_new, D_HEADS)
    return o[:, :LANE], o[:, LANE:], lse[:, :LANE], lse[:, LANE:]


def _kv_tail_kernel(kv_ref, o_ref):
    o_ref[0] = kv_ref[...].T


def _kv_tail(kv_all, n, seq, grp):
    keep = min(WINDOWS[grp], seq)
    tt = min(keep, 512)
    assert keep % tt == 0 and (seq - keep) % tt == 0
    first, width = (seq - keep) // tt, 2 * D_HEADS
    out = pl.pallas_call(
        _kv_tail_kernel,
        grid=(n, keep // tt),
        in_specs=[pl.BlockSpec((tt, width), lambda i, t: (i * (seq // tt) + first + t, grp))],
        out_specs=pl.BlockSpec((1, width, tt), lambda i, t: (i, 0, t)),
        out_shape=jax.ShapeDtypeStruct((n, width, keep), F32),
        compiler_params=_params(2),
        name=f"kv_tail_w{keep}",
    )(kv_all)
    return out.reshape(1, n, 2, HEADS, HEAD_DIM, keep).transpose(0, 1, 5, 2, 3, 4)


def _gelu_tanh(x):
    return 0.5 * x * (1.0 + jnp.tanh(math.sqrt(2.0 / math.pi) * (x + 0.044715 * (x * x * x))))


def _stage3_kernel(*refs):
    x1_ref, yp_ref = refs[:2]
    attn = refs[2:2 + 4 * N_GROUPS]
    (ga_ref, gb_ref, p_ref, wglu_ref, wbs_ref, wba_ref, wout_ref, g2_ref, wg_ref, wu_ref, wd_ref,
     gp_ref, wpg_ref, wpp_ref, gf_ref, y_ref) = refs[2 + 4 * N_GROUPS:]
    y = _gelu_tanh(yp_ref[...])
    glu = y * jax.nn.sigmoid(_dot(y.astype(BF16), wglu_ref[...]))
    full = lambda lo, hi: jnp.concatenate([lo[...], hi[...]], axis=1)
    outs = [full(attn[4 * g], attn[4 * g + 1]) for g in range(N_GROUPS)]
    lses = [full(attn[4 * g + 2], attn[4 * g + 3]) for g in range(N_GROUPS)]
    mx = jnp.maximum(jnp.maximum(lses[0], lses[1]), lses[2])
    es = [jnp.exp(l - mx) for l in lses]
    y_attn = (es[0] * outs[0] + es[1] * outs[1] + es[2] * outs[2]) / (es[0] + es[1] + es[2])
    merged = (ga_ref[...] * _dot(glu.astype(BF16), wbs_ref[...])
              + gb_ref[...] * _dot(y_attn.astype(BF16), wba_ref[...]))
    x = x1_ref[...] + _dot(merged.astype(BF16), wout_ref[...])
    x = x + 0.5 * _swiglu(_rms(x, g2_ref[...]).astype(BF16), wg_ref, wu_ref, wd_ref)
    gate = jax.nn.sigmoid(_dot(_rms(x, gp_ref[...]).astype(BF16), wpg_ref[...]))
    x = x + gate * _dot(p_ref[...].astype(BF16), wpp_ref[...])
    y_ref[...] = _rms(x, gf_ref[...])


def _stage3(tok_inputs, weights):
    t, tm = tok_inputs[0].shape[0], STAGE_TM
    tok = lambda a: pl.BlockSpec((tm, a.shape[1]), lambda i: (i, 0))
    return pl.pallas_call(
        _stage3_kernel,
        grid=(t // tm,),
        in_specs=[tok(a) for a in tok_inputs] + [_const_spec(w.shape) for w in weights],
        out_specs=pl.BlockSpec((tm, D_MODEL), lambda i: (i, 0)),
        out_shape=jax.ShapeDtypeStruct((t, D_MODEL), F32),
        compiler_params=_params(1),
        name="stage3",
    )(*tok_inputs, *weights)


def kernel(x_prompt, x_sample, p_prompt, p_sample, cache_kv_w128, cache_kv_w512, cache_kv_w2048, state_ssm,
           g_ffn1, ffn1_w_gate, ffn1_w_up, ffn1_w_down, g_mix, w_in, ssm_a_re, ssm_a_im, ssm_log_dt,
           ssm_b_re, ssm_b_im, ssm_c_re, ssm_c_im, ssm_d, ssm_w_glu, w_br_ssm, w_br_attn, w_out,
           g_ffn2, ffn2_w_gate, ffn2_w_up, ffn2_w_down, g_ple, w_ple_gate, w_ple_proj, g_final):
    assert x_prompt.shape[-1] == D_MODEL and g_ffn1.shape[0] == 1
    n_p, seq, _ = x_prompt.shape
    n_s, n_new, _ = x_sample.shape
    caches = (cache_kv_w128, cache_kv_w512, cache_kv_w2048)
    row = lambda g: g.reshape(1, -1)
    bf = lambda w: w[0].astype(BF16)

    wi = w_in[0]
    qkv0 = D_SSM
    kv_cols = wi[:, qkv0 + N_GROUPS * D_HEADS:qkv0 + 3 * N_GROUPS * D_HEADS]
    kv_cols = kv_cols.reshape(D_MODEL, 2, N_GROUPS, D_HEADS).transpose(0, 2, 1, 3).reshape(D_MODEL, -1)
    win = jnp.concatenate([wi[:, :qkv0 + N_GROUPS * D_HEADS], kv_cols,
                           wi[:, qkv0 + 3 * N_GROUPS * D_HEADS:]], axis=1).astype(BF16)
    s1_w = (row(g_ffn1[0]), bf(ffn1_w_gate), bf(ffn1_w_up), bf(ffn1_w_down), row(g_mix[0]), win)
    s3_w = (bf(ssm_w_glu), bf(w_br_ssm), bf(w_br_attn), bf(w_out), row(g_ffn2[0]),
            bf(ffn2_w_gate), bf(ffn2_w_up), bf(ffn2_w_down), row(g_ple[0]), bf(w_ple_gate),
            bf(w_ple_proj), row(g_final))

    n_chunks = seq // CHUNK
    tabs_p, tabs_s = _ssm_tables(
        ssm_a_re[0], ssm_a_im[0], ssm_log_dt[0], ssm_b_re[0], ssm_b_im[0], ssm_c_re[0], ssm_c_im[0],
        ssm_d[0], int(math.log2(n_chunks)), n_new)
    tile_state = (-1, N_SSM_TILES, 2, SSM_TILE_GROUPS, SSM_STATE)
    from_tiles = lambda h: h.reshape(tile_state).transpose(0, 1, 3, 4, 2).reshape(1, -1, N_SSM_GROUPS, SSM_STATE, 2)

    x1, u, q, kv, ga, gb = _stage1(x_prompt.reshape(n_p * seq, D_MODEL), *s1_w)
    y_pre, h_last = _ssm_prompt(u, tabs_p, n_p, seq)
    attn = [a for grp in range(N_GROUPS) for a in _attn_prompt(q, kv, n_p, seq, grp)]
    tok = (x1, y_pre, *attn, ga, gb, p_prompt[0].reshape(n_p * seq, D_PLE))
    y_prompt = _stage3(tok, s3_w).reshape(n_p, seq, D_MODEL)
    kv_prompt = [_kv_tail(kv, n_p, seq, grp) for grp in range(N_GROUPS)]
    ssm_prompt = from_tiles(h_last)

    x1, u, q, kv, ga, gb = _stage1(x_sample.reshape(n_s * n_new, D_MODEL), *s1_w)
    h0 = state_ssm[0].reshape(n_s, N_SSM_TILES, SSM_TILE_GROUPS, SSM_STATE, 2)
    h0 = h0.transpose(0, 1, 4, 2, 3).reshape(n_s, N_SSM_TILES * TILE_STATE_W)
    y_pre, h_new = _ssm_sample(u, h0, tabs_s, n_s, n_new)
    attn = [a for grp in range(N_GROUPS)
            for a in _attn_sample(q, kv, caches[grp][0], n_s, n_new, grp)]
    tok = (x1, y_pre, *attn, ga, gb, p_sample[0].reshape(n_s * n_new, D_PLE))
    y_sample = _stage3(tok, s3_w).reshape(n_s, n_new, D_MODEL)
    kv_s = kv.reshape(n_s, n_new, N_GROUPS, 2, HEADS, HEAD_DIM)
    kv_sample = [kv_s[:, :, grp][None] for grp in range(N_GROUPS)]
    ssm_sample = from_tiles(h_new)

    return (y_prompt, y_sample, kv_prompt[0], kv_prompt[1], kv_prompt[2], ssm_prompt,
            kv_sample[0], kv_sample[1], kv_sample[2], ssm_sample)
```

```python
import functools
import math

import jax
import jax.numpy as jnp
import numpy as np
from jax import lax
from jax.experimental import pallas as pl
from jax.experimental.pallas import tpu as pltpu

D_MODEL = 1024
D_PLE = 256
D_FF = 2816
SSM_GROUP = 16
SSM_STATE = 64
D_SSM = 512
N_SSM_GROUPS = D_SSM // SSM_GROUP
HEAD_DIM = 64
HEADS = 4
WINDOWS = (128, 512, 2048)
DILATIONS = (1, 4, 16)
N_GROUPS = 3
D_HEADS = HEADS * HEAD_DIM
N_STEPS = 128
ATTN_SCALE = HEAD_DIM ** -0.5
EPS = 1e-6

LANE = 128
V7X_VMEM_LIMIT_BYTES = 56 * 1024 * 1024
STAGE_TM = 512

CHUNK = 16
N_PAIRS = CHUNK // 2
PAIR_W = 2 * LANE
SSM_TILE_GROUPS = LANE // SSM_GROUP
N_SSM_TILES = N_SSM_GROUPS // SSM_TILE_GROUPS
TILE_STATE_W = SSM_TILE_GROUPS * 2 * SSM_STATE
Q_BLOCK = 128
ATTN_TB = 2048

Z_U = (0, D_SSM)
Z_Q = (D_SSM, D_SSM + N_GROUPS * D_HEADS)
Z_KV = (Z_Q[1], Z_Q[1] + 2 * N_GROUPS * D_HEADS)

BF16 = jnp.bfloat16
F32 = jnp.float32


def _dot(a, b):
    return jnp.dot(a, b, preferred_element_type=F32)


def _dot_nt(a, b):
    return lax.dot_general(a, b, (((1,), (1,)), ((), ())), preferred_element_type=F32)


def _rms(x, g):
    return x * lax.rsqrt(jnp.mean(x * x, axis=-1, keepdims=True) + EPS) * g


def _swiglu(xn, wg_ref, wu_ref, wd_ref):
    gate = _dot(xn, wg_ref[...])
    up = _dot(xn, wu_ref[...])
    act = (gate * jax.nn.sigmoid(gate) * up).astype(BF16)
    return _dot(act, wd_ref[...])


def _const_spec(shape):
    nd = len(shape)
    return pl.BlockSpec(shape, lambda *_: (0,) * nd, pipeline_mode=pl.Buffered(1))


def _params(n_grid_dims):
    return pltpu.CompilerParams(dimension_semantics=("arbitrary",) * n_grid_dims,
                                vmem_limit_bytes=V7X_VMEM_LIMIT_BYTES)


def _stage1_kernel(x_ref, g1_ref, wg_ref, wu_ref, wd_ref, gm_ref, win_ref,
                   x1_ref, u_ref, q_ref, kv_ref):
    x = x_ref[...]
    xn = _rms(x, g1_ref[...]).astype(BF16)
    x1 = x + 0.5 * _swiglu(xn, wg_ref, wu_ref, wd_ref)
    x1_ref[...] = x1
    h = _rms(x1, gm_ref[...]).astype(BF16)
    z = _dot(h, win_ref[...])
    u_ref[...] = z[:, Z_U[0]:Z_U[1]]
    q_ref[...] = z[:, Z_Q[0]:Z_Q[1]]
    kv_ref[...] = z[:, Z_KV[0]:Z_KV[1]]


def _stage1(x, g1, wg, wu, wd, gm, win):
    t, tm = x.shape[0], STAGE_TM
    tok = lambda w: pl.BlockSpec((tm, w), lambda i: (i, 0))
    widths = (D_MODEL, D_SSM, N_GROUPS * D_HEADS, 2 * N_GROUPS * D_HEADS)
    return pl.pallas_call(
        _stage1_kernel,
        grid=(t // tm,),
        in_specs=[tok(D_MODEL), _const_spec(g1.shape), _const_spec(wg.shape), _const_spec(wu.shape),
                  _const_spec(wd.shape), _const_spec(gm.shape), _const_spec(win.shape)],
        out_specs=[tok(w) for w in widths],
        out_shape=[jax.ShapeDtypeStruct((t, w), F32) for w in widths],
        compiler_params=_params(1),
        name="stage1",
    )(x, g1, wg, wu, wd, gm, win)


def _swap_halves(x):
    half = x.shape[1] // 2
    return jnp.concatenate([x[:, half:], x[:, :half]], axis=1)


def _cmul_split(x, a1, a2):
    return x * a1 + _swap_halves(x) * a2


def _shift_rows(x, sh):
    if sh % 8 == 0:
        return jnp.concatenate([jnp.zeros((sh, x.shape[1]), x.dtype), x[:x.shape[0] - sh]], axis=0)
    row = lax.broadcasted_iota(jnp.int32, x.shape, 0)
    return jnp.where(row >= sh, pltpu.roll(x, sh, axis=0), 0.0)


def _ssm_kernel(*refs, n_t, n_rows, has_h0):
    if has_h0:
        u_ref, h0_ref, kp_ref, wp_ref, vpt_ref, a1_ref, a2_ref, y_ref, hl_ref = refs
    else:
        u_ref, kp_ref, wp_ref, vpt_ref, a1_ref, a2_ref, y_ref, hl_ref = refs
    n_pairs = n_t // 2
    tok = lambda t: pl.ds(t, n_rows, stride=n_t)
    up = [jnp.concatenate([u_ref[tok(2 * a), :], u_ref[tok(2 * a + 1), :]], axis=1).astype(BF16)
          for a in range(n_pairs)]
    w_row0 = (CHUNK - n_t) * LANE
    s = _dot(jnp.concatenate(up, axis=1), wp_ref[0, w_row0:, :])
    if has_h0:
        hprev = h0_ref[...]
        h_last = _cmul_split(hprev, a1_ref[0], a2_ref[0]) + s
        hl_ref[...] = h_last
    else:
        h = s
        for lvl in range(int(math.log2(n_rows))):
            h = h + _cmul_split(_shift_rows(h, 1 << lvl), a1_ref[0, lvl:lvl + 1, :], a2_ref[0, lvl:lvl + 1, :])
        hprev = _shift_rows(h, 1)
        hl_ref[0, 0] = h[n_rows - 1:n_rows, :]
    hpb = hprev.astype(BF16)
    for b in range(n_pairs):
        acc = _dot_nt(hpb, vpt_ref[0, b])
        for a in range(b + 1):
            acc = acc + _dot(up[a], kp_ref[0, b - a])
        y_ref[tok(2 * b), :] = acc[:, :LANE]
        y_ref[tok(2 * b + 1), :] = acc[:, LANE:]


def _ssm_table_specs(tabs, idx):
    kp, wp, vp, a1, a2 = tabs
    return [pl.BlockSpec((1,) + kp.shape[1:], lambda *g: (idx(*g), 0, 0, 0)),
            pl.BlockSpec((1,) + wp.shape[1:], lambda *g: (idx(*g), 0, 0)),
            pl.BlockSpec((1,) + vp.shape[1:], lambda *g: (idx(*g), 0, 0, 0)),
            pl.BlockSpec((1,) + a1.shape[1:], lambda *g: (idx(*g), 0, 0)),
            pl.BlockSpec((1,) + a2.shape[1:], lambda *g: (idx(*g), 0, 0))]


def _ssm_prompt(u, tabs, n, seq):
    n_rows = seq // CHUNK
    tile = pl.BlockSpec((seq, LANE), lambda j, i: (i, j))
    return pl.pallas_call(
        functools.partial(_ssm_kernel, n_t=CHUNK, n_rows=n_rows, has_h0=False),
        grid=(N_SSM_TILES, n),
        in_specs=[tile] + _ssm_table_specs(tabs, lambda j, i: j),
        out_specs=[tile, pl.BlockSpec((1, 1, 1, TILE_STATE_W), lambda j, i: (i, j, 0, 0))],
        out_shape=[jax.ShapeDtypeStruct((n * seq, D_SSM), F32),
                   jax.ShapeDtypeStruct((n, N_SSM_TILES, 1, TILE_STATE_W), F32)],
        compiler_params=_params(2),
        name="ssm_prompt",
    )(u, *tabs)


def _ssm_sample(u, h0, tabs, n_seq, n_new):
    tile = pl.BlockSpec((n_seq * n_new, LANE), lambda j: (0, j))
    state = pl.BlockSpec((n_seq, TILE_STATE_W), lambda j: (0, j))
    return pl.pallas_call(
        functools.partial(_ssm_kernel, n_t=n_new, n_rows=n_seq, has_h0=True),
        grid=(N_SSM_TILES,),
        in_specs=[tile, state] + _ssm_table_specs(tabs, lambda j: j),
        out_specs=[tile, state],
        out_shape=[jax.ShapeDtypeStruct((n_seq * n_new, D_SSM), F32),
                   jax.ShapeDtypeStruct((n_seq, N_SSM_TILES * TILE_STATE_W), F32)],
        compiler_params=_params(1),
        name="ssm_sample",
    )(u, h0, *tabs)


def _ssm_prep_kernel(are_ref, aim_ref, ldt_ref, bre_ref, bim_ref, cre_ref, cim_ref, d_ref,
                     kp_ref, wp_ref, vpt_ref, a1_ref, a2_ref, a1s_ref, a2s_ref, *, n_levels, n_new):
    a_re, a_im = are_ref[0], aim_ref[0]
    dt = jnp.exp(ldt_ref[0])
    mag = jnp.exp(a_re * dt)
    ab_re, ab_im = mag * jnp.cos(a_im * dt), mag * jnp.sin(a_im * dt)
    den = a_re * a_re + a_im * a_im
    nr, ni = ab_re - 1.0, ab_im
    f_re = (nr * a_re + ni * a_im) / den
    f_im = (ni * a_re - nr * a_im) / den
    shape = bre_ref.shape[1:]
    same_group = (lax.broadcasted_iota(jnp.int32, shape, 0) // SSM_GROUP
                  == lax.broadcasted_iota(jnp.int32, shape, 1) // SSM_STATE)
    b_re, b_im = bre_ref[0], bim_ref[0]
    x_re = jnp.where(same_group, f_re * b_re - f_im * b_im, 0.0)
    x_im = jnp.where(same_group, f_re * b_im + f_im * b_re, 0.0)
    c_re = jnp.where(same_group, cre_ref[0], 0.0)
    c_im = jnp.where(same_group, cim_ref[0], 0.0)
    pw = [(jnp.ones_like(ab_re), jnp.zeros_like(ab_im))]
    for _ in range(CHUNK):
        r, i = pw[-1]
        pw.append((r * ab_re - i * ab_im, r * ab_im + i * ab_re))
    cat = lambda r, i: jnp.concatenate([r, i], axis=1)
    e_pack = [cat(c_re * r - c_im * i, -(c_re * i + c_im * r)) for r, i in pw]
    x_pack = cat(x_re, x_im)
    k_lag = [lax.dot_general(x_pack, e_pack[k], (((1,), (1,)), ((), ())), precision=lax.Precision.HIGHEST,
                             preferred_element_type=F32) for k in range(CHUNK)]
    diag = (lax.broadcasted_iota(jnp.int32, (LANE, LANE), 0) == lax.broadcasted_iota(jnp.int32, (LANE, LANE), 1))
    k_lag[0] = k_lag[0] + jnp.where(diag, d_ref[0], 0.0)
    zero = jnp.zeros((LANE, LANE), F32)
    for dl in range(N_PAIRS):
        top = cat(k_lag[2 * dl], k_lag[2 * dl + 1])
        bot = cat(k_lag[2 * dl - 1] if dl > 0 else zero, k_lag[2 * dl])
        kp_ref[0, dl] = jnp.concatenate([top, bot], axis=0).astype(BF16)
    for t in range(CHUNK):
        r, i = pw[CHUNK - 1 - t]
        wp_ref[0, t * LANE:(t + 1) * LANE, :] = cat(x_re * r - x_im * i, x_re * i + x_im * r).astype(BF16)
    for t in range(CHUNK):
        vpt_ref[0, t // 2, (t % 2) * LANE:(t % 2 + 1) * LANE, :] = e_pack[t + 1].astype(BF16)
    r, i = pw[CHUNK]
    for lvl in range(n_levels):
        a1_ref[0, lvl:lvl + 1, :], a2_ref[0, lvl:lvl + 1, :] = cat(r, r), cat(-i, i)
        r, i = r * r - i * i, 2.0 * r * i
    r, i = pw[n_new]
    a1s_ref[0], a2s_ref[0] = cat(r, r), cat(-i, i)


def _ssm_tables(a_re, a_im, log_dt, b_re, b_im, c_re, c_im, d_skip, n_levels, n_new):
    nj, tg = N_SSM_TILES, SSM_TILE_GROUPS
    half = tg * SSM_STATE
    rowv = lambda v: v.reshape(nj, 1, half)
    tiled = lambda m: jnp.tile(m.reshape(nj, LANE, SSM_STATE), (1, 1, tg))
    ins = (rowv(a_re), rowv(a_im), rowv(jnp.repeat(log_dt, SSM_STATE)),
           tiled(b_re.transpose(0, 2, 1)), tiled(b_im.transpose(0, 2, 1)), tiled(c_re), tiled(c_im),
           d_skip.reshape(nj, 1, LANE))
    shapes = ((nj, N_PAIRS, PAIR_W, PAIR_W), (nj, CHUNK * LANE, TILE_STATE_W), (nj, N_PAIRS, PAIR_W, TILE_STATE_W),
              (nj, n_levels, TILE_STATE_W), (nj, n_levels, TILE_STATE_W), (nj, 1, TILE_STATE_W), (nj, 1, TILE_STATE_W))
    dtypes = (BF16, BF16, BF16, F32, F32, F32, F32)
    per_tile = lambda s: pl.BlockSpec((1,) + tuple(s[1:]), lambda j: (j,) + (0,) * (len(s) - 1))
    kp, wp, vpt, a1, a2, a1s, a2s = pl.pallas_call(
        functools.partial(_ssm_prep_kernel, n_levels=n_levels, n_new=n_new),
        grid=(nj,),
        in_specs=[per_tile(x.shape) for x in ins],
        out_specs=[per_tile(s) for s in shapes],
        out_shape=[jax.ShapeDtypeStruct(s, dt) for s, dt in zip(shapes, dtypes)],
        compiler_params=_params(1),
        name="ssm_prep",
    )(*ins)
    return (kp, wp, vpt, a1, a2), (kp, wp, vpt, a1s, a2s)


def _head_of_lane(width):
    return lax.broadcasted_iota(jnp.int32, (1, width), 1) // HEAD_DIM


def _stack_heads(q):
    head = _head_of_lane(q.shape[1])
    q = q.astype(F32)
    return jnp.concatenate([jnp.where(head == h, q, 0.0) for h in range(HEADS)], axis=0).astype(BF16)


def _unstack_heads(x, rows):
    head = _head_of_lane(x.shape[1])
    out = jnp.zeros((rows, x.shape[1]), x.dtype)
    for h in range(HEADS):
        out = out + jnp.where(head == h, x[h * rows:(h + 1) * rows], 0.0)
    return out


def _attn_prompt_kernel(bias_ref, q0, q1, kp0, kp1, kc0, kc1, vp0, vp1, vc0, vc1, o0, o1, l0, l1, *, d, m):
    qb = Q_BLOCK
    b = pl.program_id(1)

    def rows_at(start):
        return pl.ds(start, qb, stride=d) if d > 1 else pl.ds(pl.multiple_of(start, qb), qb)

    def attend(q_rows, prev_rows, prev_refs, bias):
        both = lambda lo, hi, rows: jnp.concatenate([lo[rows, :], hi[rows, :]], axis=1)
        kp_lo, kp_hi, vp_lo, vp_hi = prev_refs
        qs = _stack_heads(both(q0, q1, q_rows) * ATTN_SCALE)
        k = jnp.concatenate([both(kp_lo, kp_hi, prev_rows), both(kc0, kc1, q_rows)], axis=0).astype(BF16)
        v = jnp.concatenate([both(vp_lo, vp_hi, prev_rows), both(vc0, vc1, q_rows)], axis=0).astype(BF16)
        s = _dot_nt(qs, k) + bias
        mx = jnp.max(s, axis=-1, keepdims=True)
        pe = jnp.exp(s - mx)
        den = jnp.sum(pe, axis=-1, keepdims=True)
        o = _unstack_heads(_dot(pe.astype(BF16), v) / den, qb)
        lse = _unstack_heads(jnp.broadcast_to(mx + jnp.log(den), (HEADS * qb, D_HEADS)), qb)
        o0[q_rows, :], o1[q_rows, :] = o[:, :LANE], o[:, LANE:]
        l0[q_rows, :], l1[q_rows, :] = lse[:, :LANE], lse[:, LANE:]

    def first_block(r, carry):
        attend(rows_at(r), rows_at(r), (kp0, kp1, vp0, vp1), bias_ref[jnp.minimum(b, 1)])
        return carry

    def later_block(idx, carry):
        r, j = idx % d, idx // d + 1
        attend(rows_at(r + j * qb * d), rows_at(r + (j - 1) * qb * d), (kc0, kc1, vc0, vc1), bias_ref[1])
        return carry

    lax.fori_loop(0, d, first_block, 0, unroll=min(d, 2))
    if m > 1:
        lax.fori_loop(0, d * (m - 1), later_block, 0, unroll=2)


def _attn_bias():
    qi = np.arange(HEADS * Q_BLOCK)[:, None] % Q_BLOCK
    ki = np.arange(2 * Q_BLOCK)[None, :]
    diff = Q_BLOCK + qi - ki
    valid = (diff >= 0) & (diff <= N_STEPS)
    return jnp.asarray(np.where(np.stack([valid & (ki >= Q_BLOCK), valid]), 0.0, -np.inf), F32)


def _attn_prompt(q_all, kv_all, n, seq, grp):
    d = DILATIONS[grp]
    tb = min(ATTN_TB, seq)
    hb = Q_BLOCK * d
    assert tb % hb == 0 and seq % tb == 0
    m, nb = tb // hb, seq // tb
    cur = lambda c: pl.BlockSpec((tb, LANE), lambda i, b: (i * nb + b, c))
    prev = lambda c: pl.BlockSpec((hb, LANE), lambda i, b: (i * nb * m + jnp.maximum(b * m - 1, 0), c))
    qc, kc, vc = 2 * grp, 4 * grp, 4 * grp + 2
    bias = _attn_bias()
    return pl.pallas_call(
        functools.partial(_attn_prompt_kernel, d=d, m=m),
        grid=(n, nb),
        in_specs=[_const_spec(bias.shape), cur(qc), cur(qc + 1), prev(kc), prev(kc + 1), cur(kc), cur(kc + 1),
                  prev(vc), prev(vc + 1), cur(vc), cur(vc + 1)],
        out_specs=[pl.BlockSpec((tb, LANE), lambda i, b: (i * nb + b, 0))] * 4,
        out_shape=[jax.ShapeDtypeStruct((n * seq, LANE), F32)] * 4,
        compiler_params=_params(2),
        name=f"attn_prompt_d{d}",
    )(bias, q_all, q_all, kv_all, kv_all, kv_all, kv_all, kv_all, kv_all, kv_all, kv_all)


def _attn_sample_kernel(q_ref, kn_ref, vn_ref, c_ref, o_ref, lse_ref, *, d, n_new, seq_blk):
    w = c_ref.shape[-1]
    n_rows = HEADS * n_new
    l_b = lax.broadcasted_iota(jnp.int32, (n_rows, w), 0) % n_new
    pos = lax.broadcasted_iota(jnp.int32, (n_rows, w), 1)
    valid_buf = (((w + l_b - pos) & (d - 1)) == 0) & (pos >= l_b)
    dn = (lax.broadcasted_iota(jnp.int32, (n_rows, LANE), 0) % n_new) - lax.broadcasted_iota(jnp.int32, (n_rows, LANE), 1)
    valid_new = (dn >= 0) & ((dn & (d - 1)) == 0)
    pad = jnp.zeros((LANE - n_new, D_HEADS), F32)

    def one_seq(s, carry):
        qs = _stack_heads(q_ref[s])
        k_t, v_t = c_ref[s, 0].astype(BF16), c_ref[s, 1].astype(BF16)
        k_n = jnp.concatenate([kn_ref[s], pad], axis=0).astype(BF16)
        v_n = jnp.concatenate([vn_ref[s], pad], axis=0).astype(BF16)
        s_b = jnp.where(valid_buf, _dot(qs, k_t) * ATTN_SCALE, -jnp.inf)
        s_n = jnp.where(valid_new, _dot_nt(qs, k_n) * ATTN_SCALE, -jnp.inf)
        mx = jnp.maximum(jnp.max(s_b, axis=-1, keepdims=True), jnp.max(s_n, axis=-1, keepdims=True))
        p_b, p_n = jnp.exp(s_b - mx), jnp.exp(s_n - mx)
        den = jnp.sum(p_b, axis=-1, keepdims=True) + jnp.sum(p_n, axis=-1, keepdims=True)
        o = (_dot_nt(p_b.astype(BF16), v_t) + _dot(p_n.astype(BF16), v_n)) / den
        o_ref[s] = _unstack_heads(o, n_new)
        lse_ref[s] = _unstack_heads(jnp.broadcast_to(mx + jnp.log(den), o.shape), n_new)
        return carry

    lax.fori_loop(0, seq_blk, one_seq, 0, unroll=min(seq_blk, 2))


def _attn_sample(q_all, kv_all, cache, n_seq, n_new, grp):
    d = DILATIONS[grp]
    w = cache.shape[1]
    assert w == N_STEPS * d and n_new <= LANE
    seq_blk = max(2, min(16, (2 * 1024 * 1024) // (w * D_HEADS * 4)))
    qv = q_all.reshape(n_seq, n_new, N_GROUPS * D_HEADS)
    kvv = kv_all.reshape(n_seq, n_new, 2 * N_GROUPS * D_HEADS)
    c_t = cache.transpose(0, 2, 3, 4, 1).reshape(n_seq, 2, D_HEADS, w)
    blk = (seq_blk, n_new, D_HEADS)
    o, lse = pl.pallas_call(
        functools.partial(_attn_sample_kernel, d=d, n_new=n_new, seq_blk=seq_blk),
        grid=(n_seq // seq_blk,),
        in_specs=[pl.BlockSpec(blk, lambda i: (i, 0, grp)),
                  pl.BlockSpec(blk, lambda i: (i, 0, 2 * grp)),
                  pl.BlockSpec(blk, lambda i: (i, 0, 2 * grp + 1)),
                  pl.BlockSpec((seq_blk, 2, D_HEADS, w), lambda i: (i, 0, 0, 0))],
        out_specs=[pl.BlockSpec(blk, lambda i: (i, 0, 0))] * 2,
        out_shape=[jax.ShapeDtypeStruct((n_seq, n_new, D_HEADS), F32)] * 2,
        compiler_params=_params(1),
        name=f"attn_sample_d{d}",
    )(qv, kvv, kvv, c_t)
    o, lse = o.reshape(n_seq * n_new, D_HEADS), lse.reshape(n_seq * n_new, D_HEADS)
    return o[:, :LANE], o[:, LANE:], lse[:, :LANE], lse[:, LANE:]


def _kv_tail_kernel(kv_ref, o_ref):
    o_ref[0] = kv_ref[...].T


def _kv_tail(kv_all, n, seq, grp):
    keep = min(WINDOWS[grp], seq)
    tt = min(keep, 512)
    assert keep % tt == 0 and (seq - keep) % tt == 0
    first, width = (seq - keep) // tt, 2 * D_HEADS
    out = pl.pallas_call(
        _kv_tail_kernel,
        grid=(n, keep // tt),
        in_specs=[pl.BlockSpec((tt, width), lambda i, t: (i * (seq // tt) + first + t, grp))],
        out_specs=pl.BlockSpec((1, width, tt), lambda i, t: (i, 0, t)),
        out_shape=jax.ShapeDtypeStruct((n, width, keep), F32),
        compiler_params=_params(2),
        name=f"kv_tail_w{keep}",
    )(kv_all)
    return out.reshape(1, n, 2, HEADS, HEAD_DIM, keep).transpose(0, 1, 5, 2, 3, 4)


def _gelu_tanh(x):
    return 0.5 * x * (1.0 + jnp.tanh(math.sqrt(2.0 / math.pi) * (x + 0.044715 * (x * x * x))))


def _stage3_kernel(*refs):
    x1_ref, yp_ref = refs[:2]
    attn = refs[2:2 + 4 * N_GROUPS]
    (p_ref, gm_ref, wgate_ref, wglu_ref, wbs_ref, wba_ref, wout_ref, g2_ref, wg_ref, wu_ref, wd_ref,
     gp_ref, wpg_ref, wpp_ref, gf_ref, y_ref) = refs[2 + 4 * N_GROUPS:]
    x1 = x1_ref[...]
    gates = jax.nn.sigmoid(_dot(_rms(x1, gm_ref[...]).astype(BF16), wgate_ref[...]))
    y = _gelu_tanh(yp_ref[...])
    glu = y * jax.nn.sigmoid(_dot(y.astype(BF16), wglu_ref[...]))
    full = lambda lo, hi: jnp.concatenate([lo[...], hi[...]], axis=1)
    outs = [full(attn[4 * g], attn[4 * g + 1]) for g in range(N_GROUPS)]
    lses = [full(attn[4 * g + 2], attn[4 * g + 3]) for g in range(N_GROUPS)]
    mx = jnp.maximum(jnp.maximum(lses[0], lses[1]), lses[2])
    es = [jnp.exp(l - mx) for l in lses]
    y_attn = (es[0] * outs[0] + es[1] * outs[1] + es[2] * outs[2]) / (es[0] + es[1] + es[2])
    merged = (gates[:, :D_MODEL] * _dot(glu.astype(BF16), wbs_ref[...])
              + gates[:, D_MODEL:] * _dot(y_attn.astype(BF16), wba_ref[...]))
    x = x1 + _dot(merged.astype(BF16), wout_ref[...])
    x = x + 0.5 * _swiglu(_rms(x, g2_ref[...]).astype(BF16), wg_ref, wu_ref, wd_ref)
    gate = jax.nn.sigmoid(_dot(_rms(x, gp_ref[...]).astype(BF16), wpg_ref[...]))
    x = x + gate * _dot(p_ref[...].astype(BF16), wpp_ref[...])
    y_ref[...] = _rms(x, gf_ref[...])


def _stage3(tok_inputs, weights):
    t, tm = tok_inputs[0].shape[0], STAGE_TM
    tok = lambda a: pl.BlockSpec((tm, a.shape[1]), lambda i: (i, 0))
    return pl.pallas_call(
        _stage3_kernel,
        grid=(t // tm,),
        in_specs=[tok(a) for a in tok_inputs] + [_const_spec(w.shape) for w in weights],
        out_specs=pl.BlockSpec((tm, D_MODEL), lambda i: (i, 0)),
        out_shape=jax.ShapeDtypeStruct((t, D_MODEL), F32),
        compiler_params=_params(1),
        name="stage3",
    )(*tok_inputs, *weights)


def kernel(x_prompt, x_sample, p_prompt, p_sample, cache_kv_w128, cache_kv_w512, cache_kv_w2048, state_ssm,
           g_ffn1, ffn1_w_gate, ffn1_w_up, ffn1_w_down, g_mix, w_in, ssm_a_re, ssm_a_im, ssm_log_dt,
           ssm_b_re, ssm_b_im, ssm_c_re, ssm_c_im, ssm_d, ssm_w_glu, w_br_ssm, w_br_attn, w_out,
           g_ffn2, ffn2_w_gate, ffn2_w_up, ffn2_w_down, g_ple, w_ple_gate, w_ple_proj, g_final):
    assert x_prompt.shape[-1] == D_MODEL and g_ffn1.shape[0] == 1
    n_p, seq, _ = x_prompt.shape
    n_s, n_new, _ = x_sample.shape
    caches = (cache_kv_w128, cache_kv_w512, cache_kv_w2048)
    row = lambda g: g.reshape(1, -1)
    bf = lambda w: w[0].astype(BF16)

    wi = w_in[0]
    qkv0 = D_SSM
    kv_cols = wi[:, qkv0 + N_GROUPS * D_HEADS:qkv0 + 3 * N_GROUPS * D_HEADS]
    kv_cols = kv_cols.reshape(D_MODEL, 2, N_GROUPS, D_HEADS).transpose(0, 2, 1, 3).reshape(D_MODEL, -1)
    win = jnp.concatenate([wi[:, :qkv0 + N_GROUPS * D_HEADS], kv_cols], axis=1).astype(BF16)
    w_gates = wi[:, qkv0 + 3 * N_GROUPS * D_HEADS:].astype(BF16)
    s1_w = (row(g_ffn1[0]), bf(ffn1_w_gate), bf(ffn1_w_up), bf(ffn1_w_down), row(g_mix[0]), win)
    s3_w = (row(g_mix[0]), w_gates, bf(ssm_w_glu), bf(w_br_ssm), bf(w_br_attn), bf(w_out), row(g_ffn2[0]),
            bf(ffn2_w_gate), bf(ffn2_w_up), bf(ffn2_w_down), row(g_ple[0]), bf(w_ple_gate),
            bf(w_ple_proj), row(g_final))

    n_chunks = seq // CHUNK
    tabs_p, tabs_s = _ssm_tables(
        ssm_a_re[0], ssm_a_im[0], ssm_log_dt[0], ssm_b_re[0], ssm_b_im[0], ssm_c_re[0], ssm_c_im[0],
        ssm_d[0], int(math.log2(n_chunks)), n_new)
    tile_state = (-1, N_SSM_TILES, 2, SSM_TILE_GROUPS, SSM_STATE)
    from_tiles = lambda h: h.reshape(tile_state).transpose(0, 1, 3, 4, 2).reshape(1, -1, N_SSM_GROUPS, SSM_STATE, 2)

    x1, u, q, kv = _stage1(x_prompt.reshape(n_p * seq, D_MODEL), *s1_w)
    y_pre, h_last = _ssm_prompt(u, tabs_p, n_p, seq)
    attn = [a for grp in range(N_GROUPS) for a in _attn_prompt(q, kv, n_p, seq, grp)]
    tok = (x1, y_pre, *attn, p_prompt[0].reshape(n_p * seq, D_PLE))
    y_prompt = _stage3(tok, s3_w).reshape(n_p, seq, D_MODEL)
    kv_prompt = [_kv_tail(kv, n_p, seq, grp) for grp in range(N_GROUPS)]
    ssm_prompt = from_tiles(h_last)

    x1, u, q, kv = _stage1(x_sample.reshape(n_s * n_new, D_MODEL), *s1_w)
    h0 = state_ssm[0].reshape(n_s, N_SSM_TILES, SSM_TILE_GROUPS, SSM_STATE, 2)
    h0 = h0.transpose(0, 1, 4, 2, 3).reshape(n_s, N_SSM_TILES * TILE_STATE_W)
    y_pre, h_new = _ssm_sample(u, h0, tabs_s, n_s, n_new)
    attn = [a for grp in range(N_GROUPS)
            for a in _attn_sample(q, kv, caches[grp][0], n_s, n_new, grp)]
    tok = (x1, y_pre, *attn, p_sample[0].reshape(n_s * n_new, D_PLE))
    y_sample = _stage3(tok, s3_w).reshape(n_s, n_new, D_MODEL)
    kv_s = kv.reshape(n_s, n_new, N_GROUPS, 2, HEADS, HEAD_DIM)
    kv_sample = [kv_s[:, :, grp][None] for grp in range(N_GROUPS)]
    ssm_sample = from_tiles(h_new)

    return (y_prompt, y_sample, kv_prompt[0], kv_prompt[1], kv_prompt[2], ssm_prompt,
            kv_sample[0], kv_sample[1], kv_sample[2], ssm_sample)
```

```python
import functools
import math

import jax
import jax.numpy as jnp
import numpy as np
from jax import lax
from jax.experimental import pallas as pl
from jax.experimental.pallas import tpu as pltpu

D_MODEL = 1024
D_PLE = 256
D_FF = 2816
SSM_GROUP = 16
SSM_STATE = 64
D_SSM = 512
N_SSM_GROUPS = D_SSM // SSM_GROUP
HEAD_DIM = 64
HEADS = 4
WINDOWS = (128, 512, 2048)
DILATIONS = (1, 4, 16)
N_GROUPS = 3
D_HEADS = HEADS * HEAD_DIM
N_STEPS = 128
ATTN_SCALE = HEAD_DIM ** -0.5
EPS = 1e-6

LANE = 128
V7X_VMEM_LIMIT_BYTES = 56 * 1024 * 1024
STAGE_TM = 512

CHUNK = 16
N_PAIRS = CHUNK // 2
PAIR_W = 2 * LANE
SSM_TILE_GROUPS = LANE // SSM_GROUP
N_SSM_TILES = N_SSM_GROUPS // SSM_TILE_GROUPS
TILE_STATE_W = SSM_TILE_GROUPS * 2 * SSM_STATE
Q_BLOCK = 128
ATTN_TB = 2048

D_QKV = 3 * N_GROUPS * D_HEADS


def _qkv_col(which, grp):
    return which * N_GROUPS + grp


BF16 = jnp.bfloat16
F32 = jnp.float32


def _dot(a, b):
    return jnp.dot(a, b, preferred_element_type=F32)


def _dot_nt(a, b):
    return lax.dot_general(a, b, (((1,), (1,)), ((), ())), preferred_element_type=F32)


def _rms(x, g):
    return x * lax.rsqrt(jnp.mean(x * x, axis=-1, keepdims=True) + EPS) * g


def _swiglu(xn, wg_ref, wu_ref, wd_ref):
    gate = _dot(xn, wg_ref[...])
    up = _dot(xn, wu_ref[...])
    act = (gate * jax.nn.sigmoid(gate) * up).astype(BF16)
    return _dot(act, wd_ref[...])


def _const_spec(shape):
    nd = len(shape)
    return pl.BlockSpec(shape, lambda *_: (0,) * nd, pipeline_mode=pl.Buffered(1))


def _params(n_grid_dims):
    return pltpu.CompilerParams(dimension_semantics=("arbitrary",) * n_grid_dims,
                                vmem_limit_bytes=V7X_VMEM_LIMIT_BYTES)


def _stage1_kernel(x_ref, g1_ref, wg_ref, wu_ref, wd_ref, gm_ref, win_ref,
                   x1_ref, u_ref, qkv_ref):
    x = x_ref[...]
    xn = _rms(x, g1_ref[...]).astype(BF16)
    x1 = x + 0.5 * _swiglu(xn, wg_ref, wu_ref, wd_ref)
    x1_ref[...] = x1
    h = _rms(x1, gm_ref[...]).astype(BF16)
    z = _dot(h, win_ref[...])
    u_ref[...] = z[:, :D_SSM]
    qkv_ref[...] = z[:, D_SSM:]


def _stage1(x, g1, wg, wu, wd, gm, win):
    t, tm = x.shape[0], STAGE_TM
    tok = lambda w: pl.BlockSpec((tm, w), lambda i: (i, 0))
    widths = (D_MODEL, D_SSM, D_QKV)
    return pl.pallas_call(
        _stage1_kernel,
        grid=(t // tm,),
        in_specs=[tok(D_MODEL), _const_spec(g1.shape), _const_spec(wg.shape), _const_spec(wu.shape),
                  _const_spec(wd.shape), _const_spec(gm.shape), _const_spec(win.shape)],
        out_specs=[tok(w) for w in widths],
        out_shape=[jax.ShapeDtypeStruct((t, w), F32) for w in widths],
        compiler_params=_params(1),
        name="stage1",
    )(x, g1, wg, wu, wd, gm, win)


def _swap_halves(x):
    half = x.shape[1] // 2
    return jnp.concatenate([x[:, half:], x[:, :half]], axis=1)


def _cmul_split(x, a1, a2):
    return x * a1 + _swap_halves(x) * a2


def _shift_rows(x, sh):
    if sh % 8 == 0:
        return jnp.concatenate([jnp.zeros((sh, x.shape[1]), x.dtype), x[:x.shape[0] - sh]], axis=0)
    row = lax.broadcasted_iota(jnp.int32, x.shape, 0)
    return jnp.where(row >= sh, pltpu.roll(x, sh, axis=0), 0.0)


def _ssm_kernel(*refs, n_t, n_rows, has_h0):
    if has_h0:
        u_ref, h0_ref, kp_ref, wp_ref, vpt_ref, a1_ref, a2_ref, y_ref, hl_ref = refs
    else:
        u_ref, kp_ref, wp_ref, vpt_ref, a1_ref, a2_ref, y_ref, hl_ref = refs
    n_pairs = n_t // 2
    tok = lambda t: pl.ds(t, n_rows, stride=n_t)
    up = [jnp.concatenate([u_ref[tok(2 * a), :], u_ref[tok(2 * a + 1), :]], axis=1).astype(BF16)
          for a in range(n_pairs)]
    w_row0 = (CHUNK - n_t) * LANE
    s = _dot(jnp.concatenate(up, axis=1), wp_ref[0, w_row0:, :])
    if has_h0:
        hprev = h0_ref[...]
        h_last = _cmul_split(hprev, a1_ref[0], a2_ref[0]) + s
        hl_ref[...] = h_last
    else:
        h = s
        for lvl in range(int(math.log2(n_rows))):
            h = h + _cmul_split(_shift_rows(h, 1 << lvl), a1_ref[0, lvl:lvl + 1, :], a2_ref[0, lvl:lvl + 1, :])
        hprev = _shift_rows(h, 1)
        hl_ref[0, 0] = h[n_rows - 1:n_rows, :]
    hpb = hprev.astype(BF16)
    for b in range(n_pairs):
        acc = _dot_nt(hpb, vpt_ref[0, b])
        for a in range(b + 1):
            acc = acc + _dot(up[a], kp_ref[0, b - a])
        y_ref[tok(2 * b), :] = acc[:, :LANE]
        y_ref[tok(2 * b + 1), :] = acc[:, LANE:]


def _ssm_table_specs(tabs, idx):
    kp, wp, vp, a1, a2 = tabs
    return [pl.BlockSpec((1,) + kp.shape[1:], lambda *g: (idx(*g), 0, 0, 0)),
            pl.BlockSpec((1,) + wp.shape[1:], lambda *g: (idx(*g), 0, 0)),
            pl.BlockSpec((1,) + vp.shape[1:], lambda *g: (idx(*g), 0, 0, 0)),
            pl.BlockSpec((1,) + a1.shape[1:], lambda *g: (idx(*g), 0, 0)),
            pl.BlockSpec((1,) + a2.shape[1:], lambda *g: (idx(*g), 0, 0))]


def _ssm_prompt(u, tabs, n, seq):
    n_rows = seq // CHUNK
    tile = pl.BlockSpec((seq, LANE), lambda j, i: (i, j))
    return pl.pallas_call(
        functools.partial(_ssm_kernel, n_t=CHUNK, n_rows=n_rows, has_h0=False),
        grid=(N_SSM_TILES, n),
        in_specs=[tile] + _ssm_table_specs(tabs, lambda j, i: j),
        out_specs=[tile, pl.BlockSpec((1, 1, 1, TILE_STATE_W), lambda j, i: (i, j, 0, 0))],
        out_shape=[jax.ShapeDtypeStruct((n * seq, D_SSM), F32),
                   jax.ShapeDtypeStruct((n, N_SSM_TILES, 1, TILE_STATE_W), F32)],
        compiler_params=_params(2),
        name="ssm_prompt",
    )(u, *tabs)


def _ssm_sample(u, h0, tabs, n_seq, n_new):
    tile = pl.BlockSpec((n_seq * n_new, LANE), lambda j: (0, j))
    state = pl.BlockSpec((n_seq, TILE_STATE_W), lambda j: (0, j))
    return pl.pallas_call(
        functools.partial(_ssm_kernel, n_t=n_new, n_rows=n_seq, has_h0=True),
        grid=(N_SSM_TILES,),
        in_specs=[tile, state] + _ssm_table_specs(tabs, lambda j: j),
        out_specs=[tile, state],
        out_shape=[jax.ShapeDtypeStruct((n_seq * n_new, D_SSM), F32),
                   jax.ShapeDtypeStruct((n_seq, N_SSM_TILES * TILE_STATE_W), F32)],
        compiler_params=_params(1),
        name="ssm_sample",
    )(u, h0, *tabs)


def _ssm_prep_kernel(are_ref, aim_ref, ldt_ref, bre_ref, bim_ref, cre_ref, cim_ref, d_ref,
                     kp_ref, wp_ref, vpt_ref, a1_ref, a2_ref, a1s_ref, a2s_ref, *, n_levels, n_new):
    a_re, a_im = are_ref[0], aim_ref[0]
    dt = jnp.exp(ldt_ref[0])
    mag = jnp.exp(a_re * dt)
    ab_re, ab_im = mag * jnp.cos(a_im * dt), mag * jnp.sin(a_im * dt)
    den = a_re * a_re + a_im * a_im
    nr, ni = ab_re - 1.0, ab_im
    f_re = (nr * a_re + ni * a_im) / den
    f_im = (ni * a_re - nr * a_im) / den
    shape = bre_ref.shape[1:]
    same_group = (lax.broadcasted_iota(jnp.int32, shape, 0) // SSM_GROUP
                  == lax.broadcasted_iota(jnp.int32, shape, 1) // SSM_STATE)
    b_re, b_im = bre_ref[0], bim_ref[0]
    x_re = jnp.where(same_group, f_re * b_re - f_im * b_im, 0.0)
    x_im = jnp.where(same_group, f_re * b_im + f_im * b_re, 0.0)
    c_re = jnp.where(same_group, cre_ref[0], 0.0)
    c_im = jnp.where(same_group, cim_ref[0], 0.0)
    pw = [(jnp.ones_like(ab_re), jnp.zeros_like(ab_im))]
    for _ in range(CHUNK):
        r, i = pw[-1]
        pw.append((r * ab_re - i * ab_im, r * ab_im + i * ab_re))
    cat = lambda r, i: jnp.concatenate([r, i], axis=1)
    e_pack = [cat(c_re * r - c_im * i, -(c_re * i + c_im * r)) for r, i in pw]
    x_pack = cat(x_re, x_im)
    k_lag = [lax.dot_general(x_pack, e_pack[k], (((1,), (1,)), ((), ())), precision=lax.Precision.HIGHEST,
                             preferred_element_type=F32) for k in range(CHUNK)]
    diag = (lax.broadcasted_iota(jnp.int32, (LANE, LANE), 0) == lax.broadcasted_iota(jnp.int32, (LANE, LANE), 1))
    k_lag[0] = k_lag[0] + jnp.where(diag, d_ref[0], 0.0)
    zero = jnp.zeros((LANE, LANE), F32)
    for dl in range(N_PAIRS):
        top = cat(k_lag[2 * dl], k_lag[2 * dl + 1])
        bot = cat(k_lag[2 * dl - 1] if dl > 0 else zero, k_lag[2 * dl])
        kp_ref[0, dl] = jnp.concatenate([top, bot], axis=0).astype(BF16)
    for t in range(CHUNK):
        r, i = pw[CHUNK - 1 - t]
        wp_ref[0, t * LANE:(t + 1) * LANE, :] = cat(x_re * r - x_im * i, x_re * i + x_im * r).astype(BF16)
    for t in range(CHUNK):
        vpt_ref[0, t // 2, (t % 2) * LANE:(t % 2 + 1) * LANE, :] = e_pack[t + 1].astype(BF16)
    r, i = pw[CHUNK]
    for lvl in range(n_levels):
        a1_ref[0, lvl:lvl + 1, :], a2_ref[0, lvl:lvl + 1, :] = cat(r, r), cat(-i, i)
        r, i = r * r - i * i, 2.0 * r * i
    r, i = pw[n_new]
    a1s_ref[0], a2s_ref[0] = cat(r, r), cat(-i, i)


def _ssm_tables(a_re, a_im, log_dt, b_re, b_im, c_re, c_im, d_skip, n_levels, n_new):
    nj, tg = N_SSM_TILES, SSM_TILE_GROUPS
    half = tg * SSM_STATE
    rowv = lambda v: v.reshape(nj, 1, half)
    tiled = lambda m: jnp.tile(m.reshape(nj, LANE, SSM_STATE), (1, 1, tg))
    ins = (rowv(a_re), rowv(a_im), rowv(jnp.repeat(log_dt, SSM_STATE)),
           tiled(b_re.transpose(0, 2, 1)), tiled(b_im.transpose(0, 2, 1)), tiled(c_re), tiled(c_im),
           d_skip.reshape(nj, 1, LANE))
    shapes = ((nj, N_PAIRS, PAIR_W, PAIR_W), (nj, CHUNK * LANE, TILE_STATE_W), (nj, N_PAIRS, PAIR_W, TILE_STATE_W),
              (nj, n_levels, TILE_STATE_W), (nj, n_levels, TILE_STATE_W), (nj, 1, TILE_STATE_W), (nj, 1, TILE_STATE_W))
    dtypes = (BF16, BF16, BF16, F32, F32, F32, F32)
    per_tile = lambda s: pl.BlockSpec((1,) + tuple(s[1:]), lambda j: (j,) + (0,) * (len(s) - 1))
    kp, wp, vpt, a1, a2, a1s, a2s = pl.pallas_call(
        functools.partial(_ssm_prep_kernel, n_levels=n_levels, n_new=n_new),
        grid=(nj,),
        in_specs=[per_tile(x.shape) for x in ins],
        out_specs=[per_tile(s) for s in shapes],
        out_shape=[jax.ShapeDtypeStruct(s, dt) for s, dt in zip(shapes, dtypes)],
        compiler_params=_params(1),
        name="ssm_prep",
    )(*ins)
    return (kp, wp, vpt, a1, a2), (kp, wp, vpt, a1s, a2s)


def _head_of_lane(width):
    return lax.broadcasted_iota(jnp.int32, (1, width), 1) // HEAD_DIM


def _stack_heads(q):
    head = _head_of_lane(q.shape[1])
    q = q.astype(F32)
    return jnp.concatenate([jnp.where(head == h, q, 0.0) for h in range(HEADS)], axis=0).astype(BF16)


def _unstack_heads(x, rows):
    head = _head_of_lane(x.shape[1])
    out = jnp.zeros((rows, x.shape[1]), x.dtype)
    for h in range(HEADS):
        out = out + jnp.where(head == h, x[h * rows:(h + 1) * rows], 0.0)
    return out


def _attn_prompt_kernel(bias_ref, q0, q1, kp0, kp1, kc0, kc1, vp0, vp1, vc0, vc1, o0, o1, l0, l1, *, d, m):
    qb = Q_BLOCK
    b = pl.program_id(1)

    def rows_at(start):
        return pl.ds(start, qb, stride=d) if d > 1 else pl.ds(pl.multiple_of(start, qb), qb)

    def attend(q_rows, prev_rows, prev_refs, bias):
        both = lambda lo, hi, rows: jnp.concatenate([lo[rows, :], hi[rows, :]], axis=1)
        kp_lo, kp_hi, vp_lo, vp_hi = prev_refs
        qs = _stack_heads(both(q0, q1, q_rows) * ATTN_SCALE)
        k = jnp.concatenate([both(kp_lo, kp_hi, prev_rows), both(kc0, kc1, q_rows)], axis=0).astype(BF16)
        v = jnp.concatenate([both(vp_lo, vp_hi, prev_rows), both(vc0, vc1, q_rows)], axis=0).astype(BF16)
        s = _dot_nt(qs, k) + bias
        mx = jnp.max(s, axis=-1, keepdims=True)
        pe = jnp.exp(s - mx)
        den = jnp.sum(pe, axis=-1, keepdims=True)
        o = _unstack_heads(_dot(pe.astype(BF16), v) / den, qb)
        lse = _unstack_heads(jnp.broadcast_to(mx + jnp.log(den), (HEADS * qb, D_HEADS)), qb)
        o0[q_rows, :], o1[q_rows, :] = o[:, :LANE], o[:, LANE:]
        l0[q_rows, :], l1[q_rows, :] = lse[:, :LANE], lse[:, LANE:]

    def first_block(r, carry):
        attend(rows_at(r), rows_at(r), (kp0, kp1, vp0, vp1), bias_ref[jnp.minimum(b, 1)])
        return carry

    def later_block(idx, carry):
        r, j = idx % d, idx // d + 1
        attend(rows_at(r + j * qb * d), rows_at(r + (j - 1) * qb * d), (kc0, kc1, vc0, vc1), bias_ref[1])
        return carry

    lax.fori_loop(0, d, first_block, 0, unroll=min(d, 2))
    if m > 1:
        lax.fori_loop(0, d * (m - 1), later_block, 0, unroll=2)


def _attn_bias():
    qi = np.arange(HEADS * Q_BLOCK)[:, None] % Q_BLOCK
    ki = np.arange(2 * Q_BLOCK)[None, :]
    diff = Q_BLOCK + qi - ki
    valid = (diff >= 0) & (diff <= N_STEPS)
    return jnp.asarray(np.where(np.stack([valid & (ki >= Q_BLOCK), valid]), 0.0, -np.inf), F32)


def _attn_prompt(qkv, n, seq, grp):
    d = DILATIONS[grp]
    tb = min(ATTN_TB, seq)
    hb = Q_BLOCK * d
    assert tb % hb == 0 and seq % tb == 0
    m, nb = tb // hb, seq // tb
    cur = lambda c: pl.BlockSpec((tb, LANE), lambda i, b: (i * nb + b, c))
    prev = lambda c: pl.BlockSpec((hb, LANE), lambda i, b: (i * nb * m + jnp.maximum(b * m - 1, 0), c))
    qc, kc, vc = (2 * _qkv_col(which, grp) for which in range(3))
    bias = _attn_bias()
    return pl.pallas_call(
        functools.partial(_attn_prompt_kernel, d=d, m=m),
        grid=(n, nb),
        in_specs=[_const_spec(bias.shape), cur(qc), cur(qc + 1), prev(kc), prev(kc + 1), cur(kc), cur(kc + 1),
                  prev(vc), prev(vc + 1), cur(vc), cur(vc + 1)],
        out_specs=[pl.BlockSpec((tb, LANE), lambda i, b: (i * nb + b, 0))] * 4,
        out_shape=[jax.ShapeDtypeStruct((n * seq, LANE), F32)] * 4,
        compiler_params=_params(2),
        name=f"attn_prompt_d{d}",
    )(bias, *([qkv] * 10))


def _attn_sample_kernel(q_ref, kn_ref, vn_ref, c_ref, o_ref, lse_ref, *, d, n_new, seq_blk):
    w = c_ref.shape[-1]
    n_rows = HEADS * n_new
    l_b = lax.broadcasted_iota(jnp.int32, (n_rows, w), 0) % n_new
    pos = lax.broadcasted_iota(jnp.int32, (n_rows, w), 1)
    valid_buf = (((w + l_b - pos) & (d - 1)) == 0) & (pos >= l_b)
    dn = (lax.broadcasted_iota(jnp.int32, (n_rows, LANE), 0) % n_new) - lax.broadcasted_iota(jnp.int32, (n_rows, LANE), 1)
    valid_new = (dn >= 0) & ((dn & (d - 1)) == 0)
    pad = jnp.zeros((LANE - n_new, D_HEADS), F32)

    def one_seq(s, carry):
        qs = _stack_heads(q_ref[s])
        k_t, v_t = c_ref[s, 0].astype(BF16), c_ref[s, 1].astype(BF16)
        k_n = jnp.concatenate([kn_ref[s], pad], axis=0).astype(BF16)
        v_n = jnp.concatenate([vn_ref[s], pad], axis=0).astype(BF16)
        s_b = jnp.where(valid_buf, _dot(qs, k_t) * ATTN_SCALE, -jnp.inf)
        s_n = jnp.where(valid_new, _dot_nt(qs, k_n) * ATTN_SCALE, -jnp.inf)
        mx = jnp.maximum(jnp.max(s_b, axis=-1, keepdims=True), jnp.max(s_n, axis=-1, keepdims=True))
        p_b, p_n = jnp.exp(s_b - mx), jnp.exp(s_n - mx)
        den = jnp.sum(p_b, axis=-1, keepdims=True) + jnp.sum(p_n, axis=-1, keepdims=True)
        o = (_dot_nt(p_b.astype(BF16), v_t) + _dot(p_n.astype(BF16), v_n)) / den
        o_ref[s] = _unstack_heads(o, n_new)
        lse_ref[s] = _unstack_heads(jnp.broadcast_to(mx + jnp.log(den), o.shape), n_new)
        return carry

    lax.fori_loop(0, seq_blk, one_seq, 0, unroll=min(seq_blk, 2))


def _attn_sample(qkv, cache, n_seq, n_new, grp):
    d = DILATIONS[grp]
    w = cache.shape[1]
    assert w == N_STEPS * d and n_new <= LANE
    seq_blk = max(2, min(16, (2 * 1024 * 1024) // (w * D_HEADS * 4)))
    qkv3 = qkv.reshape(n_seq, n_new, D_QKV)
    c_t = cache.transpose(0, 2, 3, 4, 1).reshape(n_seq, 2, D_HEADS, w)
    blk = (seq_blk, n_new, D_HEADS)
    o, lse = pl.pallas_call(
        functools.partial(_attn_sample_kernel, d=d, n_new=n_new, seq_blk=seq_blk),
        grid=(n_seq // seq_blk,),
        in_specs=[pl.BlockSpec(blk, lambda i: (i, 0, _qkv_col(0, grp))),
                  pl.BlockSpec(blk, lambda i: (i, 0, _qkv_col(1, grp))),
                  pl.BlockSpec(blk, lambda i: (i, 0, _qkv_col(2, grp))),
                  pl.BlockSpec((seq_blk, 2, D_HEADS, w), lambda i: (i, 0, 0, 0))],
        out_specs=[pl.BlockSpec(blk, lambda i: (i, 0, 0))] * 2,
        out_shape=[jax.ShapeDtypeStruct((n_seq, n_new, D_HEADS), F32)] * 2,
        compiler_params=_params(1),
        name=f"attn_sample_d{d}",
    )(qkv3, qkv3, qkv3, c_t)
    o, lse = o.reshape(n_seq * n_new, D_HEADS), lse.reshape(n_seq * n_new, D_HEADS)
    return o[:, :LANE], o[:, LANE:], lse[:, :LANE], lse[:, LANE:]


def _kv_tail_kernel(k_ref, v_ref, o_ref):
    o_ref[0, 0] = k_ref[...].T
    o_ref[0, 1] = v_ref[...].T


def _kv_tail(qkv, n, seq, grp):
    keep = min(WINDOWS[grp], seq)
    tt = min(keep, 512)
    assert keep % tt == 0 and (seq - keep) % tt == 0
    first = (seq - keep) // tt
    rows = lambda which: pl.BlockSpec((tt, D_HEADS), lambda i, t: (i * (seq // tt) + first + t, _qkv_col(which, grp)))
    out = pl.pallas_call(
        _kv_tail_kernel,
        grid=(n, keep // tt),
        in_specs=[rows(1), rows(2)],
        out_specs=pl.BlockSpec((1, 2, D_HEADS, tt), lambda i, t: (i, 0, 0, t)),
        out_shape=jax.ShapeDtypeStruct((n, 2, D_HEADS, keep), F32),
        compiler_params=_params(2),
        name=f"kv_tail_w{keep}",
    )(qkv, qkv)
    return out.reshape(1, n, 2, HEADS, HEAD_DIM, keep).transpose(0, 1, 5, 2, 3, 4)


def _kv_new_kernel(x_ref, o_ref, *, n_new, n_seq):
    for l in range(n_new):
        o_ref[0, l, 0] = x_ref[pl.ds(l, n_seq, stride=n_new), :].T


def _kv_new(qkv, n_seq, n_new):
    halves = D_HEADS // LANE
    out = pl.pallas_call(
        functools.partial(_kv_new_kernel, n_new=n_new, n_seq=n_seq),
        grid=(N_GROUPS, 2, halves),
        in_specs=[pl.BlockSpec((n_seq * n_new, LANE), lambda g, kv, c: (0, halves * _qkv_col(1 + kv, g) + c))],
        out_specs=pl.BlockSpec((1, n_new, 1, LANE, n_seq), lambda g, kv, c: (g, 0, kv, c, 0)),
        out_shape=jax.ShapeDtypeStruct((N_GROUPS, n_new, 2, D_HEADS, n_seq), F32),
        compiler_params=_params(3),
        name="kv_new",
    )(qkv)
    out = out.reshape(N_GROUPS, 1, n_new, 2, HEADS, HEAD_DIM, n_seq).transpose(0, 1, 6, 2, 3, 4, 5)
    return [out[g] for g in range(N_GROUPS)]


def _gelu_tanh(x):
    return 0.5 * x * (1.0 + jnp.tanh(math.sqrt(2.0 / math.pi) * (x + 0.044715 * (x * x * x))))


def _stage3_kernel(*refs):
    x1_ref, yp_ref = refs[:2]
    attn = refs[2:2 + 4 * N_GROUPS]
    (p_ref, gm_ref, wgate_ref, wglu_ref, wbs_ref, wba_ref, wout_ref, g2_ref, wg_ref, wu_ref, wd_ref,
     gp_ref, wpg_ref, wpp_ref, gf_ref, y_ref) = refs[2 + 4 * N_GROUPS:]
    x1 = x1_ref[...]
    gates = jax.nn.sigmoid(_dot(_rms(x1, gm_ref[...]).astype(BF16), wgate_ref[...]))
    y = _gelu_tanh(yp_ref[...])
    glu = y * jax.nn.sigmoid(_dot(y.astype(BF16), wglu_ref[...]))
    full = lambda lo, hi: jnp.concatenate([lo[...], hi[...]], axis=1)
    outs = [full(attn[4 * g], attn[4 * g + 1]) for g in range(N_GROUPS)]
    lses = [full(attn[4 * g + 2], attn[4 * g + 3]) for g in range(N_GROUPS)]
    mx = jnp.maximum(jnp.maximum(lses[0], lses[1]), lses[2])
    es = [jnp.exp(l - mx) for l in lses]
    y_attn = (es[0] * outs[0] + es[1] * outs[1] + es[2] * outs[2]) / (es[0] + es[1] + es[2])
    merged = (gates[:, :D_MODEL] * _dot(glu.astype(BF16), wbs_ref[...])
              + gates[:, D_MODEL:] * _dot(y_attn.astype(BF16), wba_ref[...]))
    x = x1 + _dot(merged.astype(BF16), wout_ref[...])
    x = x + 0.5 * _swiglu(_rms(x, g2_ref[...]).astype(BF16), wg_ref, wu_ref, wd_ref)
    gate = jax.nn.sigmoid(_dot(_rms(x, gp_ref[...]).astype(BF16), wpg_ref[...]))
    x = x + gate * _dot(p_ref[...].astype(BF16), wpp_ref[...])
    y_ref[...] = _rms(x, gf_ref[...])


def _stage3(tok_inputs, weights):
    t, tm = tok_inputs[0].shape[0], STAGE_TM
    tok = lambda a: pl.BlockSpec((tm, a.shape[1]), lambda i: (i, 0))
    return pl.pallas_call(
        _stage3_kernel,
        grid=(t // tm,),
        in_specs=[tok(a) for a in tok_inputs] + [_const_spec(w.shape) for w in weights],
        out_specs=pl.BlockSpec((tm, D_MODEL), lambda i: (i, 0)),
        out_shape=jax.ShapeDtypeStruct((t, D_MODEL), F32),
        compiler_params=_params(1),
        name="stage3",
    )(*tok_inputs, *weights)


def kernel(x_prompt, x_sample, p_prompt, p_sample, cache_kv_w128, cache_kv_w512, cache_kv_w2048, state_ssm,
           g_ffn1, ffn1_w_gate, ffn1_w_up, ffn1_w_down, g_mix, w_in, ssm_a_re, ssm_a_im, ssm_log_dt,
           ssm_b_re, ssm_b_im, ssm_c_re, ssm_c_im, ssm_d, ssm_w_glu, w_br_ssm, w_br_attn, w_out,
           g_ffn2, ffn2_w_gate, ffn2_w_up, ffn2_w_down, g_ple, w_ple_gate, w_ple_proj, g_final):
    assert x_prompt.shape[-1] == D_MODEL and g_ffn1.shape[0] == 1
    n_p, seq, _ = x_prompt.shape
    n_s, n_new, _ = x_sample.shape
    caches = (cache_kv_w128, cache_kv_w512, cache_kv_w2048)
    row = lambda g: g.reshape(1, -1)
    bf = lambda w: w[0].astype(BF16)

    wi = w_in[0]
    win = wi[:, :D_SSM + D_QKV].astype(BF16)
    w_gates = wi[:, D_SSM + D_QKV:].astype(BF16)
    s1_w = (row(g_ffn1[0]), bf(ffn1_w_gate), bf(ffn1_w_up), bf(ffn1_w_down), row(g_mix[0]), win)
    s3_w = (row(g_mix[0]), w_gates, bf(ssm_w_glu), bf(w_br_ssm), bf(w_br_attn), bf(w_out), row(g_ffn2[0]),
            bf(ffn2_w_gate), bf(ffn2_w_up), bf(ffn2_w_down), row(g_ple[0]), bf(w_ple_gate),
            bf(w_ple_proj), row(g_final))

    n_chunks = seq // CHUNK
    tabs_p, tabs_s = _ssm_tables(
        ssm_a_re[0], ssm_a_im[0], ssm_log_dt[0], ssm_b_re[0], ssm_b_im[0], ssm_c_re[0], ssm_c_im[0],
        ssm_d[0], int(math.log2(n_chunks)), n_new)
    tile_state = (-1, N_SSM_TILES, 2, SSM_TILE_GROUPS, SSM_STATE)
    from_tiles = lambda h: h.reshape(tile_state).transpose(0, 1, 3, 4, 2).reshape(1, -1, N_SSM_GROUPS, SSM_STATE, 2)

    x1, u, qkv = _stage1(x_prompt.reshape(n_p * seq, D_MODEL), *s1_w)
    y_pre, h_last = _ssm_prompt(u, tabs_p, n_p, seq)
    attn = [a for grp in range(N_GROUPS) for a in _attn_prompt(qkv, n_p, seq, grp)]
    tok = (x1, y_pre, *attn, p_prompt[0].reshape(n_p * seq, D_PLE))
    y_prompt = _stage3(tok, s3_w).reshape(n_p, seq, D_MODEL)
    kv_prompt = [_kv_tail(qkv, n_p, seq, grp) for grp in range(N_GROUPS)]
    ssm_prompt = from_tiles(h_last)

    x1, u, qkv = _stage1(x_sample.reshape(n_s * n_new, D_MODEL), *s1_w)
    h0 = state_ssm[0].reshape(n_s, N_SSM_TILES, SSM_TILE_GROUPS, SSM_STATE, 2)
    h0 = h0.transpose(0, 1, 4, 2, 3).reshape(n_s, N_SSM_TILES * TILE_STATE_W)
    y_pre, h_new = _ssm_sample(u, h0, tabs_s, n_s, n_new)
    attn = [a for grp in range(N_GROUPS)
            for a in _attn_sample(qkv, caches[grp][0], n_s, n_new, grp)]
    tok = (x1, y_pre, *attn, p_sample[0].reshape(n_s * n_new, D_PLE))
    y_sample = _stage3(tok, s3_w).reshape(n_s, n_new, D_MODEL)
    kv_sample = _kv_new(qkv, n_s, n_new)
    ssm_sample = from_tiles(h_new)

    return (y_prompt, y_sample, kv_prompt[0], kv_prompt[1], kv_prompt[2], ssm_prompt,
            kv_sample[0], kv_sample[1], kv_sample[2], ssm_sample)
```

```python
import functools
import math

import jax
import jax.numpy as jnp
import numpy as np
from jax import lax
from jax.experimental import pallas as pl
from jax.experimental.pallas import tpu as pltpu

D_MODEL = 1024
D_PLE = 256
D_FF = 2816
SSM_GROUP = 16
SSM_STATE = 64
D_SSM = 512
N_SSM_GROUPS = D_SSM // SSM_GROUP
HEAD_DIM = 64
HEADS = 4
WINDOWS = (128, 512, 2048)
DILATIONS = (1, 4, 16)
N_GROUPS = 3
D_HEADS = HEADS * HEAD_DIM
N_STEPS = 128
ATTN_SCALE = HEAD_DIM ** -0.5
EPS = 1e-6

LANE = 128
V7X_VMEM_LIMIT_BYTES = 56 * 1024 * 1024
STAGE_TM = 512

CHUNK = 16
N_PAIRS = CHUNK // 2
PAIR_W = 2 * LANE
SSM_TILE_GROUPS = LANE // SSM_GROUP
N_SSM_TILES = N_SSM_GROUPS // SSM_TILE_GROUPS
TILE_STATE_W = SSM_TILE_GROUPS * 2 * SSM_STATE
Q_BLOCK = 128
ATTN_TB = 2048

D_QKV = 3 * N_GROUPS * D_HEADS


def _qkv_col(which, grp):
    return which * N_GROUPS + grp


BF16 = jnp.bfloat16
F32 = jnp.float32


def _dot(a, b):
    return jnp.dot(a, b, preferred_element_type=F32)


def _dot_nt(a, b):
    return lax.dot_general(a, b, (((1,), (1,)), ((), ())), preferred_element_type=F32)


def _rms(x, g):
    return x * lax.rsqrt(jnp.mean(x * x, axis=-1, keepdims=True) + EPS) * g


def _swiglu(xn, wg_ref, wu_ref, wd_ref):
    gate = _dot(xn, wg_ref[...])
    up = _dot(xn, wu_ref[...])
    act = (gate * jax.nn.sigmoid(gate) * up).astype(BF16)
    return _dot(act, wd_ref[...])


def _const_spec(shape):
    nd = len(shape)
    return pl.BlockSpec(shape, lambda *_: (0,) * nd, pipeline_mode=pl.Buffered(1))


def _params(n_grid_dims):
    return pltpu.CompilerParams(dimension_semantics=("arbitrary",) * n_grid_dims,
                                vmem_limit_bytes=V7X_VMEM_LIMIT_BYTES)


def _stage1_kernel(x_ref, g1_ref, wg_ref, wu_ref, wd_ref, gm_ref, win_ref,
                   x1_ref, u_ref, qkv_ref):
    x = x_ref[...]
    xn = _rms(x, g1_ref[...]).astype(BF16)
    x1 = x + 0.5 * _swiglu(xn, wg_ref, wu_ref, wd_ref)
    x1_ref[...] = x1
    h = _rms(x1, gm_ref[...]).astype(BF16)
    z = _dot(h, win_ref[...])
    u_ref[...] = z[:, :D_SSM]
    qkv_ref[...] = z[:, D_SSM:]


def _stage1(x, g1, wg, wu, wd, gm, win):
    t, tm = x.shape[0], STAGE_TM
    tok = lambda w: pl.BlockSpec((tm, w), lambda i: (i, 0))
    widths = (D_MODEL, D_SSM, D_QKV)
    return pl.pallas_call(
        _stage1_kernel,
        grid=(t // tm,),
        in_specs=[tok(D_MODEL), _const_spec(g1.shape), _const_spec(wg.shape), _const_spec(wu.shape),
                  _const_spec(wd.shape), _const_spec(gm.shape), _const_spec(win.shape)],
        out_specs=[tok(w) for w in widths],
        out_shape=[jax.ShapeDtypeStruct((t, w), F32) for w in widths],
        compiler_params=_params(1),
        name="stage1",
    )(x, g1, wg, wu, wd, gm, win)


def _swap_halves(x):
    half = x.shape[1] // 2
    return jnp.concatenate([x[:, half:], x[:, :half]], axis=1)


def _cmul_split(x, a1, a2):
    return x * a1 + _swap_halves(x) * a2


def _shift_rows(x, sh):
    if sh % 8 == 0:
        return jnp.concatenate([jnp.zeros((sh, x.shape[1]), x.dtype), x[:x.shape[0] - sh]], axis=0)
    row = lax.broadcasted_iota(jnp.int32, x.shape, 0)
    return jnp.where(row >= sh, pltpu.roll(x, sh, axis=0), 0.0)


def _ssm_kernel(*refs, n_t, n_rows, has_h0):
    if has_h0:
        u_ref, h0_ref, kp_ref, wp_ref, vpt_ref, a1_ref, a2_ref, y_ref, hl_ref = refs
    else:
        u_ref, kp_ref, wp_ref, vpt_ref, a1_ref, a2_ref, y_ref, hl_ref = refs
    n_pairs = n_t // 2
    tok = lambda t: pl.ds(t, n_rows, stride=n_t)
    up = [jnp.concatenate([u_ref[tok(2 * a), :], u_ref[tok(2 * a + 1), :]], axis=1).astype(BF16)
          for a in range(n_pairs)]
    w_row0 = (CHUNK - n_t) * LANE
    s = _dot(jnp.concatenate(up, axis=1), wp_ref[0, w_row0:, :])
    if has_h0:
        hprev = h0_ref[...]
        h_last = _cmul_split(hprev, a1_ref[0], a2_ref[0]) + s
        hl_ref[...] = h_last
    else:
        h = s
        for lvl in range(int(math.log2(n_rows))):
            h = h + _cmul_split(_shift_rows(h, 1 << lvl), a1_ref[0, lvl:lvl + 1, :], a2_ref[0, lvl:lvl + 1, :])
        hprev = _shift_rows(h, 1)
        hl_ref[0, 0] = h[n_rows - 1:n_rows, :]
    hpb = hprev.astype(BF16)
    for b in range(n_pairs):
        acc = _dot_nt(hpb, vpt_ref[0, b])
        for a in range(b + 1):
            acc = acc + _dot(up[a], kp_ref[0, b - a])
        y_ref[tok(2 * b), :] = acc[:, :LANE]
        y_ref[tok(2 * b + 1), :] = acc[:, LANE:]


def _ssm_table_specs(tabs, idx):
    kp, wp, vp, a1, a2 = tabs
    return [pl.BlockSpec((1,) + kp.shape[1:], lambda *g: (idx(*g), 0, 0, 0)),
            pl.BlockSpec((1,) + wp.shape[1:], lambda *g: (idx(*g), 0, 0)),
            pl.BlockSpec((1,) + vp.shape[1:], lambda *g: (idx(*g), 0, 0, 0)),
            pl.BlockSpec((1,) + a1.shape[1:], lambda *g: (idx(*g), 0, 0)),
            pl.BlockSpec((1,) + a2.shape[1:], lambda *g: (idx(*g), 0, 0))]


def _ssm_prompt(u, tabs, n, seq):
    n_rows = seq // CHUNK
    tile = pl.BlockSpec((seq, LANE), lambda j, i: (i, j))
    return pl.pallas_call(
        functools.partial(_ssm_kernel, n_t=CHUNK, n_rows=n_rows, has_h0=False),
        grid=(N_SSM_TILES, n),
        in_specs=[tile] + _ssm_table_specs(tabs, lambda j, i: j),
        out_specs=[tile, pl.BlockSpec((1, 1, 1, TILE_STATE_W), lambda j, i: (i, j, 0, 0))],
        out_shape=[jax.ShapeDtypeStruct((n * seq, D_SSM), F32),
                   jax.ShapeDtypeStruct((n, N_SSM_TILES, 1, TILE_STATE_W), F32)],
        compiler_params=_params(2),
        name="ssm_prompt",
    )(u, *tabs)


def _ssm_sample(u, h0, tabs, n_seq, n_new):
    tile = pl.BlockSpec((n_seq * n_new, LANE), lambda j: (0, j))
    state = pl.BlockSpec((n_seq, TILE_STATE_W), lambda j: (0, j))
    return pl.pallas_call(
        functools.partial(_ssm_kernel, n_t=n_new, n_rows=n_seq, has_h0=True),
        grid=(N_SSM_TILES,),
        in_specs=[tile, state] + _ssm_table_specs(tabs, lambda j: j),
        out_specs=[tile, state],
        out_shape=[jax.ShapeDtypeStruct((n_seq * n_new, D_SSM), F32),
                   jax.ShapeDtypeStruct((n_seq, N_SSM_TILES * TILE_STATE_W), F32)],
        compiler_params=_params(1),
        name="ssm_sample",
    )(u, h0, *tabs)


def _ssm_prep_kernel(are_ref, aim_ref, ldt_ref, bre_ref, bim_ref, cre_ref, cim_ref, d_ref,
                     kp_ref, wp_ref, vpt_ref, a1_ref, a2_ref, a1s_ref, a2s_ref, *, n_levels, n_new):
    a_re, a_im = are_ref[0], aim_ref[0]
    dt = jnp.exp(ldt_ref[0])
    mag = jnp.exp(a_re * dt)
    ab_re, ab_im = mag * jnp.cos(a_im * dt), mag * jnp.sin(a_im * dt)
    den = a_re * a_re + a_im * a_im
    nr, ni = ab_re - 1.0, ab_im
    f_re = (nr * a_re + ni * a_im) / den
    f_im = (ni * a_re - nr * a_im) / den
    shape = bre_ref.shape[1:]
    same_group = (lax.broadcasted_iota(jnp.int32, shape, 0) // SSM_GROUP
                  == lax.broadcasted_iota(jnp.int32, shape, 1) // SSM_STATE)
    b_re, b_im = bre_ref[0], bim_ref[0]
    x_re = jnp.where(same_group, f_re * b_re - f_im * b_im, 0.0)
    x_im = jnp.where(same_group, f_re * b_im + f_im * b_re, 0.0)
    c_re = jnp.where(same_group, cre_ref[0], 0.0)
    c_im = jnp.where(same_group, cim_ref[0], 0.0)
    pw = [(jnp.ones_like(ab_re), jnp.zeros_like(ab_im))]
    for _ in range(CHUNK):
        r, i = pw[-1]
        pw.append((r * ab_re - i * ab_im, r * ab_im + i * ab_re))
    cat = lambda r, i: jnp.concatenate([r, i], axis=1)
    e_pack = [cat(c_re * r - c_im * i, -(c_re * i + c_im * r)) for r, i in pw]
    x_pack = cat(x_re, x_im)
    k_lag = [lax.dot_general(x_pack, e_pack[k], (((1,), (1,)), ((), ())), precision=lax.Precision.HIGHEST,
                             preferred_element_type=F32) for k in range(CHUNK)]
    diag = (lax.broadcasted_iota(jnp.int32, (LANE, LANE), 0) == lax.broadcasted_iota(jnp.int32, (LANE, LANE), 1))
    k_lag[0] = k_lag[0] + jnp.where(diag, d_ref[0], 0.0)
    zero = jnp.zeros((LANE, LANE), F32)
    for dl in range(N_PAIRS):
        top = cat(k_lag[2 * dl], k_lag[2 * dl + 1])
        bot = cat(k_lag[2 * dl - 1] if dl > 0 else zero, k_lag[2 * dl])
        kp_ref[0, dl] = jnp.concatenate([top, bot], axis=0).astype(BF16)
    for t in range(CHUNK):
        r, i = pw[CHUNK - 1 - t]
        wp_ref[0, t * LANE:(t + 1) * LANE, :] = cat(x_re * r - x_im * i, x_re * i + x_im * r).astype(BF16)
    for t in range(CHUNK):
        vpt_ref[0, t // 2, (t % 2) * LANE:(t % 2 + 1) * LANE, :] = e_pack[t + 1].astype(BF16)
    r, i = pw[CHUNK]
    for lvl in range(n_levels):
        a1_ref[0, lvl:lvl + 1, :], a2_ref[0, lvl:lvl + 1, :] = cat(r, r), cat(-i, i)
        r, i = r * r - i * i, 2.0 * r * i
    r, i = pw[n_new]
    a1s_ref[0], a2s_ref[0] = cat(r, r), cat(-i, i)


def _ssm_tables(a_re, a_im, log_dt, b_re, b_im, c_re, c_im, d_skip, n_levels, n_new):
    nj, tg = N_SSM_TILES, SSM_TILE_GROUPS
    half = tg * SSM_STATE
    rowv = lambda v: v.reshape(nj, 1, half)
    tiled = lambda m: jnp.tile(m.reshape(nj, LANE, SSM_STATE), (1, 1, tg))
    ins = (rowv(a_re), rowv(a_im), rowv(jnp.repeat(log_dt, SSM_STATE)),
           tiled(b_re.transpose(0, 2, 1)), tiled(b_im.transpose(0, 2, 1)), tiled(c_re), tiled(c_im),
           d_skip.reshape(nj, 1, LANE))
    shapes = ((nj, N_PAIRS, PAIR_W, PAIR_W), (nj, CHUNK * LANE, TILE_STATE_W), (nj, N_PAIRS, PAIR_W, TILE_STATE_W),
              (nj, n_levels, TILE_STATE_W), (nj, n_levels, TILE_STATE_W), (nj, 1, TILE_STATE_W), (nj, 1, TILE_STATE_W))
    dtypes = (BF16, BF16, BF16, F32, F32, F32, F32)
    per_tile = lambda s: pl.BlockSpec((1,) + tuple(s[1:]), lambda j: (j,) + (0,) * (len(s) - 1))
    kp, wp, vpt, a1, a2, a1s, a2s = pl.pallas_call(
        functools.partial(_ssm_prep_kernel, n_levels=n_levels, n_new=n_new),
        grid=(nj,),
        in_specs=[per_tile(x.shape) for x in ins],
        out_specs=[per_tile(s) for s in shapes],
        out_shape=[jax.ShapeDtypeStruct(s, dt) for s, dt in zip(shapes, dtypes)],
        compiler_params=_params(1),
        name="ssm_prep",
    )(*ins)
    return (kp, wp, vpt, a1, a2), (kp, wp, vpt, a1s, a2s)


def _head_of_lane(width):
    return lax.broadcasted_iota(jnp.int32, (1, width), 1) // HEAD_DIM


def _stack_heads(q):
    head = _head_of_lane(q.shape[1])
    q = q.astype(F32)
    return jnp.concatenate([jnp.where(head == h, q, 0.0) for h in range(HEADS)], axis=0).astype(BF16)


def _unstack_heads(x, rows):
    head = _head_of_lane(x.shape[1])
    out = x[(HEADS - 1) * rows:]
    for h in range(HEADS - 2, -1, -1):
        out = jnp.where(head == h, x[h * rows:(h + 1) * rows], out)
    return out


def _attn_prompt_kernel(bias_ref, q0, q1, k0, k1, v0, v1, o0, o1, l0, l1, kprev, vprev, *, d, m):
    qb = Q_BLOCK
    b = pl.program_id(1)
    n_par = min(d, 2)
    rd, wr = b % 2, (b + 1) % 2
    head = _head_of_lane(D_HEADS)

    @pl.when(b == 0)
    def _():
        kprev[0] = jnp.zeros(kprev.shape[1:], kprev.dtype)
        vprev[0] = jnp.zeros(vprev.shape[1:], vprev.dtype)

    def block(r, j, k_prev, v_prev):
        start = r + j * qb * d
        rows = pl.ds(start, qb, stride=d) if d > 1 else pl.ds(pl.multiple_of(start, qb), qb)
        both = lambda lo, hi: jnp.concatenate([lo[rows, :], hi[rows, :]], axis=1)
        qs = _stack_heads(both(q0, q1) * ATTN_SCALE)
        k_own, v_own = both(k0, k1).astype(BF16), both(v0, v1).astype(BF16)
        k = jnp.concatenate([k_prev, k_own], axis=0)
        v = jnp.concatenate([v_prev, v_own], axis=0)
        s = _dot_nt(qs, k) + bias_ref[jnp.minimum(b * m + j, 1)]
        mx = jnp.max(s, axis=-1, keepdims=True)
        pe = jnp.exp(s - mx)
        den = jnp.sum(pe, axis=-1, keepdims=True)
        pn = (pe * (1.0 / den)).astype(BF16)
        o = _dot(pn[:qb], jnp.where(head == 0, v, jnp.zeros_like(v)))
        for h in range(1, HEADS):
            o = o + _dot(pn[h * qb:(h + 1) * qb], jnp.where(head == h, v, jnp.zeros_like(v)))
        lse = _unstack_heads(jnp.broadcast_to(mx + jnp.log(den), (HEADS * qb, D_HEADS)), qb)
        o0[rows, :], o1[rows, :] = o[:, :LANE], o[:, LANE:]
        l0[rows, :], l1[rows, :] = lse[:, :LANE], lse[:, LANE:]
        return k_own, v_own

    def class_group(g, carry):
        classes = [g * n_par + t for t in range(n_par)]

        def step(j, prev):
            return tuple(x for t, r in enumerate(classes) for x in block(r, j, prev[2 * t], prev[2 * t + 1]))

        init = tuple(x for r in classes for x in (kprev[rd, r], vprev[rd, r]))
        last = lax.fori_loop(0, m, step, init, unroll=2 if (n_par == 1 and m > 1) else 1)
        for t, r in enumerate(classes):
            kprev[wr, r], vprev[wr, r] = last[2 * t], last[2 * t + 1]
        return carry

    lax.fori_loop(0, d // n_par, class_group, 0)


def _attn_bias():
    qi = np.arange(HEADS * Q_BLOCK)[:, None] % Q_BLOCK
    ki = np.arange(2 * Q_BLOCK)[None, :]
    diff = Q_BLOCK + qi - ki
    valid = (diff >= 0) & (diff <= N_STEPS)
    return jnp.asarray(np.where(np.stack([valid & (ki >= Q_BLOCK), valid]), 0.0, -np.inf), F32)


def _attn_prompt(qkv, n, seq, grp):
    d = DILATIONS[grp]
    tb = min(ATTN_TB, seq)
    assert tb % (Q_BLOCK * d) == 0 and seq % tb == 0
    m, nb = tb // (Q_BLOCK * d), seq // tb
    cur = lambda c: pl.BlockSpec((tb, LANE), lambda i, b: (i * nb + b, c))
    qc, kc, vc = (2 * _qkv_col(which, grp) for which in range(3))
    bias = _attn_bias()
    return pl.pallas_call(
        functools.partial(_attn_prompt_kernel, d=d, m=m),
        grid=(n, nb),
        in_specs=[_const_spec(bias.shape), cur(qc), cur(qc + 1), cur(kc), cur(kc + 1), cur(vc), cur(vc + 1)],
        out_specs=[pl.BlockSpec((tb, LANE), lambda i, b: (i * nb + b, 0))] * 4,
        out_shape=[jax.ShapeDtypeStruct((n * seq, LANE), F32)] * 4,
        scratch_shapes=[pltpu.VMEM((2, d, Q_BLOCK, D_HEADS), BF16)] * 2,
        compiler_params=_params(2),
        name=f"attn_prompt_d{d}",
    )(bias, *([qkv] * 6))


def _attn_sample_kernel(q_ref, kn_ref, vn_ref, c_ref, o_ref, lse_ref, *, d, n_new, seq_blk):
    w = c_ref.shape[-1]
    n_rows = HEADS * n_new
    l_b = lax.broadcasted_iota(jnp.int32, (n_rows, w), 0) % n_new
    pos = lax.broadcasted_iota(jnp.int32, (n_rows, w), 1)
    valid_buf = (((w + l_b - pos) & (d - 1)) == 0) & (pos >= l_b)
    dn = (lax.broadcasted_iota(jnp.int32, (n_rows, LANE), 0) % n_new) - lax.broadcasted_iota(jnp.int32, (n_rows, LANE), 1)
    valid_new = (dn >= 0) & ((dn & (d - 1)) == 0)
    pad = jnp.zeros((LANE - n_new, D_HEADS), F32)

    def one_seq(s, carry):
        qs = _stack_heads(q_ref[s])
        k_t, v_t = c_ref[s, 0].astype(BF16), c_ref[s, 1].astype(BF16)
        k_n = jnp.concatenate([kn_ref[s], pad], axis=0).astype(BF16)
        v_n = jnp.concatenate([vn_ref[s], pad], axis=0).astype(BF16)
        s_b = jnp.where(valid_buf, _dot(qs, k_t) * ATTN_SCALE, -jnp.inf)
        s_n = jnp.where(valid_new, _dot_nt(qs, k_n) * ATTN_SCALE, -jnp.inf)
        mx = jnp.maximum(jnp.max(s_b, axis=-1, keepdims=True), jnp.max(s_n, axis=-1, keepdims=True))
        p_b, p_n = jnp.exp(s_b - mx), jnp.exp(s_n - mx)
        den = jnp.sum(p_b, axis=-1, keepdims=True) + jnp.sum(p_n, axis=-1, keepdims=True)
        o = (_dot_nt(p_b.astype(BF16), v_t) + _dot(p_n.astype(BF16), v_n)) / den
        o_ref[s] = _unstack_heads(o, n_new)
        lse_ref[s] = _unstack_heads(jnp.broadcast_to(mx + jnp.log(den), o.shape), n_new)
        return carry

    lax.fori_loop(0, seq_blk, one_seq, 0, unroll=min(seq_blk, 2))


def _attn_sample(qkv, cache, n_seq, n_new, grp):
    d = DILATIONS[grp]
    w = cache.shape[1]
    assert w == N_STEPS * d and n_new <= LANE
    seq_blk = max(2, min(16, (2 * 1024 * 1024) // (w * D_HEADS * 4)))
    qkv3 = qkv.reshape(n_seq, n_new, D_QKV)
    c_t = cache.transpose(0, 2, 3, 4, 1).reshape(n_seq, 2, D_HEADS, w)
    blk = (seq_blk, n_new, D_HEADS)
    o, lse = pl.pallas_call(
        functools.partial(_attn_sample_kernel, d=d, n_new=n_new, seq_blk=seq_blk),
        grid=(n_seq // seq_blk,),
        in_specs=[pl.BlockSpec(blk, lambda i: (i, 0, _qkv_col(0, grp))),
                  pl.BlockSpec(blk, lambda i: (i, 0, _qkv_col(1, grp))),
                  pl.BlockSpec(blk, lambda i: (i, 0, _qkv_col(2, grp))),
                  pl.BlockSpec((seq_blk, 2, D_HEADS, w), lambda i: (i, 0, 0, 0))],
        out_specs=[pl.BlockSpec(blk, lambda i: (i, 0, 0))] * 2,
        out_shape=[jax.ShapeDtypeStruct((n_seq, n_new, D_HEADS), F32)] * 2,
        compiler_params=_params(1),
        name=f"attn_sample_d{d}",
    )(qkv3, qkv3, qkv3, c_t)
    o, lse = o.reshape(n_seq * n_new, D_HEADS), lse.reshape(n_seq * n_new, D_HEADS)
    return o[:, :LANE], o[:, LANE:], lse[:, :LANE], lse[:, LANE:]


def _kv_tail_kernel(k_ref, v_ref, o_ref):
    o_ref[0, 0] = k_ref[...].T
    o_ref[0, 1] = v_ref[...].T


def _kv_tail(qkv, n, seq, grp):
    keep = min(WINDOWS[grp], seq)
    tt = min(keep, 512)
    assert keep % tt == 0 and (seq - keep) % tt == 0
    first = (seq - keep) // tt
    rows = lambda which: pl.BlockSpec((tt, D_HEADS), lambda i, t: (i * (seq // tt) + first + t, _qkv_col(which, grp)))
    out = pl.pallas_call(
        _kv_tail_kernel,
        grid=(n, keep // tt),
        in_specs=[rows(1), rows(2)],
        out_specs=pl.BlockSpec((1, 2, D_HEADS, tt), lambda i, t: (i, 0, 0, t)),
        out_shape=jax.ShapeDtypeStruct((n, 2, D_HEADS, keep), F32),
        compiler_params=_params(2),
        name=f"kv_tail_w{keep}",
    )(qkv, qkv)
    return out.reshape(1, n, 2, HEADS, HEAD_DIM, keep).transpose(0, 1, 5, 2, 3, 4)


def _kv_new_kernel(x_ref, o_ref, *, n_new, n_seq):
    for l in range(n_new):
        o_ref[0, l, 0] = x_ref[pl.ds(l, n_seq, stride=n_new), :].T


def _kv_new(qkv, n_seq, n_new):
    halves = D_HEADS // LANE
    out = pl.pallas_call(
        functools.partial(_kv_new_kernel, n_new=n_new, n_seq=n_seq),
        grid=(N_GROUPS, 2, halves),
        in_specs=[pl.BlockSpec((n_seq * n_new, LANE), lambda g, kv, c: (0, halves * _qkv_col(1 + kv, g) + c))],
        out_specs=pl.BlockSpec((1, n_new, 1, LANE, n_seq), lambda g, kv, c: (g, 0, kv, c, 0)),
        out_shape=jax.ShapeDtypeStruct((N_GROUPS, n_new, 2, D_HEADS, n_seq), F32),
        compiler_params=_params(3),
        name="kv_new",
    )(qkv)
    out = out.reshape(N_GROUPS, 1, n_new, 2, HEADS, HEAD_DIM, n_seq).transpose(0, 1, 6, 2, 3, 4, 5)
    return [out[g] for g in range(N_GROUPS)]


def _gelu_tanh(x):
    return 0.5 * x * (1.0 + jnp.tanh(math.sqrt(2.0 / math.pi) * (x + 0.044715 * (x * x * x))))


def _stage3_kernel(*refs):
    x1_ref, yp_ref = refs[:2]
    attn = refs[2:2 + 4 * N_GROUPS]
    (p_ref, gm_ref, wgate_ref, wglu_ref, wbs_ref, wba_ref, wout_ref, g2_ref, wg_ref, wu_ref, wd_ref,
     gp_ref, wpg_ref, wpp_ref, gf_ref, y_ref) = refs[2 + 4 * N_GROUPS:]
    x1 = x1_ref[...]
    gates = jax.nn.sigmoid(_dot(_rms(x1, gm_ref[...]).astype(BF16), wgate_ref[...]))
    y = _gelu_tanh(yp_ref[...])
    glu = y * jax.nn.sigmoid(_dot(y.astype(BF16), wglu_ref[...]))
    full = lambda lo, hi: jnp.concatenate([lo[...], hi[...]], axis=1)
    outs = [full(attn[4 * g], attn[4 * g + 1]) for g in range(N_GROUPS)]
    lses = [full(attn[4 * g + 2], attn[4 * g + 3]) for g in range(N_GROUPS)]
    mx = jnp.maximum(jnp.maximum(lses[0], lses[1]), lses[2])
    es = [jnp.exp(l - mx) for l in lses]
    y_attn = (es[0] * outs[0] + es[1] * outs[1] + es[2] * outs[2]) / (es[0] + es[1] + es[2])
    merged = (gates[:, :D_MODEL] * _dot(glu.astype(BF16), wbs_ref[...])
              + gates[:, D_MODEL:] * _dot(y_attn.astype(BF16), wba_ref[...]))
    x = x1 + _dot(merged.astype(BF16), wout_ref[...])
    x = x + 0.5 * _swiglu(_rms(x, g2_ref[...]).astype(BF16), wg_ref, wu_ref, wd_ref)
    gate = jax.nn.sigmoid(_dot(_rms(x, gp_ref[...]).astype(BF16), wpg_ref[...]))
    x = x + gate * _dot(p_ref[...].astype(BF16), wpp_ref[...])
    y_ref[...] = _rms(x, gf_ref[...])


def _stage3(tok_inputs, weights):
    t, tm = tok_inputs[0].shape[0], STAGE_TM
    tok = lambda a: pl.BlockSpec((tm, a.shape[1]), lambda i: (i, 0))
    return pl.pallas_call(
        _stage3_kernel,
        grid=(t // tm,),
        in_specs=[tok(a) for a in tok_inputs] + [_const_spec(w.shape) for w in weights],
        out_specs=pl.BlockSpec((tm, D_MODEL), lambda i: (i, 0)),
        out_shape=jax.ShapeDtypeStruct((t, D_MODEL), F32),
        compiler_params=_params(1),
        name="stage3",
    )(*tok_inputs, *weights)


def kernel(x_prompt, x_sample, p_prompt, p_sample, cache_kv_w128, cache_kv_w512, cache_kv_w2048, state_ssm,
           g_ffn1, ffn1_w_gate, ffn1_w_up, ffn1_w_down, g_mix, w_in, ssm_a_re, ssm_a_im, ssm_log_dt,
           ssm_b_re, ssm_b_im, ssm_c_re, ssm_c_im, ssm_d, ssm_w_glu, w_br_ssm, w_br_attn, w_out,
           g_ffn2, ffn2_w_gate, ffn2_w_up, ffn2_w_down, g_ple, w_ple_gate, w_ple_proj, g_final):
    assert x_prompt.shape[-1] == D_MODEL and g_ffn1.shape[0] == 1
    n_p, seq, _ = x_prompt.shape
    n_s, n_new, _ = x_sample.shape
    caches = (cache_kv_w128, cache_kv_w512, cache_kv_w2048)
    row = lambda g: g.reshape(1, -1)
    bf = lambda w: w[0].astype(BF16)

    wi = w_in[0]
    win = wi[:, :D_SSM + D_QKV].astype(BF16)
    w_gates = wi[:, D_SSM + D_QKV:].astype(BF16)
    s1_w = (row(g_ffn1[0]), bf(ffn1_w_gate), bf(ffn1_w_up), bf(ffn1_w_down), row(g_mix[0]), win)
    s3_w = (row(g_mix[0]), w_gates, bf(ssm_w_glu), bf(w_br_ssm), bf(w_br_attn), bf(w_out), row(g_ffn2[0]),
            bf(ffn2_w_gate), bf(ffn2_w_up), bf(ffn2_w_down), row(g_ple[0]), bf(w_ple_gate),
            bf(w_ple_proj), row(g_final))

    n_chunks = seq // CHUNK
    tabs_p, tabs_s = _ssm_tables(
        ssm_a_re[0], ssm_a_im[0], ssm_log_dt[0], ssm_b_re[0], ssm_b_im[0], ssm_c_re[0], ssm_c_im[0],
        ssm_d[0], int(math.log2(n_chunks)), n_new)
    tile_state = (-1, N_SSM_TILES, 2, SSM_TILE_GROUPS, SSM_STATE)
    from_tiles = lambda h: h.reshape(tile_state).transpose(0, 1, 3, 4, 2).reshape(1, -1, N_SSM_GROUPS, SSM_STATE, 2)

    x1, u, qkv = _stage1(x_prompt.reshape(n_p * seq, D_MODEL), *s1_w)
    y_pre, h_last = _ssm_prompt(u, tabs_p, n_p, seq)
    attn = [a for grp in range(N_GROUPS) for a in _attn_prompt(qkv, n_p, seq, grp)]
    tok = (x1, y_pre, *attn, p_prompt[0].reshape(n_p * seq, D_PLE))
    y_prompt = _stage3(tok, s3_w).reshape(n_p, seq, D_MODEL)
    kv_prompt = [_kv_tail(qkv, n_p, seq, grp) for grp in range(N_GROUPS)]
    ssm_prompt = from_tiles(h_last)

    x1, u, qkv = _stage1(x_sample.reshape(n_s * n_new, D_MODEL), *s1_w)
    h0 = state_ssm[0].reshape(n_s, N_SSM_TILES, SSM_TILE_GROUPS, SSM_STATE, 2)
    h0 = h0.transpose(0, 1, 4, 2, 3).reshape(n_s, N_SSM_TILES * TILE_STATE_W)
    y_pre, h_new = _ssm_sample(u, h0, tabs_s, n_s, n_new)
    attn = [a for grp in range(N_GROUPS)
            for a in _attn_sample(qkv, caches[grp][0], n_s, n_new, grp)]
    tok = (x1, y_pre, *attn, p_sample[0].reshape(n_s * n_new, D_PLE))
    y_sample = _stage3(tok, s3_w).reshape(n_s, n_new, D_MODEL)
    kv_sample = _kv_new(qkv, n_s, n_new)
    ssm_sample = from_tiles(h_new)

    return (y_prompt, y_sample, kv_prompt[0], kv_prompt[1], kv_prompt[2], ssm_prompt,
            kv_sample[0], kv_sample[1], kv_sample[2], ssm_sample)
```

```python
import functools
import math

import jax
import jax.numpy as jnp
import numpy as np
from jax import lax
from jax.experimental import pallas as pl
from jax.experimental.pallas import tpu as pltpu

D_MODEL = 1024
D_PLE = 256
D_FF = 2816
SSM_GROUP = 16
SSM_STATE = 64
D_SSM = 512
N_SSM_GROUPS = D_SSM // SSM_GROUP
HEAD_DIM = 64
HEADS = 4
WINDOWS = (128, 512, 2048)
DILATIONS = (1, 4, 16)
N_GROUPS = 3
D_HEADS = HEADS * HEAD_DIM
N_STEPS = 128
ATTN_SCALE = HEAD_DIM ** -0.5
EPS = 1e-6

LANE = 128
V7X_VMEM_LIMIT_BYTES = 56 * 1024 * 1024
STAGE_TM = 512

CHUNK = 16
N_PAIRS = CHUNK // 2
PAIR_W = 2 * LANE
SSM_TILE_GROUPS = LANE // SSM_GROUP
N_SSM_TILES = N_SSM_GROUPS // SSM_TILE_GROUPS
TILE_STATE_W = SSM_TILE_GROUPS * 2 * SSM_STATE
Q_BLOCK = 128
ATTN_TB = 2048
ATTN_PAR = 4

D_QKV = 3 * N_GROUPS * D_HEADS


def _qkv_col(which, grp):
    return which * N_GROUPS + grp


BF16 = jnp.bfloat16
F32 = jnp.float32


def _dot(a, b):
    return jnp.dot(a, b, preferred_element_type=F32)


def _dot_nt(a, b):
    return lax.dot_general(a, b, (((1,), (1,)), ((), ())), preferred_element_type=F32)


def _rms(x, g):
    return x * lax.rsqrt(jnp.mean(x * x, axis=-1, keepdims=True) + EPS) * g


def _swiglu(xn, wg_ref, wu_ref, wd_ref):
    gate = _dot(xn, wg_ref[...])
    up = _dot(xn, wu_ref[...])
    act = (gate * jax.nn.sigmoid(gate) * up).astype(BF16)
    return _dot(act, wd_ref[...])


def _const_spec(shape):
    nd = len(shape)
    return pl.BlockSpec(shape, lambda *_: (0,) * nd, pipeline_mode=pl.Buffered(1))


def _params(n_grid_dims):
    return pltpu.CompilerParams(dimension_semantics=("arbitrary",) * n_grid_dims,
                                vmem_limit_bytes=V7X_VMEM_LIMIT_BYTES)


def _stage1_kernel(x_ref, g1_ref, wg_ref, wu_ref, wd_ref, gm_ref, win_ref,
                   x1_ref, u_ref, qkv_ref):
    x = x_ref[...]
    xn = _rms(x, g1_ref[...]).astype(BF16)
    x1 = x + 0.5 * _swiglu(xn, wg_ref, wu_ref, wd_ref)
    x1_ref[...] = x1
    h = _rms(x1, gm_ref[...]).astype(BF16)
    z = _dot(h, win_ref[...])
    u_ref[...] = z[:, :D_SSM]
    qkv_ref[...] = z[:, D_SSM:]


def _stage1(x, g1, wg, wu, wd, gm, win):
    t, tm = x.shape[0], STAGE_TM
    tok = lambda w: pl.BlockSpec((tm, w), lambda i: (i, 0))
    widths = (D_MODEL, D_SSM, D_QKV)
    return pl.pallas_call(
        _stage1_kernel,
        grid=(t // tm,),
        in_specs=[tok(D_MODEL), _const_spec(g1.shape), _const_spec(wg.shape), _const_spec(wu.shape),
                  _const_spec(wd.shape), _const_spec(gm.shape), _const_spec(win.shape)],
        out_specs=[tok(w) for w in widths],
        out_shape=[jax.ShapeDtypeStruct((t, w), F32) for w in widths],
        compiler_params=_params(1),
        name="stage1",
    )(x, g1, wg, wu, wd, gm, win)


def _swap_halves(x):
    half = x.shape[1] // 2
    return jnp.concatenate([x[:, half:], x[:, :half]], axis=1)


def _cmul_split(x, a1, a2):
    return x * a1 + _swap_halves(x) * a2


def _shift_rows(x, sh):
    if sh % 8 == 0:
        return jnp.concatenate([jnp.zeros((sh, x.shape[1]), x.dtype), x[:x.shape[0] - sh]], axis=0)
    row = lax.broadcasted_iota(jnp.int32, x.shape, 0)
    return jnp.where(row >= sh, pltpu.roll(x, sh, axis=0), 0.0)


def _ssm_kernel(*refs, n_t, n_rows, has_h0):
    if has_h0:
        u_ref, h0_ref, kp_ref, wp_ref, vpt_ref, a1_ref, a2_ref, y_ref, hl_ref = refs
    else:
        u_ref, kp_ref, wp_ref, vpt_ref, a1_ref, a2_ref, y_ref, hl_ref = refs
    n_pairs = n_t // 2
    tok = lambda t: pl.ds(t, n_rows, stride=n_t)
    up = [jnp.concatenate([u_ref[tok(2 * a), :], u_ref[tok(2 * a + 1), :]], axis=1).astype(BF16)
          for a in range(n_pairs)]
    w_row0 = (CHUNK - n_t) * LANE
    s = _dot(jnp.concatenate(up, axis=1), wp_ref[0, w_row0:, :])
    if has_h0:
        hprev = h0_ref[...]
        h_last = _cmul_split(hprev, a1_ref[0], a2_ref[0]) + s
        hl_ref[...] = h_last
    else:
        h = s
        for lvl in range(int(math.log2(n_rows))):
            h = h + _cmul_split(_shift_rows(h, 1 << lvl), a1_ref[0, lvl:lvl + 1, :], a2_ref[0, lvl:lvl + 1, :])
        hprev = _shift_rows(h, 1)
        hl_ref[0, 0] = h[n_rows - 1:n_rows, :]
    hpb = hprev.astype(BF16)
    for b in range(n_pairs):
        acc = _dot_nt(hpb, vpt_ref[0, b])
        for a in range(b + 1):
            acc = acc + _dot(up[a], kp_ref[0, b - a])
        y_ref[tok(2 * b), :] = acc[:, :LANE]
        y_ref[tok(2 * b + 1), :] = acc[:, LANE:]


def _ssm_table_specs(tabs, idx):
    kp, wp, vp, a1, a2 = tabs
    return [pl.BlockSpec((1,) + kp.shape[1:], lambda *g: (idx(*g), 0, 0, 0)),
            pl.BlockSpec((1,) + wp.shape[1:], lambda *g: (idx(*g), 0, 0)),
            pl.BlockSpec((1,) + vp.shape[1:], lambda *g: (idx(*g), 0, 0, 0)),
            pl.BlockSpec((1,) + a1.shape[1:], lambda *g: (idx(*g), 0, 0)),
            pl.BlockSpec((1,) + a2.shape[1:], lambda *g: (idx(*g), 0, 0))]


def _ssm_prompt(u, tabs, n, seq):
    n_rows = seq // CHUNK
    tile = pl.BlockSpec((seq, LANE), lambda j, i: (i, j))
    return pl.pallas_call(
        functools.partial(_ssm_kernel, n_t=CHUNK, n_rows=n_rows, has_h0=False),
        grid=(N_SSM_TILES, n),
        in_specs=[tile] + _ssm_table_specs(tabs, lambda j, i: j),
        out_specs=[tile, pl.BlockSpec((1, 1, 1, TILE_STATE_W), lambda j, i: (i, j, 0, 0))],
        out_shape=[jax.ShapeDtypeStruct((n * seq, D_SSM), F32),
                   jax.ShapeDtypeStruct((n, N_SSM_TILES, 1, TILE_STATE_W), F32)],
        compiler_params=_params(2),
        name="ssm_prompt",
    )(u, *tabs)


def _ssm_sample(u, h0, tabs, n_seq, n_new):
    tile = pl.BlockSpec((n_seq * n_new, LANE), lambda j: (0, j))
    state = pl.BlockSpec((n_seq, TILE_STATE_W), lambda j: (0, j))
    return pl.pallas_call(
        functools.partial(_ssm_kernel, n_t=n_new, n_rows=n_seq, has_h0=True),
        grid=(N_SSM_TILES,),
        in_specs=[tile, state] + _ssm_table_specs(tabs, lambda j: j),
        out_specs=[tile, state],
        out_shape=[jax.ShapeDtypeStruct((n_seq * n_new, D_SSM), F32),
                   jax.ShapeDtypeStruct((n_seq, N_SSM_TILES * TILE_STATE_W), F32)],
        compiler_params=_params(1),
        name="ssm_sample",
    )(u, h0, *tabs)


def _ssm_prep_kernel(are_ref, aim_ref, ldt_ref, bre_ref, bim_ref, cre_ref, cim_ref, d_ref,
                     kp_ref, wp_ref, vpt_ref, a1_ref, a2_ref, a1s_ref, a2s_ref, *, n_levels, n_new):
    a_re, a_im = are_ref[0], aim_ref[0]
    dt = jnp.exp(ldt_ref[0])
    mag = jnp.exp(a_re * dt)
    ab_re, ab_im = mag * jnp.cos(a_im * dt), mag * jnp.sin(a_im * dt)
    den = a_re * a_re + a_im * a_im
    nr, ni = ab_re - 1.0, ab_im
    f_re = (nr * a_re + ni * a_im) / den
    f_im = (ni * a_re - nr * a_im) / den
    shape = bre_ref.shape[1:]
    same_group = (lax.broadcasted_iota(jnp.int32, shape, 0) // SSM_GROUP
                  == lax.broadcasted_iota(jnp.int32, shape, 1) // SSM_STATE)
    b_re, b_im = bre_ref[0], bim_ref[0]
    x_re = jnp.where(same_group, f_re * b_re - f_im * b_im, 0.0)
    x_im = jnp.where(same_group, f_re * b_im + f_im * b_re, 0.0)
    c_re = jnp.where(same_group, cre_ref[0], 0.0)
    c_im = jnp.where(same_group, cim_ref[0], 0.0)
    pw = [(jnp.ones_like(ab_re), jnp.zeros_like(ab_im))]
    for _ in range(CHUNK):
        r, i = pw[-1]
        pw.append((r * ab_re - i * ab_im, r * ab_im + i * ab_re))
    cat = lambda r, i: jnp.concatenate([r, i], axis=1)
    e_pack = [cat(c_re * r - c_im * i, -(c_re * i + c_im * r)) for r, i in pw]
    x_pack = cat(x_re, x_im)
    k_lag = [lax.dot_general(x_pack, e_pack[k], (((1,), (1,)), ((), ())), precision=lax.Precision.HIGHEST,
                             preferred_element_type=F32) for k in range(CHUNK)]
    diag = (lax.broadcasted_iota(jnp.int32, (LANE, LANE), 0) == lax.broadcasted_iota(jnp.int32, (LANE, LANE), 1))
    k_lag[0] = k_lag[0] + jnp.where(diag, d_ref[0], 0.0)
    zero = jnp.zeros((LANE, LANE), F32)
    for dl in range(N_PAIRS):
        top = cat(k_lag[2 * dl], k_lag[2 * dl + 1])
        bot = cat(k_lag[2 * dl - 1] if dl > 0 else zero, k_lag[2 * dl])
        kp_ref[0, dl] = jnp.concatenate([top, bot], axis=0).astype(BF16)
    for t in range(CHUNK):
        r, i = pw[CHUNK - 1 - t]
        wp_ref[0, t * LANE:(t + 1) * LANE, :] = cat(x_re * r - x_im * i, x_re * i + x_im * r).astype(BF16)
    for t in range(CHUNK):
        vpt_ref[0, t // 2, (t % 2) * LANE:(t % 2 + 1) * LANE, :] = e_pack[t + 1].astype(BF16)
    r, i = pw[CHUNK]
    for lvl in range(n_levels):
        a1_ref[0, lvl:lvl + 1, :], a2_ref[0, lvl:lvl + 1, :] = cat(r, r), cat(-i, i)
        r, i = r * r - i * i, 2.0 * r * i
    r, i = pw[n_new]
    a1s_ref[0], a2s_ref[0] = cat(r, r), cat(-i, i)


def _ssm_tables(a_re, a_im, log_dt, b_re, b_im, c_re, c_im, d_skip, n_levels, n_new):
    nj, tg = N_SSM_TILES, SSM_TILE_GROUPS
    half = tg * SSM_STATE
    rowv = lambda v: v.reshape(nj, 1, half)
    tiled = lambda m: jnp.tile(m.reshape(nj, LANE, SSM_STATE), (1, 1, tg))
    ins = (rowv(a_re), rowv(a_im), rowv(jnp.repeat(log_dt, SSM_STATE)),
           tiled(b_re.transpose(0, 2, 1)), tiled(b_im.transpose(0, 2, 1)), tiled(c_re), tiled(c_im),
           d_skip.reshape(nj, 1, LANE))
    shapes = ((nj, N_PAIRS, PAIR_W, PAIR_W), (nj, CHUNK * LANE, TILE_STATE_W), (nj, N_PAIRS, PAIR_W, TILE_STATE_W),
              (nj, n_levels, TILE_STATE_W), (nj, n_levels, TILE_STATE_W), (nj, 1, TILE_STATE_W), (nj, 1, TILE_STATE_W))
    dtypes = (BF16, BF16, BF16, F32, F32, F32, F32)
    per_tile = lambda s: pl.BlockSpec((1,) + tuple(s[1:]), lambda j: (j,) + (0,) * (len(s) - 1))
    kp, wp, vpt, a1, a2, a1s, a2s = pl.pallas_call(
        functools.partial(_ssm_prep_kernel, n_levels=n_levels, n_new=n_new),
        grid=(nj,),
        in_specs=[per_tile(x.shape) for x in ins],
        out_specs=[per_tile(s) for s in shapes],
        out_shape=[jax.ShapeDtypeStruct(s, dt) for s, dt in zip(shapes, dtypes)],
        compiler_params=_params(1),
        name="ssm_prep",
    )(*ins)
    return (kp, wp, vpt, a1, a2), (kp, wp, vpt, a1s, a2s)


def _head_of_lane(width):
    return lax.broadcasted_iota(jnp.int32, (1, width), 1) // HEAD_DIM


def _stack_heads(q):
    head = _head_of_lane(q.shape[1])
    q = q.astype(F32)
    return jnp.concatenate([jnp.where(head == h, q, 0.0) for h in range(HEADS)], axis=0).astype(BF16)


def _unstack_heads(x, rows):
    head = _head_of_lane(x.shape[1])
    out = x[(HEADS - 1) * rows:]
    for h in range(HEADS - 2, -1, -1):
        out = jnp.where(head == h, x[h * rows:(h + 1) * rows], out)
    return out


def _attn_prompt_kernel(bias_ref, q0, q1, k0, k1, v0, v1, o0, o1, l0, l1, kprev, vprev, *, d, m):
    qb = Q_BLOCK
    b = pl.program_id(1)
    n_par = min(d, ATTN_PAR)
    rd, wr = b % 2, (b + 1) % 2
    head = _head_of_lane(D_HEADS)

    @pl.when(b == 0)
    def _():
        kprev[0] = jnp.zeros(kprev.shape[1:], kprev.dtype)
        vprev[0] = jnp.zeros(vprev.shape[1:], vprev.dtype)

    def block(r, j, k_prev, v_prev):
        start = r + j * qb * d
        rows = pl.ds(start, qb, stride=d) if d > 1 else pl.ds(pl.multiple_of(start, qb), qb)
        both = lambda lo, hi: jnp.concatenate([lo[rows, :], hi[rows, :]], axis=1)
        qs = _stack_heads(both(q0, q1) * ATTN_SCALE)
        k_own, v_own = both(k0, k1).astype(BF16), both(v0, v1).astype(BF16)
        k = jnp.concatenate([k_prev, k_own], axis=0)
        v = jnp.concatenate([v_prev, v_own], axis=0)
        s = _dot_nt(qs, k) + bias_ref[jnp.minimum(b * m + j, 1)]
        mx = jnp.max(s, axis=-1, keepdims=True)
        pe = jnp.exp(s - mx)
        den = jnp.sum(pe, axis=-1, keepdims=True)
        pn = (pe * (1.0 / den)).astype(BF16)
        o = _dot(pn[:qb], jnp.where(head == 0, v, jnp.zeros_like(v)))
        for h in range(1, HEADS):
            o = o + _dot(pn[h * qb:(h + 1) * qb], jnp.where(head == h, v, jnp.zeros_like(v)))
        lse = _unstack_heads(jnp.broadcast_to(mx + jnp.log(den), (HEADS * qb, D_HEADS)), qb)
        o0[rows, :], o1[rows, :] = o[:, :LANE], o[:, LANE:]
        l0[rows, :], l1[rows, :] = lse[:, :LANE], lse[:, LANE:]
        return k_own, v_own

    def class_group(g, carry):
        classes = [g * n_par + t for t in range(n_par)]

        def step(j, prev):
            return tuple(x for t, r in enumerate(classes) for x in block(r, j, prev[2 * t], prev[2 * t + 1]))

        init = tuple(x for r in classes for x in (kprev[rd, r], vprev[rd, r]))
        last = lax.fori_loop(0, m, step, init, unroll=max(1, min(m, ATTN_PAR // n_par)))
        for t, r in enumerate(classes):
            kprev[wr, r], vprev[wr, r] = last[2 * t], last[2 * t + 1]
        return carry

    lax.fori_loop(0, d // n_par, class_group, 0)


def _attn_bias():
    qi = np.arange(HEADS * Q_BLOCK)[:, None] % Q_BLOCK
    ki = np.arange(2 * Q_BLOCK)[None, :]
    diff = Q_BLOCK + qi - ki
    valid = (diff >= 0) & (diff <= N_STEPS)
    return jnp.asarray(np.where(np.stack([valid & (ki >= Q_BLOCK), valid]), 0.0, -np.inf), F32)


def _attn_prompt(qkv, n, seq, grp):
    d = DILATIONS[grp]
    tb = min(ATTN_TB, seq)
    assert tb % (Q_BLOCK * d) == 0 and seq % tb == 0
    m, nb = tb // (Q_BLOCK * d), seq // tb
    cur = lambda c: pl.BlockSpec((tb, LANE), lambda i, b: (i * nb + b, c))
    qc, kc, vc = (2 * _qkv_col(which, grp) for which in range(3))
    bias = _attn_bias()
    return pl.pallas_call(
        functools.partial(_attn_prompt_kernel, d=d, m=m),
        grid=(n, nb),
        in_specs=[_const_spec(bias.shape), cur(qc), cur(qc + 1), cur(kc), cur(kc + 1), cur(vc), cur(vc + 1)],
        out_specs=[pl.BlockSpec((tb, LANE), lambda i, b: (i * nb + b, 0))] * 4,
        out_shape=[jax.ShapeDtypeStruct((n * seq, LANE), F32)] * 4,
        scratch_shapes=[pltpu.VMEM((2, d, Q_BLOCK, D_HEADS), BF16)] * 2,
        compiler_params=_params(2),
        name=f"attn_prompt_d{d}",
    )(bias, *([qkv] * 6))


def _attn_sample_kernel(q_ref, kn_ref, vn_ref, c_ref, o_ref, lse_ref, *, d, n_new, seq_blk):
    w = c_ref.shape[-1]
    n_rows = HEADS * n_new
    l_b = lax.broadcasted_iota(jnp.int32, (n_rows, w), 0) % n_new
    pos = lax.broadcasted_iota(jnp.int32, (n_rows, w), 1)
    valid_buf = (((w + l_b - pos) & (d - 1)) == 0) & (pos >= l_b)
    dn = (lax.broadcasted_iota(jnp.int32, (n_rows, LANE), 0) % n_new) - lax.broadcasted_iota(jnp.int32, (n_rows, LANE), 1)
    valid_new = (dn >= 0) & ((dn & (d - 1)) == 0)
    pad = jnp.zeros((LANE - n_new, D_HEADS), F32)

    def one_seq(s, carry):
        qs = _stack_heads(q_ref[s])
        k_t, v_t = c_ref[s, 0].astype(BF16), c_ref[s, 1].astype(BF16)
        k_n = jnp.concatenate([kn_ref[s], pad], axis=0).astype(BF16)
        v_n = jnp.concatenate([vn_ref[s], pad], axis=0).astype(BF16)
        s_b = jnp.where(valid_buf, _dot(qs, k_t) * ATTN_SCALE, -jnp.inf)
        s_n = jnp.where(valid_new, _dot_nt(qs, k_n) * ATTN_SCALE, -jnp.inf)
        mx = jnp.maximum(jnp.max(s_b, axis=-1, keepdims=True), jnp.max(s_n, axis=-1, keepdims=True))
        p_b, p_n = jnp.exp(s_b - mx), jnp.exp(s_n - mx)
        den = jnp.sum(p_b, axis=-1, keepdims=True) + jnp.sum(p_n, axis=-1, keepdims=True)
        o = (_dot_nt(p_b.astype(BF16), v_t) + _dot(p_n.astype(BF16), v_n)) / den
        o_ref[s] = _unstack_heads(o, n_new)
        lse_ref[s] = _unstack_heads(jnp.broadcast_to(mx + jnp.log(den), o.shape), n_new)
        return carry

    lax.fori_loop(0, seq_blk, one_seq, 0, unroll=min(seq_blk, ATTN_PAR))


def _attn_sample(qkv, cache, n_seq, n_new, grp):
    d = DILATIONS[grp]
    w = cache.shape[1]
    assert w == N_STEPS * d and n_new <= LANE
    seq_blk = max(2, min(16, (2 * 1024 * 1024) // (w * D_HEADS * 4)))
    qkv3 = qkv.reshape(n_seq, n_new, D_QKV)
    c_t = cache.transpose(0, 2, 3, 4, 1).reshape(n_seq, 2, D_HEADS, w)
    blk = (seq_blk, n_new, D_HEADS)
    o, lse = pl.pallas_call(
        functools.partial(_attn_sample_kernel, d=d, n_new=n_new, seq_blk=seq_blk),
        grid=(n_seq // seq_blk,),
        in_specs=[pl.BlockSpec(blk, lambda i: (i, 0, _qkv_col(0, grp))),
                  pl.BlockSpec(blk, lambda i: (i, 0, _qkv_col(1, grp))),
                  pl.BlockSpec(blk, lambda i: (i, 0, _qkv_col(2, grp))),
                  pl.BlockSpec((seq_blk, 2, D_HEADS, w), lambda i: (i, 0, 0, 0))],
        out_specs=[pl.BlockSpec(blk, lambda i: (i, 0, 0))] * 2,
        out_shape=[jax.ShapeDtypeStruct((n_seq, n_new, D_HEADS), F32)] * 2,
        compiler_params=_params(1),
        name=f"attn_sample_d{d}",
    )(qkv3, qkv3, qkv3, c_t)
    o, lse = o.reshape(n_seq * n_new, D_HEADS), lse.reshape(n_seq * n_new, D_HEADS)
    return o[:, :LANE], o[:, LANE:], lse[:, :LANE], lse[:, LANE:]


def _kv_tail_kernel(k_ref, v_ref, o_ref):
    o_ref[0, 0] = k_ref[...].T
    o_ref[0, 1] = v_ref[...].T


def _kv_tail(qkv, n, seq, grp):
    keep = min(WINDOWS[grp], seq)
    tt = min(keep, 512)
    assert keep % tt == 0 and (seq - keep) % tt == 0
    first = (seq - keep) // tt
    rows = lambda which: pl.BlockSpec((tt, D_HEADS), lambda i, t: (i * (seq // tt) + first + t, _qkv_col(which, grp)))
    out = pl.pallas_call(
        _kv_tail_kernel,
        grid=(n, keep // tt),
        in_specs=[rows(1), rows(2)],
        out_specs=pl.BlockSpec((1, 2, D_HEADS, tt), lambda i, t: (i, 0, 0, t)),
        out_shape=jax.ShapeDtypeStruct((n, 2, D_HEADS, keep), F32),
        compiler_params=_params(2),
        name=f"kv_tail_w{keep}",
    )(qkv, qkv)
    return out.reshape(1, n, 2, HEADS, HEAD_DIM, keep).transpose(0, 1, 5, 2, 3, 4)


def _kv_new_kernel(x_ref, o_ref, *, n_new, n_seq):
    for l in range(n_new):
        o_ref[0, l, 0] = x_ref[pl.ds(l, n_seq, stride=n_new), :].T


def _kv_new(qkv, n_seq, n_new):
    halves = D_HEADS // LANE
    out = pl.pallas_call(
        functools.partial(_kv_new_kernel, n_new=n_new, n_seq=n_seq),
        grid=(N_GROUPS, 2, halves),
        in_specs=[pl.BlockSpec((n_seq * n_new, LANE), lambda g, kv, c: (0, halves * _qkv_col(1 + kv, g) + c))],
        out_specs=pl.BlockSpec((1, n_new, 1, LANE, n_seq), lambda g, kv, c: (g, 0, kv, c, 0)),
        out_shape=jax.ShapeDtypeStruct((N_GROUPS, n_new, 2, D_HEADS, n_seq), F32),
        compiler_params=_params(3),
        name="kv_new",
    )(qkv)
    out = out.reshape(N_GROUPS, 1, n_new, 2, HEADS, HEAD_DIM, n_seq).transpose(0, 1, 6, 2, 3, 4, 5)
    return [out[g] for g in range(N_GROUPS)]


def _gelu_tanh(x):
    return 0.5 * x * (1.0 + jnp.tanh(math.sqrt(2.0 / math.pi) * (x + 0.044715 * (x * x * x))))


def _stage3_kernel(*refs):
    x1_ref, yp_ref = refs[:2]
    attn = refs[2:2 + 4 * N_GROUPS]
    (p_ref, gm_ref, wgate_ref, wglu_ref, wbs_ref, wba_ref, wout_ref, g2_ref, wg_ref, wu_ref, wd_ref,
     gp_ref, wpg_ref, wpp_ref, gf_ref, y_ref) = refs[2 + 4 * N_GROUPS:]
    x1 = x1_ref[...]
    gates = jax.nn.sigmoid(_dot(_rms(x1, gm_ref[...]).astype(BF16), wgate_ref[...]))
    y = _gelu_tanh(yp_ref[...])
    glu = y * jax.nn.sigmoid(_dot(y.astype(BF16), wglu_ref[...]))
    full = lambda lo, hi: jnp.concatenate([lo[...], hi[...]], axis=1)
    outs = [full(attn[4 * g], attn[4 * g + 1]) for g in range(N_GROUPS)]
    lses = [full(attn[4 * g + 2], attn[4 * g + 3]) for g in range(N_GROUPS)]
    mx = jnp.maximum(jnp.maximum(lses[0], lses[1]), lses[2])
    es = [jnp.exp(l - mx) for l in lses]
    y_attn = (es[0] * outs[0] + es[1] * outs[1] + es[2] * outs[2]) / (es[0] + es[1] + es[2])
    merged = (gates[:, :D_MODEL] * _dot(glu.astype(BF16), wbs_ref[...])
              + gates[:, D_MODEL:] * _dot(y_attn.astype(BF16), wba_ref[...]))
    x = x1 + _dot(merged.astype(BF16), wout_ref[...])
    x = x + 0.5 * _swiglu(_rms(x, g2_ref[...]).astype(BF16), wg_ref, wu_ref, wd_ref)
    gate = jax.nn.sigmoid(_dot(_rms(x, gp_ref[...]).astype(BF16), wpg_ref[...]))
    x = x + gate * _dot(p_ref[...].astype(BF16), wpp_ref[...])
    y_ref[...] = _rms(x, gf_ref[...])


def _stage3(tok_inputs, weights):
    t, tm = tok_inputs[0].shape[0], STAGE_TM
    tok = lambda a: pl.BlockSpec((tm, a.shape[1]), lambda i: (i, 0))
    return pl.pallas_call(
        _stage3_kernel,
        grid=(t // tm,),
        in_specs=[tok(a) for a in tok_inputs] + [_const_spec(w.shape) for w in weights],
        out_specs=pl.BlockSpec((tm, D_MODEL), lambda i: (i, 0)),
        out_shape=jax.ShapeDtypeStruct((t, D_MODEL), F32),
        compiler_params=_params(1),
        name="stage3",
    )(*tok_inputs, *weights)


def kernel(x_prompt, x_sample, p_prompt, p_sample, cache_kv_w128, cache_kv_w512, cache_kv_w2048, state_ssm,
           g_ffn1, ffn1_w_gate, ffn1_w_up, ffn1_w_down, g_mix, w_in, ssm_a_re, ssm_a_im, ssm_log_dt,
           ssm_b_re, ssm_b_im, ssm_c_re, ssm_c_im, ssm_d, ssm_w_glu, w_br_ssm, w_br_attn, w_out,
           g_ffn2, ffn2_w_gate, ffn2_w_up, ffn2_w_down, g_ple, w_ple_gate, w_ple_proj, g_final):
    assert x_prompt.shape[-1] == D_MODEL and g_ffn1.shape[0] == 1
    n_p, seq, _ = x_prompt.shape
    n_s, n_new, _ = x_sample.shape
    caches = (cache_kv_w128, cache_kv_w512, cache_kv_w2048)
    row = lambda g: g.reshape(1, -1)
    bf = lambda w: w[0].astype(BF16)

    wi = w_in[0]
    win = wi[:, :D_SSM + D_QKV].astype(BF16)
    w_gates = wi[:, D_SSM + D_QKV:].astype(BF16)
    s1_w = (row(g_ffn1[0]), bf(ffn1_w_gate), bf(ffn1_w_up), bf(ffn1_w_down), row(g_mix[0]), win)
    s3_w = (row(g_mix[0]), w_gates, bf(ssm_w_glu), bf(w_br_ssm), bf(w_br_attn), bf(w_out), row(g_ffn2[0]),
            bf(ffn2_w_gate), bf(ffn2_w_up), bf(ffn2_w_down), row(g_ple[0]), bf(w_ple_gate),
            bf(w_ple_proj), row(g_final))

    n_chunks = seq // CHUNK
    tabs_p, tabs_s = _ssm_tables(
        ssm_a_re[0], ssm_a_im[0], ssm_log_dt[0], ssm_b_re[0], ssm_b_im[0], ssm_c_re[0], ssm_c_im[0],
        ssm_d[0], int(math.log2(n_chunks)), n_new)
    tile_state = (-1, N_SSM_TILES, 2, SSM_TILE_GROUPS, SSM_STATE)
    from_tiles = lambda h: h.reshape(tile_state).transpose(0, 1, 3, 4, 2).reshape(1, -1, N_SSM_GROUPS, SSM_STATE, 2)

    x1, u, qkv = _stage1(x_prompt.reshape(n_p * seq, D_MODEL), *s1_w)
    y_pre, h_last = _ssm_prompt(u, tabs_p, n_p, seq)
    attn = [a for grp in range(N_GROUPS) for a in _attn_prompt(qkv, n_p, seq, grp)]
    tok = (x1, y_pre, *attn, p_prompt[0].reshape(n_p * seq, D_PLE))
    y_prompt = _stage3(tok, s3_w).reshape(n_p, seq, D_MODEL)
    kv_prompt = [_kv_tail(qkv, n_p, seq, grp) for grp in range(N_GROUPS)]
    ssm_prompt = from_tiles(h_last)

    x1, u, qkv = _stage1(x_sample.reshape(n_s * n_new, D_MODEL), *s1_w)
    h0 = state_ssm[0].reshape(n_s, N_SSM_TILES, SSM_TILE_GROUPS, SSM_STATE, 2)
    h0 = h0.transpose(0, 1, 4, 2, 3).reshape(n_s, N_SSM_TILES * TILE_STATE_W)
    y_pre, h_new = _ssm_sample(u, h0, tabs_s, n_s, n_new)
    attn = [a for grp in range(N_GROUPS)
            for a in _attn_sample(qkv, caches[grp][0], n_s, n_new, grp)]
    tok = (x1, y_pre, *attn, p_sample[0].reshape(n_s * n_new, D_PLE))
    y_sample = _stage3(tok, s3_w).reshape(n_s, n_new, D_MODEL)
    kv_sample = _kv_new(qkv, n_s, n_new)
    ssm_sample = from_tiles(h_new)

    return (y_prompt, y_sample, kv_prompt[0], kv_prompt[1], kv_prompt[2], ssm_prompt,
            kv_sample[0], kv_sample[1], kv_sample[2], ssm_sample)
```

```python
import functools
import math

import jax
import jax.numpy as jnp
import numpy as np
from jax import lax
from jax.experimental import pallas as pl
from jax.experimental.pallas import tpu as pltpu

D_MODEL = 1024
D_PLE = 256
D_FF = 2816
SSM_GROUP = 16
SSM_STATE = 64
D_SSM = 512
N_SSM_GROUPS = D_SSM // SSM_GROUP
HEAD_DIM = 64
HEADS = 4
WINDOWS = (128, 512, 2048)
DILATIONS = (1, 4, 16)
N_GROUPS = 3
D_HEADS = HEADS * HEAD_DIM
N_STEPS = 128
ATTN_SCALE = HEAD_DIM ** -0.5
EPS = 1e-6

LANE = 128
V7X_VMEM_LIMIT_BYTES = 56 * 1024 * 1024
STAGE_TM = 512

CHUNK = 16
N_PAIRS = CHUNK // 2
PAIR_W = 2 * LANE
SSM_TILE_GROUPS = LANE // SSM_GROUP
N_SSM_TILES = N_SSM_GROUPS // SSM_TILE_GROUPS
TILE_STATE_W = SSM_TILE_GROUPS * 2 * SSM_STATE
Q_BLOCK = 128
ATTN_TB = 2048
ATTN_PAR = 4

D_QKV = 3 * N_GROUPS * D_HEADS


def _qkv_col(which, grp):
    return which * N_GROUPS + grp


BF16 = jnp.bfloat16
F32 = jnp.float32


def _dot(a, b):
    return jnp.dot(a, b, preferred_element_type=F32)


def _dot_nt(a, b):
    return lax.dot_general(a, b, (((1,), (1,)), ((), ())), preferred_element_type=F32)


def _rms(x, g):
    return x * lax.rsqrt(jnp.mean(x * x, axis=-1, keepdims=True) + EPS) * g


def _swiglu(xn, wg_ref, wu_ref, wd_ref):
    gate = _dot(xn, wg_ref[...])
    up = _dot(xn, wu_ref[...])
    act = (gate * jax.nn.sigmoid(gate) * up).astype(BF16)
    return _dot(act, wd_ref[...])


def _const_spec(shape):
    nd = len(shape)
    return pl.BlockSpec(shape, lambda *_: (0,) * nd, pipeline_mode=pl.Buffered(1))


def _params(n_grid_dims):
    return pltpu.CompilerParams(dimension_semantics=("arbitrary",) * n_grid_dims,
                                vmem_limit_bytes=V7X_VMEM_LIMIT_BYTES)


def _stage1_kernel(*refs, n_cast):
    x_ref, g1_ref, wg_ref, wu_ref, wd_ref, gm_ref, win_ref = refs[:7]
    cast_in = refs[7:7 + n_cast]
    x1_ref, u_ref, qkv_ref = refs[7 + n_cast:10 + n_cast]
    cast_out = refs[10 + n_cast:]
    x = x_ref[...]
    xn = _rms(x, g1_ref[...]).astype(BF16)
    x1 = x + 0.5 * _swiglu(xn, wg_ref, wu_ref, wd_ref)
    x1_ref[...] = x1
    h = _rms(x1, gm_ref[...]).astype(BF16)
    z = _dot(h, win_ref[...])
    u_ref[...] = z[:, :D_SSM]
    qkv_ref[...] = z[:, D_SSM:]
    for src, dst in zip(cast_in, cast_out):
        dst[...] = src[...].astype(BF16)


def _cast_block_spec(rows, cols, n_steps):
    steps = max(s for s in range(1, n_steps + 1) if rows % s == 0 and (rows // s) % 16 == 0)
    return pl.BlockSpec((rows // steps, cols), lambda i: (jnp.minimum(i, steps - 1), 0))


def _stage1(x, g1, wg, wu, wd, gm, win, cast=()):
    t, tm = x.shape[0], STAGE_TM
    tok = lambda w: pl.BlockSpec((tm, w), lambda i: (i, 0))
    widths = (D_MODEL, D_SSM, D_QKV)
    cast_specs = [_cast_block_spec(*w.shape, t // tm) for w in cast]
    return pl.pallas_call(
        functools.partial(_stage1_kernel, n_cast=len(cast)),
        grid=(t // tm,),
        in_specs=[tok(D_MODEL), _const_spec(g1.shape), _const_spec(wg.shape), _const_spec(wu.shape),
                  _const_spec(wd.shape), _const_spec(gm.shape), _const_spec(win.shape)] + cast_specs,
        out_specs=[tok(w) for w in widths] + cast_specs,
        out_shape=[jax.ShapeDtypeStruct((t, w), F32) for w in widths]
        + [jax.ShapeDtypeStruct(w.shape, BF16) for w in cast],
        compiler_params=_params(1),
        name="stage1",
    )(x, g1, wg, wu, wd, gm, win, *cast)


def _swap_halves(x):
    half = x.shape[1] // 2
    return jnp.concatenate([x[:, half:], x[:, :half]], axis=1)


def _cmul_split(x, a1, a2):
    return x * a1 + _swap_halves(x) * a2


def _shift_rows(x, sh):
    if sh % 8 == 0:
        return jnp.concatenate([jnp.zeros((sh, x.shape[1]), x.dtype), x[:x.shape[0] - sh]], axis=0)
    row = lax.broadcasted_iota(jnp.int32, x.shape, 0)
    return jnp.where(row >= sh, pltpu.roll(x, sh, axis=0), 0.0)


def _ssm_kernel(*refs, n_t, n_rows, has_h0):
    if has_h0:
        u_ref, h0_ref, kp_ref, wp_ref, vpt_ref, a1_ref, a2_ref, y_ref, hl_ref = refs
    else:
        u_ref, kp_ref, wp_ref, vpt_ref, a1_ref, a2_ref, y_ref, hl_ref = refs
    n_pairs = n_t // 2
    tok = lambda t: pl.ds(t, n_rows, stride=n_t)
    up = [jnp.concatenate([u_ref[tok(2 * a), :], u_ref[tok(2 * a + 1), :]], axis=1).astype(BF16)
          for a in range(n_pairs)]
    w_row0 = (CHUNK - n_t) * LANE
    s = _dot(jnp.concatenate(up, axis=1), wp_ref[0, w_row0:, :])
    if has_h0:
        hprev = h0_ref[...]
        h_last = _cmul_split(hprev, a1_ref[0], a2_ref[0]) + s
        hl_ref[...] = h_last
    else:
        h = s
        for lvl in range(int(math.log2(n_rows))):
            h = h + _cmul_split(_shift_rows(h, 1 << lvl), a1_ref[0, lvl:lvl + 1, :], a2_ref[0, lvl:lvl + 1, :])
        hprev = _shift_rows(h, 1)
        hl_ref[0, 0] = h[n_rows - 1:n_rows, :]
    hpb = hprev.astype(BF16)
    for b in range(n_pairs):
        acc = _dot_nt(hpb, vpt_ref[0, b])
        for a in range(b + 1):
            acc = acc + _dot(up[a], kp_ref[0, b - a])
        y_ref[tok(2 * b), :] = acc[:, :LANE]
        y_ref[tok(2 * b + 1), :] = acc[:, LANE:]


def _ssm_table_specs(tabs, idx):
    kp, wp, vp, a1, a2 = tabs
    return [pl.BlockSpec((1,) + kp.shape[1:], lambda *g: (idx(*g), 0, 0, 0)),
            pl.BlockSpec((1,) + wp.shape[1:], lambda *g: (idx(*g), 0, 0)),
            pl.BlockSpec((1,) + vp.shape[1:], lambda *g: (idx(*g), 0, 0, 0)),
            pl.BlockSpec((1,) + a1.shape[1:], lambda *g: (idx(*g), 0, 0)),
            pl.BlockSpec((1,) + a2.shape[1:], lambda *g: (idx(*g), 0, 0))]


def _ssm_prompt(u, tabs, n, seq):
    n_rows = seq // CHUNK
    tile = pl.BlockSpec((seq, LANE), lambda j, i: (i, j))
    return pl.pallas_call(
        functools.partial(_ssm_kernel, n_t=CHUNK, n_rows=n_rows, has_h0=False),
        grid=(N_SSM_TILES, n),
        in_specs=[tile] + _ssm_table_specs(tabs, lambda j, i: j),
        out_specs=[tile, pl.BlockSpec((1, 1, 1, TILE_STATE_W), lambda j, i: (i, j, 0, 0))],
        out_shape=[jax.ShapeDtypeStruct((n * seq, D_SSM), F32),
                   jax.ShapeDtypeStruct((n, N_SSM_TILES, 1, TILE_STATE_W), F32)],
        compiler_params=_params(2),
        name="ssm_prompt",
    )(u, *tabs)


def _ssm_sample(u, h0, tabs, n_seq, n_new):
    tile = pl.BlockSpec((n_seq * n_new, LANE), lambda j: (0, j))
    state = pl.BlockSpec((n_seq, TILE_STATE_W), lambda j: (0, j))
    return pl.pallas_call(
        functools.partial(_ssm_kernel, n_t=n_new, n_rows=n_seq, has_h0=True),
        grid=(N_SSM_TILES,),
        in_specs=[tile, state] + _ssm_table_specs(tabs, lambda j: j),
        out_specs=[tile, state],
        out_shape=[jax.ShapeDtypeStruct((n_seq * n_new, D_SSM), F32),
                   jax.ShapeDtypeStruct((n_seq, N_SSM_TILES * TILE_STATE_W), F32)],
        compiler_params=_params(1),
        name="ssm_sample",
    )(u, h0, *tabs)


def _ssm_prep_kernel(are_ref, aim_ref, ldt_ref, bre_ref, bim_ref, cre_ref, cim_ref, d_ref,
                     kp_ref, wp_ref, vpt_ref, a1_ref, a2_ref, a1s_ref, a2s_ref, *, n_levels, n_new):
    a_re, a_im = are_ref[0], aim_ref[0]
    dt = jnp.exp(ldt_ref[0])
    mag = jnp.exp(a_re * dt)
    ab_re, ab_im = mag * jnp.cos(a_im * dt), mag * jnp.sin(a_im * dt)
    den = a_re * a_re + a_im * a_im
    nr, ni = ab_re - 1.0, ab_im
    f_re = (nr * a_re + ni * a_im) / den
    f_im = (ni * a_re - nr * a_im) / den
    shape = bre_ref.shape[1:]
    same_group = (lax.broadcasted_iota(jnp.int32, shape, 0) // SSM_GROUP
                  == lax.broadcasted_iota(jnp.int32, shape, 1) // SSM_STATE)
    b_re, b_im = bre_ref[0], bim_ref[0]
    x_re = jnp.where(same_group, f_re * b_re - f_im * b_im, 0.0)
    x_im = jnp.where(same_group, f_re * b_im + f_im * b_re, 0.0)
    c_re = jnp.where(same_group, cre_ref[0], 0.0)
    c_im = jnp.where(same_group, cim_ref[0], 0.0)
    pw = [(jnp.ones_like(ab_re), jnp.zeros_like(ab_im))]
    for _ in range(CHUNK):
        r, i = pw[-1]
        pw.append((r * ab_re - i * ab_im, r * ab_im + i * ab_re))
    cat = lambda r, i: jnp.concatenate([r, i], axis=1)
    e_pack = [cat(c_re * r - c_im * i, -(c_re * i + c_im * r)) for r, i in pw]
    split = lambda a: (a.astype(BF16), (a - a.astype(BF16).astype(F32)).astype(BF16))
    x_hi, x_lo = split(cat(x_re, x_im))
    x_3 = jnp.concatenate([x_hi, x_hi, x_lo], axis=1)
    k_lag = []
    for k in range(CHUNK):
        e_hi, e_lo = split(e_pack[k])
        k_lag.append(_dot_nt(x_3, jnp.concatenate([e_hi, e_lo, e_hi], axis=1)))
    diag = (lax.broadcasted_iota(jnp.int32, (LANE, LANE), 0) == lax.broadcasted_iota(jnp.int32, (LANE, LANE), 1))
    k_lag[0] = k_lag[0] + jnp.where(diag, d_ref[0], 0.0)
    zero = jnp.zeros((LANE, LANE), F32)
    for dl in range(N_PAIRS):
        top = cat(k_lag[2 * dl], k_lag[2 * dl + 1])
        bot = cat(k_lag[2 * dl - 1] if dl > 0 else zero, k_lag[2 * dl])
        kp_ref[0, dl] = jnp.concatenate([top, bot], axis=0).astype(BF16)
    for t in range(CHUNK):
        r, i = pw[CHUNK - 1 - t]
        wp_ref[0, t * LANE:(t + 1) * LANE, :] = cat(x_re * r - x_im * i, x_re * i + x_im * r).astype(BF16)
    for t in range(CHUNK):
        vpt_ref[0, t // 2, (t % 2) * LANE:(t % 2 + 1) * LANE, :] = e_pack[t + 1].astype(BF16)
    r, i = pw[CHUNK]
    for lvl in range(n_levels):
        a1_ref[0, lvl:lvl + 1, :], a2_ref[0, lvl:lvl + 1, :] = cat(r, r), cat(-i, i)
        r, i = r * r - i * i, 2.0 * r * i
    r, i = pw[n_new]
    a1s_ref[0], a2s_ref[0] = cat(r, r), cat(-i, i)


def _ssm_tables(a_re, a_im, log_dt, b_re, b_im, c_re, c_im, d_skip, n_levels, n_new):
    nj, tg = N_SSM_TILES, SSM_TILE_GROUPS
    half = tg * SSM_STATE
    rowv = lambda v: v.reshape(nj, 1, half)
    tiled = lambda m: jnp.tile(m.reshape(nj, LANE, SSM_STATE), (1, 1, tg))
    ins = (rowv(a_re), rowv(a_im), rowv(jnp.repeat(log_dt, SSM_STATE)),
           tiled(b_re.transpose(0, 2, 1)), tiled(b_im.transpose(0, 2, 1)), tiled(c_re), tiled(c_im),
           d_skip.reshape(nj, 1, LANE))
    shapes = ((nj, N_PAIRS, PAIR_W, PAIR_W), (nj, CHUNK * LANE, TILE_STATE_W), (nj, N_PAIRS, PAIR_W, TILE_STATE_W),
              (nj, n_levels, TILE_STATE_W), (nj, n_levels, TILE_STATE_W), (nj, 1, TILE_STATE_W), (nj, 1, TILE_STATE_W))
    dtypes = (BF16, BF16, BF16, F32, F32, F32, F32)
    per_tile = lambda s: pl.BlockSpec((1,) + tuple(s[1:]), lambda j: (j,) + (0,) * (len(s) - 1))
    kp, wp, vpt, a1, a2, a1s, a2s = pl.pallas_call(
        functools.partial(_ssm_prep_kernel, n_levels=n_levels, n_new=n_new),
        grid=(nj,),
        in_specs=[per_tile(x.shape) for x in ins],
        out_specs=[per_tile(s) for s in shapes],
        out_shape=[jax.ShapeDtypeStruct(s, dt) for s, dt in zip(shapes, dtypes)],
        compiler_params=_params(1),
        name="ssm_prep",
    )(*ins)
    return (kp, wp, vpt, a1, a2), (kp, wp, vpt, a1s, a2s)


def _head_of_lane(width):
    return lax.broadcasted_iota(jnp.int32, (1, width), 1) // HEAD_DIM


def _stack_heads(q):
    head = _head_of_lane(q.shape[1])
    q = q.astype(F32)
    return jnp.concatenate([jnp.where(head == h, q, 0.0) for h in range(HEADS)], axis=0).astype(BF16)


def _unstack_heads(x, rows):
    head = _head_of_lane(x.shape[1])
    out = x[(HEADS - 1) * rows:]
    for h in range(HEADS - 2, -1, -1):
        out = jnp.where(head == h, x[h * rows:(h + 1) * rows], out)
    return out


def _attn_prompt_kernel(bias_ref, q0, q1, k0, k1, v0, v1, o0, o1, l0, l1, kprev, vprev, *, d, m):
    qb = Q_BLOCK
    b = pl.program_id(1)
    n_par = min(d, ATTN_PAR)
    rd, wr = b % 2, (b + 1) % 2
    head = _head_of_lane(D_HEADS)

    @pl.when(b == 0)
    def _():
        kprev[0] = jnp.zeros(kprev.shape[1:], kprev.dtype)
        vprev[0] = jnp.zeros(vprev.shape[1:], vprev.dtype)

    def block(r, j, k_prev, v_prev):
        start = r + j * qb * d
        rows = pl.ds(start, qb, stride=d) if d > 1 else pl.ds(pl.multiple_of(start, qb), qb)
        both = lambda lo, hi: jnp.concatenate([lo[rows, :], hi[rows, :]], axis=1)
        qs = _stack_heads(both(q0, q1) * ATTN_SCALE)
        k_own, v_own = both(k0, k1).astype(BF16), both(v0, v1).astype(BF16)
        k = jnp.concatenate([k_prev, k_own], axis=0)
        v = jnp.concatenate([v_prev, v_own], axis=0)
        s = _dot_nt(qs, k) + bias_ref[jnp.minimum(b * m + j, 1)]
        mx = jnp.max(s, axis=-1, keepdims=True)
        pe = jnp.exp(s - mx)
        den = jnp.sum(pe, axis=-1, keepdims=True)
        pn = (pe * (1.0 / den)).astype(BF16)
        o = _dot(pn[:qb], jnp.where(head == 0, v, jnp.zeros_like(v)))
        for h in range(1, HEADS):
            o = o + _dot(pn[h * qb:(h + 1) * qb], jnp.where(head == h, v, jnp.zeros_like(v)))
        lse = _unstack_heads(jnp.broadcast_to(mx + jnp.log(den), (HEADS * qb, D_HEADS)), qb)
        o0[rows, :], o1[rows, :] = o[:, :LANE], o[:, LANE:]
        l0[rows, :], l1[rows, :] = lse[:, :LANE], lse[:, LANE:]
        return k_own, v_own

    def class_group(g, carry):
        classes = [g * n_par + t for t in range(n_par)]

        def step(j, prev):
            return tuple(x for t, r in enumerate(classes) for x in block(r, j, prev[2 * t], prev[2 * t + 1]))

        init = tuple(x for r in classes for x in (kprev[rd, r], vprev[rd, r]))
        last = lax.fori_loop(0, m, step, init, unroll=max(1, min(m, ATTN_PAR // n_par)))
        for t, r in enumerate(classes):
            kprev[wr, r], vprev[wr, r] = last[2 * t], last[2 * t + 1]
        return carry

    lax.fori_loop(0, d // n_par, class_group, 0)


def _attn_bias():
    qi = np.arange(HEADS * Q_BLOCK)[:, None] % Q_BLOCK
    ki = np.arange(2 * Q_BLOCK)[None, :]
    diff = Q_BLOCK + qi - ki
    valid = (diff >= 0) & (diff <= N_STEPS)
    return jnp.asarray(np.where(np.stack([valid & (ki >= Q_BLOCK), valid]), 0.0, -np.inf), F32)


def _attn_prompt(qkv, n, seq, grp):
    d = DILATIONS[grp]
    tb = min(ATTN_TB, seq)
    assert tb % (Q_BLOCK * d) == 0 and seq % tb == 0
    m, nb = tb // (Q_BLOCK * d), seq // tb
    cur = lambda c: pl.BlockSpec((tb, LANE), lambda i, b: (i * nb + b, c))
    qc, kc, vc = (2 * _qkv_col(which, grp) for which in range(3))
    bias = _attn_bias()
    return pl.pallas_call(
        functools.partial(_attn_prompt_kernel, d=d, m=m),
        grid=(n, nb),
        in_specs=[_const_spec(bias.shape), cur(qc), cur(qc + 1), cur(kc), cur(kc + 1), cur(vc), cur(vc + 1)],
        out_specs=[pl.BlockSpec((tb, LANE), lambda i, b: (i * nb + b, 0))] * 4,
        out_shape=[jax.ShapeDtypeStruct((n * seq, LANE), F32)] * 4,
        scratch_shapes=[pltpu.VMEM((2, d, Q_BLOCK, D_HEADS), BF16)] * 2,
        compiler_params=_params(2),
        name=f"attn_prompt_d{d}",
    )(bias, *([qkv] * 6))


def _attn_sample_kernel(q_ref, kn_ref, vn_ref, c_ref, o_ref, lse_ref, *, d, n_new, seq_blk):
    w = c_ref.shape[-1]
    n_rows = HEADS * n_new
    l_b = lax.broadcasted_iota(jnp.int32, (n_rows, w), 0) % n_new
    pos = lax.broadcasted_iota(jnp.int32, (n_rows, w), 1)
    valid_buf = (((w + l_b - pos) & (d - 1)) == 0) & (pos >= l_b)
    dn = (lax.broadcasted_iota(jnp.int32, (n_rows, LANE), 0) % n_new) - lax.broadcasted_iota(jnp.int32, (n_rows, LANE), 1)
    valid_new = (dn >= 0) & ((dn & (d - 1)) == 0)
    pad = jnp.zeros((LANE - n_new, D_HEADS), F32)

    def one_seq(s, carry):
        qs = _stack_heads(q_ref[s])
        k_t, v_t = c_ref[s, 0].astype(BF16), c_ref[s, 1].astype(BF16)
        k_n = jnp.concatenate([kn_ref[s], pad], axis=0).astype(BF16)
        v_n = jnp.concatenate([vn_ref[s], pad], axis=0).astype(BF16)
        s_b = jnp.where(valid_buf, _dot(qs, k_t) * ATTN_SCALE, -jnp.inf)
        s_n = jnp.where(valid_new, _dot_nt(qs, k_n) * ATTN_SCALE, -jnp.inf)
        mx = jnp.maximum(jnp.max(s_b, axis=-1, keepdims=True), jnp.max(s_n, axis=-1, keepdims=True))
        p_b, p_n = jnp.exp(s_b - mx), jnp.exp(s_n - mx)
        den = jnp.sum(p_b, axis=-1, keepdims=True) + jnp.sum(p_n, axis=-1, keepdims=True)
        o = (_dot_nt(p_b.astype(BF16), v_t) + _dot(p_n.astype(BF16), v_n)) / den
        o_ref[s] = _unstack_heads(o, n_new)
        lse_ref[s] = _unstack_heads(jnp.broadcast_to(mx + jnp.log(den), o.shape), n_new)
        return carry

    lax.fori_loop(0, seq_blk, one_seq, 0, unroll=min(seq_blk, ATTN_PAR))


def _attn_sample(qkv, cache, n_seq, n_new, grp):
    d = DILATIONS[grp]
    w = cache.shape[1]
    assert w == N_STEPS * d and n_new <= LANE
    seq_blk = max(2, min(16, (2 * 1024 * 1024) // (w * D_HEADS * 4)))
    qkv3 = qkv.reshape(n_seq, n_new, D_QKV)
    c_t = cache.transpose(0, 2, 3, 4, 1).reshape(n_seq, 2, D_HEADS, w)
    blk = (seq_blk, n_new, D_HEADS)
    o, lse = pl.pallas_call(
        functools.partial(_attn_sample_kernel, d=d, n_new=n_new, seq_blk=seq_blk),
        grid=(n_seq // seq_blk,),
        in_specs=[pl.BlockSpec(blk, lambda i: (i, 0, _qkv_col(0, grp))),
                  pl.BlockSpec(blk, lambda i: (i, 0, _qkv_col(1, grp))),
                  pl.BlockSpec(blk, lambda i: (i, 0, _qkv_col(2, grp))),
                  pl.BlockSpec((seq_blk, 2, D_HEADS, w), lambda i: (i, 0, 0, 0))],
        out_specs=[pl.BlockSpec(blk, lambda i: (i, 0, 0))] * 2,
        out_shape=[jax.ShapeDtypeStruct((n_seq, n_new, D_HEADS), F32)] * 2,
        compiler_params=_params(1),
        name=f"attn_sample_d{d}",
    )(qkv3, qkv3, qkv3, c_t)
    o, lse = o.reshape(n_seq * n_new, D_HEADS), lse.reshape(n_seq * n_new, D_HEADS)
    return o[:, :LANE], o[:, LANE:], lse[:, :LANE], lse[:, LANE:]


def _kv_tail_kernel(k_ref, v_ref, o_ref):
    o_ref[0, 0] = k_ref[...].T
    o_ref[0, 1] = v_ref[...].T


def _kv_tail(qkv, n, seq, grp):
    keep = min(WINDOWS[grp], seq)
    tt = min(keep, 512)
    assert keep % tt == 0 and (seq - keep) % tt == 0
    first = (seq - keep) // tt
    rows = lambda which: pl.BlockSpec((tt, D_HEADS), lambda i, t: (i * (seq // tt) + first + t, _qkv_col(which, grp)))
    out = pl.pallas_call(
        _kv_tail_kernel,
        grid=(n, keep // tt),
        in_specs=[rows(1), rows(2)],
        out_specs=pl.BlockSpec((1, 2, D_HEADS, tt), lambda i, t: (i, 0, 0, t)),
        out_shape=jax.ShapeDtypeStruct((n, 2, D_HEADS, keep), F32),
        compiler_params=_params(2),
        name=f"kv_tail_w{keep}",
    )(qkv, qkv)
    return out.reshape(1, n, 2, HEADS, HEAD_DIM, keep).transpose(0, 1, 5, 2, 3, 4)


def _kv_new_kernel(x_ref, o_ref, *, n_new, n_seq):
    for l in range(n_new):
        o_ref[0, l, 0] = x_ref[pl.ds(l, n_seq, stride=n_new), :].T


def _kv_new(qkv, n_seq, n_new):
    halves = D_HEADS // LANE
    out = pl.pallas_call(
        functools.partial(_kv_new_kernel, n_new=n_new, n_seq=n_seq),
        grid=(N_GROUPS, 2, halves),
        in_specs=[pl.BlockSpec((n_seq * n_new, LANE), lambda g, kv, c: (0, halves * _qkv_col(1 + kv, g) + c))],
        out_specs=pl.BlockSpec((1, n_new, 1, LANE, n_seq), lambda g, kv, c: (g, 0, kv, c, 0)),
        out_shape=jax.ShapeDtypeStruct((N_GROUPS, n_new, 2, D_HEADS, n_seq), F32),
        compiler_params=_params(3),
        name="kv_new",
    )(qkv)
    out = out.reshape(N_GROUPS, 1, n_new, 2, HEADS, HEAD_DIM, n_seq).transpose(0, 1, 6, 2, 3, 4, 5)
    return [out[g] for g in range(N_GROUPS)]


def _gelu_tanh(x):
    return 0.5 * x * (1.0 + jnp.tanh(math.sqrt(2.0 / math.pi) * (x + 0.044715 * (x * x * x))))


def _stage3_kernel(*refs):
    x1_ref, yp_ref = refs[:2]
    attn = refs[2:2 + 4 * N_GROUPS]
    (p_ref, gm_ref, wgate_ref, wglu_ref, wbs_ref, wba_ref, wout_ref, g2_ref, wg_ref, wu_ref, wd_ref,
     gp_ref, wpg_ref, wpp_ref, gf_ref, y_ref) = refs[2 + 4 * N_GROUPS:]
    x1 = x1_ref[...]
    gates = jax.nn.sigmoid(_dot(_rms(x1, gm_ref[...]).astype(BF16), wgate_ref[...]))
    y = _gelu_tanh(yp_ref[...])
    glu = y * jax.nn.sigmoid(_dot(y.astype(BF16), wglu_ref[...]))
    full = lambda lo, hi: jnp.concatenate([lo[...], hi[...]], axis=1)
    outs = [full(attn[4 * g], attn[4 * g + 1]) for g in range(N_GROUPS)]
    lses = [full(attn[4 * g + 2], attn[4 * g + 3]) for g in range(N_GROUPS)]
    mx = jnp.maximum(jnp.maximum(lses[0], lses[1]), lses[2])
    es = [jnp.exp(l - mx) for l in lses]
    y_attn = (es[0] * outs[0] + es[1] * outs[1] + es[2] * outs[2]) / (es[0] + es[1] + es[2])
    merged = (gates[:, :D_MODEL] * _dot(glu.astype(BF16), wbs_ref[...])
              + gates[:, D_MODEL:] * _dot(y_attn.astype(BF16), wba_ref[...]))
    x = x1 + _dot(merged.astype(BF16), wout_ref[...])
    x = x + 0.5 * _swiglu(_rms(x, g2_ref[...]).astype(BF16), wg_ref, wu_ref, wd_ref)
    gate = jax.nn.sigmoid(_dot(_rms(x, gp_ref[...]).astype(BF16), wpg_ref[...]))
    x = x + gate * _dot(p_ref[...].astype(BF16), wpp_ref[...])
    y_ref[...] = _rms(x, gf_ref[...])


def _stage3(tok_inputs, weights):
    t, tm = tok_inputs[0].shape[0], STAGE_TM
    tok = lambda a: pl.BlockSpec((tm, a.shape[1]), lambda i: (i, 0))
    return pl.pallas_call(
        _stage3_kernel,
        grid=(t // tm,),
        in_specs=[tok(a) for a in tok_inputs] + [_const_spec(w.shape) for w in weights],
        out_specs=pl.BlockSpec((tm, D_MODEL), lambda i: (i, 0)),
        out_shape=jax.ShapeDtypeStruct((t, D_MODEL), F32),
        compiler_params=_params(1),
        name="stage3",
    )(*tok_inputs, *weights)


def kernel(x_prompt, x_sample, p_prompt, p_sample, cache_kv_w128, cache_kv_w512, cache_kv_w2048, state_ssm,
           g_ffn1, ffn1_w_gate, ffn1_w_up, ffn1_w_down, g_mix, w_in, ssm_a_re, ssm_a_im, ssm_log_dt,
           ssm_b_re, ssm_b_im, ssm_c_re, ssm_c_im, ssm_d, ssm_w_glu, w_br_ssm, w_br_attn, w_out,
           g_ffn2, ffn2_w_gate, ffn2_w_up, ffn2_w_down, g_ple, w_ple_gate, w_ple_proj, g_final):
    assert x_prompt.shape[-1] == D_MODEL and g_ffn1.shape[0] == 1
    n_p, seq, _ = x_prompt.shape
    n_s, n_new, _ = x_sample.shape
    caches = (cache_kv_w128, cache_kv_w512, cache_kv_w2048)
    row = lambda g: g.reshape(1, -1)
    bf = lambda w: w[0].astype(BF16)

    wi = w_in[0]
    win = wi[:, :D_SSM + D_QKV].astype(BF16)
    w_gates = wi[:, D_SSM + D_QKV:].astype(BF16)
    s1_w = (row(g_ffn1[0]), bf(ffn1_w_gate), bf(ffn1_w_up), bf(ffn1_w_down), row(g_mix[0]), win)
    s3_f32 = [w[0] for w in (ssm_w_glu, w_br_ssm, w_br_attn, w_out, ffn2_w_gate, ffn2_w_up, ffn2_w_down,
                             w_ple_gate, w_ple_proj)]

    n_chunks = seq // CHUNK
    tabs_p, tabs_s = _ssm_tables(
        ssm_a_re[0], ssm_a_im[0], ssm_log_dt[0], ssm_b_re[0], ssm_b_im[0], ssm_c_re[0], ssm_c_im[0],
        ssm_d[0], int(math.log2(n_chunks)), n_new)
    tile_state = (-1, N_SSM_TILES, 2, SSM_TILE_GROUPS, SSM_STATE)
    from_tiles = lambda h: h.reshape(tile_state).transpose(0, 1, 3, 4, 2).reshape(1, -1, N_SSM_GROUPS, SSM_STATE, 2)

    x1, u, qkv, *s3_bf = _stage1(x_prompt.reshape(n_p * seq, D_MODEL), *s1_w, cast=s3_f32)
    wglu, wbs, wba, wout, wg2, wu2, wd2, wpg, wpp = s3_bf
    s3_w = (row(g_mix[0]), w_gates, wglu, wbs, wba, wout, row(g_ffn2[0]), wg2, wu2, wd2, row(g_ple[0]), wpg, wpp,
            row(g_final))
    y_pre, h_last = _ssm_prompt(u, tabs_p, n_p, seq)
    attn = [a for grp in range(N_GROUPS) for a in _attn_prompt(qkv, n_p, seq, grp)]
    tok = (x1, y_pre, *attn, p_prompt[0].reshape(n_p * seq, D_PLE))
    y_prompt = _stage3(tok, s3_w).reshape(n_p, seq, D_MODEL)
    kv_prompt = [_kv_tail(qkv, n_p, seq, grp) for grp in range(N_GROUPS)]
    ssm_prompt = from_tiles(h_last)

    x1, u, qkv = _stage1(x_sample.reshape(n_s * n_new, D_MODEL), *s1_w)
    h0 = state_ssm[0].reshape(n_s, N_SSM_TILES, SSM_TILE_GROUPS, SSM_STATE, 2)
    h0 = h0.transpose(0, 1, 4, 2, 3).reshape(n_s, N_SSM_TILES * TILE_STATE_W)
    y_pre, h_new = _ssm_sample(u, h0, tabs_s, n_s, n_new)
    attn = [a for grp in range(N_GROUPS)
            for a in _attn_sample(qkv, caches[grp][0], n_s, n_new, grp)]
    tok = (x1, y_pre, *attn, p_sample[0].reshape(n_s * n_new, D_PLE))
    y_sample = _stage3(tok, s3_w).reshape(n_s, n_new, D_MODEL)
    kv_sample = _kv_new(qkv, n_s, n_new)
    ssm_sample = from_tiles(h_new)

    return (y_prompt, y_sample, kv_prompt[0], kv_prompt[1], kv_prompt[2], ssm_prompt,
            kv_sample[0], kv_sample[1], kv_sample[2], ssm_sample)
```

```python
import functools
import math
from typing import Callable, NamedTuple

import jax
import jax.numpy as jnp
import numpy as np
from jax import lax
from jax.experimental import pallas as pl
from jax.experimental.pallas import tpu as pltpu

D_MODEL = 1024
D_PLE = 256
D_FF = 2816
SSM_GROUP = 16
SSM_STATE = 64
D_SSM = 512
N_SSM_GROUPS = D_SSM // SSM_GROUP
HEAD_DIM = 64
HEADS = 4
WINDOWS = (128, 512, 2048)
DILATIONS = (1, 4, 16)
N_GROUPS = 3
D_HEADS = HEADS * HEAD_DIM
N_STEPS = 128
ATTN_SCALE = HEAD_DIM ** -0.5
EPS = 1e-6

LANE = 128
V7X_VMEM_LIMIT_BYTES = 56 * 1024 * 1024
STAGE_TM = 512
FFN1_HOST_TM = 256

CHUNK = 16
N_PAIRS = CHUNK // 2
PAIR_W = 2 * LANE
SSM_TILE_GROUPS = LANE // SSM_GROUP
N_SSM_TILES = N_SSM_GROUPS // SSM_TILE_GROUPS
TILE_STATE_W = SSM_TILE_GROUPS * 2 * SSM_STATE
Q_BLOCK = 128
ATTN_TB = 2048
ATTN_PAR = 4

D_QKV = 3 * N_GROUPS * D_HEADS


def _qkv_col(which, grp):
    return which * N_GROUPS + grp


BF16 = jnp.bfloat16
F32 = jnp.float32


def _dot(a, b):
    return jnp.dot(a, b, preferred_element_type=F32)


def _dot_nt(a, b):
    return lax.dot_general(a, b, (((1,), (1,)), ((), ())), preferred_element_type=F32)


def _rms(x, g):
    return x * lax.rsqrt(jnp.mean(x * x, axis=-1, keepdims=True) + EPS) * g


def _swiglu(xn, wg_ref, wu_ref, wd_ref):
    gate = _dot(xn, wg_ref[...])
    up = _dot(xn, wu_ref[...])
    act = (gate * jax.nn.sigmoid(gate) * up).astype(BF16)
    return _dot(act, wd_ref[...])


def _const_spec(shape):
    nd = len(shape)
    return pl.BlockSpec(shape, lambda *_: (0,) * nd, pipeline_mode=pl.Buffered(1))


def _params(n_grid_dims):
    return pltpu.CompilerParams(dimension_semantics=("arbitrary",) * n_grid_dims,
                                vmem_limit_bytes=V7X_VMEM_LIMIT_BYTES)


class _SideJob(NamedTuple):
    arrays: tuple
    in_specs: tuple
    out_shapes: tuple
    out_specs: tuple
    body: Callable


def _cast_job(w, n_steps):
    rows, cols = w.shape
    steps = max(s for s in range(1, n_steps + 1) if rows % s == 0 and (rows // s) % 16 == 0)
    spec = pl.BlockSpec((rows // steps, cols), lambda i: (jnp.minimum(i, steps - 1), 0))

    def body(src, dst):
        dst[...] = src[...].astype(BF16)

    return _SideJob((w,), (spec,), (jax.ShapeDtypeStruct(w.shape, BF16),), (spec,), body)


def _hosted_call(main_body, n_steps, arrays, in_specs, out_shapes, out_specs, jobs, name):
    n_in, n_out = len(arrays), len(out_shapes)

    def body(*refs):
        pos, job_in, job_out = n_in, [], []
        for job in jobs:
            job_in.append(refs[pos:pos + len(job.arrays)])
            pos += len(job.arrays)
        main_out = refs[pos:pos + n_out]
        pos += n_out
        for job in jobs:
            job_out.append(refs[pos:pos + len(job.out_shapes)])
            pos += len(job.out_shapes)
        for job, ins, outs in zip(jobs, job_in, job_out):
            job.body(*ins, *outs)
        main_body(*refs[:n_in], *main_out)

    outs = pl.pallas_call(
        body,
        grid=(n_steps,),
        in_specs=list(in_specs) + [sp for job in jobs for sp in job.in_specs],
        out_specs=list(out_specs) + [sp for job in jobs for sp in job.out_specs],
        out_shape=list(out_shapes) + [sh for job in jobs for sh in job.out_shapes],
        compiler_params=_params(1),
        name=name,
    )(*arrays, *[a for job in jobs for a in job.arrays])
    main, pos, per_job = outs[:n_out], n_out, []
    for job in jobs:
        per_job.append(outs[pos:pos + len(job.out_shapes)])
        pos += len(job.out_shapes)
    return main, per_job


def _ffn1_kernel(x_ref, g1_ref, wg_ref, wu_ref, wd_ref, x1_ref):
    x = x_ref[...]
    x1_ref[...] = x + 0.5 * _swiglu(_rms(x, g1_ref[...]).astype(BF16), wg_ref, wu_ref, wd_ref)


def _ffn1(x, g1, wg, wu, wd, tm, jobs=()):
    t = x.shape[0]
    tok = pl.BlockSpec((tm, D_MODEL), lambda i: (i, 0))
    (x1,), per_job = _hosted_call(
        _ffn1_kernel, t // tm, (x, g1, wg, wu, wd),
        [tok] + [_const_spec(a.shape) for a in (g1, wg, wu, wd)],
        [jax.ShapeDtypeStruct((t, D_MODEL), F32)], [tok], jobs, "ffn1")
    return x1, per_job


def _proj_kernel(x1_ref, gm_ref, win_ref, u_ref, qkv_ref):
    z = _dot(_rms(x1_ref[...], gm_ref[...]).astype(BF16), win_ref[...])
    u_ref[...] = z[:, :D_SSM]
    qkv_ref[...] = z[:, D_SSM:]


def _proj(x1, gm, win, tm, jobs=()):
    t = x1.shape[0]
    tok = lambda w: pl.BlockSpec((tm, w), lambda i: (i, 0))
    (u, qkv), per_job = _hosted_call(
        _proj_kernel, t // tm, (x1, gm, win),
        [tok(D_MODEL), _const_spec(gm.shape), _const_spec(win.shape)],
        [jax.ShapeDtypeStruct((t, D_SSM), F32), jax.ShapeDtypeStruct((t, D_QKV), F32)], [tok(D_SSM), tok(D_QKV)],
        jobs, "proj")
    return u, qkv, per_job


def _swap_halves(x):
    half = x.shape[1] // 2
    return jnp.concatenate([x[:, half:], x[:, :half]], axis=1)


def _cmul_split(x, a1, a2):
    return x * a1 + _swap_halves(x) * a2


def _shift_rows(x, sh):
    if sh % 8 == 0:
        return jnp.concatenate([jnp.zeros((sh, x.shape[1]), x.dtype), x[:x.shape[0] - sh]], axis=0)
    row = lax.broadcasted_iota(jnp.int32, x.shape, 0)
    return jnp.where(row >= sh, pltpu.roll(x, sh, axis=0), 0.0)


def _ssm_kernel(*refs, n_t, n_rows, has_h0):
    if has_h0:
        u_ref, h0_ref, kp_ref, wp_ref, vpt_ref, a1_ref, a2_ref, y_ref, hl_ref = refs
    else:
        u_ref, kp_ref, wp_ref, vpt_ref, a1_ref, a2_ref, y_ref, hl_ref = refs
    n_pairs = n_t // 2
    tok = lambda t: pl.ds(t, n_rows, stride=n_t)
    up = [jnp.concatenate([u_ref[tok(2 * a), :], u_ref[tok(2 * a + 1), :]], axis=1).astype(BF16)
          for a in range(n_pairs)]
    w_row0 = (CHUNK - n_t) * LANE
    s = _dot(jnp.concatenate(up, axis=1), wp_ref[0, w_row0:, :])
    if has_h0:
        hprev = h0_ref[...]
        h_last = _cmul_split(hprev, a1_ref[0], a2_ref[0]) + s
        hl_ref[...] = h_last
    else:
        h = s
        for lvl in range(int(math.log2(n_rows))):
            h = h + _cmul_split(_shift_rows(h, 1 << lvl), a1_ref[0, lvl:lvl + 1, :], a2_ref[0, lvl:lvl + 1, :])
        hprev = _shift_rows(h, 1)
        hl_ref[0, 0] = h[n_rows - 1:n_rows, :]
    hpb = hprev.astype(BF16)
    for b in range(n_pairs):
        acc = _dot_nt(hpb, vpt_ref[0, b])
        for a in range(b + 1):
            acc = acc + _dot(up[a], kp_ref[0, b - a])
        y_ref[tok(2 * b), :] = acc[:, :LANE]
        y_ref[tok(2 * b + 1), :] = acc[:, LANE:]


def _ssm_table_specs(tabs, idx):
    kp, wp, vp, a1, a2 = tabs
    return [pl.BlockSpec((1,) + kp.shape[1:], lambda *g: (idx(*g), 0, 0, 0)),
            pl.BlockSpec((1,) + wp.shape[1:], lambda *g: (idx(*g), 0, 0)),
            pl.BlockSpec((1,) + vp.shape[1:], lambda *g: (idx(*g), 0, 0, 0)),
            pl.BlockSpec((1,) + a1.shape[1:], lambda *g: (idx(*g), 0, 0)),
            pl.BlockSpec((1,) + a2.shape[1:], lambda *g: (idx(*g), 0, 0))]


def _ssm_prompt(u, tabs, n, seq):
    n_rows = seq // CHUNK
    tile = pl.BlockSpec((seq, LANE), lambda j, i: (i, j))
    return pl.pallas_call(
        functools.partial(_ssm_kernel, n_t=CHUNK, n_rows=n_rows, has_h0=False),
        grid=(N_SSM_TILES, n),
        in_specs=[tile] + _ssm_table_specs(tabs, lambda j, i: j),
        out_specs=[tile, pl.BlockSpec((1, 1, 1, TILE_STATE_W), lambda j, i: (i, j, 0, 0))],
        out_shape=[jax.ShapeDtypeStruct((n * seq, D_SSM), F32),
                   jax.ShapeDtypeStruct((n, N_SSM_TILES, 1, TILE_STATE_W), F32)],
        compiler_params=_params(2),
        name="ssm_prompt",
    )(u, *tabs)


def _ssm_sample(u, h0, tabs, n_seq, n_new):
    tile = pl.BlockSpec((n_seq * n_new, LANE), lambda j: (0, j))
    state = pl.BlockSpec((n_seq, TILE_STATE_W), lambda j: (0, j))
    return pl.pallas_call(
        functools.partial(_ssm_kernel, n_t=n_new, n_rows=n_seq, has_h0=True),
        grid=(N_SSM_TILES,),
        in_specs=[tile, state] + _ssm_table_specs(tabs, lambda j: j),
        out_specs=[tile, state],
        out_shape=[jax.ShapeDtypeStruct((n_seq * n_new, D_SSM), F32),
                   jax.ShapeDtypeStruct((n_seq, N_SSM_TILES * TILE_STATE_W), F32)],
        compiler_params=_params(1),
        name="ssm_sample",
    )(u, h0, *tabs)


def _ssm_prep_kernel(are_ref, aim_ref, ldt_ref, bre_ref, bim_ref, cre_ref, cim_ref, d_ref,
                     kp_ref, wp_ref, vpt_ref, a1_ref, a2_ref, a1s_ref, a2s_ref, *, n_levels, n_new):
    a_re, a_im = are_ref[0], aim_ref[0]
    dt = jnp.exp(ldt_ref[0])
    mag = jnp.exp(a_re * dt)
    ab_re, ab_im = mag * jnp.cos(a_im * dt), mag * jnp.sin(a_im * dt)
    den = a_re * a_re + a_im * a_im
    nr, ni = ab_re - 1.0, ab_im
    f_re = (nr * a_re + ni * a_im) / den
    f_im = (ni * a_re - nr * a_im) / den
    shape = bre_ref.shape[1:]
    same_group = (lax.broadcasted_iota(jnp.int32, shape, 0) // SSM_GROUP
                  == lax.broadcasted_iota(jnp.int32, shape, 1) // SSM_STATE)
    b_re, b_im = bre_ref[0], bim_ref[0]
    x_re = jnp.where(same_group, f_re * b_re - f_im * b_im, 0.0)
    x_im = jnp.where(same_group, f_re * b_im + f_im * b_re, 0.0)
    c_re = jnp.where(same_group, cre_ref[0], 0.0)
    c_im = jnp.where(same_group, cim_ref[0], 0.0)
    pw = [(jnp.ones_like(ab_re), jnp.zeros_like(ab_im))]
    for _ in range(CHUNK):
        r, i = pw[-1]
        pw.append((r * ab_re - i * ab_im, r * ab_im + i * ab_re))
    cat = lambda r, i: jnp.concatenate([r, i], axis=1)
    e_pack = [cat(c_re * r - c_im * i, -(c_re * i + c_im * r)) for r, i in pw]
    split = lambda a: (a.astype(BF16), (a - a.astype(BF16).astype(F32)).astype(BF16))
    x_hi, x_lo = split(cat(x_re, x_im))
    x_3 = jnp.concatenate([x_hi, x_hi, x_lo], axis=1)
    k_lag = []
    for k in range(CHUNK):
        e_hi, e_lo = split(e_pack[k])
        k_lag.append(_dot_nt(x_3, jnp.concatenate([e_hi, e_lo, e_hi], axis=1)))
    diag = (lax.broadcasted_iota(jnp.int32, (LANE, LANE), 0) == lax.broadcasted_iota(jnp.int32, (LANE, LANE), 1))
    k_lag[0] = k_lag[0] + jnp.where(diag, d_ref[0], 0.0)
    zero = jnp.zeros((LANE, LANE), F32)
    for dl in range(N_PAIRS):
        top = cat(k_lag[2 * dl], k_lag[2 * dl + 1])
        bot = cat(k_lag[2 * dl - 1] if dl > 0 else zero, k_lag[2 * dl])
        kp_ref[0, dl] = jnp.concatenate([top, bot], axis=0).astype(BF16)
    for t in range(CHUNK):
        r, i = pw[CHUNK - 1 - t]
        wp_ref[0, t * LANE:(t + 1) * LANE, :] = cat(x_re * r - x_im * i, x_re * i + x_im * r).astype(BF16)
    for t in range(CHUNK):
        vpt_ref[0, t // 2, (t % 2) * LANE:(t % 2 + 1) * LANE, :] = e_pack[t + 1].astype(BF16)
    r, i = pw[CHUNK]
    for lvl in range(n_levels):
        a1_ref[0, lvl:lvl + 1, :], a2_ref[0, lvl:lvl + 1, :] = cat(r, r), cat(-i, i)
        r, i = r * r - i * i, 2.0 * r * i
    r, i = pw[n_new]
    a1s_ref[0], a2s_ref[0] = cat(r, r), cat(-i, i)


def _ssm_tables(a_re, a_im, log_dt, b_re, b_im, c_re, c_im, d_skip, n_levels, n_new):
    nj, tg = N_SSM_TILES, SSM_TILE_GROUPS
    half = tg * SSM_STATE
    rowv = lambda v: v.reshape(nj, 1, half)
    tiled = lambda m: jnp.tile(m.reshape(nj, LANE, SSM_STATE), (1, 1, tg))
    ins = (rowv(a_re), rowv(a_im), rowv(jnp.repeat(log_dt, SSM_STATE)),
           tiled(b_re.transpose(0, 2, 1)), tiled(b_im.transpose(0, 2, 1)), tiled(c_re), tiled(c_im),
           d_skip.reshape(nj, 1, LANE))
    shapes = ((nj, N_PAIRS, PAIR_W, PAIR_W), (nj, CHUNK * LANE, TILE_STATE_W), (nj, N_PAIRS, PAIR_W, TILE_STATE_W),
              (nj, n_levels, TILE_STATE_W), (nj, n_levels, TILE_STATE_W), (nj, 1, TILE_STATE_W), (nj, 1, TILE_STATE_W))
    dtypes = (BF16, BF16, BF16, F32, F32, F32, F32)
    per_tile = lambda s: pl.BlockSpec((1,) + tuple(s[1:]), lambda j: (j,) + (0,) * (len(s) - 1))
    kp, wp, vpt, a1, a2, a1s, a2s = pl.pallas_call(
        functools.partial(_ssm_prep_kernel, n_levels=n_levels, n_new=n_new),
        grid=(nj,),
        in_specs=[per_tile(x.shape) for x in ins],
        out_specs=[per_tile(s) for s in shapes],
        out_shape=[jax.ShapeDtypeStruct(s, dt) for s, dt in zip(shapes, dtypes)],
        compiler_params=_params(1),
        name="ssm_prep",
    )(*ins)
    return (kp, wp, vpt, a1, a2), (kp, wp, vpt, a1s, a2s)


def _head_of_lane(width):
    return lax.broadcasted_iota(jnp.int32, (1, width), 1) // HEAD_DIM


def _stack_heads(q):
    head = _head_of_lane(q.shape[1])
    q = q.astype(F32)
    return jnp.concatenate([jnp.where(head == h, q, 0.0) for h in range(HEADS)], axis=0).astype(BF16)


def _unstack_heads(x, rows):
    head = _head_of_lane(x.shape[1])
    out = x[(HEADS - 1) * rows:]
    for h in range(HEADS - 2, -1, -1):
        out = jnp.where(head == h, x[h * rows:(h + 1) * rows], out)
    return out


def _attn_prompt_kernel(bias_ref, q0, q1, k0, k1, v0, v1, o0, o1, l0, l1, kprev, vprev, *, d, m):
    qb = Q_BLOCK
    b = pl.program_id(1)
    n_par = min(d, ATTN_PAR)
    rd, wr = b % 2, (b + 1) % 2
    head = _head_of_lane(D_HEADS)

    @pl.when(b == 0)
    def _():
        kprev[0] = jnp.zeros(kprev.shape[1:], kprev.dtype)
        vprev[0] = jnp.zeros(vprev.shape[1:], vprev.dtype)

    def block(r, j, k_prev, v_prev):
        start = r + j * qb * d
        rows = pl.ds(start, qb, stride=d) if d > 1 else pl.ds(pl.multiple_of(start, qb), qb)
        both = lambda lo, hi: jnp.concatenate([lo[rows, :], hi[rows, :]], axis=1)
        qs = _stack_heads(both(q0, q1) * ATTN_SCALE)
        k_own, v_own = both(k0, k1).astype(BF16), both(v0, v1).astype(BF16)
        k = jnp.concatenate([k_prev, k_own], axis=0)
        v = jnp.concatenate([v_prev, v_own], axis=0)
        s = _dot_nt(qs, k) + bias_ref[jnp.minimum(b * m + j, 1)]
        mx = jnp.max(s, axis=-1, keepdims=True)
        pe = jnp.exp(s - mx)
        den = jnp.sum(pe, axis=-1, keepdims=True)
        pn = (pe * (1.0 / den)).astype(BF16)
        o = _dot(pn[:qb], jnp.where(head == 0, v, jnp.zeros_like(v)))
        for h in range(1, HEADS):
            o = o + _dot(pn[h * qb:(h + 1) * qb], jnp.where(head == h, v, jnp.zeros_like(v)))
        lse = _unstack_heads(jnp.broadcast_to(mx + jnp.log(den), (HEADS * qb, D_HEADS)), qb)
        o0[rows, :], o1[rows, :] = o[:, :LANE], o[:, LANE:]
        l0[rows, :], l1[rows, :] = lse[:, :LANE], lse[:, LANE:]
        return k_own, v_own

    def class_group(g, carry):
        classes = [g * n_par + t for t in range(n_par)]

        def step(j, prev):
            return tuple(x for t, r in enumerate(classes) for x in block(r, j, prev[2 * t], prev[2 * t + 1]))

        init = tuple(x for r in classes for x in (kprev[rd, r], vprev[rd, r]))
        last = lax.fori_loop(0, m, step, init, unroll=max(1, min(m, ATTN_PAR // n_par)))
        for t, r in enumerate(classes):
            kprev[wr, r], vprev[wr, r] = last[2 * t], last[2 * t + 1]
        return carry

    lax.fori_loop(0, d // n_par, class_group, 0)


def _attn_bias():
    qi = np.arange(HEADS * Q_BLOCK)[:, None] % Q_BLOCK
    ki = np.arange(2 * Q_BLOCK)[None, :]
    diff = Q_BLOCK + qi - ki
    valid = (diff >= 0) & (diff <= N_STEPS)
    return jnp.asarray(np.where(np.stack([valid & (ki >= Q_BLOCK), valid]), 0.0, -np.inf), F32)


def _attn_prompt(qkv, n, seq, grp):
    d = DILATIONS[grp]
    tb = min(ATTN_TB, seq)
    assert tb % (Q_BLOCK * d) == 0 and seq % tb == 0
    m, nb = tb // (Q_BLOCK * d), seq // tb
    cur = lambda c: pl.BlockSpec((tb, LANE), lambda i, b: (i * nb + b, c))
    qc, kc, vc = (2 * _qkv_col(which, grp) for which in range(3))
    bias = _attn_bias()
    return pl.pallas_call(
        functools.partial(_attn_prompt_kernel, d=d, m=m),
        grid=(n, nb),
        in_specs=[_const_spec(bias.shape), cur(qc), cur(qc + 1), cur(kc), cur(kc + 1), cur(vc), cur(vc + 1)],
        out_specs=[pl.BlockSpec((tb, LANE), lambda i, b: (i * nb + b, 0))] * 4,
        out_shape=[jax.ShapeDtypeStruct((n * seq, LANE), F32)] * 4,
        scratch_shapes=[pltpu.VMEM((2, d, Q_BLOCK, D_HEADS), BF16)] * 2,
        compiler_params=_params(2),
        name=f"attn_prompt_d{d}",
    )(bias, *([qkv] * 6))


def _attn_sample_kernel(q_ref, kn_ref, vn_ref, c_ref, o_ref, lse_ref, *, d, n_new, seq_blk):
    w = c_ref.shape[-1]
    n_rows = HEADS * n_new
    l_b = lax.broadcasted_iota(jnp.int32, (n_rows, w), 0) % n_new
    pos = lax.broadcasted_iota(jnp.int32, (n_rows, w), 1)
    valid_buf = (((w + l_b - pos) & (d - 1)) == 0) & (pos >= l_b)
    dn = (lax.broadcasted_iota(jnp.int32, (n_rows, LANE), 0) % n_new) - lax.broadcasted_iota(jnp.int32, (n_rows, LANE), 1)
    valid_new = (dn >= 0) & ((dn & (d - 1)) == 0)
    pad = jnp.zeros((LANE - n_new, D_HEADS), F32)

    def one_seq(s, carry):
        qs = _stack_heads(q_ref[s])
        k_t, v_t = c_ref[s, 0].astype(BF16), c_ref[s, 1].astype(BF16)
        k_n = jnp.concatenate([kn_ref[s], pad], axis=0).astype(BF16)
        v_n = jnp.concatenate([vn_ref[s], pad], axis=0).astype(BF16)
        s_b = jnp.where(valid_buf, _dot(qs, k_t) * ATTN_SCALE, -jnp.inf)
        s_n = jnp.where(valid_new, _dot_nt(qs, k_n) * ATTN_SCALE, -jnp.inf)
        mx = jnp.maximum(jnp.max(s_b, axis=-1, keepdims=True), jnp.max(s_n, axis=-1, keepdims=True))
        p_b, p_n = jnp.exp(s_b - mx), jnp.exp(s_n - mx)
        den = jnp.sum(p_b, axis=-1, keepdims=True) + jnp.sum(p_n, axis=-1, keepdims=True)
        o = (_dot_nt(p_b.astype(BF16), v_t) + _dot(p_n.astype(BF16), v_n)) / den
        o_ref[s] = _unstack_heads(o, n_new)
        lse_ref[s] = _unstack_heads(jnp.broadcast_to(mx + jnp.log(den), o.shape), n_new)
        return carry

    lax.fori_loop(0, seq_blk, one_seq, 0, unroll=True)


def _attn_sample_job(qkv, cache, n_seq, n_new, grp, n_steps):
    d = DILATIONS[grp]
    w = cache.shape[1]
    assert w == N_STEPS * d and n_new <= LANE and n_seq % n_steps == 0
    seq_blk = n_seq // n_steps
    qkv3 = qkv.reshape(n_seq, n_new, D_QKV)
    c_t = cache.transpose(0, 2, 3, 4, 1).reshape(n_seq, 2, D_HEADS, w)
    blk = (seq_blk, n_new, D_HEADS)
    col = lambda which: pl.BlockSpec(blk, lambda i: (i, 0, _qkv_col(which, grp)))
    out_spec = pl.BlockSpec(blk, lambda i: (i, 0, 0))
    out_shape = jax.ShapeDtypeStruct((n_seq, n_new, D_HEADS), F32)
    return _SideJob((qkv3, qkv3, qkv3, c_t),
                    (col(0), col(1), col(2), pl.BlockSpec((seq_blk, 2, D_HEADS, w), lambda i: (i, 0, 0, 0))),
                    (out_shape, out_shape), (out_spec, out_spec),
                    functools.partial(_attn_sample_kernel, d=d, n_new=n_new, seq_blk=seq_blk))


def _halves(o, lse):
    o, lse = o.reshape(-1, D_HEADS), lse.reshape(-1, D_HEADS)
    return o[:, :LANE], o[:, LANE:], lse[:, :LANE], lse[:, LANE:]


def _kv_tail_kernel(k_ref, v_ref, o_ref):
    o_ref[0, 0] = k_ref[...].T
    o_ref[0, 1] = v_ref[...].T


def _kv_tail(qkv, n, seq, grp):
    keep = min(WINDOWS[grp], seq)
    tt = min(keep, 512)
    assert keep % tt == 0 and (seq - keep) % tt == 0
    first = (seq - keep) // tt
    rows = lambda which: pl.BlockSpec((tt, D_HEADS), lambda i, t: (i * (seq // tt) + first + t, _qkv_col(which, grp)))
    out = pl.pallas_call(
        _kv_tail_kernel,
        grid=(n, keep // tt),
        in_specs=[rows(1), rows(2)],
        out_specs=pl.BlockSpec((1, 2, D_HEADS, tt), lambda i, t: (i, 0, 0, t)),
        out_shape=jax.ShapeDtypeStruct((n, 2, D_HEADS, keep), F32),
        compiler_params=_params(2),
        name=f"kv_tail_w{keep}",
    )(qkv, qkv)
    return out.reshape(1, n, 2, HEADS, HEAD_DIM, keep).transpose(0, 1, 5, 2, 3, 4)


def _kv_new_kernel(x_ref, o_ref, *, n_new, n_seq):
    for l in range(n_new):
        o_ref[0, l, 0] = x_ref[pl.ds(l, n_seq, stride=n_new), :].T


def _kv_new(qkv, n_seq, n_new):
    halves = D_HEADS // LANE
    out = pl.pallas_call(
        functools.partial(_kv_new_kernel, n_new=n_new, n_seq=n_seq),
        grid=(N_GROUPS, 2, halves),
        in_specs=[pl.BlockSpec((n_seq * n_new, LANE), lambda g, kv, c: (0, halves * _qkv_col(1 + kv, g) + c))],
        out_specs=pl.BlockSpec((1, n_new, 1, LANE, n_seq), lambda g, kv, c: (g, 0, kv, c, 0)),
        out_shape=jax.ShapeDtypeStruct((N_GROUPS, n_new, 2, D_HEADS, n_seq), F32),
        compiler_params=_params(3),
        name="kv_new",
    )(qkv)
    out = out.reshape(N_GROUPS, 1, n_new, 2, HEADS, HEAD_DIM, n_seq).transpose(0, 1, 6, 2, 3, 4, 5)
    return [out[g] for g in range(N_GROUPS)]


def _gelu_tanh(x):
    return 0.5 * x * (1.0 + jnp.tanh(math.sqrt(2.0 / math.pi) * (x + 0.044715 * (x * x * x))))


def _stage3_kernel(*refs):
    x1_ref, yp_ref = refs[:2]
    attn = refs[2:2 + 4 * N_GROUPS]
    (p_ref, gm_ref, wgate_ref, wglu_ref, wbs_ref, wba_ref, wout_ref, g2_ref, wg_ref, wu_ref, wd_ref,
     gp_ref, wpg_ref, wpp_ref, gf_ref, y_ref) = refs[2 + 4 * N_GROUPS:]
    x1 = x1_ref[...]
    gates = jax.nn.sigmoid(_dot(_rms(x1, gm_ref[...]).astype(BF16), wgate_ref[...]))
    y = _gelu_tanh(yp_ref[...])
    glu = y * jax.nn.sigmoid(_dot(y.astype(BF16), wglu_ref[...]))
    full = lambda lo, hi: jnp.concatenate([lo[...], hi[...]], axis=1)
    outs = [full(attn[4 * g], attn[4 * g + 1]) for g in range(N_GROUPS)]
    lses = [full(attn[4 * g + 2], attn[4 * g + 3]) for g in range(N_GROUPS)]
    mx = jnp.maximum(jnp.maximum(lses[0], lses[1]), lses[2])
    es = [jnp.exp(l - mx) for l in lses]
    y_attn = (es[0] * outs[0] + es[1] * outs[1] + es[2] * outs[2]) / (es[0] + es[1] + es[2])
    merged = (gates[:, :D_MODEL] * _dot(glu.astype(BF16), wbs_ref[...])
              + gates[:, D_MODEL:] * _dot(y_attn.astype(BF16), wba_ref[...]))
    x = x1 + _dot(merged.astype(BF16), wout_ref[...])
    x = x + 0.5 * _swiglu(_rms(x, g2_ref[...]).astype(BF16), wg_ref, wu_ref, wd_ref)
    gate = jax.nn.sigmoid(_dot(_rms(x, gp_ref[...]).astype(BF16), wpg_ref[...]))
    x = x + gate * _dot(p_ref[...].astype(BF16), wpp_ref[...])
    y_ref[...] = _rms(x, gf_ref[...])


def _stage3(tok_inputs, weights):
    t, tm = tok_inputs[0].shape[0], STAGE_TM
    tok = lambda a: pl.BlockSpec((tm, a.shape[1]), lambda i: (i, 0))
    return pl.pallas_call(
        _stage3_kernel,
        grid=(t // tm,),
        in_specs=[tok(a) for a in tok_inputs] + [_const_spec(w.shape) for w in weights],
        out_specs=pl.BlockSpec((tm, D_MODEL), lambda i: (i, 0)),
        out_shape=jax.ShapeDtypeStruct((t, D_MODEL), F32),
        compiler_params=_params(1),
        name="stage3",
    )(*tok_inputs, *weights)


def kernel(x_prompt, x_sample, p_prompt, p_sample, cache_kv_w128, cache_kv_w512, cache_kv_w2048, state_ssm,
           g_ffn1, ffn1_w_gate, ffn1_w_up, ffn1_w_down, g_mix, w_in, ssm_a_re, ssm_a_im, ssm_log_dt,
           ssm_b_re, ssm_b_im, ssm_c_re, ssm_c_im, ssm_d, ssm_w_glu, w_br_ssm, w_br_attn, w_out,
           g_ffn2, ffn2_w_gate, ffn2_w_up, ffn2_w_down, g_ple, w_ple_gate, w_ple_proj, g_final):
    assert x_prompt.shape[-1] == D_MODEL and g_ffn1.shape[0] == 1
    n_p, seq, _ = x_prompt.shape
    n_s, n_new, _ = x_sample.shape
    caches = (cache_kv_w128, cache_kv_w512, cache_kv_w2048)
    row = lambda g: g.reshape(1, -1)
    bf = lambda w: w[0].astype(BF16)

    wi = w_in[0]
    win = wi[:, :D_SSM + D_QKV].astype(BF16)
    w_gates = wi[:, D_SSM + D_QKV:].astype(BF16)
    ffn1_w = (row(g_ffn1[0]), bf(ffn1_w_gate), bf(ffn1_w_up), bf(ffn1_w_down))
    s3_f32 = [w[0] for w in (ssm_w_glu, w_br_ssm, w_br_attn, w_out, ffn2_w_gate, ffn2_w_up, ffn2_w_down,
                             w_ple_gate, w_ple_proj)]

    n_chunks = seq // CHUNK
    tabs_p, tabs_s = _ssm_tables(
        ssm_a_re[0], ssm_a_im[0], ssm_log_dt[0], ssm_b_re[0], ssm_b_im[0], ssm_c_re[0], ssm_c_im[0],
        ssm_d[0], int(math.log2(n_chunks)), n_new)
    tile_state = (-1, N_SSM_TILES, 2, SSM_TILE_GROUPS, SSM_STATE)
    from_tiles = lambda h: h.reshape(tile_state).transpose(0, 1, 3, 4, 2).reshape(1, -1, N_SSM_GROUPS, SSM_STATE, 2)

    xs = x_sample.reshape(n_s * n_new, D_MODEL)
    x1_s, _ = _ffn1(xs, *ffn1_w, tm=min(STAGE_TM, xs.shape[0]))
    u_s, qkv_s, _ = _proj(x1_s, row(g_mix[0]), win, tm=min(STAGE_TM, xs.shape[0]))

    xp = x_prompt.reshape(n_p * seq, D_MODEL)
    attn_job = lambda grp, n_steps: _attn_sample_job(qkv_s, caches[grp][0], n_s, n_new, grp, n_steps)
    steps_a, steps_b = xp.shape[0] // FFN1_HOST_TM, xp.shape[0] // STAGE_TM
    x1, side = _ffn1(xp, *ffn1_w, tm=FFN1_HOST_TM,
                     jobs=[attn_job(2, steps_a), attn_job(0, steps_a)] + [_cast_job(w, steps_a) for w in s3_f32])
    attn_s = {2: side[0], 0: side[1]}
    wglu, wbs, wba, wout, wg2, wu2, wd2, wpg, wpp = (c[0] for c in side[2:])
    s3_w = (row(g_mix[0]), w_gates, wglu, wbs, wba, wout, row(g_ffn2[0]), wg2, wu2, wd2, row(g_ple[0]), wpg, wpp,
            row(g_final))
    u, qkv, side = _proj(x1, row(g_mix[0]), win, tm=STAGE_TM, jobs=[attn_job(1, steps_b)])
    attn_s[1] = side[0]
    y_pre, h_last = _ssm_prompt(u, tabs_p, n_p, seq)
    attn = [a for grp in range(N_GROUPS) for a in _attn_prompt(qkv, n_p, seq, grp)]
    tok = (x1, y_pre, *attn, p_prompt[0].reshape(n_p * seq, D_PLE))
    y_prompt = _stage3(tok, s3_w).reshape(n_p, seq, D_MODEL)
    kv_prompt = [_kv_tail(qkv, n_p, seq, grp) for grp in range(N_GROUPS)]
    ssm_prompt = from_tiles(h_last)

    h0 = state_ssm[0].reshape(n_s, N_SSM_TILES, SSM_TILE_GROUPS, SSM_STATE, 2)
    h0 = h0.transpose(0, 1, 4, 2, 3).reshape(n_s, N_SSM_TILES * TILE_STATE_W)
    y_pre, h_new = _ssm_sample(u_s, h0, tabs_s, n_s, n_new)
    attn = [a for grp in range(N_GROUPS) for a in _halves(*attn_s[grp])]
    tok = (x1_s, y_pre, *attn, p_sample[0].reshape(n_s * n_new, D_PLE))
    y_sample = _stage3(tok, s3_w).reshape(n_s, n_new, D_MODEL)
    kv_sample = _kv_new(qkv_s, n_s, n_new)
    ssm_sample = from_tiles(h_new)

    return (y_prompt, y_sample, kv_prompt[0], kv_prompt[1], kv_prompt[2], ssm_prompt,
            kv_sample[0], kv_sample[1], kv_sample[2], ssm_sample)
```

```python
import functools
import math
from typing import Callable, NamedTuple

import jax
import jax.numpy as jnp
import numpy as np
from jax import lax
from jax.experimental import pallas as pl
from jax.experimental.pallas import tpu as pltpu

D_MODEL = 1024
D_PLE = 256
D_FF = 2816
SSM_GROUP = 16
SSM_STATE = 64
D_SSM = 512
N_SSM_GROUPS = D_SSM // SSM_GROUP
HEAD_DIM = 64
HEADS = 4
WINDOWS = (128, 512, 2048)
DILATIONS = (1, 4, 16)
N_GROUPS = 3
D_HEADS = HEADS * HEAD_DIM
N_STEPS = 128
ATTN_SCALE = HEAD_DIM ** -0.5
EPS = 1e-6

LANE = 128
V7X_VMEM_LIMIT_BYTES = 56 * 1024 * 1024
STAGE_TM = 512
FFN1_HOST_TM = 256

CHUNK = 16
N_PAIRS = CHUNK // 2
PAIR_W = 2 * LANE
SSM_TILE_GROUPS = LANE // SSM_GROUP
N_SSM_TILES = N_SSM_GROUPS // SSM_TILE_GROUPS
TILE_STATE_W = SSM_TILE_GROUPS * 2 * SSM_STATE
Q_BLOCK = 128
ATTN_TB = 2048
ATTN_PAR = 4

D_QKV = 3 * N_GROUPS * D_HEADS


def _qkv_col(which, grp):
    return which * N_GROUPS + grp


BF16 = jnp.bfloat16
F32 = jnp.float32


def _dot(a, b):
    return jnp.dot(a, b, preferred_element_type=F32)


def _dot_nt(a, b):
    return lax.dot_general(a, b, (((1,), (1,)), ((), ())), preferred_element_type=F32)


def _rms(x, g):
    return x * lax.rsqrt(jnp.mean(x * x, axis=-1, keepdims=True) + EPS) * g


def _swiglu(xn, wg_ref, wu_ref, wd_ref):
    gate = _dot(xn, wg_ref[...])
    up = _dot(xn, wu_ref[...])
    act = (gate * jax.nn.sigmoid(gate) * up).astype(BF16)
    return _dot(act, wd_ref[...])


def _const_spec(shape):
    nd = len(shape)
    return pl.BlockSpec(shape, lambda *_: (0,) * nd, pipeline_mode=pl.Buffered(1))


def _params(n_grid_dims):
    return pltpu.CompilerParams(dimension_semantics=("arbitrary",) * n_grid_dims,
                                vmem_limit_bytes=V7X_VMEM_LIMIT_BYTES)


class _SideJob(NamedTuple):
    arrays: tuple
    in_specs: tuple
    out_shapes: tuple
    out_specs: tuple
    body: Callable


def _cast_job(w, n_steps):
    rows, cols = w.shape
    steps = max(s for s in range(1, n_steps + 1) if rows % s == 0 and (rows // s) % 16 == 0)
    spec = pl.BlockSpec((rows // steps, cols), lambda i: (jnp.minimum(i, steps - 1), 0))

    def body(src, dst):
        dst[...] = src[...].astype(BF16)

    return _SideJob((w,), (spec,), (jax.ShapeDtypeStruct(w.shape, BF16),), (spec,), body)


def _hosted_call(main_body, n_steps, arrays, in_specs, out_shapes, out_specs, jobs, name):
    n_in, n_out = len(arrays), len(out_shapes)

    def body(*refs):
        pos, job_in, job_out = n_in, [], []
        for job in jobs:
            job_in.append(refs[pos:pos + len(job.arrays)])
            pos += len(job.arrays)
        main_out = refs[pos:pos + n_out]
        pos += n_out
        for job in jobs:
            job_out.append(refs[pos:pos + len(job.out_shapes)])
            pos += len(job.out_shapes)
        for job, ins, outs in zip(jobs, job_in, job_out):
            job.body(*ins, *outs)
        main_body(*refs[:n_in], *main_out)

    outs = pl.pallas_call(
        body,
        grid=(n_steps,),
        in_specs=list(in_specs) + [sp for job in jobs for sp in job.in_specs],
        out_specs=list(out_specs) + [sp for job in jobs for sp in job.out_specs],
        out_shape=list(out_shapes) + [sh for job in jobs for sh in job.out_shapes],
        compiler_params=_params(1),
        name=name,
    )(*arrays, *[a for job in jobs for a in job.arrays])
    main, pos, per_job = outs[:n_out], n_out, []
    for job in jobs:
        per_job.append(outs[pos:pos + len(job.out_shapes)])
        pos += len(job.out_shapes)
    return main, per_job


def _ffn1_kernel(x_ref, g1_ref, wg_ref, wu_ref, wd_ref, x1_ref):
    x = x_ref[...]
    x1_ref[...] = x + 0.5 * _swiglu(_rms(x, g1_ref[...]).astype(BF16), wg_ref, wu_ref, wd_ref)


def _ffn1(x, g1, wg, wu, wd, tm, jobs=()):
    t = x.shape[0]
    tok = pl.BlockSpec((tm, D_MODEL), lambda i: (i, 0))
    (x1,), per_job = _hosted_call(
        _ffn1_kernel, t // tm, (x, g1, wg, wu, wd),
        [tok] + [_const_spec(a.shape) for a in (g1, wg, wu, wd)],
        [jax.ShapeDtypeStruct((t, D_MODEL), F32)], [tok], jobs, "ffn1")
    return x1, per_job


def _proj_kernel(x1_ref, gm_ref, win_ref, u_ref, qkv_ref):
    z = _dot(_rms(x1_ref[...], gm_ref[...]).astype(BF16), win_ref[...])
    u_ref[...] = z[:, :D_SSM]
    qkv_ref[...] = z[:, D_SSM:]


def _proj(x1, gm, win, tm, jobs=()):
    t = x1.shape[0]
    tok = lambda w: pl.BlockSpec((tm, w), lambda i: (i, 0))
    (u, qkv), per_job = _hosted_call(
        _proj_kernel, t // tm, (x1, gm, win),
        [tok(D_MODEL), _const_spec(gm.shape), _const_spec(win.shape)],
        [jax.ShapeDtypeStruct((t, D_SSM), F32), jax.ShapeDtypeStruct((t, D_QKV), F32)], [tok(D_SSM), tok(D_QKV)],
        jobs, "proj")
    return u, qkv, per_job


def _swap_halves(x):
    half = x.shape[1] // 2
    return jnp.concatenate([x[:, half:], x[:, :half]], axis=1)


def _cmul_split(x, a1, a2):
    return x * a1 + _swap_halves(x) * a2


def _shift_rows(x, sh):
    if sh % 8 == 0:
        return jnp.concatenate([jnp.zeros((sh, x.shape[1]), x.dtype), x[:x.shape[0] - sh]], axis=0)
    row = lax.broadcasted_iota(jnp.int32, x.shape, 0)
    return jnp.where(row >= sh, pltpu.roll(x, sh, axis=0), 0.0)


def _ssm_kernel(*refs, n_t, n_rows, has_h0):
    if has_h0:
        u_ref, h0_ref, kp_ref, wp_ref, vpt_ref, a1_ref, a2_ref, y_ref, hl_ref = refs
    else:
        u_ref, kp_ref, wp_ref, vpt_ref, a1_ref, a2_ref, y_ref, hl_ref = refs
    n_pairs = n_t // 2
    tok = lambda t: pl.ds(t, n_rows, stride=n_t)
    up = [jnp.concatenate([u_ref[tok(2 * a), :], u_ref[tok(2 * a + 1), :]], axis=1).astype(BF16)
          for a in range(n_pairs)]
    s = _dot(jnp.concatenate(up, axis=1), wp_ref[0])
    if has_h0:
        hprev = h0_ref[...]
        h_last = _cmul_split(hprev, a1_ref[0], a2_ref[0]) + s
        hl_ref[...] = h_last
    else:
        h = s
        for lvl in range(int(math.log2(n_rows))):
            h = h + _cmul_split(_shift_rows(h, 1 << lvl), a1_ref[0, lvl:lvl + 1, :], a2_ref[0, lvl:lvl + 1, :])
        hprev = _shift_rows(h, 1)
        hl_ref[0, 0] = h[n_rows - 1:n_rows, :]
    hpb = hprev.astype(BF16)
    for b in range(n_pairs):
        acc = _dot_nt(hpb, vpt_ref[0, b])
        for a in range(b + 1):
            acc = acc + _dot(up[a], kp_ref[0, b - a])
        y_ref[tok(2 * b), :] = acc[:, :LANE]
        y_ref[tok(2 * b + 1), :] = acc[:, LANE:]


def _ssm_table_specs(tabs, idx, n_t):
    kp, wp, vp, a1, a2 = tabs
    n_pairs, w_rows = n_t // 2, n_t * LANE
    return [pl.BlockSpec((1, n_pairs) + kp.shape[2:], lambda *g: (idx(*g), 0, 0, 0)),
            pl.BlockSpec((1, w_rows, wp.shape[2]), lambda *g: (idx(*g), wp.shape[1] // w_rows - 1, 0)),
            pl.BlockSpec((1, n_pairs) + vp.shape[2:], lambda *g: (idx(*g), 0, 0, 0)),
            pl.BlockSpec((1,) + a1.shape[1:], lambda *g: (idx(*g), 0, 0)),
            pl.BlockSpec((1,) + a2.shape[1:], lambda *g: (idx(*g), 0, 0))]


def _ssm_prompt(u, tabs, n, seq):
    n_rows = seq // CHUNK
    tile = pl.BlockSpec((seq, LANE), lambda j, i: (i, j))
    return pl.pallas_call(
        functools.partial(_ssm_kernel, n_t=CHUNK, n_rows=n_rows, has_h0=False),
        grid=(N_SSM_TILES, n),
        in_specs=[tile] + _ssm_table_specs(tabs, lambda j, i: j, CHUNK),
        out_specs=[tile, pl.BlockSpec((1, 1, 1, TILE_STATE_W), lambda j, i: (i, j, 0, 0))],
        out_shape=[jax.ShapeDtypeStruct((n * seq, D_SSM), F32),
                   jax.ShapeDtypeStruct((n, N_SSM_TILES, 1, TILE_STATE_W), F32)],
        compiler_params=_params(2),
        name="ssm_prompt",
    )(u, *tabs)


def _ssm_sample(u, h0, tabs, n_seq, n_new):
    tile = pl.BlockSpec((n_seq * n_new, LANE), lambda j: (0, j))
    state = pl.BlockSpec((n_seq, TILE_STATE_W), lambda j: (0, j))
    return pl.pallas_call(
        functools.partial(_ssm_kernel, n_t=n_new, n_rows=n_seq, has_h0=True),
        grid=(N_SSM_TILES,),
        in_specs=[tile, state] + _ssm_table_specs(tabs, lambda j: j, n_new),
        out_specs=[tile, state],
        out_shape=[jax.ShapeDtypeStruct((n_seq * n_new, D_SSM), F32),
                   jax.ShapeDtypeStruct((n_seq, N_SSM_TILES * TILE_STATE_W), F32)],
        compiler_params=_params(1),
        name="ssm_sample",
    )(u, h0, *tabs)


def _ssm_prep_kernel(are_ref, aim_ref, ldt_ref, bre_ref, bim_ref, cre_ref, cim_ref, d_ref,
                     kp_ref, wp_ref, vpt_ref, a1_ref, a2_ref, a1s_ref, a2s_ref, *, n_levels, n_new):
    a_re, a_im = are_ref[0], aim_ref[0]
    dt = jnp.exp(ldt_ref[0])
    mag = jnp.exp(a_re * dt)
    ab_re, ab_im = mag * jnp.cos(a_im * dt), mag * jnp.sin(a_im * dt)
    den = a_re * a_re + a_im * a_im
    nr, ni = ab_re - 1.0, ab_im
    f_re = (nr * a_re + ni * a_im) / den
    f_im = (ni * a_re - nr * a_im) / den
    shape = bre_ref.shape[1:]
    same_group = (lax.broadcasted_iota(jnp.int32, shape, 0) // SSM_GROUP
                  == lax.broadcasted_iota(jnp.int32, shape, 1) // SSM_STATE)
    b_re, b_im = bre_ref[0], bim_ref[0]
    x_re = jnp.where(same_group, f_re * b_re - f_im * b_im, 0.0)
    x_im = jnp.where(same_group, f_re * b_im + f_im * b_re, 0.0)
    c_re = jnp.where(same_group, cre_ref[0], 0.0)
    c_im = jnp.where(same_group, cim_ref[0], 0.0)
    pw = [(jnp.ones_like(ab_re), jnp.zeros_like(ab_im))]
    for _ in range(CHUNK):
        r, i = pw[-1]
        pw.append((r * ab_re - i * ab_im, r * ab_im + i * ab_re))
    cat = lambda r, i: jnp.concatenate([r, i], axis=1)
    e_pack = [cat(c_re * r - c_im * i, -(c_re * i + c_im * r)) for r, i in pw]
    split = lambda a: (a.astype(BF16), (a - a.astype(BF16).astype(F32)).astype(BF16))
    x_hi, x_lo = split(cat(x_re, x_im))
    x_3 = jnp.concatenate([x_hi, x_hi, x_lo], axis=1)
    k_lag = []
    for k in range(CHUNK):
        e_hi, e_lo = split(e_pack[k])
        k_lag.append(_dot_nt(x_3, jnp.concatenate([e_hi, e_lo, e_hi], axis=1)))
    diag = (lax.broadcasted_iota(jnp.int32, (LANE, LANE), 0) == lax.broadcasted_iota(jnp.int32, (LANE, LANE), 1))
    k_lag[0] = k_lag[0] + jnp.where(diag, d_ref[0], 0.0)
    zero = jnp.zeros((LANE, LANE), F32)
    for dl in range(N_PAIRS):
        top = cat(k_lag[2 * dl], k_lag[2 * dl + 1])
        bot = cat(k_lag[2 * dl - 1] if dl > 0 else zero, k_lag[2 * dl])
        kp_ref[0, dl] = jnp.concatenate([top, bot], axis=0).astype(BF16)
    for t in range(CHUNK):
        r, i = pw[CHUNK - 1 - t]
        wp_ref[0, t * LANE:(t + 1) * LANE, :] = cat(x_re * r - x_im * i, x_re * i + x_im * r).astype(BF16)
    for t in range(CHUNK):
        vpt_ref[0, t // 2, (t % 2) * LANE:(t % 2 + 1) * LANE, :] = e_pack[t + 1].astype(BF16)
    r, i = pw[CHUNK]
    for lvl in range(n_levels):
        a1_ref[0, lvl:lvl + 1, :], a2_ref[0, lvl:lvl + 1, :] = cat(r, r), cat(-i, i)
        r, i = r * r - i * i, 2.0 * r * i
    r, i = pw[n_new]
    a1s_ref[0], a2s_ref[0] = cat(r, r), cat(-i, i)


def _ssm_tables(a_re, a_im, log_dt, b_re, b_im, c_re, c_im, d_skip, n_levels, n_new):
    nj, tg = N_SSM_TILES, SSM_TILE_GROUPS
    half = tg * SSM_STATE
    rowv = lambda v: v.reshape(nj, 1, half)
    tiled = lambda m: jnp.tile(m.reshape(nj, LANE, SSM_STATE), (1, 1, tg))
    ins = (rowv(a_re), rowv(a_im), rowv(jnp.repeat(log_dt, SSM_STATE)),
           tiled(b_re.transpose(0, 2, 1)), tiled(b_im.transpose(0, 2, 1)), tiled(c_re), tiled(c_im),
           d_skip.reshape(nj, 1, LANE))
    shapes = ((nj, N_PAIRS, PAIR_W, PAIR_W), (nj, CHUNK * LANE, TILE_STATE_W), (nj, N_PAIRS, PAIR_W, TILE_STATE_W),
              (nj, n_levels, TILE_STATE_W), (nj, n_levels, TILE_STATE_W), (nj, 1, TILE_STATE_W), (nj, 1, TILE_STATE_W))
    dtypes = (BF16, BF16, BF16, F32, F32, F32, F32)
    per_tile = lambda s: pl.BlockSpec((1,) + tuple(s[1:]), lambda j: (j,) + (0,) * (len(s) - 1))
    kp, wp, vpt, a1, a2, a1s, a2s = pl.pallas_call(
        functools.partial(_ssm_prep_kernel, n_levels=n_levels, n_new=n_new),
        grid=(nj,),
        in_specs=[per_tile(x.shape) for x in ins],
        out_specs=[per_tile(s) for s in shapes],
        out_shape=[jax.ShapeDtypeStruct(s, dt) for s, dt in zip(shapes, dtypes)],
        compiler_params=_params(1),
        name="ssm_prep",
    )(*ins)
    return (kp, wp, vpt, a1, a2), (kp, wp, vpt, a1s, a2s)


def _head_of_lane(width):
    return lax.broadcasted_iota(jnp.int32, (1, width), 1) // HEAD_DIM


def _stack_heads(q):
    head = _head_of_lane(q.shape[1])
    q = q.astype(F32)
    return jnp.concatenate([jnp.where(head == h, q, 0.0) for h in range(HEADS)], axis=0).astype(BF16)


def _unstack_heads(x, rows):
    head = _head_of_lane(x.shape[1])
    out = x[(HEADS - 1) * rows:]
    for h in range(HEADS - 2, -1, -1):
        out = jnp.where(head == h, x[h * rows:(h + 1) * rows], out)
    return out


def _attn_prompt_kernel(bias_ref, q0, q1, k0, k1, v0, v1, o0, o1, l0, l1, kprev, vprev, *, d, m):
    qb = Q_BLOCK
    b = pl.program_id(1)
    n_par = min(d, ATTN_PAR)
    rd, wr = b % 2, (b + 1) % 2
    head = _head_of_lane(D_HEADS)

    @pl.when(b == 0)
    def _():
        kprev[0] = jnp.zeros(kprev.shape[1:], kprev.dtype)
        vprev[0] = jnp.zeros(vprev.shape[1:], vprev.dtype)

    def block(r, j, k_prev, v_prev):
        start = r + j * qb * d
        rows = pl.ds(start, qb, stride=d) if d > 1 else pl.ds(pl.multiple_of(start, qb), qb)
        both = lambda lo, hi: jnp.concatenate([lo[rows, :], hi[rows, :]], axis=1)
        qs = _stack_heads(both(q0, q1) * ATTN_SCALE)
        k_own, v_own = both(k0, k1).astype(BF16), both(v0, v1).astype(BF16)
        k = jnp.concatenate([k_prev, k_own], axis=0)
        v = jnp.concatenate([v_prev, v_own], axis=0)
        s = _dot_nt(qs, k) + bias_ref[jnp.minimum(b * m + j, 1)]
        mx = jnp.max(s, axis=-1, keepdims=True)
        pe = jnp.exp(s - mx)
        den = jnp.sum(pe, axis=-1, keepdims=True)
        pn = (pe * (1.0 / den)).astype(BF16)
        o = _dot(pn[:qb], jnp.where(head == 0, v, jnp.zeros_like(v)))
        for h in range(1, HEADS):
            o = o + _dot(pn[h * qb:(h + 1) * qb], jnp.where(head == h, v, jnp.zeros_like(v)))
        lse = _unstack_heads(jnp.broadcast_to(mx + jnp.log(den), (HEADS * qb, D_HEADS)), qb)
        o0[rows, :], o1[rows, :] = o[:, :LANE], o[:, LANE:]
        l0[rows, :], l1[rows, :] = lse[:, :LANE], lse[:, LANE:]
        return k_own, v_own

    def class_group(g, carry):
        classes = [g * n_par + t for t in range(n_par)]

        def step(j, prev):
            return tuple(x for t, r in enumerate(classes) for x in block(r, j, prev[2 * t], prev[2 * t + 1]))

        init = tuple(x for r in classes for x in (kprev[rd, r], vprev[rd, r]))
        last = lax.fori_loop(0, m, step, init, unroll=max(1, min(m, ATTN_PAR // n_par)))
        for t, r in enumerate(classes):
            kprev[wr, r], vprev[wr, r] = last[2 * t], last[2 * t + 1]
        return carry

    lax.fori_loop(0, d // n_par, class_group, 0)


def _attn_bias():
    qi = np.arange(HEADS * Q_BLOCK)[:, None] % Q_BLOCK
    ki = np.arange(2 * Q_BLOCK)[None, :]
    diff = Q_BLOCK + qi - ki
    valid = (diff >= 0) & (diff <= N_STEPS)
    return jnp.asarray(np.where(np.stack([valid & (ki >= Q_BLOCK), valid]), 0.0, -np.inf), F32)


def _attn_prompt(qkv, n, seq, grp):
    d = DILATIONS[grp]
    tb = min(ATTN_TB, seq)
    assert tb % (Q_BLOCK * d) == 0 and seq % tb == 0
    m, nb = tb // (Q_BLOCK * d), seq // tb
    cur = lambda c: pl.BlockSpec((tb, LANE), lambda i, b: (i * nb + b, c))
    qc, kc, vc = (2 * _qkv_col(which, grp) for which in range(3))
    bias = _attn_bias()
    return pl.pallas_call(
        functools.partial(_attn_prompt_kernel, d=d, m=m),
        grid=(n, nb),
        in_specs=[_const_spec(bias.shape), cur(qc), cur(qc + 1), cur(kc), cur(kc + 1), cur(vc), cur(vc + 1)],
        out_specs=[pl.BlockSpec((tb, LANE), lambda i, b: (i * nb + b, 0))] * 4,
        out_shape=[jax.ShapeDtypeStruct((n * seq, LANE), F32)] * 4,
        scratch_shapes=[pltpu.VMEM((2, d, Q_BLOCK, D_HEADS), BF16)] * 2,
        compiler_params=_params(2),
        name=f"attn_prompt_d{d}",
    )(bias, *([qkv] * 6))


def _attn_sample_kernel(q_ref, kn_ref, vn_ref, c_ref, o_ref, lse_ref, *, d, n_new, seq_blk):
    w = c_ref.shape[-1]
    n_rows = HEADS * n_new
    l_b = lax.broadcasted_iota(jnp.int32, (n_rows, w), 0) % n_new
    pos = lax.broadcasted_iota(jnp.int32, (n_rows, w), 1)
    valid_buf = (((w + l_b - pos) & (d - 1)) == 0) & (pos >= l_b)
    dn = (lax.broadcasted_iota(jnp.int32, (n_rows, LANE), 0) % n_new) - lax.broadcasted_iota(jnp.int32, (n_rows, LANE), 1)
    valid_new = (dn >= 0) & ((dn & (d - 1)) == 0)
    pad = jnp.zeros((LANE - n_new, D_HEADS), F32)

    def one_seq(s, carry):
        qs = _stack_heads(q_ref[s])
        k_t, v_t = c_ref[s, 0].astype(BF16), c_ref[s, 1].astype(BF16)
        k_n = jnp.concatenate([kn_ref[s], pad], axis=0).astype(BF16)
        v_n = jnp.concatenate([vn_ref[s], pad], axis=0).astype(BF16)
        s_b = jnp.where(valid_buf, _dot(qs, k_t) * ATTN_SCALE, -jnp.inf)
        s_n = jnp.where(valid_new, _dot_nt(qs, k_n) * ATTN_SCALE, -jnp.inf)
        mx = jnp.maximum(jnp.max(s_b, axis=-1, keepdims=True), jnp.max(s_n, axis=-1, keepdims=True))
        p_b, p_n = jnp.exp(s_b - mx), jnp.exp(s_n - mx)
        den = jnp.sum(p_b, axis=-1, keepdims=True) + jnp.sum(p_n, axis=-1, keepdims=True)
        o = (_dot_nt(p_b.astype(BF16), v_t) + _dot(p_n.astype(BF16), v_n)) / den
        o_ref[s] = _unstack_heads(o, n_new)
        lse_ref[s] = _unstack_heads(jnp.broadcast_to(mx + jnp.log(den), o.shape), n_new)
        return carry

    lax.fori_loop(0, seq_blk, one_seq, 0, unroll=True)


def _attn_sample_job(qkv, cache, n_seq, n_new, grp, n_steps):
    d = DILATIONS[grp]
    w = cache.shape[1]
    assert w == N_STEPS * d and n_new <= LANE and n_seq % n_steps == 0
    seq_blk = n_seq // n_steps
    qkv3 = qkv.reshape(n_seq, n_new, D_QKV)
    c_t = cache.transpose(0, 2, 3, 4, 1).reshape(n_seq, 2, D_HEADS, w)
    blk = (seq_blk, n_new, D_HEADS)
    col = lambda which: pl.BlockSpec(blk, lambda i: (i, 0, _qkv_col(which, grp)))
    out_spec = pl.BlockSpec(blk, lambda i: (i, 0, 0))
    out_shape = jax.ShapeDtypeStruct((n_seq, n_new, D_HEADS), F32)
    return _SideJob((qkv3, qkv3, qkv3, c_t),
                    (col(0), col(1), col(2), pl.BlockSpec((seq_blk, 2, D_HEADS, w), lambda i: (i, 0, 0, 0))),
                    (out_shape, out_shape), (out_spec, out_spec),
                    functools.partial(_attn_sample_kernel, d=d, n_new=n_new, seq_blk=seq_blk))


def _halves(o, lse):
    o, lse = o.reshape(-1, D_HEADS), lse.reshape(-1, D_HEADS)
    return o[:, :LANE], o[:, LANE:], lse[:, :LANE], lse[:, LANE:]


def _kv_tail_kernel(k_ref, v_ref, o_ref):
    o_ref[0, 0] = k_ref[...].T
    o_ref[0, 1] = v_ref[...].T


def _kv_tail(qkv, n, seq, grp):
    keep = min(WINDOWS[grp], seq)
    tt = min(keep, 512)
    assert keep % tt == 0 and (seq - keep) % tt == 0
    first = (seq - keep) // tt
    rows = lambda which: pl.BlockSpec((tt, D_HEADS), lambda i, t: (i * (seq // tt) + first + t, _qkv_col(which, grp)))
    out = pl.pallas_call(
        _kv_tail_kernel,
        grid=(n, keep // tt),
        in_specs=[rows(1), rows(2)],
        out_specs=pl.BlockSpec((1, 2, D_HEADS, tt), lambda i, t: (i, 0, 0, t)),
        out_shape=jax.ShapeDtypeStruct((n, 2, D_HEADS, keep), F32),
        compiler_params=_params(2),
        name=f"kv_tail_w{keep}",
    )(qkv, qkv)
    return out.reshape(1, n, 2, HEADS, HEAD_DIM, keep).transpose(0, 1, 5, 2, 3, 4)


def _kv_new_job(qkv, n_seq, n_new, n_steps):
    halves, l_parts = D_HEADS // LANE, 2
    n_units, l_blk = N_GROUPS * 2 * halves * l_parts, n_new // l_parts
    assert n_steps >= n_units and n_new % l_parts == 0
    unit = lambda i: jnp.minimum(i, n_units - 1)
    l_part = lambda i: unit(i) % l_parts
    half = lambda i: unit(i) // l_parts % halves
    kv = lambda i: unit(i) // (l_parts * halves) % 2
    grp = lambda i: unit(i) // (l_parts * halves * 2)
    in_spec = pl.BlockSpec((n_seq * n_new, LANE), lambda i: (0, halves * ((1 + kv(i)) * N_GROUPS + grp(i)) + half(i)))
    out_spec = pl.BlockSpec((1, l_blk, 1, LANE, n_seq), lambda i: (grp(i), l_part(i), kv(i), half(i), 0))
    out_shape = jax.ShapeDtypeStruct((N_GROUPS, n_new, 2, D_HEADS, n_seq), F32)

    def body(x_ref, o_ref):
        l0 = l_part(pl.program_id(0)) * l_blk
        for dl in range(l_blk):
            o_ref[0, dl, 0] = x_ref[pl.ds(l0 + dl, n_seq, stride=n_new), :].T

    return _SideJob((qkv,), (in_spec,), (out_shape,), (out_spec,), body)


def _kv_new_outputs(out, n_seq, n_new):
    out = out.reshape(N_GROUPS, 1, n_new, 2, HEADS, HEAD_DIM, n_seq).transpose(0, 1, 6, 2, 3, 4, 5)
    return [out[g] for g in range(N_GROUPS)]


def _gelu_tanh(x):
    return 0.5 * x * (1.0 + jnp.tanh(math.sqrt(2.0 / math.pi) * (x + 0.044715 * (x * x * x))))


def _stage3_kernel(*refs):
    x1_ref, yp_ref = refs[:2]
    attn = refs[2:2 + 4 * N_GROUPS]
    (p_ref, gm_ref, wgate_ref, wglu_ref, wbs_ref, wba_ref, wout_ref, g2_ref, wg_ref, wu_ref, wd_ref,
     gp_ref, wpg_ref, wpp_ref, gf_ref, y_ref) = refs[2 + 4 * N_GROUPS:]
    x1 = x1_ref[...]
    gates = jax.nn.sigmoid(_dot(_rms(x1, gm_ref[...]).astype(BF16), wgate_ref[...]))
    y = _gelu_tanh(yp_ref[...])
    glu = y * jax.nn.sigmoid(_dot(y.astype(BF16), wglu_ref[...]))
    full = lambda lo, hi: jnp.concatenate([lo[...], hi[...]], axis=1)
    outs = [full(attn[4 * g], attn[4 * g + 1]) for g in range(N_GROUPS)]
    lses = [full(attn[4 * g + 2], attn[4 * g + 3]) for g in range(N_GROUPS)]
    mx = jnp.maximum(jnp.maximum(lses[0], lses[1]), lses[2])
    es = [jnp.exp(l - mx) for l in lses]
    y_attn = (es[0] * outs[0] + es[1] * outs[1] + es[2] * outs[2]) / (es[0] + es[1] + es[2])
    merged = (gates[:, :D_MODEL] * _dot(glu.astype(BF16), wbs_ref[...])
              + gates[:, D_MODEL:] * _dot(y_attn.astype(BF16), wba_ref[...]))
    x = x1 + _dot(merged.astype(BF16), wout_ref[...])
    x = x + 0.5 * _swiglu(_rms(x, g2_ref[...]).astype(BF16), wg_ref, wu_ref, wd_ref)
    gate = jax.nn.sigmoid(_dot(_rms(x, gp_ref[...]).astype(BF16), wpg_ref[...]))
    x = x + gate * _dot(p_ref[...].astype(BF16), wpp_ref[...])
    y_ref[...] = _rms(x, gf_ref[...])


def _stage3(tok_inputs, weights):
    t, tm = tok_inputs[0].shape[0], STAGE_TM
    tok = lambda a: pl.BlockSpec((tm, a.shape[1]), lambda i: (i, 0))
    return pl.pallas_call(
        _stage3_kernel,
        grid=(t // tm,),
        in_specs=[tok(a) for a in tok_inputs] + [_const_spec(w.shape) for w in weights],
        out_specs=pl.BlockSpec((tm, D_MODEL), lambda i: (i, 0)),
        out_shape=jax.ShapeDtypeStruct((t, D_MODEL), F32),
        compiler_params=_params(1),
        name="stage3",
    )(*tok_inputs, *weights)


def kernel(x_prompt, x_sample, p_prompt, p_sample, cache_kv_w128, cache_kv_w512, cache_kv_w2048, state_ssm,
           g_ffn1, ffn1_w_gate, ffn1_w_up, ffn1_w_down, g_mix, w_in, ssm_a_re, ssm_a_im, ssm_log_dt,
           ssm_b_re, ssm_b_im, ssm_c_re, ssm_c_im, ssm_d, ssm_w_glu, w_br_ssm, w_br_attn, w_out,
           g_ffn2, ffn2_w_gate, ffn2_w_up, ffn2_w_down, g_ple, w_ple_gate, w_ple_proj, g_final):
    assert x_prompt.shape[-1] == D_MODEL and g_ffn1.shape[0] == 1
    n_p, seq, _ = x_prompt.shape
    n_s, n_new, _ = x_sample.shape
    caches = (cache_kv_w128, cache_kv_w512, cache_kv_w2048)
    row = lambda g: g.reshape(1, -1)
    bf = lambda w: w[0].astype(BF16)

    wi = w_in[0]
    win = wi[:, :D_SSM + D_QKV].astype(BF16)
    w_gates = wi[:, D_SSM + D_QKV:].astype(BF16)
    ffn1_w = (row(g_ffn1[0]), bf(ffn1_w_gate), bf(ffn1_w_up), bf(ffn1_w_down))
    s3_f32 = [w[0] for w in (ssm_w_glu, w_br_ssm, w_br_attn, w_out, ffn2_w_gate, ffn2_w_up, ffn2_w_down,
                             w_ple_gate, w_ple_proj)]

    n_chunks = seq // CHUNK
    tabs_p, tabs_s = _ssm_tables(
        ssm_a_re[0], ssm_a_im[0], ssm_log_dt[0], ssm_b_re[0], ssm_b_im[0], ssm_c_re[0], ssm_c_im[0],
        ssm_d[0], int(math.log2(n_chunks)), n_new)
    tile_state = (-1, N_SSM_TILES, 2, SSM_TILE_GROUPS, SSM_STATE)
    from_tiles = lambda h: h.reshape(tile_state).transpose(0, 1, 3, 4, 2).reshape(1, -1, N_SSM_GROUPS, SSM_STATE, 2)

    xs = x_sample.reshape(n_s * n_new, D_MODEL)
    x1_s, _ = _ffn1(xs, *ffn1_w, tm=min(STAGE_TM, xs.shape[0]))
    u_s, qkv_s, _ = _proj(x1_s, row(g_mix[0]), win, tm=min(STAGE_TM, xs.shape[0]))

    xp = x_prompt.reshape(n_p * seq, D_MODEL)
    attn_job = lambda grp, n_steps: _attn_sample_job(qkv_s, caches[grp][0], n_s, n_new, grp, n_steps)
    steps_a, steps_b = xp.shape[0] // FFN1_HOST_TM, xp.shape[0] // STAGE_TM
    x1, side = _ffn1(xp, *ffn1_w, tm=FFN1_HOST_TM,
                     jobs=[attn_job(2, steps_a), attn_job(0, steps_a)] + [_cast_job(w, steps_a) for w in s3_f32])
    attn_s = {2: side[0], 0: side[1]}
    wglu, wbs, wba, wout, wg2, wu2, wd2, wpg, wpp = (c[0] for c in side[2:])
    s3_w = (row(g_mix[0]), w_gates, wglu, wbs, wba, wout, row(g_ffn2[0]), wg2, wu2, wd2, row(g_ple[0]), wpg, wpp,
            row(g_final))
    u, qkv, side = _proj(x1, row(g_mix[0]), win, tm=STAGE_TM,
                         jobs=[attn_job(1, steps_b), _kv_new_job(qkv_s, n_s, n_new, steps_b)])
    attn_s[1] = side[0]
    kv_sample = _kv_new_outputs(side[1][0], n_s, n_new)
    y_pre, h_last = _ssm_prompt(u, tabs_p, n_p, seq)
    attn = [a for grp in range(N_GROUPS) for a in _attn_prompt(qkv, n_p, seq, grp)]
    tok = (x1, y_pre, *attn, p_prompt[0].reshape(n_p * seq, D_PLE))
    y_prompt = _stage3(tok, s3_w).reshape(n_p, seq, D_MODEL)
    kv_prompt = [_kv_tail(qkv, n_p, seq, grp) for grp in range(N_GROUPS)]
    ssm_prompt = from_tiles(h_last)

    h0 = state_ssm[0].reshape(n_s, N_SSM_TILES, SSM_TILE_GROUPS, SSM_STATE, 2)
    h0 = h0.transpose(0, 1, 4, 2, 3).reshape(n_s, N_SSM_TILES * TILE_STATE_W)
    y_pre, h_new = _ssm_sample(u_s, h0, tabs_s, n_s, n_new)
    attn = [a for grp in range(N_GROUPS) for a in _halves(*attn_s[grp])]
    tok = (x1_s, y_pre, *attn, p_sample[0].reshape(n_s * n_new, D_PLE))
    y_sample = _stage3(tok, s3_w).reshape(n_s, n_new, D_MODEL)
    ssm_sample = from_tiles(h_new)

    return (y_prompt, y_sample, kv_prompt[0], kv_prompt[1], kv_prompt[2], ssm_prompt,
            kv_sample[0], kv_sample[1], kv_sample[2], ssm_sample)
```

```python
import functools
import math
from typing import Callable, NamedTuple

import jax
import jax.numpy as jnp
import numpy as np
from jax import lax
from jax.experimental import pallas as pl
from jax.experimental.pallas import tpu as pltpu

D_MODEL = 1024
D_PLE = 256
D_FF = 2816
SSM_GROUP = 16
SSM_STATE = 64
D_SSM = 512
N_SSM_GROUPS = D_SSM // SSM_GROUP
HEAD_DIM = 64
HEADS = 4
WINDOWS = (128, 512, 2048)
DILATIONS = (1, 4, 16)
N_GROUPS = 3
D_HEADS = HEADS * HEAD_DIM
N_STEPS = 128
ATTN_SCALE = HEAD_DIM ** -0.5
EPS = 1e-6

LANE = 128
V7X_VMEM_LIMIT_BYTES = 56 * 1024 * 1024
STAGE_TM = 512
FFN1_HOST_TM = 256

CHUNK = 16
SCAN_BLOCK = 8
N_PAIRS = CHUNK // 2
PAIR_W = 2 * LANE
SSM_TILE_GROUPS = LANE // SSM_GROUP
N_SSM_TILES = N_SSM_GROUPS // SSM_TILE_GROUPS
TILE_STATE_W = SSM_TILE_GROUPS * 2 * SSM_STATE
Q_BLOCK = 128
ATTN_TB = 2048
ATTN_PAR = 4

D_QKV = 3 * N_GROUPS * D_HEADS


def _qkv_col(which, grp):
    return which * N_GROUPS + grp


BF16 = jnp.bfloat16
F32 = jnp.float32


def _dot(a, b):
    return jnp.dot(a, b, preferred_element_type=F32)


def _dot_nt(a, b):
    return lax.dot_general(a, b, (((1,), (1,)), ((), ())), preferred_element_type=F32)


def _rms(x, g):
    return x * lax.rsqrt(jnp.mean(x * x, axis=-1, keepdims=True) + EPS) * g


def _swiglu(xn, wg_ref, wu_ref, wd_ref):
    gate = _dot(xn, wg_ref[...])
    up = _dot(xn, wu_ref[...])
    act = (gate * jax.nn.sigmoid(gate) * up).astype(BF16)
    return _dot(act, wd_ref[...])


def _const_spec(shape):
    nd = len(shape)
    return pl.BlockSpec(shape, lambda *_: (0,) * nd, pipeline_mode=pl.Buffered(1))


def _params(n_grid_dims):
    return pltpu.CompilerParams(dimension_semantics=("arbitrary",) * n_grid_dims,
                                vmem_limit_bytes=V7X_VMEM_LIMIT_BYTES)


class _SideJob(NamedTuple):
    arrays: tuple
    in_specs: tuple
    out_shapes: tuple
    out_specs: tuple
    body: Callable


def _cast_job(w, n_steps):
    rows, cols = w.shape
    steps = max(s for s in range(1, n_steps + 1) if rows % s == 0 and (rows // s) % 16 == 0)
    spec = pl.BlockSpec((rows // steps, cols), lambda i: (jnp.minimum(i, steps - 1), 0))

    def body(src, dst):
        dst[...] = src[...].astype(BF16)

    return _SideJob((w,), (spec,), (jax.ShapeDtypeStruct(w.shape, BF16),), (spec,), body)


def _hosted_call(main_body, n_steps, arrays, in_specs, out_shapes, out_specs, jobs, name):
    n_in, n_out = len(arrays), len(out_shapes)

    def body(*refs):
        pos, job_in, job_out = n_in, [], []
        for job in jobs:
            job_in.append(refs[pos:pos + len(job.arrays)])
            pos += len(job.arrays)
        main_out = refs[pos:pos + n_out]
        pos += n_out
        for job in jobs:
            job_out.append(refs[pos:pos + len(job.out_shapes)])
            pos += len(job.out_shapes)
        for job, ins, outs in zip(jobs, job_in, job_out):
            job.body(*ins, *outs)
        main_body(*refs[:n_in], *main_out)

    outs = pl.pallas_call(
        body,
        grid=(n_steps,),
        in_specs=list(in_specs) + [sp for job in jobs for sp in job.in_specs],
        out_specs=list(out_specs) + [sp for job in jobs for sp in job.out_specs],
        out_shape=list(out_shapes) + [sh for job in jobs for sh in job.out_shapes],
        compiler_params=_params(1),
        name=name,
    )(*arrays, *[a for job in jobs for a in job.arrays])
    main, pos, per_job = outs[:n_out], n_out, []
    for job in jobs:
        per_job.append(outs[pos:pos + len(job.out_shapes)])
        pos += len(job.out_shapes)
    return main, per_job


def _ffn1_kernel(x_ref, g1_ref, wg_ref, wu_ref, wd_ref, x1_ref):
    x = x_ref[...]
    x1_ref[...] = x + 0.5 * _swiglu(_rms(x, g1_ref[...]).astype(BF16), wg_ref, wu_ref, wd_ref)


def _ffn1(x, g1, wg, wu, wd, tm, jobs=()):
    t = x.shape[0]
    tok = pl.BlockSpec((tm, D_MODEL), lambda i: (i, 0))
    (x1,), per_job = _hosted_call(
        _ffn1_kernel, t // tm, (x, g1, wg, wu, wd),
        [tok] + [_const_spec(a.shape) for a in (g1, wg, wu, wd)],
        [jax.ShapeDtypeStruct((t, D_MODEL), F32)], [tok], jobs, "ffn1")
    return x1, per_job


def _proj_kernel(x1_ref, gm_ref, win_ref, u_ref, qkv_ref):
    z = _dot(_rms(x1_ref[...], gm_ref[...]).astype(BF16), win_ref[...])
    u_ref[...] = z[:, :D_SSM]
    qkv_ref[...] = z[:, D_SSM:]


def _proj(x1, gm, win, tm, jobs=()):
    t = x1.shape[0]
    tok = lambda w: pl.BlockSpec((tm, w), lambda i: (i, 0))
    (u, qkv), per_job = _hosted_call(
        _proj_kernel, t // tm, (x1, gm, win),
        [tok(D_MODEL), _const_spec(gm.shape), _const_spec(win.shape)],
        [jax.ShapeDtypeStruct((t, D_SSM), F32), jax.ShapeDtypeStruct((t, D_QKV), F32)], [tok(D_SSM), tok(D_QKV)],
        jobs, "proj")
    return u, qkv, per_job


def _swap_halves(x):
    half = x.shape[1] // 2
    return jnp.concatenate([x[:, half:], x[:, :half]], axis=1)


def _cmul_split(x, a1, a2):
    return x * a1 + _swap_halves(x) * a2


def _ssm_kernel(*refs, n_t, n_rows, has_h0):
    if has_h0:
        u_ref, h0_ref, kp_ref, wp_ref, vpt_ref, a1_ref, a2_ref, y_ref, hl_ref = refs
    else:
        u_ref, kp_ref, wp_ref, vpt_ref, a1_ref, a2_ref, y_ref, hl_ref = refs
    n_pairs = n_t // 2
    tok = lambda t: pl.ds(t, n_rows, stride=n_t)
    up = [jnp.concatenate([u_ref[tok(2 * a), :], u_ref[tok(2 * a + 1), :]], axis=1).astype(BF16)
          for a in range(n_pairs)]
    s = _dot(jnp.concatenate(up, axis=1), wp_ref[0])
    in_chunk = [None] * n_pairs
    todo = [(b, a) for b in range(n_pairs) for a in range(b + 1)]

    def in_chunk_products(count):
        for b, a in [todo.pop(0) for _ in range(min(count, len(todo)))]:
            prod = _dot(up[a], kp_ref[0, b - a])
            in_chunk[b] = prod if in_chunk[b] is None else in_chunk[b] + prod

    if has_h0:
        hprev = h0_ref[...]
        h_last = _cmul_split(hprev, a1_ref[0], a2_ref[0]) + s
        hl_ref[...] = h_last
    else:
        blk = SCAN_BLOCK
        assert n_rows % blk == 0 and a1_ref.shape[1] == blk
        row = lax.broadcasted_iota(jnp.int32, (blk, s.shape[1]), 0)
        carry = jnp.zeros((1, s.shape[1]), F32)
        entering = []
        for i in range(n_rows // blk):
            h = s[i * blk:(i + 1) * blk]
            for lvl in range(int(math.log2(blk))):
                sh = 1 << lvl
                shifted = jnp.where(row >= sh, pltpu.roll(h, sh, axis=0), 0.0)
                h = h + _cmul_split(shifted, a1_ref[0, sh - 1:sh, :], a2_ref[0, sh - 1:sh, :])
            carry_b = jnp.broadcast_to(carry, h.shape)
            h = h + _cmul_split(carry_b, a1_ref[0], a2_ref[0])
            entering.append(jnp.where(row == 0, carry_b, pltpu.roll(h, 1, axis=0)))
            carry = h[blk - 1:blk]
            in_chunk_products(-(-len(todo) // (n_rows // blk - i)))
        hprev = jnp.concatenate(entering, axis=0)
        hl_ref[0, 0] = carry
    in_chunk_products(len(todo))
    hpb = hprev.astype(BF16)
    for b in range(n_pairs):
        acc = _dot_nt(hpb, vpt_ref[0, b]) + in_chunk[b]
        y_ref[tok(2 * b), :] = acc[:, :LANE]
        y_ref[tok(2 * b + 1), :] = acc[:, LANE:]


def _ssm_table_specs(tabs, idx, n_t):
    kp, wp, vp, a1, a2 = tabs
    n_pairs, w_rows = n_t // 2, n_t * LANE
    return [pl.BlockSpec((1, n_pairs) + kp.shape[2:], lambda *g: (idx(*g), 0, 0, 0)),
            pl.BlockSpec((1, w_rows, wp.shape[2]), lambda *g: (idx(*g), wp.shape[1] // w_rows - 1, 0)),
            pl.BlockSpec((1, n_pairs) + vp.shape[2:], lambda *g: (idx(*g), 0, 0, 0)),
            pl.BlockSpec((1,) + a1.shape[1:], lambda *g: (idx(*g), 0, 0)),
            pl.BlockSpec((1,) + a2.shape[1:], lambda *g: (idx(*g), 0, 0))]


def _ssm_prompt(u, tabs, n, seq):
    n_rows = seq // CHUNK
    tile = pl.BlockSpec((seq, LANE), lambda j, i: (i, j))
    return pl.pallas_call(
        functools.partial(_ssm_kernel, n_t=CHUNK, n_rows=n_rows, has_h0=False),
        grid=(N_SSM_TILES, n),
        in_specs=[tile] + _ssm_table_specs(tabs, lambda j, i: j, CHUNK),
        out_specs=[tile, pl.BlockSpec((1, 1, 1, TILE_STATE_W), lambda j, i: (i, j, 0, 0))],
        out_shape=[jax.ShapeDtypeStruct((n * seq, D_SSM), F32),
                   jax.ShapeDtypeStruct((n, N_SSM_TILES, 1, TILE_STATE_W), F32)],
        compiler_params=_params(2),
        name="ssm_prompt",
    )(u, *tabs)


def _ssm_sample(u, h0, tabs, n_seq, n_new):
    tile = pl.BlockSpec((n_seq * n_new, LANE), lambda j: (0, j))
    state = pl.BlockSpec((n_seq, TILE_STATE_W), lambda j: (0, j))
    return pl.pallas_call(
        functools.partial(_ssm_kernel, n_t=n_new, n_rows=n_seq, has_h0=True),
        grid=(N_SSM_TILES,),
        in_specs=[tile, state] + _ssm_table_specs(tabs, lambda j: j, n_new),
        out_specs=[tile, state],
        out_shape=[jax.ShapeDtypeStruct((n_seq * n_new, D_SSM), F32),
                   jax.ShapeDtypeStruct((n_seq, N_SSM_TILES * TILE_STATE_W), F32)],
        compiler_params=_params(1),
        name="ssm_sample",
    )(u, h0, *tabs)


def _ssm_prep_kernel(are_ref, aim_ref, ldt_ref, bre_ref, bim_ref, cre_ref, cim_ref, d_ref,
                     kp_ref, wp_ref, vpt_ref, a1_ref, a2_ref, a1s_ref, a2s_ref, *, n_levels, n_new):
    a_re, a_im = are_ref[0], aim_ref[0]
    dt = jnp.exp(ldt_ref[0])
    mag = jnp.exp(a_re * dt)
    ab_re, ab_im = mag * jnp.cos(a_im * dt), mag * jnp.sin(a_im * dt)
    den = a_re * a_re + a_im * a_im
    nr, ni = ab_re - 1.0, ab_im
    f_re = (nr * a_re + ni * a_im) / den
    f_im = (ni * a_re - nr * a_im) / den
    shape = bre_ref.shape[1:]
    same_group = (lax.broadcasted_iota(jnp.int32, shape, 0) // SSM_GROUP
                  == lax.broadcasted_iota(jnp.int32, shape, 1) // SSM_STATE)
    b_re, b_im = bre_ref[0], bim_ref[0]
    x_re = jnp.where(same_group, f_re * b_re - f_im * b_im, 0.0)
    x_im = jnp.where(same_group, f_re * b_im + f_im * b_re, 0.0)
    c_re = jnp.where(same_group, cre_ref[0], 0.0)
    c_im = jnp.where(same_group, cim_ref[0], 0.0)
    pw = [(jnp.ones_like(ab_re), jnp.zeros_like(ab_im))]
    for _ in range(CHUNK):
        r, i = pw[-1]
        pw.append((r * ab_re - i * ab_im, r * ab_im + i * ab_re))
    cat = lambda r, i: jnp.concatenate([r, i], axis=1)
    e_pack = [cat(c_re * r - c_im * i, -(c_re * i + c_im * r)) for r, i in pw]
    split = lambda a: (a.astype(BF16), (a - a.astype(BF16).astype(F32)).astype(BF16))
    x_hi, x_lo = split(cat(x_re, x_im))
    x_3 = jnp.concatenate([x_hi, x_hi, x_lo], axis=1)
    k_lag = []
    for k in range(CHUNK):
        e_hi, e_lo = split(e_pack[k])
        k_lag.append(_dot_nt(x_3, jnp.concatenate([e_hi, e_lo, e_hi], axis=1)))
    diag = (lax.broadcasted_iota(jnp.int32, (LANE, LANE), 0) == lax.broadcasted_iota(jnp.int32, (LANE, LANE), 1))
    k_lag[0] = k_lag[0] + jnp.where(diag, d_ref[0], 0.0)
    zero = jnp.zeros((LANE, LANE), F32)
    for dl in range(N_PAIRS):
        top = cat(k_lag[2 * dl], k_lag[2 * dl + 1])
        bot = cat(k_lag[2 * dl - 1] if dl > 0 else zero, k_lag[2 * dl])
        kp_ref[0, dl] = jnp.concatenate([top, bot], axis=0).astype(BF16)
    for t in range(CHUNK):
        r, i = pw[CHUNK - 1 - t]
        wp_ref[0, t * LANE:(t + 1) * LANE, :] = cat(x_re * r - x_im * i, x_re * i + x_im * r).astype(BF16)
    for t in range(CHUNK):
        vpt_ref[0, t // 2, (t % 2) * LANE:(t % 2 + 1) * LANE, :] = e_pack[t + 1].astype(BF16)
    r, i = step_r, step_i = pw[CHUNK]
    for s in range(n_levels):
        a1_ref[0, s:s + 1, :], a2_ref[0, s:s + 1, :] = cat(r, r), cat(-i, i)
        r, i = r * step_r - i * step_i, r * step_i + i * step_r
    r, i = pw[n_new]
    a1s_ref[0], a2s_ref[0] = cat(r, r), cat(-i, i)


def _ssm_tables(a_re, a_im, log_dt, b_re, b_im, c_re, c_im, d_skip, n_levels, n_new):
    nj, tg = N_SSM_TILES, SSM_TILE_GROUPS
    half = tg * SSM_STATE
    rowv = lambda v: v.reshape(nj, 1, half)
    tiled = lambda m: jnp.tile(m.reshape(nj, LANE, SSM_STATE), (1, 1, tg))
    ins = (rowv(a_re), rowv(a_im), rowv(jnp.repeat(log_dt, SSM_STATE)),
           tiled(b_re.transpose(0, 2, 1)), tiled(b_im.transpose(0, 2, 1)), tiled(c_re), tiled(c_im),
           d_skip.reshape(nj, 1, LANE))
    shapes = ((nj, N_PAIRS, PAIR_W, PAIR_W), (nj, CHUNK * LANE, TILE_STATE_W), (nj, N_PAIRS, PAIR_W, TILE_STATE_W),
              (nj, n_levels, TILE_STATE_W), (nj, n_levels, TILE_STATE_W), (nj, 1, TILE_STATE_W), (nj, 1, TILE_STATE_W))
    dtypes = (BF16, BF16, BF16, F32, F32, F32, F32)
    per_tile = lambda s: pl.BlockSpec((1,) + tuple(s[1:]), lambda j: (j,) + (0,) * (len(s) - 1))
    kp, wp, vpt, a1, a2, a1s, a2s = pl.pallas_call(
        functools.partial(_ssm_prep_kernel, n_levels=n_levels, n_new=n_new),
        grid=(nj,),
        in_specs=[per_tile(x.shape) for x in ins],
        out_specs=[per_tile(s) for s in shapes],
        out_shape=[jax.ShapeDtypeStruct(s, dt) for s, dt in zip(shapes, dtypes)],
        compiler_params=_params(1),
        name="ssm_prep",
    )(*ins)
    return (kp, wp, vpt, a1, a2), (kp, wp, vpt, a1s, a2s)


def _head_of_lane(width):
    return lax.broadcasted_iota(jnp.int32, (1, width), 1) // HEAD_DIM


def _stack_heads(q):
    head = _head_of_lane(q.shape[1])
    q = q.astype(F32)
    return jnp.concatenate([jnp.where(head == h, q, 0.0) for h in range(HEADS)], axis=0).astype(BF16)


def _unstack_heads(x, rows):
    head = _head_of_lane(x.shape[1])
    out = x[(HEADS - 1) * rows:]
    for h in range(HEADS - 2, -1, -1):
        out = jnp.where(head == h, x[h * rows:(h + 1) * rows], out)
    return out


def _attn_prompt_kernel(bias_ref, q0, q1, k0, k1, v0, v1, o0, o1, l0, l1, kprev, vprev, *, d, m):
    qb = Q_BLOCK
    b = pl.program_id(1)
    n_par = min(d, ATTN_PAR)
    rd, wr = b % 2, (b + 1) % 2
    head = _head_of_lane(D_HEADS)

    @pl.when(b == 0)
    def _():
        kprev[0] = jnp.zeros(kprev.shape[1:], kprev.dtype)
        vprev[0] = jnp.zeros(vprev.shape[1:], vprev.dtype)

    def block(r, j, k_prev, v_prev):
        start = r + j * qb * d
        rows = pl.ds(start, qb, stride=d) if d > 1 else pl.ds(pl.multiple_of(start, qb), qb)
        both = lambda lo, hi: jnp.concatenate([lo[rows, :], hi[rows, :]], axis=1)
        qs = _stack_heads(both(q0, q1) * ATTN_SCALE)
        k_own, v_own = both(k0, k1).astype(BF16), both(v0, v1).astype(BF16)
        k = jnp.concatenate([k_prev, k_own], axis=0)
        v = jnp.concatenate([v_prev, v_own], axis=0)
        s = _dot_nt(qs, k) + bias_ref[jnp.minimum(b * m + j, 1)]
        mx = jnp.max(s, axis=-1, keepdims=True)
        pe = jnp.exp(s - mx)
        den = jnp.sum(pe, axis=-1, keepdims=True)
        pn = (pe * (1.0 / den)).astype(BF16)
        o = _dot(pn[:qb], jnp.where(head == 0, v, jnp.zeros_like(v)))
        for h in range(1, HEADS):
            o = o + _dot(pn[h * qb:(h + 1) * qb], jnp.where(head == h, v, jnp.zeros_like(v)))
        lse = _unstack_heads(jnp.broadcast_to(mx + jnp.log(den), (HEADS * qb, D_HEADS)), qb)
        o0[rows, :], o1[rows, :] = o[:, :LANE], o[:, LANE:]
        l0[rows, :], l1[rows, :] = lse[:, :LANE], lse[:, LANE:]
        return k_own, v_own

    def class_group(g, carry):
        classes = [g * n_par + t for t in range(n_par)]

        def step(j, prev):
            return tuple(x for t, r in enumerate(classes) for x in block(r, j, prev[2 * t], prev[2 * t + 1]))

        init = tuple(x for r in classes for x in (kprev[rd, r], vprev[rd, r]))
        last = lax.fori_loop(0, m, step, init, unroll=max(1, min(m, ATTN_PAR // n_par)))
        for t, r in enumerate(classes):
            kprev[wr, r], vprev[wr, r] = last[2 * t], last[2 * t + 1]
        return carry

    lax.fori_loop(0, d // n_par, class_group, 0)


def _attn_bias():
    qi = np.arange(HEADS * Q_BLOCK)[:, None] % Q_BLOCK
    ki = np.arange(2 * Q_BLOCK)[None, :]
    diff = Q_BLOCK + qi - ki
    valid = (diff >= 0) & (diff <= N_STEPS)
    return jnp.asarray(np.where(np.stack([valid & (ki >= Q_BLOCK), valid]), 0.0, -np.inf), F32)


def _attn_prompt(qkv, n, seq, grp):
    d = DILATIONS[grp]
    tb = min(ATTN_TB, seq)
    assert tb % (Q_BLOCK * d) == 0 and seq % tb == 0
    m, nb = tb // (Q_BLOCK * d), seq // tb
    cur = lambda c: pl.BlockSpec((tb, LANE), lambda i, b: (i * nb + b, c))
    qc, kc, vc = (2 * _qkv_col(which, grp) for which in range(3))
    bias = _attn_bias()
    return pl.pallas_call(
        functools.partial(_attn_prompt_kernel, d=d, m=m),
        grid=(n, nb),
        in_specs=[_const_spec(bias.shape), cur(qc), cur(qc + 1), cur(kc), cur(kc + 1), cur(vc), cur(vc + 1)],
        out_specs=[pl.BlockSpec((tb, LANE), lambda i, b: (i * nb + b, 0))] * 4,
        out_shape=[jax.ShapeDtypeStruct((n * seq, LANE), F32)] * 4,
        scratch_shapes=[pltpu.VMEM((2, d, Q_BLOCK, D_HEADS), BF16)] * 2,
        compiler_params=_params(2),
        name=f"attn_prompt_d{d}",
    )(bias, *([qkv] * 6))


def _attn_sample_kernel(q_ref, kn_ref, vn_ref, c_ref, o_ref, lse_ref, *, d, n_new, seq_blk):
    w = c_ref.shape[-1]
    n_rows = HEADS * n_new
    l_b = lax.broadcasted_iota(jnp.int32, (n_rows, w), 0) % n_new
    pos = lax.broadcasted_iota(jnp.int32, (n_rows, w), 1)
    valid_buf = (((w + l_b - pos) & (d - 1)) == 0) & (pos >= l_b)
    dn = (lax.broadcasted_iota(jnp.int32, (n_rows, LANE), 0) % n_new) - lax.broadcasted_iota(jnp.int32, (n_rows, LANE), 1)
    valid_new = (dn >= 0) & ((dn & (d - 1)) == 0)
    pad = jnp.zeros((LANE - n_new, D_HEADS), F32)

    def one_seq(s, carry):
        qs = _stack_heads(q_ref[s])
        k_t, v_t = c_ref[s, 0].astype(BF16), c_ref[s, 1].astype(BF16)
        k_n = jnp.concatenate([kn_ref[s], pad], axis=0).astype(BF16)
        v_n = jnp.concatenate([vn_ref[s], pad], axis=0).astype(BF16)
        s_b = jnp.where(valid_buf, _dot(qs, k_t) * ATTN_SCALE, -jnp.inf)
        s_n = jnp.where(valid_new, _dot_nt(qs, k_n) * ATTN_SCALE, -jnp.inf)
        mx = jnp.maximum(jnp.max(s_b, axis=-1, keepdims=True), jnp.max(s_n, axis=-1, keepdims=True))
        p_b, p_n = jnp.exp(s_b - mx), jnp.exp(s_n - mx)
        den = jnp.sum(p_b, axis=-1, keepdims=True) + jnp.sum(p_n, axis=-1, keepdims=True)
        o = (_dot_nt(p_b.astype(BF16), v_t) + _dot(p_n.astype(BF16), v_n)) / den
        o_ref[s] = _unstack_heads(o, n_new)
        lse_ref[s] = _unstack_heads(jnp.broadcast_to(mx + jnp.log(den), o.shape), n_new)
        return carry

    lax.fori_loop(0, seq_blk, one_seq, 0, unroll=True)


def _attn_sample_job(qkv, cache, n_seq, n_new, grp, n_steps):
    d = DILATIONS[grp]
    w = cache.shape[1]
    assert w == N_STEPS * d and n_new <= LANE and n_seq % n_steps == 0
    seq_blk = n_seq // n_steps
    qkv3 = qkv.reshape(n_seq, n_new, D_QKV)
    c_t = cache.transpose(0, 2, 3, 4, 1).reshape(n_seq, 2, D_HEADS, w)
    blk = (seq_blk, n_new, D_HEADS)
    col = lambda which: pl.BlockSpec(blk, lambda i: (i, 0, _qkv_col(which, grp)))
    out_spec = pl.BlockSpec(blk, lambda i: (i, 0, 0))
    out_shape = jax.ShapeDtypeStruct((n_seq, n_new, D_HEADS), F32)
    return _SideJob((qkv3, qkv3, qkv3, c_t),
                    (col(0), col(1), col(2), pl.BlockSpec((seq_blk, 2, D_HEADS, w), lambda i: (i, 0, 0, 0))),
                    (out_shape, out_shape), (out_spec, out_spec),
                    functools.partial(_attn_sample_kernel, d=d, n_new=n_new, seq_blk=seq_blk))


def _halves(o, lse):
    o, lse = o.reshape(-1, D_HEADS), lse.reshape(-1, D_HEADS)
    return o[:, :LANE], o[:, LANE:], lse[:, :LANE], lse[:, LANE:]


def _kv_tail_kernel(k_ref, v_ref, o_ref):
    o_ref[0, 0] = k_ref[...].T
    o_ref[0, 1] = v_ref[...].T


def _kv_tail(qkv, n, seq, grp):
    keep = min(WINDOWS[grp], seq)
    tt = min(keep, 512)
    assert keep % tt == 0 and (seq - keep) % tt == 0
    first = (seq - keep) // tt
    rows = lambda which: pl.BlockSpec((tt, D_HEADS), lambda i, t: (i * (seq // tt) + first + t, _qkv_col(which, grp)))
    out = pl.pallas_call(
        _kv_tail_kernel,
        grid=(n, keep // tt),
        in_specs=[rows(1), rows(2)],
        out_specs=pl.BlockSpec((1, 2, D_HEADS, tt), lambda i, t: (i, 0, 0, t)),
        out_shape=jax.ShapeDtypeStruct((n, 2, D_HEADS, keep), F32),
        compiler_params=_params(2),
        name=f"kv_tail_w{keep}",
    )(qkv, qkv)
    return out.reshape(1, n, 2, HEADS, HEAD_DIM, keep).transpose(0, 1, 5, 2, 3, 4)


def _kv_new_job(qkv, n_seq, n_new, n_steps):
    halves, l_parts = D_HEADS // LANE, 2
    n_units, l_blk = N_GROUPS * 2 * halves * l_parts, n_new // l_parts
    assert n_steps >= n_units and n_new % l_parts == 0
    unit = lambda i: jnp.minimum(i, n_units - 1)
    l_part = lambda i: unit(i) % l_parts
    half = lambda i: unit(i) // l_parts % halves
    kv = lambda i: unit(i) // (l_parts * halves) % 2
    grp = lambda i: unit(i) // (l_parts * halves * 2)
    in_spec = pl.BlockSpec((n_seq * n_new, LANE), lambda i: (0, halves * ((1 + kv(i)) * N_GROUPS + grp(i)) + half(i)))
    out_spec = pl.BlockSpec((1, l_blk, 1, LANE, n_seq), lambda i: (grp(i), l_part(i), kv(i), half(i), 0))
    out_shape = jax.ShapeDtypeStruct((N_GROUPS, n_new, 2, D_HEADS, n_seq), F32)

    def body(x_ref, o_ref):
        l0 = l_part(pl.program_id(0)) * l_blk
        for dl in range(l_blk):
            o_ref[0, dl, 0] = x_ref[pl.ds(l0 + dl, n_seq, stride=n_new), :].T

    return _SideJob((qkv,), (in_spec,), (out_shape,), (out_spec,), body)


def _kv_new_outputs(out, n_seq, n_new):
    out = out.reshape(N_GROUPS, 1, n_new, 2, HEADS, HEAD_DIM, n_seq).transpose(0, 1, 6, 2, 3, 4, 5)
    return [out[g] for g in range(N_GROUPS)]


def _gelu_tanh(x):
    return 0.5 * x * (1.0 + jnp.tanh(math.sqrt(2.0 / math.pi) * (x + 0.044715 * (x * x * x))))


def _stage3_kernel(*refs):
    x1_ref, yp_ref = refs[:2]
    attn = refs[2:2 + 4 * N_GROUPS]
    (p_ref, gm_ref, wgate_ref, wglu_ref, wbs_ref, wba_ref, wout_ref, g2_ref, wg_ref, wu_ref, wd_ref,
     gp_ref, wpg_ref, wpp_ref, gf_ref, y_ref) = refs[2 + 4 * N_GROUPS:]
    x1 = x1_ref[...]
    gates = jax.nn.sigmoid(_dot(_rms(x1, gm_ref[...]).astype(BF16), wgate_ref[...]))
    y = _gelu_tanh(yp_ref[...])
    glu = y * jax.nn.sigmoid(_dot(y.astype(BF16), wglu_ref[...]))
    full = lambda lo, hi: jnp.concatenate([lo[...], hi[...]], axis=1)
    outs = [full(attn[4 * g], attn[4 * g + 1]) for g in range(N_GROUPS)]
    lses = [full(attn[4 * g + 2], attn[4 * g + 3]) for g in range(N_GROUPS)]
    mx = jnp.maximum(jnp.maximum(lses[0], lses[1]), lses[2])
    es = [jnp.exp(l - mx) for l in lses]
    y_attn = (es[0] * outs[0] + es[1] * outs[1] + es[2] * outs[2]) / (es[0] + es[1] + es[2])
    merged = (gates[:, :D_MODEL] * _dot(glu.astype(BF16), wbs_ref[...])
              + gates[:, D_MODEL:] * _dot(y_attn.astype(BF16), wba_ref[...]))
    x = x1 + _dot(merged.astype(BF16), wout_ref[...])
    x = x + 0.5 * _swiglu(_rms(x, g2_ref[...]).astype(BF16), wg_ref, wu_ref, wd_ref)
    gate = jax.nn.sigmoid(_dot(_rms(x, gp_ref[...]).astype(BF16), wpg_ref[...]))
    x = x + gate * _dot(p_ref[...].astype(BF16), wpp_ref[...])
    y_ref[...] = _rms(x, gf_ref[...])


def _stage3(tok_inputs, weights):
    t, tm = tok_inputs[0].shape[0], STAGE_TM
    tok = lambda a: pl.BlockSpec((tm, a.shape[1]), lambda i: (i, 0))
    return pl.pallas_call(
        _stage3_kernel,
        grid=(t // tm,),
        in_specs=[tok(a) for a in tok_inputs] + [_const_spec(w.shape) for w in weights],
        out_specs=pl.BlockSpec((tm, D_MODEL), lambda i: (i, 0)),
        out_shape=jax.ShapeDtypeStruct((t, D_MODEL), F32),
        compiler_params=_params(1),
        name="stage3",
    )(*tok_inputs, *weights)


def kernel(x_prompt, x_sample, p_prompt, p_sample, cache_kv_w128, cache_kv_w512, cache_kv_w2048, state_ssm,
           g_ffn1, ffn1_w_gate, ffn1_w_up, ffn1_w_down, g_mix, w_in, ssm_a_re, ssm_a_im, ssm_log_dt,
           ssm_b_re, ssm_b_im, ssm_c_re, ssm_c_im, ssm_d, ssm_w_glu, w_br_ssm, w_br_attn, w_out,
           g_ffn2, ffn2_w_gate, ffn2_w_up, ffn2_w_down, g_ple, w_ple_gate, w_ple_proj, g_final):
    assert x_prompt.shape[-1] == D_MODEL and g_ffn1.shape[0] == 1
    n_p, seq, _ = x_prompt.shape
    n_s, n_new, _ = x_sample.shape
    caches = (cache_kv_w128, cache_kv_w512, cache_kv_w2048)
    row = lambda g: g.reshape(1, -1)
    bf = lambda w: w[0].astype(BF16)

    wi = w_in[0]
    win = wi[:, :D_SSM + D_QKV].astype(BF16)
    w_gates = wi[:, D_SSM + D_QKV:].astype(BF16)
    ffn1_w = (row(g_ffn1[0]), bf(ffn1_w_gate), bf(ffn1_w_up), bf(ffn1_w_down))
    s3_f32 = [w[0] for w in (ssm_w_glu, w_br_ssm, w_br_attn, w_out, ffn2_w_gate, ffn2_w_up, ffn2_w_down,
                             w_ple_gate, w_ple_proj)]

    tabs_p, tabs_s = _ssm_tables(
        ssm_a_re[0], ssm_a_im[0], ssm_log_dt[0], ssm_b_re[0], ssm_b_im[0], ssm_c_re[0], ssm_c_im[0],
        ssm_d[0], SCAN_BLOCK, n_new)
    tile_state = (-1, N_SSM_TILES, 2, SSM_TILE_GROUPS, SSM_STATE)
    from_tiles = lambda h: h.reshape(tile_state).transpose(0, 1, 3, 4, 2).reshape(1, -1, N_SSM_GROUPS, SSM_STATE, 2)

    xs = x_sample.reshape(n_s * n_new, D_MODEL)
    x1_s, _ = _ffn1(xs, *ffn1_w, tm=min(STAGE_TM, xs.shape[0]))
    u_s, qkv_s, _ = _proj(x1_s, row(g_mix[0]), win, tm=min(STAGE_TM, xs.shape[0]))

    xp = x_prompt.reshape(n_p * seq, D_MODEL)
    attn_job = lambda grp, n_steps: _attn_sample_job(qkv_s, caches[grp][0], n_s, n_new, grp, n_steps)
    steps_a, steps_b = xp.shape[0] // FFN1_HOST_TM, xp.shape[0] // STAGE_TM
    x1, side = _ffn1(xp, *ffn1_w, tm=FFN1_HOST_TM,
                     jobs=[attn_job(2, steps_a), attn_job(0, steps_a)] + [_cast_job(w, steps_a) for w in s3_f32])
    attn_s = {2: side[0], 0: side[1]}
    wglu, wbs, wba, wout, wg2, wu2, wd2, wpg, wpp = (c[0] for c in side[2:])
    s3_w = (row(g_mix[0]), w_gates, wglu, wbs, wba, wout, row(g_ffn2[0]), wg2, wu2, wd2, row(g_ple[0]), wpg, wpp,
            row(g_final))
    u, qkv, side = _proj(x1, row(g_mix[0]), win, tm=STAGE_TM,
                         jobs=[attn_job(1, steps_b), _kv_new_job(qkv_s, n_s, n_new, steps_b)])
    attn_s[1] = side[0]
    kv_sample = _kv_new_outputs(side[1][0], n_s, n_new)
    y_pre, h_last = _ssm_prompt(u, tabs_p, n_p, seq)
    attn = [a for grp in range(N_GROUPS) for a in _attn_prompt(qkv, n_p, seq, grp)]
    tok = (x1, y_pre, *attn, p_prompt[0].reshape(n_p * seq, D_PLE))
    y_prompt = _stage3(tok, s3_w).reshape(n_p, seq, D_MODEL)
    kv_prompt = [_kv_tail(qkv, n_p, seq, grp) for grp in range(N_GROUPS)]
    ssm_prompt = from_tiles(h_last)

    h0 = state_ssm[0].reshape(n_s, N_SSM_TILES, SSM_TILE_GROUPS, SSM_STATE, 2)
    h0 = h0.transpose(0, 1, 4, 2, 3).reshape(n_s, N_SSM_TILES * TILE_STATE_W)
    y_pre, h_new = _ssm_sample(u_s, h0, tabs_s, n_s, n_new)
    attn = [a for grp in range(N_GROUPS) for a in _halves(*attn_s[grp])]
    tok = (x1_s, y_pre, *attn, p_sample[0].reshape(n_s * n_new, D_PLE))
    y_sample = _stage3(tok, s3_w).reshape(n_s, n_new, D_MODEL)
    ssm_sample = from_tiles(h_new)

    return (y_prompt, y_sample, kv_prompt[0], kv_prompt[1], kv_prompt[2], ssm_prompt,
            kv_sample[0], kv_sample[1], kv_sample[2], ssm_sample)
```

```python
import functools
import math
from typing import Callable, NamedTuple

import jax
import jax.numpy as jnp
import numpy as np
from jax import lax
from jax.experimental import pallas as pl
from jax.experimental.pallas import tpu as pltpu

D_MODEL = 1024
D_PLE = 256
D_FF = 2816
SSM_GROUP = 16
SSM_STATE = 64
D_SSM = 512
N_SSM_GROUPS = D_SSM // SSM_GROUP
HEAD_DIM = 64
HEADS = 4
WINDOWS = (128, 512, 2048)
DILATIONS = (1, 4, 16)
N_GROUPS = 3
D_HEADS = HEADS * HEAD_DIM
N_STEPS = 128
ATTN_SCALE = HEAD_DIM ** -0.5
EPS = 1e-6

LANE = 128
V7X_VMEM_LIMIT_BYTES = 56 * 1024 * 1024
STAGE_TM = 512
FFN1_HOST_TM = 256

CHUNK = 16
SCAN_BLOCK = 8
N_PAIRS = CHUNK // 2
PAIR_W = 2 * LANE
SSM_TILE_GROUPS = LANE // SSM_GROUP
N_SSM_TILES = N_SSM_GROUPS // SSM_TILE_GROUPS
TILE_STATE_W = SSM_TILE_GROUPS * 2 * SSM_STATE
Q_BLOCK = 128
ATTN_TB = 2048
ATTN_PAR = 4

D_QKV = 3 * N_GROUPS * D_HEADS


def _qkv_col(which, grp):
    return which * N_GROUPS + grp


BF16 = jnp.bfloat16
F32 = jnp.float32


def _dot(a, b):
    return jnp.dot(a, b, preferred_element_type=F32)


def _dot_nt(a, b):
    return lax.dot_general(a, b, (((1,), (1,)), ((), ())), preferred_element_type=F32)


def _rms(x, g):
    return x * lax.rsqrt(jnp.mean(x * x, axis=-1, keepdims=True) + EPS) * g


def _swiglu(xn, wg_ref, wu_ref, wd_ref):
    gate = _dot(xn, wg_ref[...])
    up = _dot(xn, wu_ref[...])
    act = (gate * jax.nn.sigmoid(gate) * up).astype(BF16)
    return _dot(act, wd_ref[...])


def _const_spec(shape):
    nd = len(shape)
    return pl.BlockSpec(shape, lambda *_: (0,) * nd, pipeline_mode=pl.Buffered(1))


def _params(n_grid_dims):
    return pltpu.CompilerParams(dimension_semantics=("arbitrary",) * n_grid_dims,
                                vmem_limit_bytes=V7X_VMEM_LIMIT_BYTES)


class _SideJob(NamedTuple):
    arrays: tuple
    in_specs: tuple
    out_shapes: tuple
    out_specs: tuple
    body: Callable


def _cast_job(w, n_steps):
    rows, cols = w.shape
    steps = max(s for s in range(1, n_steps + 1) if rows % s == 0 and (rows // s) % 16 == 0)
    spec = pl.BlockSpec((rows // steps, cols), lambda i: (jnp.minimum(i, steps - 1), 0))

    def body(src, dst):
        dst[...] = src[...].astype(BF16)

    return _SideJob((w,), (spec,), (jax.ShapeDtypeStruct(w.shape, BF16),), (spec,), body)


def _hosted_call(main_body, n_steps, arrays, in_specs, out_shapes, out_specs, jobs, name):
    n_in, n_out = len(arrays), len(out_shapes)

    def body(*refs):
        pos, job_in, job_out = n_in, [], []
        for job in jobs:
            job_in.append(refs[pos:pos + len(job.arrays)])
            pos += len(job.arrays)
        main_out = refs[pos:pos + n_out]
        pos += n_out
        for job in jobs:
            job_out.append(refs[pos:pos + len(job.out_shapes)])
            pos += len(job.out_shapes)
        for job, ins, outs in zip(jobs, job_in, job_out):
            job.body(*ins, *outs)
        main_body(*refs[:n_in], *main_out)

    outs = pl.pallas_call(
        body,
        grid=(n_steps,),
        in_specs=list(in_specs) + [sp for job in jobs for sp in job.in_specs],
        out_specs=list(out_specs) + [sp for job in jobs for sp in job.out_specs],
        out_shape=list(out_shapes) + [sh for job in jobs for sh in job.out_shapes],
        compiler_params=_params(1),
        name=name,
    )(*arrays, *[a for job in jobs for a in job.arrays])
    main, pos, per_job = outs[:n_out], n_out, []
    for job in jobs:
        per_job.append(outs[pos:pos + len(job.out_shapes)])
        pos += len(job.out_shapes)
    return main, per_job


def _ffn1_kernel(x_ref, g1_ref, wg_ref, wu_ref, wd_ref, x1_ref):
    x = x_ref[...]
    x1_ref[...] = x + 0.5 * _swiglu(_rms(x, g1_ref[...]).astype(BF16), wg_ref, wu_ref, wd_ref)


def _ffn1(x, g1, wg, wu, wd, tm, jobs=()):
    t = x.shape[0]
    tok = pl.BlockSpec((tm, D_MODEL), lambda i: (i, 0))
    (x1,), per_job = _hosted_call(
        _ffn1_kernel, t // tm, (x, g1, wg, wu, wd),
        [tok] + [_const_spec(a.shape) for a in (g1, wg, wu, wd)],
        [jax.ShapeDtypeStruct((t, D_MODEL), F32)], [tok], jobs, "ffn1")
    return x1, per_job


def _proj_kernel(x1_ref, gm_ref, win_ref, u_ref, qkv_ref):
    z = _dot(_rms(x1_ref[...], gm_ref[...]).astype(BF16), win_ref[...])
    u_ref[...] = z[:, :D_SSM]
    qkv_ref[...] = z[:, D_SSM:]


def _proj(x1, gm, win, tm, jobs=()):
    t = x1.shape[0]
    tok = lambda w: pl.BlockSpec((tm, w), lambda i: (i, 0))
    (u, qkv), per_job = _hosted_call(
        _proj_kernel, t // tm, (x1, gm, win),
        [tok(D_MODEL), _const_spec(gm.shape), _const_spec(win.shape)],
        [jax.ShapeDtypeStruct((t, D_SSM), F32), jax.ShapeDtypeStruct((t, D_QKV), F32)], [tok(D_SSM), tok(D_QKV)],
        jobs, "proj")
    return u, qkv, per_job


def _swap_halves(x):
    half = x.shape[1] // 2
    return jnp.concatenate([x[:, half:], x[:, :half]], axis=1)


def _cmul_split(x, a1, a2):
    return x * a1 + _swap_halves(x) * a2


def _ssm_kernel(*refs, n_t, n_rows, has_h0):
    if has_h0:
        u_ref, h0_ref, kp_ref, wp_ref, vpt_ref, a1_ref, a2_ref, y_ref, hl_ref = refs
    else:
        u_ref, kp_ref, wp_ref, vpt_ref, a1_ref, a2_ref, y_ref, hl_ref = refs
    n_pairs = n_t // 2
    tok = lambda t: pl.ds(t, n_rows, stride=n_t)
    up = [jnp.concatenate([u_ref[tok(2 * a), :], u_ref[tok(2 * a + 1), :]], axis=1).astype(BF16)
          for a in range(n_pairs)]
    s = _dot(jnp.concatenate(up, axis=1), wp_ref[0])
    in_chunk = [None] * n_pairs
    todo = [(b, a) for b in range(n_pairs) for a in range(b + 1)]

    def in_chunk_products(count):
        for b, a in [todo.pop(0) for _ in range(min(count, len(todo)))]:
            prod = _dot(up[a], kp_ref[0, b - a])
            in_chunk[b] = prod if in_chunk[b] is None else in_chunk[b] + prod

    if has_h0:
        hprev = h0_ref[...]
        h_last = _cmul_split(hprev, a1_ref[0], a2_ref[0]) + s
        hl_ref[...] = h_last
    else:
        blk = SCAN_BLOCK
        assert n_rows % blk == 0 and a1_ref.shape[1] == blk
        row = lax.broadcasted_iota(jnp.int32, (blk, s.shape[1]), 0)
        carry = jnp.zeros((1, s.shape[1]), F32)
        entering = []
        for i in range(n_rows // blk):
            h = s[i * blk:(i + 1) * blk]
            for lvl in range(int(math.log2(blk))):
                sh = 1 << lvl
                shifted = jnp.where(row >= sh, pltpu.roll(h, sh, axis=0), 0.0)
                h = h + _cmul_split(shifted, a1_ref[0, sh - 1:sh, :], a2_ref[0, sh - 1:sh, :])
            carry_b = jnp.broadcast_to(carry, h.shape)
            h = h + _cmul_split(carry_b, a1_ref[0], a2_ref[0])
            entering.append(jnp.where(row == 0, carry_b, pltpu.roll(h, 1, axis=0)))
            carry = h[blk - 1:blk]
            in_chunk_products(-(-len(todo) // (n_rows // blk - i)))
        hprev = jnp.concatenate(entering, axis=0)
        hl_ref[0, 0] = carry
    in_chunk_products(len(todo))
    hpb = hprev.astype(BF16)
    for b in range(n_pairs):
        acc = _dot_nt(hpb, vpt_ref[0, b]) + in_chunk[b]
        y_ref[tok(2 * b), :] = acc[:, :LANE]
        y_ref[tok(2 * b + 1), :] = acc[:, LANE:]


def _ssm_table_specs(tabs, idx, n_t):
    kp, wp, vp, a1, a2 = tabs
    n_pairs, w_rows = n_t // 2, n_t * LANE
    return [pl.BlockSpec((1, n_pairs) + kp.shape[2:], lambda *g: (idx(*g), 0, 0, 0)),
            pl.BlockSpec((1, w_rows, wp.shape[2]), lambda *g: (idx(*g), wp.shape[1] // w_rows - 1, 0)),
            pl.BlockSpec((1, n_pairs) + vp.shape[2:], lambda *g: (idx(*g), 0, 0, 0)),
            pl.BlockSpec((1,) + a1.shape[1:], lambda *g: (idx(*g), 0, 0)),
            pl.BlockSpec((1,) + a2.shape[1:], lambda *g: (idx(*g), 0, 0))]


def _ssm_prompt(u, tabs, n, seq):
    n_rows = seq // CHUNK
    tile = pl.BlockSpec((seq, LANE), lambda j, i: (i, j))
    return pl.pallas_call(
        functools.partial(_ssm_kernel, n_t=CHUNK, n_rows=n_rows, has_h0=False),
        grid=(N_SSM_TILES, n),
        in_specs=[tile] + _ssm_table_specs(tabs, lambda j, i: j, CHUNK),
        out_specs=[tile, pl.BlockSpec((1, 1, 1, TILE_STATE_W), lambda j, i: (i, j, 0, 0))],
        out_shape=[jax.ShapeDtypeStruct((n * seq, D_SSM), F32),
                   jax.ShapeDtypeStruct((n, N_SSM_TILES, 1, TILE_STATE_W), F32)],
        compiler_params=_params(2),
        name="ssm_prompt",
    )(u, *tabs)


def _ssm_sample(u, h0, tabs, n_seq, n_new):
    tile = pl.BlockSpec((n_seq * n_new, LANE), lambda j: (0, j))
    state = pl.BlockSpec((n_seq, TILE_STATE_W), lambda j: (0, j))
    return pl.pallas_call(
        functools.partial(_ssm_kernel, n_t=n_new, n_rows=n_seq, has_h0=True),
        grid=(N_SSM_TILES,),
        in_specs=[tile, state] + _ssm_table_specs(tabs, lambda j: j, n_new),
        out_specs=[tile, state],
        out_shape=[jax.ShapeDtypeStruct((n_seq * n_new, D_SSM), F32),
                   jax.ShapeDtypeStruct((n_seq, N_SSM_TILES * TILE_STATE_W), F32)],
        compiler_params=_params(1),
        name="ssm_sample",
    )(u, h0, *tabs)


def _ssm_prep_kernel(are_ref, aim_ref, ldt_ref, bre_ref, bim_ref, cre_ref, cim_ref, d_ref,
                     kp_ref, wp_ref, vpt_ref, a1_ref, a2_ref, a1s_ref, a2s_ref, *, n_levels, n_new):
    a_re, a_im = are_ref[0], aim_ref[0]
    dt = jnp.exp(ldt_ref[0])
    mag = jnp.exp(a_re * dt)
    ab_re, ab_im = mag * jnp.cos(a_im * dt), mag * jnp.sin(a_im * dt)
    den = a_re * a_re + a_im * a_im
    nr, ni = ab_re - 1.0, ab_im
    f_re = (nr * a_re + ni * a_im) / den
    f_im = (ni * a_re - nr * a_im) / den
    shape = bre_ref.shape[1:]
    same_group = (lax.broadcasted_iota(jnp.int32, shape, 0) // SSM_GROUP
                  == lax.broadcasted_iota(jnp.int32, shape, 1) // SSM_STATE)
    b_re, b_im = bre_ref[0], bim_ref[0]
    x_re = jnp.where(same_group, f_re * b_re - f_im * b_im, 0.0)
    x_im = jnp.where(same_group, f_re * b_im + f_im * b_re, 0.0)
    c_re = jnp.where(same_group, cre_ref[0], 0.0)
    c_im = jnp.where(same_group, cim_ref[0], 0.0)
    pw = [(jnp.ones_like(ab_re), jnp.zeros_like(ab_im))]
    for _ in range(CHUNK):
        r, i = pw[-1]
        pw.append((r * ab_re - i * ab_im, r * ab_im + i * ab_re))
    cat = lambda r, i: jnp.concatenate([r, i], axis=1)
    e_pack = [cat(c_re * r - c_im * i, -(c_re * i + c_im * r)) for r, i in pw]
    split = lambda a: (a.astype(BF16), (a - a.astype(BF16).astype(F32)).astype(BF16))
    x_hi, x_lo = split(cat(x_re, x_im))
    x_3 = jnp.concatenate([x_hi, x_hi, x_lo], axis=1)
    k_lag = []
    for k in range(CHUNK):
        e_hi, e_lo = split(e_pack[k])
        k_lag.append(_dot_nt(x_3, jnp.concatenate([e_hi, e_lo, e_hi], axis=1)))
    diag = (lax.broadcasted_iota(jnp.int32, (LANE, LANE), 0) == lax.broadcasted_iota(jnp.int32, (LANE, LANE), 1))
    k_lag[0] = k_lag[0] + jnp.where(diag, d_ref[0], 0.0)
    zero = jnp.zeros((LANE, LANE), F32)
    for dl in range(N_PAIRS):
        top = cat(k_lag[2 * dl], k_lag[2 * dl + 1])
        bot = cat(k_lag[2 * dl - 1] if dl > 0 else zero, k_lag[2 * dl])
        kp_ref[0, dl] = jnp.concatenate([top, bot], axis=0).astype(BF16)
    for t in range(CHUNK):
        r, i = pw[CHUNK - 1 - t]
        wp_ref[0, t * LANE:(t + 1) * LANE, :] = cat(x_re * r - x_im * i, x_re * i + x_im * r).astype(BF16)
    for t in range(CHUNK):
        vpt_ref[0, t // 2, (t % 2) * LANE:(t % 2 + 1) * LANE, :] = e_pack[t + 1].astype(BF16)
    r, i = step_r, step_i = pw[CHUNK]
    for s in range(n_levels):
        a1_ref[0, s:s + 1, :], a2_ref[0, s:s + 1, :] = cat(r, r), cat(-i, i)
        r, i = r * step_r - i * step_i, r * step_i + i * step_r
    r, i = pw[n_new]
    a1s_ref[0], a2s_ref[0] = cat(r, r), cat(-i, i)


def _ssm_tables(a_re, a_im, log_dt, b_re, b_im, c_re, c_im, d_skip, n_levels, n_new):
    nj, tg = N_SSM_TILES, SSM_TILE_GROUPS
    half = tg * SSM_STATE
    rowv = lambda v: v.reshape(nj, 1, half)
    tiled = lambda m: jnp.tile(m.reshape(nj, LANE, SSM_STATE), (1, 1, tg))
    ins = (rowv(a_re), rowv(a_im), rowv(jnp.repeat(log_dt, SSM_STATE)),
           tiled(b_re.transpose(0, 2, 1)), tiled(b_im.transpose(0, 2, 1)), tiled(c_re), tiled(c_im),
           d_skip.reshape(nj, 1, LANE))
    shapes = ((nj, N_PAIRS, PAIR_W, PAIR_W), (nj, CHUNK * LANE, TILE_STATE_W), (nj, N_PAIRS, PAIR_W, TILE_STATE_W),
              (nj, n_levels, TILE_STATE_W), (nj, n_levels, TILE_STATE_W), (nj, 1, TILE_STATE_W), (nj, 1, TILE_STATE_W))
    dtypes = (BF16, BF16, BF16, F32, F32, F32, F32)
    per_tile = lambda s: pl.BlockSpec((1,) + tuple(s[1:]), lambda j: (j,) + (0,) * (len(s) - 1))
    kp, wp, vpt, a1, a2, a1s, a2s = pl.pallas_call(
        functools.partial(_ssm_prep_kernel, n_levels=n_levels, n_new=n_new),
        grid=(nj,),
        in_specs=[per_tile(x.shape) for x in ins],
        out_specs=[per_tile(s) for s in shapes],
        out_shape=[jax.ShapeDtypeStruct(s, dt) for s, dt in zip(shapes, dtypes)],
        compiler_params=_params(1),
        name="ssm_prep",
    )(*ins)
    return (kp, wp, vpt, a1, a2), (kp, wp, vpt, a1s, a2s)


def _head_of_lane(width):
    return lax.broadcasted_iota(jnp.int32, (1, width), 1) // HEAD_DIM


def _stack_heads(q):
    head = _head_of_lane(q.shape[1])
    q = q.astype(F32)
    return jnp.concatenate([jnp.where(head == h, q, 0.0) for h in range(HEADS)], axis=0).astype(BF16)


def _unstack_heads(x, rows):
    head = _head_of_lane(x.shape[1])
    out = x[(HEADS - 1) * rows:]
    for h in range(HEADS - 2, -1, -1):
        out = jnp.where(head == h, x[h * rows:(h + 1) * rows], out)
    return out


def _attn_prompt_kernel(bias_ref, q0, q1, k0, k1, v0, v1, kt_ref, vt_ref, o0, o1, l0, l1, tail_ref, kprev, vprev, *, d, m):
    qb = Q_BLOCK
    b = pl.program_id(1)
    n_par = min(d, ATTN_PAR)
    rd, wr = b % 2, (b + 1) % 2
    head = _head_of_lane(D_HEADS)

    @pl.when(b == 0)
    def _():
        kprev[0] = jnp.zeros(kprev.shape[1:], kprev.dtype)
        vprev[0] = jnp.zeros(vprev.shape[1:], vprev.dtype)

    def block(r, j, k_prev, v_prev):
        start = r + j * qb * d
        rows = pl.ds(start, qb, stride=d) if d > 1 else pl.ds(pl.multiple_of(start, qb), qb)
        both = lambda lo, hi: jnp.concatenate([lo[rows, :], hi[rows, :]], axis=1)
        qs = _stack_heads(both(q0, q1) * ATTN_SCALE)
        k_own, v_own = both(k0, k1).astype(BF16), both(v0, v1).astype(BF16)
        k = jnp.concatenate([k_prev, k_own], axis=0)
        v = jnp.concatenate([v_prev, v_own], axis=0)
        s = _dot_nt(qs, k) + bias_ref[jnp.minimum(b * m + j, 1)]
        mx = jnp.max(s, axis=-1, keepdims=True)
        pe = jnp.exp(s - mx)
        den = jnp.sum(pe, axis=-1, keepdims=True)
        pn = (pe * (1.0 / den)).astype(BF16)
        o = _dot(pn[:qb], jnp.where(head == 0, v, jnp.zeros_like(v)))
        for h in range(1, HEADS):
            o = o + _dot(pn[h * qb:(h + 1) * qb], jnp.where(head == h, v, jnp.zeros_like(v)))
        lse = _unstack_heads(jnp.broadcast_to(mx + jnp.log(den), (HEADS * qb, D_HEADS)), qb)
        o0[rows, :], o1[rows, :] = o[:, :LANE], o[:, LANE:]
        l0[rows, :], l1[rows, :] = lse[:, :LANE], lse[:, LANE:]
        return k_own, v_own

    def class_group(g, carry):
        classes = [g * n_par + t for t in range(n_par)]

        def step(j, prev):
            return tuple(x for t, r in enumerate(classes) for x in block(r, j, prev[2 * t], prev[2 * t + 1]))

        init = tuple(x for r in classes for x in (kprev[rd, r], vprev[rd, r]))
        last = lax.fori_loop(0, m, step, init, unroll=max(1, min(m, ATTN_PAR // n_par)))
        for t, r in enumerate(classes):
            kprev[wr, r], vprev[wr, r] = last[2 * t], last[2 * t + 1]
        return carry

    lax.fori_loop(0, d // n_par, class_group, 0)
    tail_ref[0, 0] = kt_ref[...].T
    tail_ref[0, 1] = vt_ref[...].T


def _attn_bias():
    qi = np.arange(HEADS * Q_BLOCK)[:, None] % Q_BLOCK
    ki = np.arange(2 * Q_BLOCK)[None, :]
    diff = Q_BLOCK + qi - ki
    valid = (diff >= 0) & (diff <= N_STEPS)
    return jnp.asarray(np.where(np.stack([valid & (ki >= Q_BLOCK), valid]), 0.0, -np.inf), F32)


def _attn_prompt(qkv, n, seq, grp):
    d = DILATIONS[grp]
    tb = min(ATTN_TB, seq)
    assert tb % (Q_BLOCK * d) == 0 and seq % tb == 0
    m, nb = tb // (Q_BLOCK * d), seq // tb
    cur = lambda c: pl.BlockSpec((tb, LANE), lambda i, b: (i * nb + b, c))
    qc, kc, vc = (2 * _qkv_col(which, grp) for which in range(3))
    bias = _attn_bias()
    keep = min(WINDOWS[grp], seq)
    tt = min(keep, 512)
    n_tiles, first = keep // tt, (seq - keep) // tt
    assert keep % tt == 0 and (seq - keep) % tt == 0 and nb >= n_tiles
    tile = lambda b: jnp.minimum(b, n_tiles - 1)
    tail_rows = lambda which: pl.BlockSpec(
        (tt, D_HEADS), lambda i, b: (i * (seq // tt) + first + tile(b), _qkv_col(which, grp)))
    *attn, tail = pl.pallas_call(
        functools.partial(_attn_prompt_kernel, d=d, m=m),
        grid=(n, nb),
        in_specs=[_const_spec(bias.shape), cur(qc), cur(qc + 1), cur(kc), cur(kc + 1), cur(vc), cur(vc + 1),
                  tail_rows(1), tail_rows(2)],
        out_specs=[pl.BlockSpec((tb, LANE), lambda i, b: (i * nb + b, 0))] * 4
        + [pl.BlockSpec((1, 2, D_HEADS, tt), lambda i, b: (i, 0, 0, tile(b)))],
        out_shape=[jax.ShapeDtypeStruct((n * seq, LANE), F32)] * 4
        + [jax.ShapeDtypeStruct((n, 2, D_HEADS, keep), F32)],
        scratch_shapes=[pltpu.VMEM((2, d, Q_BLOCK, D_HEADS), BF16)] * 2,
        compiler_params=_params(2),
        name=f"attn_prompt_d{d}",
    )(bias, *([qkv] * 8))
    return attn, tail.reshape(1, n, 2, HEADS, HEAD_DIM, keep).transpose(0, 1, 5, 2, 3, 4)


def _attn_sample_kernel(q_ref, kn_ref, vn_ref, c_ref, o_ref, lse_ref, *, d, n_new, seq_blk):
    w = c_ref.shape[-1]
    n_rows = HEADS * n_new
    l_b = lax.broadcasted_iota(jnp.int32, (n_rows, w), 0) % n_new
    pos = lax.broadcasted_iota(jnp.int32, (n_rows, w), 1)
    valid_buf = (((w + l_b - pos) & (d - 1)) == 0) & (pos >= l_b)
    dn = (lax.broadcasted_iota(jnp.int32, (n_rows, LANE), 0) % n_new) - lax.broadcasted_iota(jnp.int32, (n_rows, LANE), 1)
    valid_new = (dn >= 0) & ((dn & (d - 1)) == 0)
    pad = jnp.zeros((LANE - n_new, D_HEADS), F32)

    def one_seq(s, carry):
        qs = _stack_heads(q_ref[s])
        k_t, v_t = c_ref[s, 0].astype(BF16), c_ref[s, 1].astype(BF16)
        k_n = jnp.concatenate([kn_ref[s], pad], axis=0).astype(BF16)
        v_n = jnp.concatenate([vn_ref[s], pad], axis=0).astype(BF16)
        s_b = jnp.where(valid_buf, _dot(qs, k_t) * ATTN_SCALE, -jnp.inf)
        s_n = jnp.where(valid_new, _dot_nt(qs, k_n) * ATTN_SCALE, -jnp.inf)
        mx = jnp.maximum(jnp.max(s_b, axis=-1, keepdims=True), jnp.max(s_n, axis=-1, keepdims=True))
        p_b, p_n = jnp.exp(s_b - mx), jnp.exp(s_n - mx)
        den = jnp.sum(p_b, axis=-1, keepdims=True) + jnp.sum(p_n, axis=-1, keepdims=True)
        o = (_dot_nt(p_b.astype(BF16), v_t) + _dot(p_n.astype(BF16), v_n)) / den
        o_ref[s] = _unstack_heads(o, n_new)
        lse_ref[s] = _unstack_heads(jnp.broadcast_to(mx + jnp.log(den), o.shape), n_new)
        return carry

    lax.fori_loop(0, seq_blk, one_seq, 0, unroll=True)


def _attn_sample_job(qkv, cache, n_seq, n_new, grp, n_steps):
    d = DILATIONS[grp]
    w = cache.shape[1]
    assert w == N_STEPS * d and n_new <= LANE and n_seq % n_steps == 0
    seq_blk = n_seq // n_steps
    qkv3 = qkv.reshape(n_seq, n_new, D_QKV)
    c_t = cache.transpose(0, 2, 3, 4, 1).reshape(n_seq, 2, D_HEADS, w)
    blk = (seq_blk, n_new, D_HEADS)
    col = lambda which: pl.BlockSpec(blk, lambda i: (i, 0, _qkv_col(which, grp)))
    out_spec = pl.BlockSpec(blk, lambda i: (i, 0, 0))
    out_shape = jax.ShapeDtypeStruct((n_seq, n_new, D_HEADS), F32)
    return _SideJob((qkv3, qkv3, qkv3, c_t),
                    (col(0), col(1), col(2), pl.BlockSpec((seq_blk, 2, D_HEADS, w), lambda i: (i, 0, 0, 0))),
                    (out_shape, out_shape), (out_spec, out_spec),
                    functools.partial(_attn_sample_kernel, d=d, n_new=n_new, seq_blk=seq_blk))


def _halves(o, lse):
    o, lse = o.reshape(-1, D_HEADS), lse.reshape(-1, D_HEADS)
    return o[:, :LANE], o[:, LANE:], lse[:, :LANE], lse[:, LANE:]


def _kv_new_job(qkv, n_seq, n_new, n_steps):
    halves, l_parts = D_HEADS // LANE, 2
    n_units, l_blk = N_GROUPS * 2 * halves * l_parts, n_new // l_parts
    assert n_steps >= n_units and n_new % l_parts == 0
    unit = lambda i: jnp.minimum(i, n_units - 1)
    l_part = lambda i: unit(i) % l_parts
    half = lambda i: unit(i) // l_parts % halves
    kv = lambda i: unit(i) // (l_parts * halves) % 2
    grp = lambda i: unit(i) // (l_parts * halves * 2)
    in_spec = pl.BlockSpec((n_seq * n_new, LANE), lambda i: (0, halves * ((1 + kv(i)) * N_GROUPS + grp(i)) + half(i)))
    out_spec = pl.BlockSpec((1, l_blk, 1, LANE, n_seq), lambda i: (grp(i), l_part(i), kv(i), half(i), 0))
    out_shape = jax.ShapeDtypeStruct((N_GROUPS, n_new, 2, D_HEADS, n_seq), F32)

    def body(x_ref, o_ref):
        l0 = l_part(pl.program_id(0)) * l_blk
        for dl in range(l_blk):
            o_ref[0, dl, 0] = x_ref[pl.ds(l0 + dl, n_seq, stride=n_new), :].T

    return _SideJob((qkv,), (in_spec,), (out_shape,), (out_spec,), body)


def _kv_new_outputs(out, n_seq, n_new):
    out = out.reshape(N_GROUPS, 1, n_new, 2, HEADS, HEAD_DIM, n_seq).transpose(0, 1, 6, 2, 3, 4, 5)
    return [out[g] for g in range(N_GROUPS)]


def _gelu_tanh(x):
    return 0.5 * x * (1.0 + jnp.tanh(math.sqrt(2.0 / math.pi) * (x + 0.044715 * (x * x * x))))


def _stage3_kernel(*refs):
    x1_ref, yp_ref = refs[:2]
    attn = refs[2:2 + 4 * N_GROUPS]
    (p_ref, gm_ref, wgate_ref, wglu_ref, wbs_ref, wba_ref, wout_ref, g2_ref, wg_ref, wu_ref, wd_ref,
     gp_ref, wpg_ref, wpp_ref, gf_ref, y_ref) = refs[2 + 4 * N_GROUPS:]
    x1 = x1_ref[...]
    gates = jax.nn.sigmoid(_dot(_rms(x1, gm_ref[...]).astype(BF16), wgate_ref[...]))
    y = _gelu_tanh(yp_ref[...])
    glu = y * jax.nn.sigmoid(_dot(y.astype(BF16), wglu_ref[...]))
    full = lambda lo, hi: jnp.concatenate([lo[...], hi[...]], axis=1)
    outs = [full(attn[4 * g], attn[4 * g + 1]) for g in range(N_GROUPS)]
    lses = [full(attn[4 * g + 2], attn[4 * g + 3]) for g in range(N_GROUPS)]
    mx = jnp.maximum(jnp.maximum(lses[0], lses[1]), lses[2])
    es = [jnp.exp(l - mx) for l in lses]
    y_attn = (es[0] * outs[0] + es[1] * outs[1] + es[2] * outs[2]) / (es[0] + es[1] + es[2])
    merged = (gates[:, :D_MODEL] * _dot(glu.astype(BF16), wbs_ref[...])
              + gates[:, D_MODEL:] * _dot(y_attn.astype(BF16), wba_ref[...]))
    x = x1 + _dot(merged.astype(BF16), wout_ref[...])
    x = x + 0.5 * _swiglu(_rms(x, g2_ref[...]).astype(BF16), wg_ref, wu_ref, wd_ref)
    gate = jax.nn.sigmoid(_dot(_rms(x, gp_ref[...]).astype(BF16), wpg_ref[...]))
    x = x + gate * _dot(p_ref[...].astype(BF16), wpp_ref[...])
    y_ref[...] = _rms(x, gf_ref[...])


def _stage3(tok_inputs, weights):
    t, tm = tok_inputs[0].shape[0], STAGE_TM
    tok = lambda a: pl.BlockSpec((tm, a.shape[1]), lambda i: (i, 0))
    return pl.pallas_call(
        _stage3_kernel,
        grid=(t // tm,),
        in_specs=[tok(a) for a in tok_inputs] + [_const_spec(w.shape) for w in weights],
        out_specs=pl.BlockSpec((tm, D_MODEL), lambda i: (i, 0)),
        out_shape=jax.ShapeDtypeStruct((t, D_MODEL), F32),
        compiler_params=_params(1),
        name="stage3",
    )(*tok_inputs, *weights)


def kernel(x_prompt, x_sample, p_prompt, p_sample, cache_kv_w128, cache_kv_w512, cache_kv_w2048, state_ssm,
           g_ffn1, ffn1_w_gate, ffn1_w_up, ffn1_w_down, g_mix, w_in, ssm_a_re, ssm_a_im, ssm_log_dt,
           ssm_b_re, ssm_b_im, ssm_c_re, ssm_c_im, ssm_d, ssm_w_glu, w_br_ssm, w_br_attn, w_out,
           g_ffn2, ffn2_w_gate, ffn2_w_up, ffn2_w_down, g_ple, w_ple_gate, w_ple_proj, g_final):
    assert x_prompt.shape[-1] == D_MODEL and g_ffn1.shape[0] == 1
    n_p, seq, _ = x_prompt.shape
    n_s, n_new, _ = x_sample.shape
    caches = (cache_kv_w128, cache_kv_w512, cache_kv_w2048)
    row = lambda g: g.reshape(1, -1)
    bf = lambda w: w[0].astype(BF16)

    wi = w_in[0]
    win = wi[:, :D_SSM + D_QKV].astype(BF16)
    w_gates = wi[:, D_SSM + D_QKV:].astype(BF16)
    ffn1_w = (row(g_ffn1[0]), bf(ffn1_w_gate), bf(ffn1_w_up), bf(ffn1_w_down))
    s3_f32 = [w[0] for w in (ssm_w_glu, w_br_ssm, w_br_attn, w_out, ffn2_w_gate, ffn2_w_up, ffn2_w_down,
                             w_ple_gate, w_ple_proj)]

    tabs_p, tabs_s = _ssm_tables(
        ssm_a_re[0], ssm_a_im[0], ssm_log_dt[0], ssm_b_re[0], ssm_b_im[0], ssm_c_re[0], ssm_c_im[0],
        ssm_d[0], SCAN_BLOCK, n_new)
    tile_state = (-1, N_SSM_TILES, 2, SSM_TILE_GROUPS, SSM_STATE)
    from_tiles = lambda h: h.reshape(tile_state).transpose(0, 1, 3, 4, 2).reshape(1, -1, N_SSM_GROUPS, SSM_STATE, 2)

    xs = x_sample.reshape(n_s * n_new, D_MODEL)
    x1_s, _ = _ffn1(xs, *ffn1_w, tm=min(STAGE_TM, xs.shape[0]))
    u_s, qkv_s, _ = _proj(x1_s, row(g_mix[0]), win, tm=min(STAGE_TM, xs.shape[0]))

    xp = x_prompt.reshape(n_p * seq, D_MODEL)
    attn_job = lambda grp, n_steps: _attn_sample_job(qkv_s, caches[grp][0], n_s, n_new, grp, n_steps)
    steps_a, steps_b = xp.shape[0] // FFN1_HOST_TM, xp.shape[0] // STAGE_TM
    x1, side = _ffn1(xp, *ffn1_w, tm=FFN1_HOST_TM,
                     jobs=[attn_job(2, steps_a), attn_job(0, steps_a)] + [_cast_job(w, steps_a) for w in s3_f32])
    attn_s = {2: side[0], 0: side[1]}
    wglu, wbs, wba, wout, wg2, wu2, wd2, wpg, wpp = (c[0] for c in side[2:])
    s3_w = (row(g_mix[0]), w_gates, wglu, wbs, wba, wout, row(g_ffn2[0]), wg2, wu2, wd2, row(g_ple[0]), wpg, wpp,
            row(g_final))
    u, qkv, side = _proj(x1, row(g_mix[0]), win, tm=STAGE_TM,
                         jobs=[attn_job(1, steps_b), _kv_new_job(qkv_s, n_s, n_new, steps_b)])
    attn_s[1] = side[0]
    kv_sample = _kv_new_outputs(side[1][0], n_s, n_new)
    y_pre, h_last = _ssm_prompt(u, tabs_p, n_p, seq)
    attn_p = [_attn_prompt(qkv, n_p, seq, grp) for grp in range(N_GROUPS)]
    attn, kv_prompt = [a for per_group, _ in attn_p for a in per_group], [tail for _, tail in attn_p]
    tok = (x1, y_pre, *attn, p_prompt[0].reshape(n_p * seq, D_PLE))
    y_prompt = _stage3(tok, s3_w).reshape(n_p, seq, D_MODEL)
    ssm_prompt = from_tiles(h_last)

    h0 = state_ssm[0].reshape(n_s, N_SSM_TILES, SSM_TILE_GROUPS, SSM_STATE, 2)
    h0 = h0.transpose(0, 1, 4, 2, 3).reshape(n_s, N_SSM_TILES * TILE_STATE_W)
    y_pre, h_new = _ssm_sample(u_s, h0, tabs_s, n_s, n_new)
    attn = [a for grp in range(N_GROUPS) for a in _halves(*attn_s[grp])]
    tok = (x1_s, y_pre, *attn, p_sample[0].reshape(n_s * n_new, D_PLE))
    y_sample = _stage3(tok, s3_w).reshape(n_s, n_new, D_MODEL)
    ssm_sample = from_tiles(h_new)

    return (y_prompt, y_sample, kv_prompt[0], kv_prompt[1], kv_prompt[2], ssm_prompt,
            kv_sample[0], kv_sample[1], kv_sample[2], ssm_sample)
```

```python
import functools
import math
from typing import Callable, NamedTuple

import jax
import jax.numpy as jnp
import numpy as np
from jax import lax
from jax.experimental import pallas as pl
from jax.experimental.pallas import tpu as pltpu

D_MODEL = 1024
D_PLE = 256
D_FF = 2816
SSM_GROUP = 16
SSM_STATE = 64
D_SSM = 512
N_SSM_GROUPS = D_SSM // SSM_GROUP
HEAD_DIM = 64
HEADS = 4
WINDOWS = (128, 512, 2048)
DILATIONS = (1, 4, 16)
N_GROUPS = 3
D_HEADS = HEADS * HEAD_DIM
N_STEPS = 128
ATTN_SCALE = HEAD_DIM ** -0.5
EPS = 1e-6

LANE = 128
V7X_VMEM_LIMIT_BYTES = 56 * 1024 * 1024
STAGE_TM = 512
FFN1_HOST_TM = 256

CHUNK = 16
SCAN_BLOCK = 8
N_PAIRS = CHUNK // 2
PAIR_W = 2 * LANE
SSM_TILE_GROUPS = LANE // SSM_GROUP
N_SSM_TILES = N_SSM_GROUPS // SSM_TILE_GROUPS
TILE_STATE_W = SSM_TILE_GROUPS * 2 * SSM_STATE
Q_BLOCK = 128
ATTN_TB = 2048
ATTN_PAR = 8

D_QKV = 3 * N_GROUPS * D_HEADS


def _qkv_col(which, grp):
    return which * N_GROUPS + grp


BF16 = jnp.bfloat16
F32 = jnp.float32


def _dot(a, b):
    return jnp.dot(a, b, preferred_element_type=F32)


def _dot_nt(a, b):
    return lax.dot_general(a, b, (((1,), (1,)), ((), ())), preferred_element_type=F32)


def _rms(x, g):
    return x * lax.rsqrt(jnp.mean(x * x, axis=-1, keepdims=True) + EPS) * g


def _swiglu(xn, wg_ref, wu_ref, wd_ref):
    gate = _dot(xn, wg_ref[...])
    up = _dot(xn, wu_ref[...])
    act = (gate * jax.nn.sigmoid(gate) * up).astype(BF16)
    return _dot(act, wd_ref[...])


def _const_spec(shape):
    nd = len(shape)
    return pl.BlockSpec(shape, lambda *_: (0,) * nd, pipeline_mode=pl.Buffered(1))


def _params(n_grid_dims):
    return pltpu.CompilerParams(dimension_semantics=("arbitrary",) * n_grid_dims,
                                vmem_limit_bytes=V7X_VMEM_LIMIT_BYTES)


class _SideJob(NamedTuple):
    arrays: tuple
    in_specs: tuple
    out_shapes: tuple
    out_specs: tuple
    body: Callable


def _cast_job(w, n_steps):
    rows, cols = w.shape
    steps = max(s for s in range(1, n_steps + 1) if rows % s == 0 and (rows // s) % 16 == 0)
    spec = pl.BlockSpec((rows // steps, cols), lambda i: (jnp.minimum(i, steps - 1), 0))

    def body(src, dst):
        dst[...] = src[...].astype(BF16)

    return _SideJob((w,), (spec,), (jax.ShapeDtypeStruct(w.shape, BF16),), (spec,), body)


def _hosted_call(main_body, n_steps, arrays, in_specs, out_shapes, out_specs, jobs, name):
    n_in, n_out = len(arrays), len(out_shapes)

    def body(*refs):
        pos, job_in, job_out = n_in, [], []
        for job in jobs:
            job_in.append(refs[pos:pos + len(job.arrays)])
            pos += len(job.arrays)
        main_out = refs[pos:pos + n_out]
        pos += n_out
        for job in jobs:
            job_out.append(refs[pos:pos + len(job.out_shapes)])
            pos += len(job.out_shapes)
        for job, ins, outs in zip(jobs, job_in, job_out):
            job.body(*ins, *outs)
        main_body(*refs[:n_in], *main_out)

    outs = pl.pallas_call(
        body,
        grid=(n_steps,),
        in_specs=list(in_specs) + [sp for job in jobs for sp in job.in_specs],
        out_specs=list(out_specs) + [sp for job in jobs for sp in job.out_specs],
        out_shape=list(out_shapes) + [sh for job in jobs for sh in job.out_shapes],
        compiler_params=_params(1),
        name=name,
    )(*arrays, *[a for job in jobs for a in job.arrays])
    main, pos, per_job = outs[:n_out], n_out, []
    for job in jobs:
        per_job.append(outs[pos:pos + len(job.out_shapes)])
        pos += len(job.out_shapes)
    return main, per_job


def _ffn1_kernel(x_ref, g1_ref, wg_ref, wu_ref, wd_ref, x1_ref):
    x = x_ref[...]
    x1_ref[...] = x + 0.5 * _swiglu(_rms(x, g1_ref[...]).astype(BF16), wg_ref, wu_ref, wd_ref)


def _ffn1(x, g1, wg, wu, wd, tm, jobs=()):
    t = x.shape[0]
    tok = pl.BlockSpec((tm, D_MODEL), lambda i: (i, 0))
    (x1,), per_job = _hosted_call(
        _ffn1_kernel, t // tm, (x, g1, wg, wu, wd),
        [tok] + [_const_spec(a.shape) for a in (g1, wg, wu, wd)],
        [jax.ShapeDtypeStruct((t, D_MODEL), F32)], [tok], jobs, "ffn1")
    return x1, per_job


def _proj_kernel(x1_ref, gm_ref, win_ref, u_ref, qkv_ref):
    z = _dot(_rms(x1_ref[...], gm_ref[...]).astype(BF16), win_ref[...])
    u_ref[...] = z[:, :D_SSM]
    qkv_ref[...] = z[:, D_SSM:]


def _proj(x1, gm, win, tm, jobs=()):
    t = x1.shape[0]
    tok = lambda w: pl.BlockSpec((tm, w), lambda i: (i, 0))
    (u, qkv), per_job = _hosted_call(
        _proj_kernel, t // tm, (x1, gm, win),
        [tok(D_MODEL), _const_spec(gm.shape), _const_spec(win.shape)],
        [jax.ShapeDtypeStruct((t, D_SSM), F32), jax.ShapeDtypeStruct((t, D_QKV), F32)], [tok(D_SSM), tok(D_QKV)],
        jobs, "proj")
    return u, qkv, per_job


def _swap_halves(x):
    half = x.shape[1] // 2
    return jnp.concatenate([x[:, half:], x[:, :half]], axis=1)


def _cmul_split(x, a1, a2):
    return x * a1 + _swap_halves(x) * a2


def _ssm_kernel(*refs, n_t, n_rows, has_h0):
    if has_h0:
        u_ref, h0_ref, kp_ref, wp_ref, vpt_ref, a1_ref, a2_ref, y_ref, hl_ref = refs
    else:
        u_ref, kp_ref, wp_ref, vpt_ref, a1_ref, a2_ref, y_ref, hl_ref = refs
    n_pairs = n_t // 2
    tok = lambda t: pl.ds(t, n_rows, stride=n_t)
    up = [jnp.concatenate([u_ref[tok(2 * a), :], u_ref[tok(2 * a + 1), :]], axis=1).astype(BF16)
          for a in range(n_pairs)]
    s = _dot(jnp.concatenate(up, axis=1), wp_ref[0])
    in_chunk = [None] * n_pairs
    todo = [(b, a) for b in range(n_pairs) for a in range(b + 1)]

    def in_chunk_products(count):
        for b, a in [todo.pop(0) for _ in range(min(count, len(todo)))]:
            prod = _dot(up[a], kp_ref[0, b - a])
            in_chunk[b] = prod if in_chunk[b] is None else in_chunk[b] + prod

    if has_h0:
        hprev = h0_ref[...]
        h_last = _cmul_split(hprev, a1_ref[0], a2_ref[0]) + s
        hl_ref[...] = h_last
    else:
        blk = SCAN_BLOCK
        assert n_rows % blk == 0 and a1_ref.shape[1] == blk
        row = lax.broadcasted_iota(jnp.int32, (blk, s.shape[1]), 0)
        carry = jnp.zeros((1, s.shape[1]), F32)
        entering = []
        for i in range(n_rows // blk):
            h = s[i * blk:(i + 1) * blk]
            for lvl in range(int(math.log2(blk))):
                sh = 1 << lvl
                shifted = jnp.where(row >= sh, pltpu.roll(h, sh, axis=0), 0.0)
                h = h + _cmul_split(shifted, a1_ref[0, sh - 1:sh, :], a2_ref[0, sh - 1:sh, :])
            carry_b = jnp.broadcast_to(carry, h.shape)
            h = h + _cmul_split(carry_b, a1_ref[0], a2_ref[0])
            entering.append(jnp.where(row == 0, carry_b, pltpu.roll(h, 1, axis=0)))
            carry = h[blk - 1:blk]
            in_chunk_products(-(-len(todo) // (n_rows // blk - i)))
        hprev = jnp.concatenate(entering, axis=0)
        hl_ref[0, 0] = carry
    in_chunk_products(len(todo))
    hpb = hprev.astype(BF16)
    for b in range(n_pairs):
        acc = _dot_nt(hpb, vpt_ref[0, b]) + in_chunk[b]
        y_ref[tok(2 * b), :] = acc[:, :LANE]
        y_ref[tok(2 * b + 1), :] = acc[:, LANE:]


def _ssm_table_specs(tabs, idx, n_t):
    kp, wp, vp, a1, a2 = tabs
    n_pairs, w_rows = n_t // 2, n_t * LANE
    return [pl.BlockSpec((1, n_pairs) + kp.shape[2:], lambda *g: (idx(*g), 0, 0, 0)),
            pl.BlockSpec((1, w_rows, wp.shape[2]), lambda *g: (idx(*g), wp.shape[1] // w_rows - 1, 0)),
            pl.BlockSpec((1, n_pairs) + vp.shape[2:], lambda *g: (idx(*g), 0, 0, 0)),
            pl.BlockSpec((1,) + a1.shape[1:], lambda *g: (idx(*g), 0, 0)),
            pl.BlockSpec((1,) + a2.shape[1:], lambda *g: (idx(*g), 0, 0))]


def _ssm_prompt(u, tabs, n, seq):
    n_rows = seq // CHUNK
    tile = pl.BlockSpec((seq, LANE), lambda j, i: (i, j))
    return pl.pallas_call(
        functools.partial(_ssm_kernel, n_t=CHUNK, n_rows=n_rows, has_h0=False),
        grid=(N_SSM_TILES, n),
        in_specs=[tile] + _ssm_table_specs(tabs, lambda j, i: j, CHUNK),
        out_specs=[tile, pl.BlockSpec((1, 1, 1, TILE_STATE_W), lambda j, i: (i, j, 0, 0))],
        out_shape=[jax.ShapeDtypeStruct((n * seq, D_SSM), F32),
                   jax.ShapeDtypeStruct((n, N_SSM_TILES, 1, TILE_STATE_W), F32)],
        compiler_params=_params(2),
        name="ssm_prompt",
    )(u, *tabs)


def _ssm_sample(u, h0, tabs, n_seq, n_new):
    tile = pl.BlockSpec((n_seq * n_new, LANE), lambda j: (0, j))
    state = pl.BlockSpec((n_seq, TILE_STATE_W), lambda j: (0, j))
    return pl.pallas_call(
        functools.partial(_ssm_kernel, n_t=n_new, n_rows=n_seq, has_h0=True),
        grid=(N_SSM_TILES,),
        in_specs=[tile, state] + _ssm_table_specs(tabs, lambda j: j, n_new),
        out_specs=[tile, state],
        out_shape=[jax.ShapeDtypeStruct((n_seq * n_new, D_SSM), F32),
                   jax.ShapeDtypeStruct((n_seq, N_SSM_TILES * TILE_STATE_W), F32)],
        compiler_params=_params(1),
        name="ssm_sample",
    )(u, h0, *tabs)


def _ssm_prep_kernel(are_ref, aim_ref, ldt_ref, bre_ref, bim_ref, cre_ref, cim_ref, d_ref,
                     kp_ref, wp_ref, vpt_ref, a1_ref, a2_ref, a1s_ref, a2s_ref, *, n_levels, n_new):
    a_re, a_im = are_ref[0], aim_ref[0]
    dt = jnp.exp(ldt_ref[0])
    mag = jnp.exp(a_re * dt)
    ab_re, ab_im = mag * jnp.cos(a_im * dt), mag * jnp.sin(a_im * dt)
    den = a_re * a_re + a_im * a_im
    nr, ni = ab_re - 1.0, ab_im
    f_re = (nr * a_re + ni * a_im) / den
    f_im = (ni * a_re - nr * a_im) / den
    shape = bre_ref.shape[1:]
    same_group = (lax.broadcasted_iota(jnp.int32, shape, 0) // SSM_GROUP
                  == lax.broadcasted_iota(jnp.int32, shape, 1) // SSM_STATE)
    b_re, b_im = bre_ref[0], bim_ref[0]
    x_re = jnp.where(same_group, f_re * b_re - f_im * b_im, 0.0)
    x_im = jnp.where(same_group, f_re * b_im + f_im * b_re, 0.0)
    c_re = jnp.where(same_group, cre_ref[0], 0.0)
    c_im = jnp.where(same_group, cim_ref[0], 0.0)
    pw = [(jnp.ones_like(ab_re), jnp.zeros_like(ab_im))]
    for _ in range(CHUNK):
        r, i = pw[-1]
        pw.append((r * ab_re - i * ab_im, r * ab_im + i * ab_re))
    cat = lambda r, i: jnp.concatenate([r, i], axis=1)
    e_pack = [cat(c_re * r - c_im * i, -(c_re * i + c_im * r)) for r, i in pw]
    split = lambda a: (a.astype(BF16), (a - a.astype(BF16).astype(F32)).astype(BF16))
    x_hi, x_lo = split(cat(x_re, x_im))
    x_3 = jnp.concatenate([x_hi, x_hi, x_lo], axis=1)
    k_lag = []
    for k in range(CHUNK):
        e_hi, e_lo = split(e_pack[k])
        k_lag.append(_dot_nt(x_3, jnp.concatenate([e_hi, e_lo, e_hi], axis=1)))
    diag = (lax.broadcasted_iota(jnp.int32, (LANE, LANE), 0) == lax.broadcasted_iota(jnp.int32, (LANE, LANE), 1))
    k_lag[0] = k_lag[0] + jnp.where(diag, d_ref[0], 0.0)
    zero = jnp.zeros((LANE, LANE), F32)
    for dl in range(N_PAIRS):
        top = cat(k_lag[2 * dl], k_lag[2 * dl + 1])
        bot = cat(k_lag[2 * dl - 1] if dl > 0 else zero, k_lag[2 * dl])
        kp_ref[0, dl] = jnp.concatenate([top, bot], axis=0).astype(BF16)
    for t in range(CHUNK):
        r, i = pw[CHUNK - 1 - t]
        wp_ref[0, t * LANE:(t + 1) * LANE, :] = cat(x_re * r - x_im * i, x_re * i + x_im * r).astype(BF16)
    for t in range(CHUNK):
        vpt_ref[0, t // 2, (t % 2) * LANE:(t % 2 + 1) * LANE, :] = e_pack[t + 1].astype(BF16)
    r, i = step_r, step_i = pw[CHUNK]
    for s in range(n_levels):
        a1_ref[0, s:s + 1, :], a2_ref[0, s:s + 1, :] = cat(r, r), cat(-i, i)
        r, i = r * step_r - i * step_i, r * step_i + i * step_r
    r, i = pw[n_new]
    a1s_ref[0], a2s_ref[0] = cat(r, r), cat(-i, i)


def _ssm_tables(a_re, a_im, log_dt, b_re, b_im, c_re, c_im, d_skip, n_levels, n_new):
    nj, tg = N_SSM_TILES, SSM_TILE_GROUPS
    half = tg * SSM_STATE
    rowv = lambda v: v.reshape(nj, 1, half)
    tiled = lambda m: jnp.tile(m.reshape(nj, LANE, SSM_STATE), (1, 1, tg))
    ins = (rowv(a_re), rowv(a_im), rowv(jnp.repeat(log_dt, SSM_STATE)),
           tiled(b_re.transpose(0, 2, 1)), tiled(b_im.transpose(0, 2, 1)), tiled(c_re), tiled(c_im),
           d_skip.reshape(nj, 1, LANE))
    shapes = ((nj, N_PAIRS, PAIR_W, PAIR_W), (nj, CHUNK * LANE, TILE_STATE_W), (nj, N_PAIRS, PAIR_W, TILE_STATE_W),
              (nj, n_levels, TILE_STATE_W), (nj, n_levels, TILE_STATE_W), (nj, 1, TILE_STATE_W), (nj, 1, TILE_STATE_W))
    dtypes = (BF16, BF16, BF16, F32, F32, F32, F32)
    per_tile = lambda s: pl.BlockSpec((1,) + tuple(s[1:]), lambda j: (j,) + (0,) * (len(s) - 1))
    kp, wp, vpt, a1, a2, a1s, a2s = pl.pallas_call(
        functools.partial(_ssm_prep_kernel, n_levels=n_levels, n_new=n_new),
        grid=(nj,),
        in_specs=[per_tile(x.shape) for x in ins],
        out_specs=[per_tile(s) for s in shapes],
        out_shape=[jax.ShapeDtypeStruct(s, dt) for s, dt in zip(shapes, dtypes)],
        compiler_params=_params(1),
        name="ssm_prep",
    )(*ins)
    return (kp, wp, vpt, a1, a2), (kp, wp, vpt, a1s, a2s)


def _head_of_lane(width):
    return lax.broadcasted_iota(jnp.int32, (1, width), 1) // HEAD_DIM


def _stack_heads(q):
    head = _head_of_lane(q.shape[1])
    q = q.astype(F32)
    return jnp.concatenate([jnp.where(head == h, q, 0.0) for h in range(HEADS)], axis=0).astype(BF16)


def _unstack_heads(x, rows):
    head = _head_of_lane(x.shape[1])
    out = x[(HEADS - 1) * rows:]
    for h in range(HEADS - 2, -1, -1):
        out = jnp.where(head == h, x[h * rows:(h + 1) * rows], out)
    return out


def _attn_prompt_kernel(bias_ref, q0, q1, k0, k1, v0, v1, kt_ref, vt_ref, o0, o1, l0, l1, tail_ref, kprev, vprev, *, d, m):
    qb = Q_BLOCK
    b = pl.program_id(1)
    n_par = min(d, ATTN_PAR)
    rd, wr = b % 2, (b + 1) % 2
    head = _head_of_lane(D_HEADS)

    @pl.when(b == 0)
    def _():
        kprev[0] = jnp.zeros(kprev.shape[1:], kprev.dtype)
        vprev[0] = jnp.zeros(vprev.shape[1:], vprev.dtype)

    def block(r, j, k_prev, v_prev):
        start = r + j * qb * d
        rows = pl.ds(start, qb, stride=d) if d > 1 else pl.ds(pl.multiple_of(start, qb), qb)
        both = lambda lo, hi: jnp.concatenate([lo[rows, :], hi[rows, :]], axis=1)
        qs = _stack_heads(both(q0, q1) * ATTN_SCALE)
        k_own, v_own = both(k0, k1).astype(BF16), both(v0, v1).astype(BF16)
        k = jnp.concatenate([k_prev, k_own], axis=0)
        v = jnp.concatenate([v_prev, v_own], axis=0)
        s = _dot_nt(qs, k) + bias_ref[jnp.minimum(b * m + j, 1)]
        mx = jnp.max(s, axis=-1, keepdims=True)
        pe = jnp.exp(s - mx)
        den = jnp.sum(pe, axis=-1, keepdims=True)
        pn = (pe * (1.0 / den)).astype(BF16)
        o = _dot(pn[:qb], jnp.where(head == 0, v, jnp.zeros_like(v)))
        for h in range(1, HEADS):
            o = o + _dot(pn[h * qb:(h + 1) * qb], jnp.where(head == h, v, jnp.zeros_like(v)))
        lse = _unstack_heads(jnp.broadcast_to(mx + jnp.log(den), (HEADS * qb, D_HEADS)), qb)
        o0[rows, :], o1[rows, :] = o[:, :LANE], o[:, LANE:]
        l0[rows, :], l1[rows, :] = lse[:, :LANE], lse[:, LANE:]
        return k_own, v_own

    def class_group(g, carry):
        classes = [g * n_par + t for t in range(n_par)]

        def step(j, prev):
            return tuple(x for t, r in enumerate(classes) for x in block(r, j, prev[2 * t], prev[2 * t + 1]))

        init = tuple(x for r in classes for x in (kprev[rd, r], vprev[rd, r]))
        last = lax.fori_loop(0, m, step, init, unroll=max(1, min(m, ATTN_PAR // n_par)))
        for t, r in enumerate(classes):
            kprev[wr, r], vprev[wr, r] = last[2 * t], last[2 * t + 1]
        return carry

    lax.fori_loop(0, d // n_par, class_group, 0)
    tail_ref[0, 0] = kt_ref[...].T
    tail_ref[0, 1] = vt_ref[...].T


def _attn_bias():
    qi = np.arange(HEADS * Q_BLOCK)[:, None] % Q_BLOCK
    ki = np.arange(2 * Q_BLOCK)[None, :]
    diff = Q_BLOCK + qi - ki
    valid = (diff >= 0) & (diff <= N_STEPS)
    return jnp.asarray(np.where(np.stack([valid & (ki >= Q_BLOCK), valid]), 0.0, -np.inf), F32)


def _attn_prompt(qkv, n, seq, grp):
    d = DILATIONS[grp]
    tb = min(ATTN_TB, seq)
    assert tb % (Q_BLOCK * d) == 0 and seq % tb == 0
    m, nb = tb // (Q_BLOCK * d), seq // tb
    cur = lambda c: pl.BlockSpec((tb, LANE), lambda i, b: (i * nb + b, c))
    qc, kc, vc = (2 * _qkv_col(which, grp) for which in range(3))
    bias = _attn_bias()
    keep = min(WINDOWS[grp], seq)
    tt = min(keep, 512)
    n_tiles, first = keep // tt, (seq - keep) // tt
    assert keep % tt == 0 and (seq - keep) % tt == 0 and nb >= n_tiles
    tile = lambda b: jnp.minimum(b, n_tiles - 1)
    tail_rows = lambda which: pl.BlockSpec(
        (tt, D_HEADS), lambda i, b: (i * (seq // tt) + first + tile(b), _qkv_col(which, grp)))
    *attn, tail = pl.pallas_call(
        functools.partial(_attn_prompt_kernel, d=d, m=m),
        grid=(n, nb),
        in_specs=[_const_spec(bias.shape), cur(qc), cur(qc + 1), cur(kc), cur(kc + 1), cur(vc), cur(vc + 1),
                  tail_rows(1), tail_rows(2)],
        out_specs=[pl.BlockSpec((tb, LANE), lambda i, b: (i * nb + b, 0))] * 4
        + [pl.BlockSpec((1, 2, D_HEADS, tt), lambda i, b: (i, 0, 0, tile(b)))],
        out_shape=[jax.ShapeDtypeStruct((n * seq, LANE), F32)] * 4
        + [jax.ShapeDtypeStruct((n, 2, D_HEADS, keep), F32)],
        scratch_shapes=[pltpu.VMEM((2, d, Q_BLOCK, D_HEADS), BF16)] * 2,
        compiler_params=_params(2),
        name=f"attn_prompt_d{d}",
    )(bias, *([qkv] * 8))
    return attn, tail.reshape(1, n, 2, HEADS, HEAD_DIM, keep).transpose(0, 1, 5, 2, 3, 4)


def _attn_sample_kernel(q_ref, kn_ref, vn_ref, c_ref, o_ref, lse_ref, *, d, n_new, seq_blk):
    w = c_ref.shape[-1]
    n_rows = HEADS * n_new
    l_b = lax.broadcasted_iota(jnp.int32, (n_rows, w), 0) % n_new
    pos = lax.broadcasted_iota(jnp.int32, (n_rows, w), 1)
    valid_buf = (((w + l_b - pos) & (d - 1)) == 0) & (pos >= l_b)
    dn = (lax.broadcasted_iota(jnp.int32, (n_rows, LANE), 0) % n_new) - lax.broadcasted_iota(jnp.int32, (n_rows, LANE), 1)
    valid_new = (dn >= 0) & ((dn & (d - 1)) == 0)
    pad = jnp.zeros((LANE - n_new, D_HEADS), F32)

    def one_seq(s, carry):
        qs = _stack_heads(q_ref[s])
        k_t, v_t = c_ref[s, 0].astype(BF16), c_ref[s, 1].astype(BF16)
        k_n = jnp.concatenate([kn_ref[s], pad], axis=0).astype(BF16)
        v_n = jnp.concatenate([vn_ref[s], pad], axis=0).astype(BF16)
        s_b = jnp.where(valid_buf, _dot(qs, k_t) * ATTN_SCALE, -jnp.inf)
        s_n = jnp.where(valid_new, _dot_nt(qs, k_n) * ATTN_SCALE, -jnp.inf)
        mx = jnp.maximum(jnp.max(s_b, axis=-1, keepdims=True), jnp.max(s_n, axis=-1, keepdims=True))
        p_b, p_n = jnp.exp(s_b - mx), jnp.exp(s_n - mx)
        den = jnp.sum(p_b, axis=-1, keepdims=True) + jnp.sum(p_n, axis=-1, keepdims=True)
        o = (_dot_nt(p_b.astype(BF16), v_t) + _dot(p_n.astype(BF16), v_n)) / den
        o_ref[s] = _unstack_heads(o, n_new)
        lse_ref[s] = _unstack_heads(jnp.broadcast_to(mx + jnp.log(den), o.shape), n_new)
        return carry

    lax.fori_loop(0, seq_blk, one_seq, 0, unroll=True)


def _attn_sample_job(qkv, cache, n_seq, n_new, grp, n_steps):
    d = DILATIONS[grp]
    w = cache.shape[1]
    assert w == N_STEPS * d and n_new <= LANE and n_seq % n_steps == 0
    seq_blk = n_seq // n_steps
    qkv3 = qkv.reshape(n_seq, n_new, D_QKV)
    c_t = cache.transpose(0, 2, 3, 4, 1).reshape(n_seq, 2, D_HEADS, w)
    blk = (seq_blk, n_new, D_HEADS)
    col = lambda which: pl.BlockSpec(blk, lambda i: (i, 0, _qkv_col(which, grp)))
    out_spec = pl.BlockSpec(blk, lambda i: (i, 0, 0))
    out_shape = jax.ShapeDtypeStruct((n_seq, n_new, D_HEADS), F32)
    return _SideJob((qkv3, qkv3, qkv3, c_t),
                    (col(0), col(1), col(2), pl.BlockSpec((seq_blk, 2, D_HEADS, w), lambda i: (i, 0, 0, 0))),
                    (out_shape, out_shape), (out_spec, out_spec),
                    functools.partial(_attn_sample_kernel, d=d, n_new=n_new, seq_blk=seq_blk))


def _halves(o, lse):
    o, lse = o.reshape(-1, D_HEADS), lse.reshape(-1, D_HEADS)
    return o[:, :LANE], o[:, LANE:], lse[:, :LANE], lse[:, LANE:]


def _kv_new_job(qkv, n_seq, n_new, n_steps):
    halves, l_parts = D_HEADS // LANE, 2
    n_units, l_blk = N_GROUPS * 2 * halves * l_parts, n_new // l_parts
    assert n_steps >= n_units and n_new % l_parts == 0
    unit = lambda i: jnp.minimum(i, n_units - 1)
    l_part = lambda i: unit(i) % l_parts
    half = lambda i: unit(i) // l_parts % halves
    kv = lambda i: unit(i) // (l_parts * halves) % 2
    grp = lambda i: unit(i) // (l_parts * halves * 2)
    in_spec = pl.BlockSpec((n_seq * n_new, LANE), lambda i: (0, halves * ((1 + kv(i)) * N_GROUPS + grp(i)) + half(i)))
    out_spec = pl.BlockSpec((1, l_blk, 1, LANE, n_seq), lambda i: (grp(i), l_part(i), kv(i), half(i), 0))
    out_shape = jax.ShapeDtypeStruct((N_GROUPS, n_new, 2, D_HEADS, n_seq), F32)

    def body(x_ref, o_ref):
        l0 = l_part(pl.program_id(0)) * l_blk
        for dl in range(l_blk):
            o_ref[0, dl, 0] = x_ref[pl.ds(l0 + dl, n_seq, stride=n_new), :].T

    return _SideJob((qkv,), (in_spec,), (out_shape,), (out_spec,), body)


def _kv_new_outputs(out, n_seq, n_new):
    out = out.reshape(N_GROUPS, 1, n_new, 2, HEADS, HEAD_DIM, n_seq).transpose(0, 1, 6, 2, 3, 4, 5)
    return [out[g] for g in range(N_GROUPS)]


def _gelu_tanh(x):
    return 0.5 * x * (1.0 + jnp.tanh(math.sqrt(2.0 / math.pi) * (x + 0.044715 * (x * x * x))))


def _stage3_kernel(*refs):
    x1_ref, yp_ref = refs[:2]
    attn = refs[2:2 + 4 * N_GROUPS]
    (p_ref, gm_ref, wgate_ref, wglu_ref, wbs_ref, wba_ref, wout_ref, g2_ref, wg_ref, wu_ref, wd_ref,
     gp_ref, wpg_ref, wpp_ref, gf_ref, y_ref) = refs[2 + 4 * N_GROUPS:]
    x1 = x1_ref[...]
    gates = jax.nn.sigmoid(_dot(_rms(x1, gm_ref[...]).astype(BF16), wgate_ref[...]))
    y = _gelu_tanh(yp_ref[...])
    glu = y * jax.nn.sigmoid(_dot(y.astype(BF16), wglu_ref[...]))
    full = lambda lo, hi: jnp.concatenate([lo[...], hi[...]], axis=1)
    outs = [full(attn[4 * g], attn[4 * g + 1]) for g in range(N_GROUPS)]
    lses = [full(attn[4 * g + 2], attn[4 * g + 3]) for g in range(N_GROUPS)]
    mx = jnp.maximum(jnp.maximum(lses[0], lses[1]), lses[2])
    es = [jnp.exp(l - mx) for l in lses]
    y_attn = (es[0] * outs[0] + es[1] * outs[1] + es[2] * outs[2]) / (es[0] + es[1] + es[2])
    merged = (gates[:, :D_MODEL] * _dot(glu.astype(BF16), wbs_ref[...])
              + gates[:, D_MODEL:] * _dot(y_attn.astype(BF16), wba_ref[...]))
    x = x1 + _dot(merged.astype(BF16), wout_ref[...])
    x = x + 0.5 * _swiglu(_rms(x, g2_ref[...]).astype(BF16), wg_ref, wu_ref, wd_ref)
    gate = jax.nn.sigmoid(_dot(_rms(x, gp_ref[...]).astype(BF16), wpg_ref[...]))
    x = x + gate * _dot(p_ref[...].astype(BF16), wpp_ref[...])
    y_ref[...] = _rms(x, gf_ref[...])


def _stage3(tok_inputs, weights):
    t, tm = tok_inputs[0].shape[0], STAGE_TM
    tok = lambda a: pl.BlockSpec((tm, a.shape[1]), lambda i: (i, 0))
    return pl.pallas_call(
        _stage3_kernel,
        grid=(t // tm,),
        in_specs=[tok(a) for a in tok_inputs] + [_const_spec(w.shape) for w in weights],
        out_specs=pl.BlockSpec((tm, D_MODEL), lambda i: (i, 0)),
        out_shape=jax.ShapeDtypeStruct((t, D_MODEL), F32),
        compiler_params=_params(1),
        name="stage3",
    )(*tok_inputs, *weights)


def kernel(x_prompt, x_sample, p_prompt, p_sample, cache_kv_w128, cache_kv_w512, cache_kv_w2048, state_ssm,
           g_ffn1, ffn1_w_gate, ffn1_w_up, ffn1_w_down, g_mix, w_in, ssm_a_re, ssm_a_im, ssm_log_dt,
           ssm_b_re, ssm_b_im, ssm_c_re, ssm_c_im, ssm_d, ssm_w_glu, w_br_ssm, w_br_attn, w_out,
           g_ffn2, ffn2_w_gate, ffn2_w_up, ffn2_w_down, g_ple, w_ple_gate, w_ple_proj, g_final):
    assert x_prompt.shape[-1] == D_MODEL and g_ffn1.shape[0] == 1
    n_p, seq, _ = x_prompt.shape
    n_s, n_new, _ = x_sample.shape
    caches = (cache_kv_w128, cache_kv_w512, cache_kv_w2048)
    row = lambda g: g.reshape(1, -1)
    bf = lambda w: w[0].astype(BF16)

    wi = w_in[0]
    win = wi[:, :D_SSM + D_QKV].astype(BF16)
    w_gates = wi[:, D_SSM + D_QKV:].astype(BF16)
    ffn1_w = (row(g_ffn1[0]), bf(ffn1_w_gate), bf(ffn1_w_up), bf(ffn1_w_down))
    s3_f32 = [w[0] for w in (ssm_w_glu, w_br_ssm, w_br_attn, w_out, ffn2_w_gate, ffn2_w_up, ffn2_w_down,
                             w_ple_gate, w_ple_proj)]

    tabs_p, tabs_s = _ssm_tables(
        ssm_a_re[0], ssm_a_im[0], ssm_log_dt[0], ssm_b_re[0], ssm_b_im[0], ssm_c_re[0], ssm_c_im[0],
        ssm_d[0], SCAN_BLOCK, n_new)
    tile_state = (-1, N_SSM_TILES, 2, SSM_TILE_GROUPS, SSM_STATE)
    from_tiles = lambda h: h.reshape(tile_state).transpose(0, 1, 3, 4, 2).reshape(1, -1, N_SSM_GROUPS, SSM_STATE, 2)

    xs = x_sample.reshape(n_s * n_new, D_MODEL)
    x1_s, _ = _ffn1(xs, *ffn1_w, tm=min(STAGE_TM, xs.shape[0]))
    u_s, qkv_s, _ = _proj(x1_s, row(g_mix[0]), win, tm=min(STAGE_TM, xs.shape[0]))

    xp = x_prompt.reshape(n_p * seq, D_MODEL)
    attn_job = lambda grp, n_steps: _attn_sample_job(qkv_s, caches[grp][0], n_s, n_new, grp, n_steps)
    steps_a, steps_b = xp.shape[0] // FFN1_HOST_TM, xp.shape[0] // STAGE_TM
    x1, side = _ffn1(xp, *ffn1_w, tm=FFN1_HOST_TM,
                     jobs=[attn_job(2, steps_a), attn_job(0, steps_a)] + [_cast_job(w, steps_a) for w in s3_f32])
    attn_s = {2: side[0], 0: side[1]}
    wglu, wbs, wba, wout, wg2, wu2, wd2, wpg, wpp = (c[0] for c in side[2:])
    s3_w = (row(g_mix[0]), w_gates, wglu, wbs, wba, wout, row(g_ffn2[0]), wg2, wu2, wd2, row(g_ple[0]), wpg, wpp,
            row(g_final))
    u, qkv, side = _proj(x1, row(g_mix[0]), win, tm=STAGE_TM,
                         jobs=[attn_job(1, steps_b), _kv_new_job(qkv_s, n_s, n_new, steps_b)])
    attn_s[1] = side[0]
    kv_sample = _kv_new_outputs(side[1][0], n_s, n_new)
    y_pre, h_last = _ssm_prompt(u, tabs_p, n_p, seq)
    attn_p = [_attn_prompt(qkv, n_p, seq, grp) for grp in range(N_GROUPS)]
    attn, kv_prompt = [a for per_group, _ in attn_p for a in per_group], [tail for _, tail in attn_p]
    tok = (x1, y_pre, *attn, p_prompt[0].reshape(n_p * seq, D_PLE))
    y_prompt = _stage3(tok, s3_w).reshape(n_p, seq, D_MODEL)
    ssm_prompt = from_tiles(h_last)

    h0 = state_ssm[0].reshape(n_s, N_SSM_TILES, SSM_TILE_GROUPS, SSM_STATE, 2)
    h0 = h0.transpose(0, 1, 4, 2, 3).reshape(n_s, N_SSM_TILES * TILE_STATE_W)
    y_pre, h_new = _ssm_sample(u_s, h0, tabs_s, n_s, n_new)
    attn = [a for grp in range(N_GROUPS) for a in _halves(*attn_s[grp])]
    tok = (x1_s, y_pre, *attn, p_sample[0].reshape(n_s * n_new, D_PLE))
    y_sample = _stage3(tok, s3_w).reshape(n_s, n_new, D_MODEL)
    ssm_sample = from_tiles(h_new)

    return (y_prompt, y_sample, kv_prompt[0], kv_prompt[1], kv_prompt[2], ssm_prompt,
            kv_sample[0], kv_sample[1], kv_sample[2], ssm_sample)
```

```python
import functools
import math
from typing import Callable, NamedTuple

import jax
import jax.numpy as jnp
import numpy as np
from jax import lax
from jax.experimental import pallas as pl
from jax.experimental.pallas import tpu as pltpu

D_MODEL = 1024
D_PLE = 256
D_FF = 2816
SSM_GROUP = 16
SSM_STATE = 64
D_SSM = 512
N_SSM_GROUPS = D_SSM // SSM_GROUP
HEAD_DIM = 64
HEADS = 4
WINDOWS = (128, 512, 2048)
DILATIONS = (1, 4, 16)
N_GROUPS = 3
D_HEADS = HEADS * HEAD_DIM
N_STEPS = 128
ATTN_SCALE = HEAD_DIM ** -0.5
EPS = 1e-6

LANE = 128
V7X_VMEM_LIMIT_BYTES = 56 * 1024 * 1024
STAGE_TM = 512
FFN1_HOST_TM = 256

CHUNK = 16
SCAN_BLOCK = 8
N_PAIRS = CHUNK // 2
PAIR_W = 2 * LANE
SSM_TILE_GROUPS = LANE // SSM_GROUP
N_SSM_TILES = N_SSM_GROUPS // SSM_TILE_GROUPS
TILE_STATE_W = SSM_TILE_GROUPS * 2 * SSM_STATE
Q_BLOCK = 128
ATTN_TB = 2048
ATTN_PAR = 16

D_QKV = 3 * N_GROUPS * D_HEADS


def _qkv_col(which, grp):
    return which * N_GROUPS + grp


BF16 = jnp.bfloat16
F32 = jnp.float32


def _dot(a, b):
    return jnp.dot(a, b, preferred_element_type=F32)


def _dot_nt(a, b):
    return lax.dot_general(a, b, (((1,), (1,)), ((), ())), preferred_element_type=F32)


def _rms(x, g):
    return x * lax.rsqrt(jnp.mean(x * x, axis=-1, keepdims=True) + EPS) * g


def _swiglu(xn, wg_ref, wu_ref, wd_ref):
    gate = _dot(xn, wg_ref[...])
    up = _dot(xn, wu_ref[...])
    act = (gate * jax.nn.sigmoid(gate) * up).astype(BF16)
    return _dot(act, wd_ref[...])


def _const_spec(shape):
    nd = len(shape)
    return pl.BlockSpec(shape, lambda *_: (0,) * nd, pipeline_mode=pl.Buffered(1))


def _params(n_grid_dims):
    return pltpu.CompilerParams(dimension_semantics=("arbitrary",) * n_grid_dims,
                                vmem_limit_bytes=V7X_VMEM_LIMIT_BYTES)


class _SideJob(NamedTuple):
    arrays: tuple
    in_specs: tuple
    out_shapes: tuple
    out_specs: tuple
    body: Callable


def _cast_job(w, n_steps):
    rows, cols = w.shape
    steps = max(s for s in range(1, n_steps + 1) if rows % s == 0 and (rows // s) % 16 == 0)
    spec = pl.BlockSpec((rows // steps, cols), lambda i: (jnp.minimum(i, steps - 1), 0))

    def body(src, dst):
        dst[...] = src[...].astype(BF16)

    return _SideJob((w,), (spec,), (jax.ShapeDtypeStruct(w.shape, BF16),), (spec,), body)


def _hosted_call(main_body, n_steps, arrays, in_specs, out_shapes, out_specs, jobs, name):
    n_in, n_out = len(arrays), len(out_shapes)

    def body(*refs):
        pos, job_in, job_out = n_in, [], []
        for job in jobs:
            job_in.append(refs[pos:pos + len(job.arrays)])
            pos += len(job.arrays)
        main_out = refs[pos:pos + n_out]
        pos += n_out
        for job in jobs:
            job_out.append(refs[pos:pos + len(job.out_shapes)])
            pos += len(job.out_shapes)
        for job, ins, outs in zip(jobs, job_in, job_out):
            job.body(*ins, *outs)
        main_body(*refs[:n_in], *main_out)

    outs = pl.pallas_call(
        body,
        grid=(n_steps,),
        in_specs=list(in_specs) + [sp for job in jobs for sp in job.in_specs],
        out_specs=list(out_specs) + [sp for job in jobs for sp in job.out_specs],
        out_shape=list(out_shapes) + [sh for job in jobs for sh in job.out_shapes],
        compiler_params=_params(1),
        name=name,
    )(*arrays, *[a for job in jobs for a in job.arrays])
    main, pos, per_job = outs[:n_out], n_out, []
    for job in jobs:
        per_job.append(outs[pos:pos + len(job.out_shapes)])
        pos += len(job.out_shapes)
    return main, per_job


def _ffn1_kernel(x_ref, g1_ref, wg_ref, wu_ref, wd_ref, x1_ref):
    x = x_ref[...]
    x1_ref[...] = x + 0.5 * _swiglu(_rms(x, g1_ref[...]).astype(BF16), wg_ref, wu_ref, wd_ref)


def _ffn1(x, g1, wg, wu, wd, tm, jobs=()):
    t = x.shape[0]
    tok = pl.BlockSpec((tm, D_MODEL), lambda i: (i, 0))
    (x1,), per_job = _hosted_call(
        _ffn1_kernel, t // tm, (x, g1, wg, wu, wd),
        [tok] + [_const_spec(a.shape) for a in (g1, wg, wu, wd)],
        [jax.ShapeDtypeStruct((t, D_MODEL), F32)], [tok], jobs, "ffn1")
    return x1, per_job


def _proj_kernel(x1_ref, gm_ref, win_ref, u_ref, qkv_ref):
    z = _dot(_rms(x1_ref[...], gm_ref[...]).astype(BF16), win_ref[...])
    u_ref[...] = z[:, :D_SSM]
    qkv_ref[...] = z[:, D_SSM:]


def _proj(x1, gm, win, tm, jobs=()):
    t = x1.shape[0]
    tok = lambda w: pl.BlockSpec((tm, w), lambda i: (i, 0))
    (u, qkv), per_job = _hosted_call(
        _proj_kernel, t // tm, (x1, gm, win),
        [tok(D_MODEL), _const_spec(gm.shape), _const_spec(win.shape)],
        [jax.ShapeDtypeStruct((t, D_SSM), F32), jax.ShapeDtypeStruct((t, D_QKV), F32)], [tok(D_SSM), tok(D_QKV)],
        jobs, "proj")
    return u, qkv, per_job


def _swap_halves(x):
    half = x.shape[1] // 2
    return jnp.concatenate([x[:, half:], x[:, :half]], axis=1)


def _cmul_split(x, a1, a2):
    return x * a1 + _swap_halves(x) * a2


def _ssm_kernel(*refs, n_t, n_rows, has_h0):
    if has_h0:
        u_ref, h0_ref, kp_ref, wp_ref, vpt_ref, a1_ref, a2_ref, y_ref, hl_ref = refs
    else:
        u_ref, kp_ref, wp_ref, vpt_ref, a1_ref, a2_ref, y_ref, hl_ref = refs
    n_pairs = n_t // 2
    tok = lambda t: pl.ds(t, n_rows, stride=n_t)
    up = [jnp.concatenate([u_ref[tok(2 * a), :], u_ref[tok(2 * a + 1), :]], axis=1).astype(BF16)
          for a in range(n_pairs)]
    s = _dot(jnp.concatenate(up, axis=1), wp_ref[0])
    in_chunk = [None] * n_pairs
    todo = [(b, a) for b in range(n_pairs) for a in range(b + 1)]

    def in_chunk_products(count):
        for b, a in [todo.pop(0) for _ in range(min(count, len(todo)))]:
            prod = _dot(up[a], kp_ref[0, b - a])
            in_chunk[b] = prod if in_chunk[b] is None else in_chunk[b] + prod

    if has_h0:
        hprev = h0_ref[...]
        h_last = _cmul_split(hprev, a1_ref[0], a2_ref[0]) + s
        hl_ref[...] = h_last
    else:
        blk = SCAN_BLOCK
        assert n_rows % blk == 0 and a1_ref.shape[1] == blk
        row = lax.broadcasted_iota(jnp.int32, (blk, s.shape[1]), 0)
        carry = jnp.zeros((1, s.shape[1]), F32)
        entering = []
        for i in range(n_rows // blk):
            h = s[i * blk:(i + 1) * blk]
            for lvl in range(int(math.log2(blk))):
                sh = 1 << lvl
                shifted = jnp.where(row >= sh, pltpu.roll(h, sh, axis=0), 0.0)
                h = h + _cmul_split(shifted, a1_ref[0, sh - 1:sh, :], a2_ref[0, sh - 1:sh, :])
            carry_b = jnp.broadcast_to(carry, h.shape)
            h = h + _cmul_split(carry_b, a1_ref[0], a2_ref[0])
            entering.append(jnp.where(row == 0, carry_b, pltpu.roll(h, 1, axis=0)))
            carry = h[blk - 1:blk]
            in_chunk_products(-(-len(todo) // (n_rows // blk - i)))
        hprev = jnp.concatenate(entering, axis=0)
        hl_ref[0, 0] = carry
    in_chunk_products(len(todo))
    hpb = hprev.astype(BF16)
    for b in range(n_pairs):
        acc = _dot_nt(hpb, vpt_ref[0, b]) + in_chunk[b]
        y_ref[tok(2 * b), :] = acc[:, :LANE]
        y_ref[tok(2 * b + 1), :] = acc[:, LANE:]


def _ssm_table_specs(tabs, idx, n_t):
    kp, wp, vp, a1, a2 = tabs
    n_pairs, w_rows = n_t // 2, n_t * LANE
    return [pl.BlockSpec((1, n_pairs) + kp.shape[2:], lambda *g: (idx(*g), 0, 0, 0)),
            pl.BlockSpec((1, w_rows, wp.shape[2]), lambda *g: (idx(*g), wp.shape[1] // w_rows - 1, 0)),
            pl.BlockSpec((1, n_pairs) + vp.shape[2:], lambda *g: (idx(*g), 0, 0, 0)),
            pl.BlockSpec((1,) + a1.shape[1:], lambda *g: (idx(*g), 0, 0)),
            pl.BlockSpec((1,) + a2.shape[1:], lambda *g: (idx(*g), 0, 0))]


def _ssm_prompt(u, tabs, n, seq):
    n_rows = seq // CHUNK
    tile = pl.BlockSpec((seq, LANE), lambda j, i: (i, j))
    return pl.pallas_call(
        functools.partial(_ssm_kernel, n_t=CHUNK, n_rows=n_rows, has_h0=False),
        grid=(N_SSM_TILES, n),
        in_specs=[tile] + _ssm_table_specs(tabs, lambda j, i: j, CHUNK),
        out_specs=[tile, pl.BlockSpec((1, 1, 1, TILE_STATE_W), lambda j, i: (i, j, 0, 0))],
        out_shape=[jax.ShapeDtypeStruct((n * seq, D_SSM), F32),
                   jax.ShapeDtypeStruct((n, N_SSM_TILES, 1, TILE_STATE_W), F32)],
        compiler_params=_params(2),
        name="ssm_prompt",
    )(u, *tabs)


def _ssm_sample(u, h0, tabs, n_seq, n_new):
    tile = pl.BlockSpec((n_seq * n_new, LANE), lambda j: (0, j))
    state = pl.BlockSpec((n_seq, TILE_STATE_W), lambda j: (0, j))
    return pl.pallas_call(
        functools.partial(_ssm_kernel, n_t=n_new, n_rows=n_seq, has_h0=True),
        grid=(N_SSM_TILES,),
        in_specs=[tile, state] + _ssm_table_specs(tabs, lambda j: j, n_new),
        out_specs=[tile, state],
        out_shape=[jax.ShapeDtypeStruct((n_seq * n_new, D_SSM), F32),
                   jax.ShapeDtypeStruct((n_seq, N_SSM_TILES * TILE_STATE_W), F32)],
        compiler_params=_params(1),
        name="ssm_sample",
    )(u, h0, *tabs)


def _ssm_prep_kernel(are_ref, aim_ref, ldt_ref, bre_ref, bim_ref, cre_ref, cim_ref, d_ref,
                     kp_ref, wp_ref, vpt_ref, a1_ref, a2_ref, a1s_ref, a2s_ref, *, n_levels, n_new):
    a_re, a_im = are_ref[0], aim_ref[0]
    dt = jnp.exp(ldt_ref[0])
    mag = jnp.exp(a_re * dt)
    ab_re, ab_im = mag * jnp.cos(a_im * dt), mag * jnp.sin(a_im * dt)
    den = a_re * a_re + a_im * a_im
    nr, ni = ab_re - 1.0, ab_im
    f_re = (nr * a_re + ni * a_im) / den
    f_im = (ni * a_re - nr * a_im) / den
    shape = bre_ref.shape[1:]
    same_group = (lax.broadcasted_iota(jnp.int32, shape, 0) // SSM_GROUP
                  == lax.broadcasted_iota(jnp.int32, shape, 1) // SSM_STATE)
    b_re, b_im = bre_ref[0], bim_ref[0]
    x_re = jnp.where(same_group, f_re * b_re - f_im * b_im, 0.0)
    x_im = jnp.where(same_group, f_re * b_im + f_im * b_re, 0.0)
    c_re = jnp.where(same_group, cre_ref[0], 0.0)
    c_im = jnp.where(same_group, cim_ref[0], 0.0)
    pw = [(jnp.ones_like(ab_re), jnp.zeros_like(ab_im))]
    for _ in range(CHUNK):
        r, i = pw[-1]
        pw.append((r * ab_re - i * ab_im, r * ab_im + i * ab_re))
    cat = lambda r, i: jnp.concatenate([r, i], axis=1)
    e_pack = [cat(c_re * r - c_im * i, -(c_re * i + c_im * r)) for r, i in pw]
    split = lambda a: (a.astype(BF16), (a - a.astype(BF16).astype(F32)).astype(BF16))
    x_hi, x_lo = split(cat(x_re, x_im))
    x_3 = jnp.concatenate([x_hi, x_hi, x_lo], axis=1)
    k_lag = []
    for k in range(CHUNK):
        e_hi, e_lo = split(e_pack[k])
        k_lag.append(_dot_nt(x_3, jnp.concatenate([e_hi, e_lo, e_hi], axis=1)))
    diag = (lax.broadcasted_iota(jnp.int32, (LANE, LANE), 0) == lax.broadcasted_iota(jnp.int32, (LANE, LANE), 1))
    k_lag[0] = k_lag[0] + jnp.where(diag, d_ref[0], 0.0)
    zero = jnp.zeros((LANE, LANE), F32)
    for dl in range(N_PAIRS):
        top = cat(k_lag[2 * dl], k_lag[2 * dl + 1])
        bot = cat(k_lag[2 * dl - 1] if dl > 0 else zero, k_lag[2 * dl])
        kp_ref[0, dl] = jnp.concatenate([top, bot], axis=0).astype(BF16)
    for t in range(CHUNK):
        r, i = pw[CHUNK - 1 - t]
        wp_ref[0, t * LANE:(t + 1) * LANE, :] = cat(x_re * r - x_im * i, x_re * i + x_im * r).astype(BF16)
    for t in range(CHUNK):
        vpt_ref[0, t // 2, (t % 2) * LANE:(t % 2 + 1) * LANE, :] = e_pack[t + 1].astype(BF16)
    r, i = step_r, step_i = pw[CHUNK]
    for s in range(n_levels):
        a1_ref[0, s:s + 1, :], a2_ref[0, s:s + 1, :] = cat(r, r), cat(-i, i)
        r, i = r * step_r - i * step_i, r * step_i + i * step_r
    r, i = pw[n_new]
    a1s_ref[0], a2s_ref[0] = cat(r, r), cat(-i, i)


def _ssm_tables(a_re, a_im, log_dt, b_re, b_im, c_re, c_im, d_skip, n_levels, n_new):
    nj, tg = N_SSM_TILES, SSM_TILE_GROUPS
    half = tg * SSM_STATE
    rowv = lambda v: v.reshape(nj, 1, half)
    tiled = lambda m: jnp.tile(m.reshape(nj, LANE, SSM_STATE), (1, 1, tg))
    ins = (rowv(a_re), rowv(a_im), rowv(jnp.repeat(log_dt, SSM_STATE)),
           tiled(b_re.transpose(0, 2, 1)), tiled(b_im.transpose(0, 2, 1)), tiled(c_re), tiled(c_im),
           d_skip.reshape(nj, 1, LANE))
    shapes = ((nj, N_PAIRS, PAIR_W, PAIR_W), (nj, CHUNK * LANE, TILE_STATE_W), (nj, N_PAIRS, PAIR_W, TILE_STATE_W),
              (nj, n_levels, TILE_STATE_W), (nj, n_levels, TILE_STATE_W), (nj, 1, TILE_STATE_W), (nj, 1, TILE_STATE_W))
    dtypes = (BF16, BF16, BF16, F32, F32, F32, F32)
    per_tile = lambda s: pl.BlockSpec((1,) + tuple(s[1:]), lambda j: (j,) + (0,) * (len(s) - 1))
    kp, wp, vpt, a1, a2, a1s, a2s = pl.pallas_call(
        functools.partial(_ssm_prep_kernel, n_levels=n_levels, n_new=n_new),
        grid=(nj,),
        in_specs=[per_tile(x.shape) for x in ins],
        out_specs=[per_tile(s) for s in shapes],
        out_shape=[jax.ShapeDtypeStruct(s, dt) for s, dt in zip(shapes, dtypes)],
        compiler_params=_params(1),
        name="ssm_prep",
    )(*ins)
    return (kp, wp, vpt, a1, a2), (kp, wp, vpt, a1s, a2s)


def _head_of_lane(width):
    return lax.broadcasted_iota(jnp.int32, (1, width), 1) // HEAD_DIM


def _stack_heads(q):
    head = _head_of_lane(q.shape[1])
    q = q.astype(F32)
    return jnp.concatenate([jnp.where(head == h, q, 0.0) for h in range(HEADS)], axis=0).astype(BF16)


def _unstack_heads(x, rows):
    head = _head_of_lane(x.shape[1])
    out = x[(HEADS - 1) * rows:]
    for h in range(HEADS - 2, -1, -1):
        out = jnp.where(head == h, x[h * rows:(h + 1) * rows], out)
    return out


def _attn_prompt_kernel(bias_ref, q0, q1, k0, k1, v0, v1, kt_ref, vt_ref, o0, o1, l0, l1, tail_ref, kprev, vprev, *, d, m):
    qb = Q_BLOCK
    b = pl.program_id(1)
    n_par = min(d, ATTN_PAR)
    rd, wr = b % 2, (b + 1) % 2
    head = _head_of_lane(D_HEADS)

    @pl.when(b == 0)
    def _():
        kprev[0] = jnp.zeros(kprev.shape[1:], kprev.dtype)
        vprev[0] = jnp.zeros(vprev.shape[1:], vprev.dtype)

    def block(r, j, k_prev, v_prev):
        start = r + j * qb * d
        rows = pl.ds(start, qb, stride=d) if d > 1 else pl.ds(pl.multiple_of(start, qb), qb)
        both = lambda lo, hi: jnp.concatenate([lo[rows, :], hi[rows, :]], axis=1)
        qs = _stack_heads(both(q0, q1) * ATTN_SCALE)
        k_own, v_own = both(k0, k1).astype(BF16), both(v0, v1).astype(BF16)
        k = jnp.concatenate([k_prev, k_own], axis=0)
        v = jnp.concatenate([v_prev, v_own], axis=0)
        s = _dot_nt(qs, k) + bias_ref[jnp.minimum(b * m + j, 1)]
        mx = jnp.max(s, axis=-1, keepdims=True)
        pe = jnp.exp(s - mx)
        den = jnp.sum(pe, axis=-1, keepdims=True)
        pn = (pe * (1.0 / den)).astype(BF16)
        o = _dot(pn[:qb], jnp.where(head == 0, v, jnp.zeros_like(v)))
        for h in range(1, HEADS):
            o = o + _dot(pn[h * qb:(h + 1) * qb], jnp.where(head == h, v, jnp.zeros_like(v)))
        lse = _unstack_heads(jnp.broadcast_to(mx + jnp.log(den), (HEADS * qb, D_HEADS)), qb)
        o0[rows, :], o1[rows, :] = o[:, :LANE], o[:, LANE:]
        l0[rows, :], l1[rows, :] = lse[:, :LANE], lse[:, LANE:]
        return k_own, v_own

    def class_group(g, carry):
        classes = [g * n_par + t for t in range(n_par)]

        def step(j, prev):
            return tuple(x for t, r in enumerate(classes) for x in block(r, j, prev[2 * t], prev[2 * t + 1]))

        init = tuple(x for r in classes for x in (kprev[rd, r], vprev[rd, r]))
        last = lax.fori_loop(0, m, step, init, unroll=max(1, min(m, ATTN_PAR // n_par)))
        for t, r in enumerate(classes):
            kprev[wr, r], vprev[wr, r] = last[2 * t], last[2 * t + 1]
        return carry

    lax.fori_loop(0, d // n_par, class_group, 0)
    tail_ref[0, 0] = kt_ref[...].T
    tail_ref[0, 1] = vt_ref[...].T


def _attn_bias():
    qi = np.arange(HEADS * Q_BLOCK)[:, None] % Q_BLOCK
    ki = np.arange(2 * Q_BLOCK)[None, :]
    diff = Q_BLOCK + qi - ki
    valid = (diff >= 0) & (diff <= N_STEPS)
    return jnp.asarray(np.where(np.stack([valid & (ki >= Q_BLOCK), valid]), 0.0, -np.inf), F32)


def _attn_prompt(qkv, n, seq, grp):
    d = DILATIONS[grp]
    tb = min(ATTN_TB, seq)
    assert tb % (Q_BLOCK * d) == 0 and seq % tb == 0
    m, nb = tb // (Q_BLOCK * d), seq // tb
    cur = lambda c: pl.BlockSpec((tb, LANE), lambda i, b: (i * nb + b, c))
    qc, kc, vc = (2 * _qkv_col(which, grp) for which in range(3))
    bias = _attn_bias()
    keep = min(WINDOWS[grp], seq)
    tt = min(keep, 512)
    n_tiles, first = keep // tt, (seq - keep) // tt
    assert keep % tt == 0 and (seq - keep) % tt == 0 and nb >= n_tiles
    tile = lambda b: jnp.minimum(b, n_tiles - 1)
    tail_rows = lambda which: pl.BlockSpec(
        (tt, D_HEADS), lambda i, b: (i * (seq // tt) + first + tile(b), _qkv_col(which, grp)))
    *attn, tail = pl.pallas_call(
        functools.partial(_attn_prompt_kernel, d=d, m=m),
        grid=(n, nb),
        in_specs=[_const_spec(bias.shape), cur(qc), cur(qc + 1), cur(kc), cur(kc + 1), cur(vc), cur(vc + 1),
                  tail_rows(1), tail_rows(2)],
        out_specs=[pl.BlockSpec((tb, LANE), lambda i, b: (i * nb + b, 0))] * 4
        + [pl.BlockSpec((1, 2, D_HEADS, tt), lambda i, b: (i, 0, 0, tile(b)))],
        out_shape=[jax.ShapeDtypeStruct((n * seq, LANE), F32)] * 4
        + [jax.ShapeDtypeStruct((n, 2, D_HEADS, keep), F32)],
        scratch_shapes=[pltpu.VMEM((2, d, Q_BLOCK, D_HEADS), BF16)] * 2,
        compiler_params=_params(2),
        name=f"attn_prompt_d{d}",
    )(bias, *([qkv] * 8))
    return attn, tail.reshape(1, n, 2, HEADS, HEAD_DIM, keep).transpose(0, 1, 5, 2, 3, 4)


def _attn_sample_kernel(q_ref, kn_ref, vn_ref, c_ref, o_ref, lse_ref, *, d, n_new, seq_blk):
    w = c_ref.shape[-1]
    n_rows = HEADS * n_new
    l_b = lax.broadcasted_iota(jnp.int32, (n_rows, w), 0) % n_new
    pos = lax.broadcasted_iota(jnp.int32, (n_rows, w), 1)
    valid_buf = (((w + l_b - pos) & (d - 1)) == 0) & (pos >= l_b)
    dn = (lax.broadcasted_iota(jnp.int32, (n_rows, LANE), 0) % n_new) - lax.broadcasted_iota(jnp.int32, (n_rows, LANE), 1)
    valid_new = (dn >= 0) & ((dn & (d - 1)) == 0)
    pad = jnp.zeros((LANE - n_new, D_HEADS), F32)

    def one_seq(s, carry):
        qs = _stack_heads(q_ref[s])
        k_t, v_t = c_ref[s, 0].astype(BF16), c_ref[s, 1].astype(BF16)
        k_n = jnp.concatenate([kn_ref[s], pad], axis=0).astype(BF16)
        v_n = jnp.concatenate([vn_ref[s], pad], axis=0).astype(BF16)
        s_b = jnp.where(valid_buf, _dot(qs, k_t) * ATTN_SCALE, -jnp.inf)
        s_n = jnp.where(valid_new, _dot_nt(qs, k_n) * ATTN_SCALE, -jnp.inf)
        mx = jnp.maximum(jnp.max(s_b, axis=-1, keepdims=True), jnp.max(s_n, axis=-1, keepdims=True))
        p_b, p_n = jnp.exp(s_b - mx), jnp.exp(s_n - mx)
        den = jnp.sum(p_b, axis=-1, keepdims=True) + jnp.sum(p_n, axis=-1, keepdims=True)
        o = (_dot_nt(p_b.astype(BF16), v_t) + _dot(p_n.astype(BF16), v_n)) / den
        o_ref[s] = _unstack_heads(o, n_new)
        lse_ref[s] = _unstack_heads(jnp.broadcast_to(mx + jnp.log(den), o.shape), n_new)
        return carry

    lax.fori_loop(0, seq_blk, one_seq, 0, unroll=True)


def _attn_sample_job(qkv, cache, n_seq, n_new, grp, n_steps):
    d = DILATIONS[grp]
    w = cache.shape[1]
    assert w == N_STEPS * d and n_new <= LANE and n_seq % n_steps == 0
    seq_blk = n_seq // n_steps
    qkv3 = qkv.reshape(n_seq, n_new, D_QKV)
    c_t = cache.transpose(0, 2, 3, 4, 1).reshape(n_seq, 2, D_HEADS, w)
    blk = (seq_blk, n_new, D_HEADS)
    col = lambda which: pl.BlockSpec(blk, lambda i: (i, 0, _qkv_col(which, grp)))
    out_spec = pl.BlockSpec(blk, lambda i: (i, 0, 0))
    out_shape = jax.ShapeDtypeStruct((n_seq, n_new, D_HEADS), F32)
    return _SideJob((qkv3, qkv3, qkv3, c_t),
                    (col(0), col(1), col(2), pl.BlockSpec((seq_blk, 2, D_HEADS, w), lambda i: (i, 0, 0, 0))),
                    (out_shape, out_shape), (out_spec, out_spec),
                    functools.partial(_attn_sample_kernel, d=d, n_new=n_new, seq_blk=seq_blk))


def _halves(o, lse):
    o, lse = o.reshape(-1, D_HEADS), lse.reshape(-1, D_HEADS)
    return o[:, :LANE], o[:, LANE:], lse[:, :LANE], lse[:, LANE:]


def _kv_new_job(qkv, n_seq, n_new, n_steps):
    halves, l_parts = D_HEADS // LANE, 2
    n_units, l_blk = N_GROUPS * 2 * halves * l_parts, n_new // l_parts
    assert n_steps >= n_units and n_new % l_parts == 0
    unit = lambda i: jnp.minimum(i, n_units - 1)
    l_part = lambda i: unit(i) % l_parts
    half = lambda i: unit(i) // l_parts % halves
    kv = lambda i: unit(i) // (l_parts * halves) % 2
    grp = lambda i: unit(i) // (l_parts * halves * 2)
    in_spec = pl.BlockSpec((n_seq * n_new, LANE), lambda i: (0, halves * ((1 + kv(i)) * N_GROUPS + grp(i)) + half(i)))
    out_spec = pl.BlockSpec((1, l_blk, 1, LANE, n_seq), lambda i: (grp(i), l_part(i), kv(i), half(i), 0))
    out_shape = jax.ShapeDtypeStruct((N_GROUPS, n_new, 2, D_HEADS, n_seq), F32)

    def body(x_ref, o_ref):
        l0 = l_part(pl.program_id(0)) * l_blk
        for dl in range(l_blk):
            o_ref[0, dl, 0] = x_ref[pl.ds(l0 + dl, n_seq, stride=n_new), :].T

    return _SideJob((qkv,), (in_spec,), (out_shape,), (out_spec,), body)


def _kv_new_outputs(out, n_seq, n_new):
    out = out.reshape(N_GROUPS, 1, n_new, 2, HEADS, HEAD_DIM, n_seq).transpose(0, 1, 6, 2, 3, 4, 5)
    return [out[g] for g in range(N_GROUPS)]


def _gelu_tanh(x):
    return 0.5 * x * (1.0 + jnp.tanh(math.sqrt(2.0 / math.pi) * (x + 0.044715 * (x * x * x))))


def _stage3_kernel(*refs):
    x1_ref, yp_ref = refs[:2]
    attn = refs[2:2 + 4 * N_GROUPS]
    (p_ref, gm_ref, wgate_ref, wglu_ref, wbs_ref, wba_ref, wout_ref, g2_ref, wg_ref, wu_ref, wd_ref,
     gp_ref, wpg_ref, wpp_ref, gf_ref, y_ref) = refs[2 + 4 * N_GROUPS:]
    x1 = x1_ref[...]
    gates = jax.nn.sigmoid(_dot(_rms(x1, gm_ref[...]).astype(BF16), wgate_ref[...]))
    y = _gelu_tanh(yp_ref[...])
    glu = y * jax.nn.sigmoid(_dot(y.astype(BF16), wglu_ref[...]))
    full = lambda lo, hi: jnp.concatenate([lo[...], hi[...]], axis=1)
    outs = [full(attn[4 * g], attn[4 * g + 1]) for g in range(N_GROUPS)]
    lses = [full(attn[4 * g + 2], attn[4 * g + 3]) for g in range(N_GROUPS)]
    mx = jnp.maximum(jnp.maximum(lses[0], lses[1]), lses[2])
    es = [jnp.exp(l - mx) for l in lses]
    y_attn = (es[0] * outs[0] + es[1] * outs[1] + es[2] * outs[2]) / (es[0] + es[1] + es[2])
    merged = (gates[:, :D_MODEL] * _dot(glu.astype(BF16), wbs_ref[...])
              + gates[:, D_MODEL:] * _dot(y_attn.astype(BF16), wba_ref[...]))
    x = x1 + _dot(merged.astype(BF16), wout_ref[...])
    x = x + 0.5 * _swiglu(_rms(x, g2_ref[...]).astype(BF16), wg_ref, wu_ref, wd_ref)
    gate = jax.nn.sigmoid(_dot(_rms(x, gp_ref[...]).astype(BF16), wpg_ref[...]))
    x = x + gate * _dot(p_ref[...].astype(BF16), wpp_ref[...])
    y_ref[...] = _rms(x, gf_ref[...])


def _stage3(tok_inputs, weights):
    t, tm = tok_inputs[0].shape[0], STAGE_TM
    tok = lambda a: pl.BlockSpec((tm, a.shape[1]), lambda i: (i, 0))
    return pl.pallas_call(
        _stage3_kernel,
        grid=(t // tm,),
        in_specs=[tok(a) for a in tok_inputs] + [_const_spec(w.shape) for w in weights],
        out_specs=pl.BlockSpec((tm, D_MODEL), lambda i: (i, 0)),
        out_shape=jax.ShapeDtypeStruct((t, D_MODEL), F32),
        compiler_params=_params(1),
        name="stage3",
    )(*tok_inputs, *weights)


def kernel(x_prompt, x_sample, p_prompt, p_sample, cache_kv_w128, cache_kv_w512, cache_kv_w2048, state_ssm,
           g_ffn1, ffn1_w_gate, ffn1_w_up, ffn1_w_down, g_mix, w_in, ssm_a_re, ssm_a_im, ssm_log_dt,
           ssm_b_re, ssm_b_im, ssm_c_re, ssm_c_im, ssm_d, ssm_w_glu, w_br_ssm, w_br_attn, w_out,
           g_ffn2, ffn2_w_gate, ffn2_w_up, ffn2_w_down, g_ple, w_ple_gate, w_ple_proj, g_final):
    assert x_prompt.shape[-1] == D_MODEL and g_ffn1.shape[0] == 1
    n_p, seq, _ = x_prompt.shape
    n_s, n_new, _ = x_sample.shape
    caches = (cache_kv_w128, cache_kv_w512, cache_kv_w2048)
    row = lambda g: g.reshape(1, -1)
    bf = lambda w: w[0].astype(BF16)

    wi = w_in[0]
    win = wi[:, :D_SSM + D_QKV].astype(BF16)
    w_gates = wi[:, D_SSM + D_QKV:].astype(BF16)
    ffn1_w = (row(g_ffn1[0]), bf(ffn1_w_gate), bf(ffn1_w_up), bf(ffn1_w_down))
    s3_f32 = [w[0] for w in (ssm_w_glu, w_br_ssm, w_br_attn, w_out, ffn2_w_gate, ffn2_w_up, ffn2_w_down,
                             w_ple_gate, w_ple_proj)]

    tabs_p, tabs_s = _ssm_tables(
        ssm_a_re[0], ssm_a_im[0], ssm_log_dt[0], ssm_b_re[0], ssm_b_im[0], ssm_c_re[0], ssm_c_im[0],
        ssm_d[0], SCAN_BLOCK, n_new)
    tile_state = (-1, N_SSM_TILES, 2, SSM_TILE_GROUPS, SSM_STATE)
    from_tiles = lambda h: h.reshape(tile_state).transpose(0, 1, 3, 4, 2).reshape(1, -1, N_SSM_GROUPS, SSM_STATE, 2)

    xs = x_sample.reshape(n_s * n_new, D_MODEL)
    x1_s, _ = _ffn1(xs, *ffn1_w, tm=min(STAGE_TM, xs.shape[0]))
    u_s, qkv_s, _ = _proj(x1_s, row(g_mix[0]), win, tm=min(STAGE_TM, xs.shape[0]))

    xp = x_prompt.reshape(n_p * seq, D_MODEL)
    attn_job = lambda grp, n_steps: _attn_sample_job(qkv_s, caches[grp][0], n_s, n_new, grp, n_steps)
    steps_a, steps_b = xp.shape[0] // FFN1_HOST_TM, xp.shape[0] // STAGE_TM
    x1, side = _ffn1(xp, *ffn1_w, tm=FFN1_HOST_TM,
                     jobs=[attn_job(2, steps_a), attn_job(0, steps_a)] + [_cast_job(w, steps_a) for w in s3_f32])
    attn_s = {2: side[0], 0: side[1]}
    wglu, wbs, wba, wout, wg2, wu2, wd2, wpg, wpp = (c[0] for c in side[2:])
    s3_w = (row(g_mix[0]), w_gates, wglu, wbs, wba, wout, row(g_ffn2[0]), wg2, wu2, wd2, row(g_ple[0]), wpg, wpp,
            row(g_final))
    u, qkv, side = _proj(x1, row(g_mix[0]), win, tm=STAGE_TM,
                         jobs=[attn_job(1, steps_b), _kv_new_job(qkv_s, n_s, n_new, steps_b)])
    attn_s[1] = side[0]
    kv_sample = _kv_new_outputs(side[1][0], n_s, n_new)
    y_pre, h_last = _ssm_prompt(u, tabs_p, n_p, seq)
    attn_p = [_attn_prompt(qkv, n_p, seq, grp) for grp in range(N_GROUPS)]
    attn, kv_prompt = [a for per_group, _ in attn_p for a in per_group], [tail for _, tail in attn_p]
    tok = (x1, y_pre, *attn, p_prompt[0].reshape(n_p * seq, D_PLE))
    y_prompt = _stage3(tok, s3_w).reshape(n_p, seq, D_MODEL)
    ssm_prompt = from_tiles(h_last)

    h0 = state_ssm[0].reshape(n_s, N_SSM_TILES, SSM_TILE_GROUPS, SSM_STATE, 2)
    h0 = h0.transpose(0, 1, 4, 2, 3).reshape(n_s, N_SSM_TILES * TILE_STATE_W)
    y_pre, h_new = _ssm_sample(u_s, h0, tabs_s, n_s, n_new)
    attn = [a for grp in range(N_GROUPS) for a in _halves(*attn_s[grp])]
    tok = (x1_s, y_pre, *attn, p_sample[0].reshape(n_s * n_new, D_PLE))
    y_sample = _stage3(tok, s3_w).reshape(n_s, n_new, D_MODEL)
    ssm_sample = from_tiles(h_new)

    return (y_prompt, y_sample, kv_prompt[0], kv_prompt[1], kv_prompt[2], ssm_prompt,
            kv_sample[0], kv_sample[1], kv_sample[2], ssm_sample)
```

```python
import functools
import math
from typing import Callable, NamedTuple

import jax
import jax.numpy as jnp
import numpy as np
from jax import lax
from jax.experimental import pallas as pl
from jax.experimental.pallas import tpu as pltpu

D_MODEL = 1024
D_PLE = 256
D_FF = 2816
SSM_GROUP = 16
SSM_STATE = 64
D_SSM = 512
N_SSM_GROUPS = D_SSM // SSM_GROUP
HEAD_DIM = 64
HEADS = 4
WINDOWS = (128, 512, 2048)
DILATIONS = (1, 4, 16)
N_GROUPS = 3
D_HEADS = HEADS * HEAD_DIM
N_STEPS = 128
ATTN_SCALE = HEAD_DIM ** -0.5
EPS = 1e-6

LANE = 128
V7X_VMEM_LIMIT_BYTES = 56 * 1024 * 1024
STAGE_TM = 512
FFN_STREAM_CHUNK = 256
FFN1_HOST_TM = 256

CHUNK = 16
SCAN_BLOCK = 8
N_PAIRS = CHUNK // 2
PAIR_W = 2 * LANE
SSM_TILE_GROUPS = LANE // SSM_GROUP
N_SSM_TILES = N_SSM_GROUPS // SSM_TILE_GROUPS
TILE_STATE_W = SSM_TILE_GROUPS * 2 * SSM_STATE
Q_BLOCK = 128
ATTN_TB = 2048
ATTN_PAR = 16

D_QKV = 3 * N_GROUPS * D_HEADS


def _qkv_col(which, grp):
    return which * N_GROUPS + grp


BF16 = jnp.bfloat16
F32 = jnp.float32


def _dot(a, b):
    return jnp.dot(a, b, preferred_element_type=F32)


def _dot_nt(a, b):
    return lax.dot_general(a, b, (((1,), (1,)), ((), ())), preferred_element_type=F32)


def _rms(x, g):
    return x * lax.rsqrt(jnp.mean(x * x, axis=-1, keepdims=True) + EPS) * g


def _swiglu(xn, wg_ref, wu_ref, wd_ref):
    gate = _dot(xn, wg_ref[...])
    up = _dot(xn, wu_ref[...])
    act = (gate * jax.nn.sigmoid(gate) * up).astype(BF16)
    return _dot(act, wd_ref[...])


def _const_spec(shape):
    nd = len(shape)
    return pl.BlockSpec(shape, lambda *_: (0,) * nd, pipeline_mode=pl.Buffered(1))


def _params(n_grid_dims):
    return pltpu.CompilerParams(dimension_semantics=("arbitrary",) * n_grid_dims,
                                vmem_limit_bytes=V7X_VMEM_LIMIT_BYTES)


class _SideJob(NamedTuple):
    arrays: tuple
    in_specs: tuple
    out_shapes: tuple
    out_specs: tuple
    body: Callable


def _cast_job(w, n_steps):
    rows, cols = w.shape
    steps = max(s for s in range(1, n_steps + 1) if rows % s == 0 and (rows // s) % 16 == 0)
    spec = pl.BlockSpec((rows // steps, cols), lambda i: (jnp.minimum(i, steps - 1), 0))

    def body(src, dst):
        dst[...] = src[...].astype(BF16)

    return _SideJob((w,), (spec,), (jax.ShapeDtypeStruct(w.shape, BF16),), (spec,), body)


def _hosted_call(main_body, n_steps, arrays, in_specs, out_shapes, out_specs, jobs, name):
    n_in, n_out = len(arrays), len(out_shapes)

    def body(*refs):
        pos, job_in, job_out = n_in, [], []
        for job in jobs:
            job_in.append(refs[pos:pos + len(job.arrays)])
            pos += len(job.arrays)
        main_out = refs[pos:pos + n_out]
        pos += n_out
        for job in jobs:
            job_out.append(refs[pos:pos + len(job.out_shapes)])
            pos += len(job.out_shapes)
        for job, ins, outs in zip(jobs, job_in, job_out):
            job.body(*ins, *outs)
        main_body(*refs[:n_in], *main_out)

    outs = pl.pallas_call(
        body,
        grid=(n_steps,),
        in_specs=list(in_specs) + [sp for job in jobs for sp in job.in_specs],
        out_specs=list(out_specs) + [sp for job in jobs for sp in job.out_specs],
        out_shape=list(out_shapes) + [sh for job in jobs for sh in job.out_shapes],
        compiler_params=_params(1),
        name=name,
    )(*arrays, *[a for job in jobs for a in job.arrays])
    main, pos, per_job = outs[:n_out], n_out, []
    for job in jobs:
        per_job.append(outs[pos:pos + len(job.out_shapes)])
        pos += len(job.out_shapes)
    return main, per_job


def _ffn1_kernel(x_ref, g1_ref, wg_ref, wu_ref, wd_ref, x1_ref):
    x = x_ref[...]
    x1_ref[...] = x + 0.5 * _swiglu(_rms(x, g1_ref[...]).astype(BF16), wg_ref, wu_ref, wd_ref)


def _ffn1(x, g1, wg, wu, wd, tm, jobs=()):
    t = x.shape[0]
    tok = pl.BlockSpec((tm, D_MODEL), lambda i: (i, 0))
    (x1,), per_job = _hosted_call(
        _ffn1_kernel, t // tm, (x, g1, wg, wu, wd),
        [tok] + [_const_spec(a.shape) for a in (g1, wg, wu, wd)],
        [jax.ShapeDtypeStruct((t, D_MODEL), F32)], [tok], jobs, "ffn1")
    return x1, per_job


def _ffn1_stream_kernel(x_ref, g1_ref, wg_ref, wu_ref, wd_ref, x1_ref, wg_bf_ref, wu_bf_ref, wd_bf_ref, xn_ref, acc_ref):
    c = pl.program_id(0)

    @pl.when(c == 0)
    def _():
        xn_ref[...] = _rms(x_ref[...], g1_ref[...]).astype(BF16)
        acc_ref[...] = jnp.zeros_like(acc_ref)

    wg, wu, wd = wg_ref[...].astype(BF16), wu_ref[...].astype(BF16), wd_ref[...].astype(BF16)
    wg_bf_ref[...], wu_bf_ref[...], wd_bf_ref[...] = wg, wu, wd
    xn = xn_ref[...]
    gate = _dot(xn, wg)
    act = (gate * jax.nn.sigmoid(gate) * _dot(xn, wu)).astype(BF16)
    acc_ref[...] += _dot(act, wd)

    @pl.when(c == pl.num_programs(0) - 1)
    def _():
        x1_ref[...] = x_ref[...] + 0.5 * acc_ref[...]


def _ffn1_stream(x, g1, wg, wu, wd):
    t, d_ff = x.shape[0], wg.shape[1]
    ck = FFN_STREAM_CHUNK
    assert d_ff % ck == 0
    whole = lambda a: pl.BlockSpec(a.shape, lambda c: (0,) * a.ndim, pipeline_mode=pl.Buffered(1))
    cols = pl.BlockSpec((D_MODEL, ck), lambda c: (0, c))
    rows = pl.BlockSpec((ck, D_MODEL), lambda c: (c, 0))
    return pl.pallas_call(
        _ffn1_stream_kernel,
        grid=(d_ff // ck,),
        in_specs=[whole(x), whole(g1), cols, cols, rows],
        out_specs=[pl.BlockSpec(x.shape, lambda c: (0, 0)), cols, cols, rows],
        out_shape=[jax.ShapeDtypeStruct(x.shape, F32), jax.ShapeDtypeStruct(wg.shape, BF16),
                   jax.ShapeDtypeStruct(wu.shape, BF16), jax.ShapeDtypeStruct(wd.shape, BF16)],
        scratch_shapes=[pltpu.VMEM(x.shape, BF16), pltpu.VMEM(x.shape, F32)],
        compiler_params=_params(1),
        name="ffn1_stream",
    )(x, g1, wg, wu, wd)


def _proj_kernel(x1_ref, gm_ref, win_ref, u_ref, qkv_ref):
    z = _dot(_rms(x1_ref[...], gm_ref[...]).astype(BF16), win_ref[...])
    u_ref[...] = z[:, :D_SSM]
    qkv_ref[...] = z[:, D_SSM:]


def _proj(x1, gm, win, tm, jobs=()):
    t = x1.shape[0]
    tok = lambda w: pl.BlockSpec((tm, w), lambda i: (i, 0))
    (u, qkv), per_job = _hosted_call(
        _proj_kernel, t // tm, (x1, gm, win),
        [tok(D_MODEL), _const_spec(gm.shape), _const_spec(win.shape)],
        [jax.ShapeDtypeStruct((t, D_SSM), F32), jax.ShapeDtypeStruct((t, D_QKV), F32)], [tok(D_SSM), tok(D_QKV)],
        jobs, "proj")
    return u, qkv, per_job


def _swap_halves(x):
    half = x.shape[1] // 2
    return jnp.concatenate([x[:, half:], x[:, :half]], axis=1)


def _cmul_split(x, a1, a2):
    return x * a1 + _swap_halves(x) * a2


def _ssm_kernel(*refs, n_t, n_rows, has_h0):
    if has_h0:
        u_ref, h0_ref, kp_ref, wp_ref, vpt_ref, a1_ref, a2_ref, y_ref, hl_ref = refs
    else:
        u_ref, kp_ref, wp_ref, vpt_ref, a1_ref, a2_ref, y_ref, hl_ref = refs
    n_pairs = n_t // 2
    tok = lambda t: pl.ds(t, n_rows, stride=n_t)
    up = [jnp.concatenate([u_ref[tok(2 * a), :], u_ref[tok(2 * a + 1), :]], axis=1).astype(BF16)
          for a in range(n_pairs)]
    s = _dot(jnp.concatenate(up, axis=1), wp_ref[0])
    in_chunk = [None] * n_pairs
    todo = [(b, a) for b in range(n_pairs) for a in range(b + 1)]

    def in_chunk_products(count):
        for b, a in [todo.pop(0) for _ in range(min(count, len(todo)))]:
            prod = _dot(up[a], kp_ref[0, b - a])
            in_chunk[b] = prod if in_chunk[b] is None else in_chunk[b] + prod

    if has_h0:
        hprev = h0_ref[...]
        h_last = _cmul_split(hprev, a1_ref[0], a2_ref[0]) + s
        hl_ref[...] = h_last
    else:
        blk = SCAN_BLOCK
        assert n_rows % blk == 0 and a1_ref.shape[1] == blk
        row = lax.broadcasted_iota(jnp.int32, (blk, s.shape[1]), 0)
        carry = jnp.zeros((1, s.shape[1]), F32)
        entering = []
        for i in range(n_rows // blk):
            h = s[i * blk:(i + 1) * blk]
            for lvl in range(int(math.log2(blk))):
                sh = 1 << lvl
                shifted = jnp.where(row >= sh, pltpu.roll(h, sh, axis=0), 0.0)
                h = h + _cmul_split(shifted, a1_ref[0, sh - 1:sh, :], a2_ref[0, sh - 1:sh, :])
            carry_b = jnp.broadcast_to(carry, h.shape)
            h = h + _cmul_split(carry_b, a1_ref[0], a2_ref[0])
            entering.append(jnp.where(row == 0, carry_b, pltpu.roll(h, 1, axis=0)))
            carry = h[blk - 1:blk]
            in_chunk_products(-(-len(todo) // (n_rows // blk - i)))
        hprev = jnp.concatenate(entering, axis=0)
        hl_ref[0, 0] = carry
    in_chunk_products(len(todo))
    hpb = hprev.astype(BF16)
    for b in range(n_pairs):
        acc = _dot_nt(hpb, vpt_ref[0, b]) + in_chunk[b]
        y_ref[tok(2 * b), :] = acc[:, :LANE]
        y_ref[tok(2 * b + 1), :] = acc[:, LANE:]


def _ssm_table_specs(tabs, idx, n_t):
    kp, wp, vp, a1, a2 = tabs
    n_pairs, w_rows = n_t // 2, n_t * LANE
    return [pl.BlockSpec((1, n_pairs) + kp.shape[2:], lambda *g: (idx(*g), 0, 0, 0)),
            pl.BlockSpec((1, w_rows, wp.shape[2]), lambda *g: (idx(*g), wp.shape[1] // w_rows - 1, 0)),
            pl.BlockSpec((1, n_pairs) + vp.shape[2:], lambda *g: (idx(*g), 0, 0, 0)),
            pl.BlockSpec((1,) + a1.shape[1:], lambda *g: (idx(*g), 0, 0)),
            pl.BlockSpec((1,) + a2.shape[1:], lambda *g: (idx(*g), 0, 0))]


def _ssm_prompt(u, tabs, n, seq):
    n_rows = seq // CHUNK
    tile = pl.BlockSpec((seq, LANE), lambda j, i: (i, j))
    return pl.pallas_call(
        functools.partial(_ssm_kernel, n_t=CHUNK, n_rows=n_rows, has_h0=False),
        grid=(N_SSM_TILES, n),
        in_specs=[tile] + _ssm_table_specs(tabs, lambda j, i: j, CHUNK),
        out_specs=[tile, pl.BlockSpec((1, 1, 1, TILE_STATE_W), lambda j, i: (i, j, 0, 0))],
        out_shape=[jax.ShapeDtypeStruct((n * seq, D_SSM), F32),
                   jax.ShapeDtypeStruct((n, N_SSM_TILES, 1, TILE_STATE_W), F32)],
        compiler_params=_params(2),
        name="ssm_prompt",
    )(u, *tabs)


def _ssm_sample(u, h0, tabs, n_seq, n_new):
    tile = pl.BlockSpec((n_seq * n_new, LANE), lambda j: (0, j))
    state = pl.BlockSpec((n_seq, TILE_STATE_W), lambda j: (0, j))
    return pl.pallas_call(
        functools.partial(_ssm_kernel, n_t=n_new, n_rows=n_seq, has_h0=True),
        grid=(N_SSM_TILES,),
        in_specs=[tile, state] + _ssm_table_specs(tabs, lambda j: j, n_new),
        out_specs=[tile, state],
        out_shape=[jax.ShapeDtypeStruct((n_seq * n_new, D_SSM), F32),
                   jax.ShapeDtypeStruct((n_seq, N_SSM_TILES * TILE_STATE_W), F32)],
        compiler_params=_params(1),
        name="ssm_sample",
    )(u, h0, *tabs)


def _ssm_prep_kernel(are_ref, aim_ref, ldt_ref, bre_ref, bim_ref, cre_ref, cim_ref, d_ref,
                     kp_ref, wp_ref, vpt_ref, a1_ref, a2_ref, a1s_ref, a2s_ref, *, n_levels, n_new):
    a_re, a_im = are_ref[0], aim_ref[0]
    dt = jnp.exp(ldt_ref[0])
    mag = jnp.exp(a_re * dt)
    ab_re, ab_im = mag * jnp.cos(a_im * dt), mag * jnp.sin(a_im * dt)
    den = a_re * a_re + a_im * a_im
    nr, ni = ab_re - 1.0, ab_im
    f_re = (nr * a_re + ni * a_im) / den
    f_im = (ni * a_re - nr * a_im) / den
    shape = bre_ref.shape[1:]
    same_group = (lax.broadcasted_iota(jnp.int32, shape, 0) // SSM_GROUP
                  == lax.broadcasted_iota(jnp.int32, shape, 1) // SSM_STATE)
    b_re, b_im = bre_ref[0], bim_ref[0]
    x_re = jnp.where(same_group, f_re * b_re - f_im * b_im, 0.0)
    x_im = jnp.where(same_group, f_re * b_im + f_im * b_re, 0.0)
    c_re = jnp.where(same_group, cre_ref[0], 0.0)
    c_im = jnp.where(same_group, cim_ref[0], 0.0)
    pw = [(jnp.ones_like(ab_re), jnp.zeros_like(ab_im))]
    for _ in range(CHUNK):
        r, i = pw[-1]
        pw.append((r * ab_re - i * ab_im, r * ab_im + i * ab_re))
    cat = lambda r, i: jnp.concatenate([r, i], axis=1)
    e_pack = [cat(c_re * r - c_im * i, -(c_re * i + c_im * r)) for r, i in pw]
    split = lambda a: (a.astype(BF16), (a - a.astype(BF16).astype(F32)).astype(BF16))
    x_hi, x_lo = split(cat(x_re, x_im))
    x_3 = jnp.concatenate([x_hi, x_hi, x_lo], axis=1)
    k_lag = []
    for k in range(CHUNK):
        e_hi, e_lo = split(e_pack[k])
        k_lag.append(_dot_nt(x_3, jnp.concatenate([e_hi, e_lo, e_hi], axis=1)))
    diag = (lax.broadcasted_iota(jnp.int32, (LANE, LANE), 0) == lax.broadcasted_iota(jnp.int32, (LANE, LANE), 1))
    k_lag[0] = k_lag[0] + jnp.where(diag, d_ref[0], 0.0)
    zero = jnp.zeros((LANE, LANE), F32)
    for dl in range(N_PAIRS):
        top = cat(k_lag[2 * dl], k_lag[2 * dl + 1])
        bot = cat(k_lag[2 * dl - 1] if dl > 0 else zero, k_lag[2 * dl])
        kp_ref[0, dl] = jnp.concatenate([top, bot], axis=0).astype(BF16)
    for t in range(CHUNK):
        r, i = pw[CHUNK - 1 - t]
        wp_ref[0, t * LANE:(t + 1) * LANE, :] = cat(x_re * r - x_im * i, x_re * i + x_im * r).astype(BF16)
    for t in range(CHUNK):
        vpt_ref[0, t // 2, (t % 2) * LANE:(t % 2 + 1) * LANE, :] = e_pack[t + 1].astype(BF16)
    r, i = step_r, step_i = pw[CHUNK]
    for s in range(n_levels):
        a1_ref[0, s:s + 1, :], a2_ref[0, s:s + 1, :] = cat(r, r), cat(-i, i)
        r, i = r * step_r - i * step_i, r * step_i + i * step_r
    r, i = pw[n_new]
    a1s_ref[0], a2s_ref[0] = cat(r, r), cat(-i, i)


def _ssm_tables(a_re, a_im, log_dt, b_re, b_im, c_re, c_im, d_skip, n_levels, n_new):
    nj, tg = N_SSM_TILES, SSM_TILE_GROUPS
    half = tg * SSM_STATE
    rowv = lambda v: v.reshape(nj, 1, half)
    tiled = lambda m: jnp.tile(m.reshape(nj, LANE, SSM_STATE), (1, 1, tg))
    ins = (rowv(a_re), rowv(a_im), rowv(jnp.repeat(log_dt, SSM_STATE)),
           tiled(b_re.transpose(0, 2, 1)), tiled(b_im.transpose(0, 2, 1)), tiled(c_re), tiled(c_im),
           d_skip.reshape(nj, 1, LANE))
    shapes = ((nj, N_PAIRS, PAIR_W, PAIR_W), (nj, CHUNK * LANE, TILE_STATE_W), (nj, N_PAIRS, PAIR_W, TILE_STATE_W),
              (nj, n_levels, TILE_STATE_W), (nj, n_levels, TILE_STATE_W), (nj, 1, TILE_STATE_W), (nj, 1, TILE_STATE_W))
    dtypes = (BF16, BF16, BF16, F32, F32, F32, F32)
    per_tile = lambda s: pl.BlockSpec((1,) + tuple(s[1:]), lambda j: (j,) + (0,) * (len(s) - 1))
    kp, wp, vpt, a1, a2, a1s, a2s = pl.pallas_call(
        functools.partial(_ssm_prep_kernel, n_levels=n_levels, n_new=n_new),
        grid=(nj,),
        in_specs=[per_tile(x.shape) for x in ins],
        out_specs=[per_tile(s) for s in shapes],
        out_shape=[jax.ShapeDtypeStruct(s, dt) for s, dt in zip(shapes, dtypes)],
        compiler_params=_params(1),
        name="ssm_prep",
    )(*ins)
    return (kp, wp, vpt, a1, a2), (kp, wp, vpt, a1s, a2s)


def _head_of_lane(width):
    return lax.broadcasted_iota(jnp.int32, (1, width), 1) // HEAD_DIM


def _stack_heads(q):
    head = _head_of_lane(q.shape[1])
    q = q.astype(F32)
    return jnp.concatenate([jnp.where(head == h, q, 0.0) for h in range(HEADS)], axis=0).astype(BF16)


def _unstack_heads(x, rows):
    head = _head_of_lane(x.shape[1])
    out = x[(HEADS - 1) * rows:]
    for h in range(HEADS - 2, -1, -1):
        out = jnp.where(head == h, x[h * rows:(h + 1) * rows], out)
    return out


def _attn_prompt_kernel(bias_ref, q0, q1, k0, k1, v0, v1, kt_ref, vt_ref, o0, o1, l0, l1, tail_ref, kprev, vprev, *, d, m):
    qb = Q_BLOCK
    b = pl.program_id(1)
    n_par = min(d, ATTN_PAR)
    rd, wr = b % 2, (b + 1) % 2
    head = _head_of_lane(D_HEADS)

    @pl.when(b == 0)
    def _():
        kprev[0] = jnp.zeros(kprev.shape[1:], kprev.dtype)
        vprev[0] = jnp.zeros(vprev.shape[1:], vprev.dtype)

    def block(r, j, k_prev, v_prev):
        start = r + j * qb * d
        rows = pl.ds(start, qb, stride=d) if d > 1 else pl.ds(pl.multiple_of(start, qb), qb)
        both = lambda lo, hi: jnp.concatenate([lo[rows, :], hi[rows, :]], axis=1)
        qs = _stack_heads(both(q0, q1) * ATTN_SCALE)
        k_own, v_own = both(k0, k1).astype(BF16), both(v0, v1).astype(BF16)
        k = jnp.concatenate([k_prev, k_own], axis=0)
        v = jnp.concatenate([v_prev, v_own], axis=0)
        s = _dot_nt(qs, k) + bias_ref[jnp.minimum(b * m + j, 1)]
        mx = jnp.max(s, axis=-1, keepdims=True)
        pe = jnp.exp(s - mx)
        den = jnp.sum(pe, axis=-1, keepdims=True)
        pn = (pe * (1.0 / den)).astype(BF16)
        o = _dot(pn[:qb], jnp.where(head == 0, v, jnp.zeros_like(v)))
        for h in range(1, HEADS):
            o = o + _dot(pn[h * qb:(h + 1) * qb], jnp.where(head == h, v, jnp.zeros_like(v)))
        lse = _unstack_heads(jnp.broadcast_to(mx + jnp.log(den), (HEADS * qb, D_HEADS)), qb)
        o0[rows, :], o1[rows, :] = o[:, :LANE], o[:, LANE:]
        l0[rows, :], l1[rows, :] = lse[:, :LANE], lse[:, LANE:]
        return k_own, v_own

    def class_group(g, carry):
        classes = [g * n_par + t for t in range(n_par)]

        def step(j, prev):
            return tuple(x for t, r in enumerate(classes) for x in block(r, j, prev[2 * t], prev[2 * t + 1]))

        init = tuple(x for r in classes for x in (kprev[rd, r], vprev[rd, r]))
        last = lax.fori_loop(0, m, step, init, unroll=max(1, min(m, ATTN_PAR // n_par)))
        for t, r in enumerate(classes):
            kprev[wr, r], vprev[wr, r] = last[2 * t], last[2 * t + 1]
        return carry

    lax.fori_loop(0, d // n_par, class_group, 0)
    tail_ref[0, 0] = kt_ref[...].T
    tail_ref[0, 1] = vt_ref[...].T


def _attn_bias():
    qi = np.arange(HEADS * Q_BLOCK)[:, None] % Q_BLOCK
    ki = np.arange(2 * Q_BLOCK)[None, :]
    diff = Q_BLOCK + qi - ki
    valid = (diff >= 0) & (diff <= N_STEPS)
    return jnp.asarray(np.where(np.stack([valid & (ki >= Q_BLOCK), valid]), 0.0, -np.inf), F32)


def _attn_prompt(qkv, n, seq, grp):
    d = DILATIONS[grp]
    tb = min(ATTN_TB, seq)
    assert tb % (Q_BLOCK * d) == 0 and seq % tb == 0
    m, nb = tb // (Q_BLOCK * d), seq // tb
    cur = lambda c: pl.BlockSpec((tb, LANE), lambda i, b: (i * nb + b, c))
    qc, kc, vc = (2 * _qkv_col(which, grp) for which in range(3))
    bias = _attn_bias()
    keep = min(WINDOWS[grp], seq)
    tt = min(keep, 512)
    n_tiles, first = keep // tt, (seq - keep) // tt
    assert keep % tt == 0 and (seq - keep) % tt == 0 and nb >= n_tiles
    tile = lambda b: jnp.minimum(b, n_tiles - 1)
    tail_rows = lambda which: pl.BlockSpec(
        (tt, D_HEADS), lambda i, b: (i * (seq // tt) + first + tile(b), _qkv_col(which, grp)))
    *attn, tail = pl.pallas_call(
        functools.partial(_attn_prompt_kernel, d=d, m=m),
        grid=(n, nb),
        in_specs=[_const_spec(bias.shape), cur(qc), cur(qc + 1), cur(kc), cur(kc + 1), cur(vc), cur(vc + 1),
                  tail_rows(1), tail_rows(2)],
        out_specs=[pl.BlockSpec((tb, LANE), lambda i, b: (i * nb + b, 0))] * 4
        + [pl.BlockSpec((1, 2, D_HEADS, tt), lambda i, b: (i, 0, 0, tile(b)))],
        out_shape=[jax.ShapeDtypeStruct((n * seq, LANE), F32)] * 4
        + [jax.ShapeDtypeStruct((n, 2, D_HEADS, keep), F32)],
        scratch_shapes=[pltpu.VMEM((2, d, Q_BLOCK, D_HEADS), BF16)] * 2,
        compiler_params=_params(2),
        name=f"attn_prompt_d{d}",
    )(bias, *([qkv] * 8))
    return attn, tail.reshape(1, n, 2, HEADS, HEAD_DIM, keep).transpose(0, 1, 5, 2, 3, 4)


def _attn_sample_kernel(q_ref, kn_ref, vn_ref, c_ref, o_ref, lse_ref, *, d, n_new, seq_blk):
    w = c_ref.shape[-1]
    n_rows = HEADS * n_new
    l_b = lax.broadcasted_iota(jnp.int32, (n_rows, w), 0) % n_new
    pos = lax.broadcasted_iota(jnp.int32, (n_rows, w), 1)
    valid_buf = (((w + l_b - pos) & (d - 1)) == 0) & (pos >= l_b)
    dn = (lax.broadcasted_iota(jnp.int32, (n_rows, LANE), 0) % n_new) - lax.broadcasted_iota(jnp.int32, (n_rows, LANE), 1)
    valid_new = (dn >= 0) & ((dn & (d - 1)) == 0)
    pad = jnp.zeros((LANE - n_new, D_HEADS), F32)

    def one_seq(s, carry):
        qs = _stack_heads(q_ref[s])
        k_t, v_t = c_ref[s, 0].astype(BF16), c_ref[s, 1].astype(BF16)
        k_n = jnp.concatenate([kn_ref[s], pad], axis=0).astype(BF16)
        v_n = jnp.concatenate([vn_ref[s], pad], axis=0).astype(BF16)
        s_b = jnp.where(valid_buf, _dot(qs, k_t) * ATTN_SCALE, -jnp.inf)
        s_n = jnp.where(valid_new, _dot_nt(qs, k_n) * ATTN_SCALE, -jnp.inf)
        mx = jnp.maximum(jnp.max(s_b, axis=-1, keepdims=True), jnp.max(s_n, axis=-1, keepdims=True))
        p_b, p_n = jnp.exp(s_b - mx), jnp.exp(s_n - mx)
        den = jnp.sum(p_b, axis=-1, keepdims=True) + jnp.sum(p_n, axis=-1, keepdims=True)
        o = (_dot_nt(p_b.astype(BF16), v_t) + _dot(p_n.astype(BF16), v_n)) / den
        o_ref[s] = _unstack_heads(o, n_new)
        lse_ref[s] = _unstack_heads(jnp.broadcast_to(mx + jnp.log(den), o.shape), n_new)
        return carry

    lax.fori_loop(0, seq_blk, one_seq, 0, unroll=True)


def _attn_sample_job(qkv, cache, n_seq, n_new, grp, n_steps):
    d = DILATIONS[grp]
    w = cache.shape[1]
    assert w == N_STEPS * d and n_new <= LANE and n_seq % n_steps == 0
    seq_blk = n_seq // n_steps
    qkv3 = qkv.reshape(n_seq, n_new, D_QKV)
    c_t = cache.transpose(0, 2, 3, 4, 1).reshape(n_seq, 2, D_HEADS, w)
    blk = (seq_blk, n_new, D_HEADS)
    col = lambda which: pl.BlockSpec(blk, lambda i: (i, 0, _qkv_col(which, grp)))
    out_spec = pl.BlockSpec(blk, lambda i: (i, 0, 0))
    out_shape = jax.ShapeDtypeStruct((n_seq, n_new, D_HEADS), F32)
    return _SideJob((qkv3, qkv3, qkv3, c_t),
                    (col(0), col(1), col(2), pl.BlockSpec((seq_blk, 2, D_HEADS, w), lambda i: (i, 0, 0, 0))),
                    (out_shape, out_shape), (out_spec, out_spec),
                    functools.partial(_attn_sample_kernel, d=d, n_new=n_new, seq_blk=seq_blk))


def _halves(o, lse):
    o, lse = o.reshape(-1, D_HEADS), lse.reshape(-1, D_HEADS)
    return o[:, :LANE], o[:, LANE:], lse[:, :LANE], lse[:, LANE:]


def _kv_new_job(qkv, n_seq, n_new, n_steps):
    halves, l_parts = D_HEADS // LANE, 2
    n_units, l_blk = N_GROUPS * 2 * halves * l_parts, n_new // l_parts
    assert n_steps >= n_units and n_new % l_parts == 0
    unit = lambda i: jnp.minimum(i, n_units - 1)
    l_part = lambda i: unit(i) % l_parts
    half = lambda i: unit(i) // l_parts % halves
    kv = lambda i: unit(i) // (l_parts * halves) % 2
    grp = lambda i: unit(i) // (l_parts * halves * 2)
    in_spec = pl.BlockSpec((n_seq * n_new, LANE), lambda i: (0, halves * ((1 + kv(i)) * N_GROUPS + grp(i)) + half(i)))
    out_spec = pl.BlockSpec((1, l_blk, 1, LANE, n_seq), lambda i: (grp(i), l_part(i), kv(i), half(i), 0))
    out_shape = jax.ShapeDtypeStruct((N_GROUPS, n_new, 2, D_HEADS, n_seq), F32)

    def body(x_ref, o_ref):
        l0 = l_part(pl.program_id(0)) * l_blk
        for dl in range(l_blk):
            o_ref[0, dl, 0] = x_ref[pl.ds(l0 + dl, n_seq, stride=n_new), :].T

    return _SideJob((qkv,), (in_spec,), (out_shape,), (out_spec,), body)


def _kv_new_outputs(out, n_seq, n_new):
    out = out.reshape(N_GROUPS, 1, n_new, 2, HEADS, HEAD_DIM, n_seq).transpose(0, 1, 6, 2, 3, 4, 5)
    return [out[g] for g in range(N_GROUPS)]


def _gelu_tanh(x):
    return 0.5 * x * (1.0 + jnp.tanh(math.sqrt(2.0 / math.pi) * (x + 0.044715 * (x * x * x))))


def _stage3_kernel(*refs):
    x1_ref, yp_ref = refs[:2]
    attn = refs[2:2 + 4 * N_GROUPS]
    (p_ref, gm_ref, wgate_ref, wglu_ref, wbs_ref, wba_ref, wout_ref, g2_ref, wg_ref, wu_ref, wd_ref,
     gp_ref, wpg_ref, wpp_ref, gf_ref, y_ref) = refs[2 + 4 * N_GROUPS:]
    x1 = x1_ref[...]
    gates = jax.nn.sigmoid(_dot(_rms(x1, gm_ref[...]).astype(BF16), wgate_ref[...]))
    y = _gelu_tanh(yp_ref[...])
    glu = y * jax.nn.sigmoid(_dot(y.astype(BF16), wglu_ref[...]))
    full = lambda lo, hi: jnp.concatenate([lo[...], hi[...]], axis=1)
    outs = [full(attn[4 * g], attn[4 * g + 1]) for g in range(N_GROUPS)]
    lses = [full(attn[4 * g + 2], attn[4 * g + 3]) for g in range(N_GROUPS)]
    mx = jnp.maximum(jnp.maximum(lses[0], lses[1]), lses[2])
    es = [jnp.exp(l - mx) for l in lses]
    y_attn = (es[0] * outs[0] + es[1] * outs[1] + es[2] * outs[2]) / (es[0] + es[1] + es[2])
    merged = (gates[:, :D_MODEL] * _dot(glu.astype(BF16), wbs_ref[...])
              + gates[:, D_MODEL:] * _dot(y_attn.astype(BF16), wba_ref[...]))
    x = x1 + _dot(merged.astype(BF16), wout_ref[...])
    x = x + 0.5 * _swiglu(_rms(x, g2_ref[...]).astype(BF16), wg_ref, wu_ref, wd_ref)
    gate = jax.nn.sigmoid(_dot(_rms(x, gp_ref[...]).astype(BF16), wpg_ref[...]))
    x = x + gate * _dot(p_ref[...].astype(BF16), wpp_ref[...])
    y_ref[...] = _rms(x, gf_ref[...])


def _stage3(tok_inputs, weights):
    t, tm = tok_inputs[0].shape[0], STAGE_TM
    tok = lambda a: pl.BlockSpec((tm, a.shape[1]), lambda i: (i, 0))
    return pl.pallas_call(
        _stage3_kernel,
        grid=(t // tm,),
        in_specs=[tok(a) for a in tok_inputs] + [_const_spec(w.shape) for w in weights],
        out_specs=pl.BlockSpec((tm, D_MODEL), lambda i: (i, 0)),
        out_shape=jax.ShapeDtypeStruct((t, D_MODEL), F32),
        compiler_params=_params(1),
        name="stage3",
    )(*tok_inputs, *weights)


def kernel(x_prompt, x_sample, p_prompt, p_sample, cache_kv_w128, cache_kv_w512, cache_kv_w2048, state_ssm,
           g_ffn1, ffn1_w_gate, ffn1_w_up, ffn1_w_down, g_mix, w_in, ssm_a_re, ssm_a_im, ssm_log_dt,
           ssm_b_re, ssm_b_im, ssm_c_re, ssm_c_im, ssm_d, ssm_w_glu, w_br_ssm, w_br_attn, w_out,
           g_ffn2, ffn2_w_gate, ffn2_w_up, ffn2_w_down, g_ple, w_ple_gate, w_ple_proj, g_final):
    assert x_prompt.shape[-1] == D_MODEL and g_ffn1.shape[0] == 1
    n_p, seq, _ = x_prompt.shape
    n_s, n_new, _ = x_sample.shape
    caches = (cache_kv_w128, cache_kv_w512, cache_kv_w2048)
    row = lambda g: g.reshape(1, -1)

    wi = w_in[0]
    win = wi[:, :D_SSM + D_QKV].astype(BF16)
    w_gates = wi[:, D_SSM + D_QKV:].astype(BF16)
    s3_f32 = [w[0] for w in (ssm_w_glu, w_br_ssm, w_br_attn, w_out, ffn2_w_gate, ffn2_w_up, ffn2_w_down,
                             w_ple_gate, w_ple_proj)]

    tabs_p, tabs_s = _ssm_tables(
        ssm_a_re[0], ssm_a_im[0], ssm_log_dt[0], ssm_b_re[0], ssm_b_im[0], ssm_c_re[0], ssm_c_im[0],
        ssm_d[0], SCAN_BLOCK, n_new)
    tile_state = (-1, N_SSM_TILES, 2, SSM_TILE_GROUPS, SSM_STATE)
    from_tiles = lambda h: h.reshape(tile_state).transpose(0, 1, 3, 4, 2).reshape(1, -1, N_SSM_GROUPS, SSM_STATE, 2)

    xs = x_sample.reshape(n_s * n_new, D_MODEL)
    x1_s, *ffn1_bf = _ffn1_stream(xs, row(g_ffn1[0]), ffn1_w_gate[0], ffn1_w_up[0], ffn1_w_down[0])
    ffn1_w = (row(g_ffn1[0]), *ffn1_bf)
    u_s, qkv_s, _ = _proj(x1_s, row(g_mix[0]), win, tm=min(STAGE_TM, xs.shape[0]))

    xp = x_prompt.reshape(n_p * seq, D_MODEL)
    attn_job = lambda grp, n_steps: _attn_sample_job(qkv_s, caches[grp][0], n_s, n_new, grp, n_steps)
    steps_a, steps_b = xp.shape[0] // FFN1_HOST_TM, xp.shape[0] // STAGE_TM
    x1, side = _ffn1(xp, *ffn1_w, tm=FFN1_HOST_TM,
                     jobs=[attn_job(2, steps_a), attn_job(0, steps_a)] + [_cast_job(w, steps_a) for w in s3_f32])
    attn_s = {2: side[0], 0: side[1]}
    wglu, wbs, wba, wout, wg2, wu2, wd2, wpg, wpp = (c[0] for c in side[2:])
    s3_w = (row(g_mix[0]), w_gates, wglu, wbs, wba, wout, row(g_ffn2[0]), wg2, wu2, wd2, row(g_ple[0]), wpg, wpp,
            row(g_final))
    u, qkv, side = _proj(x1, row(g_mix[0]), win, tm=STAGE_TM,
                         jobs=[attn_job(1, steps_b), _kv_new_job(qkv_s, n_s, n_new, steps_b)])
    attn_s[1] = side[0]
    kv_sample = _kv_new_outputs(side[1][0], n_s, n_new)
    y_pre, h_last = _ssm_prompt(u, tabs_p, n_p, seq)
    attn_p = [_attn_prompt(qkv, n_p, seq, grp) for grp in range(N_GROUPS)]
    attn, kv_prompt = [a for per_group, _ in attn_p for a in per_group], [tail for _, tail in attn_p]
    tok = (x1, y_pre, *attn, p_prompt[0].reshape(n_p * seq, D_PLE))
    y_prompt = _stage3(tok, s3_w).reshape(n_p, seq, D_MODEL)
    ssm_prompt = from_tiles(h_last)

    h0 = state_ssm[0].reshape(n_s, N_SSM_TILES, SSM_TILE_GROUPS, SSM_STATE, 2)
    h0 = h0.transpose(0, 1, 4, 2, 3).reshape(n_s, N_SSM_TILES * TILE_STATE_W)
    y_pre, h_new = _ssm_sample(u_s, h0, tabs_s, n_s, n_new)
    attn = [a for grp in range(N_GROUPS) for a in _halves(*attn_s[grp])]
    tok = (x1_s, y_pre, *attn, p_sample[0].reshape(n_s * n_new, D_PLE))
    y_sample = _stage3(tok, s3_w).reshape(n_s, n_new, D_MODEL)
    ssm_sample = from_tiles(h_new)

    return (y_prompt, y_sample, kv_prompt[0], kv_prompt[1], kv_prompt[2], ssm_prompt,
            kv_sample[0], kv_sample[1], kv_sample[2], ssm_sample)
```

```python
import functools
import math
from typing import Callable, NamedTuple

import jax
import jax.numpy as jnp
import numpy as np
from jax import lax
from jax.experimental import pallas as pl
from jax.experimental.pallas import tpu as pltpu

D_MODEL = 1024
D_PLE = 256
D_FF = 2816
SSM_GROUP = 16
SSM_STATE = 64
D_SSM = 512
N_SSM_GROUPS = D_SSM // SSM_GROUP
HEAD_DIM = 64
HEADS = 4
WINDOWS = (128, 512, 2048)
DILATIONS = (1, 4, 16)
N_GROUPS = 3
D_HEADS = HEADS * HEAD_DIM
N_STEPS = 128
ATTN_SCALE = HEAD_DIM ** -0.5
EPS = 1e-6

LANE = 128
V7X_VMEM_LIMIT_BYTES = 56 * 1024 * 1024
STAGE_TM = 512
FFN_STREAM_CHUNK = 256
FFN1_HOST_TM = 256

CHUNK = 16
SCAN_BLOCK = 8
N_PAIRS = CHUNK // 2
PAIR_W = 2 * LANE
SSM_TILE_GROUPS = LANE // SSM_GROUP
N_SSM_TILES = N_SSM_GROUPS // SSM_TILE_GROUPS
TILE_STATE_W = SSM_TILE_GROUPS * 2 * SSM_STATE
Q_BLOCK = 128
ATTN_TB = 2048
ATTN_PAR = 16

D_QKV = 3 * N_GROUPS * D_HEADS


def _qkv_col(which, grp):
    return which * N_GROUPS + grp


BF16 = jnp.bfloat16
F32 = jnp.float32


def _dot(a, b):
    return jnp.dot(a, b, preferred_element_type=F32)


def _dot_nt(a, b):
    return lax.dot_general(a, b, (((1,), (1,)), ((), ())), preferred_element_type=F32)


def _rms(x, g):
    return x * lax.rsqrt(jnp.mean(x * x, axis=-1, keepdims=True) + EPS) * g


def _swiglu(xn, wg_ref, wu_ref, wd_ref):
    gate = _dot(xn, wg_ref[...])
    up = _dot(xn, wu_ref[...])
    act = (gate * jax.nn.sigmoid(gate) * up).astype(BF16)
    return _dot(act, wd_ref[...])


def _const_spec(shape):
    nd = len(shape)
    return pl.BlockSpec(shape, lambda *_: (0,) * nd, pipeline_mode=pl.Buffered(1))


def _params(n_grid_dims):
    return pltpu.CompilerParams(dimension_semantics=("arbitrary",) * n_grid_dims,
                                vmem_limit_bytes=V7X_VMEM_LIMIT_BYTES)


class _SideJob(NamedTuple):
    arrays: tuple
    in_specs: tuple
    out_shapes: tuple
    out_specs: tuple
    body: Callable


def _cast_job(w, n_steps):
    rows, cols = w.shape
    steps = max(s for s in range(1, n_steps + 1) if rows % s == 0 and (rows // s) % 16 == 0)
    spec = pl.BlockSpec((rows // steps, cols), lambda i: (jnp.minimum(i, steps - 1), 0))

    def body(src, dst):
        dst[...] = src[...].astype(BF16)

    return _SideJob((w,), (spec,), (jax.ShapeDtypeStruct(w.shape, BF16),), (spec,), body)


def _hosted_call(main_body, n_steps, arrays, in_specs, out_shapes, out_specs, jobs, name):
    n_in, n_out = len(arrays), len(out_shapes)

    def body(*refs):
        pos, job_in, job_out = n_in, [], []
        for job in jobs:
            job_in.append(refs[pos:pos + len(job.arrays)])
            pos += len(job.arrays)
        main_out = refs[pos:pos + n_out]
        pos += n_out
        for job in jobs:
            job_out.append(refs[pos:pos + len(job.out_shapes)])
            pos += len(job.out_shapes)
        for job, ins, outs in zip(jobs, job_in, job_out):
            job.body(*ins, *outs)
        main_body(*refs[:n_in], *main_out)

    outs = pl.pallas_call(
        body,
        grid=(n_steps,),
        in_specs=list(in_specs) + [sp for job in jobs for sp in job.in_specs],
        out_specs=list(out_specs) + [sp for job in jobs for sp in job.out_specs],
        out_shape=list(out_shapes) + [sh for job in jobs for sh in job.out_shapes],
        compiler_params=_params(1),
        name=name,
    )(*arrays, *[a for job in jobs for a in job.arrays])
    main, pos, per_job = outs[:n_out], n_out, []
    for job in jobs:
        per_job.append(outs[pos:pos + len(job.out_shapes)])
        pos += len(job.out_shapes)
    return main, per_job


def _ffn1_kernel(x_ref, g1_ref, wg_ref, wu_ref, wd_ref, x1_ref):
    x = x_ref[...]
    x1_ref[...] = x + 0.5 * _swiglu(_rms(x, g1_ref[...]).astype(BF16), wg_ref, wu_ref, wd_ref)


def _ffn1(x, g1, wg, wu, wd, tm, jobs=()):
    t = x.shape[0]
    tok = pl.BlockSpec((tm, D_MODEL), lambda i: (i, 0))
    (x1,), per_job = _hosted_call(
        _ffn1_kernel, t // tm, (x, g1, wg, wu, wd),
        [tok] + [_const_spec(a.shape) for a in (g1, wg, wu, wd)],
        [jax.ShapeDtypeStruct((t, D_MODEL), F32)], [tok], jobs, "ffn1")
    return x1, per_job


def _ffn1_stream_kernel(x_ref, g1_ref, wg_ref, wu_ref, wd_ref, x1_ref, wg_bf_ref, wu_bf_ref, wd_bf_ref, xn_ref, acc_ref):
    c = pl.program_id(0)

    @pl.when(c == 0)
    def _():
        xn_ref[...] = _rms(x_ref[...], g1_ref[...]).astype(BF16)
        acc_ref[...] = jnp.zeros_like(acc_ref)

    wg, wu, wd = wg_ref[...].astype(BF16), wu_ref[...].astype(BF16), wd_ref[...].astype(BF16)
    wg_bf_ref[...], wu_bf_ref[...], wd_bf_ref[...] = wg, wu, wd
    xn = xn_ref[...]
    gate = _dot(xn, wg)
    act = (gate * jax.nn.sigmoid(gate) * _dot(xn, wu)).astype(BF16)
    acc_ref[...] += _dot(act, wd)

    @pl.when(c == pl.num_programs(0) - 1)
    def _():
        x1_ref[...] = x_ref[...] + 0.5 * acc_ref[...]


def _ffn1_stream(x, g1, wg, wu, wd):
    t, d_ff = x.shape[0], wg.shape[1]
    ck = FFN_STREAM_CHUNK
    assert d_ff % ck == 0
    whole = lambda a: pl.BlockSpec(a.shape, lambda c: (0,) * a.ndim, pipeline_mode=pl.Buffered(1))
    cols = pl.BlockSpec((D_MODEL, ck), lambda c: (0, c))
    rows = pl.BlockSpec((ck, D_MODEL), lambda c: (c, 0))
    return pl.pallas_call(
        _ffn1_stream_kernel,
        grid=(d_ff // ck,),
        in_specs=[whole(x), whole(g1), cols, cols, rows],
        out_specs=[pl.BlockSpec(x.shape, lambda c: (0, 0)), cols, cols, rows],
        out_shape=[jax.ShapeDtypeStruct(x.shape, F32), jax.ShapeDtypeStruct(wg.shape, BF16),
                   jax.ShapeDtypeStruct(wu.shape, BF16), jax.ShapeDtypeStruct(wd.shape, BF16)],
        scratch_shapes=[pltpu.VMEM(x.shape, BF16), pltpu.VMEM(x.shape, F32)],
        compiler_params=_params(1),
        name="ffn1_stream",
    )(x, g1, wg, wu, wd)


def _proj_kernel(x1_ref, gm_ref, win_ref, u_ref, qkv_ref):
    z = _dot(_rms(x1_ref[...], gm_ref[...]).astype(BF16), win_ref[...])
    u_ref[...] = z[:, :D_SSM]
    qkv_ref[...] = z[:, D_SSM:]


def _proj(x1, gm, win, tm, jobs=()):
    t = x1.shape[0]
    tok = lambda w: pl.BlockSpec((tm, w), lambda i: (i, 0))
    (u, qkv), per_job = _hosted_call(
        _proj_kernel, t // tm, (x1, gm, win),
        [tok(D_MODEL), _const_spec(gm.shape), _const_spec(win.shape)],
        [jax.ShapeDtypeStruct((t, D_SSM), F32), jax.ShapeDtypeStruct((t, D_QKV), F32)], [tok(D_SSM), tok(D_QKV)],
        jobs, "proj")
    return u, qkv, per_job


def _swap_halves(x):
    half = x.shape[1] // 2
    return jnp.concatenate([x[:, half:], x[:, :half]], axis=1)


def _cmul_split(x, a1, a2):
    return x * a1 + _swap_halves(x) * a2


def _ssm_kernel(*refs, n_t, n_rows, has_h0):
    if has_h0:
        u_ref, h0_ref, kp_ref, wp_ref, vpt_ref, a1_ref, a2_ref, y_ref, hl_ref = refs
    else:
        u_ref, kp_ref, wp_ref, vpt_ref, a1_ref, a2_ref, y_ref, hl_ref = refs
    n_pairs = n_t // 2
    tok = lambda t: pl.ds(t, n_rows, stride=n_t)
    up = [jnp.concatenate([u_ref[tok(2 * a), :], u_ref[tok(2 * a + 1), :]], axis=1).astype(BF16)
          for a in range(n_pairs)]
    s = _dot(jnp.concatenate(up, axis=1), wp_ref[0])
    in_chunk = [None] * n_pairs
    todo = [(b, a) for b in range(n_pairs) for a in range(b + 1)]

    def in_chunk_products(count):
        for b, a in [todo.pop(0) for _ in range(min(count, len(todo)))]:
            prod = _dot(up[a], kp_ref[0, b - a])
            in_chunk[b] = prod if in_chunk[b] is None else in_chunk[b] + prod

    if has_h0:
        hprev = h0_ref[...]
        h_last = _cmul_split(hprev, a1_ref[0], a2_ref[0]) + s
        hl_ref[...] = h_last
    else:
        blk = SCAN_BLOCK
        assert n_rows % blk == 0 and a1_ref.shape[1] == blk
        row = lax.broadcasted_iota(jnp.int32, (blk, s.shape[1]), 0)
        carry = jnp.zeros((1, s.shape[1]), F32)
        entering = []
        for i in range(n_rows // blk):
            h = s[i * blk:(i + 1) * blk]
            for lvl in range(int(math.log2(blk))):
                sh = 1 << lvl
                shifted = jnp.where(row >= sh, pltpu.roll(h, sh, axis=0), 0.0)
                h = h + _cmul_split(shifted, a1_ref[0, sh - 1:sh, :], a2_ref[0, sh - 1:sh, :])
            carry_b = jnp.broadcast_to(carry, h.shape)
            h = h + _cmul_split(carry_b, a1_ref[0], a2_ref[0])
            entering.append(jnp.where(row == 0, carry_b, pltpu.roll(h, 1, axis=0)))
            carry = h[blk - 1:blk]
            in_chunk_products(-(-len(todo) // (n_rows // blk - i)))
        hprev = jnp.concatenate(entering, axis=0)
        hl_ref[0, 0] = carry
    in_chunk_products(len(todo))
    hpb = hprev.astype(BF16)
    for b in range(n_pairs):
        acc = _dot_nt(hpb, vpt_ref[0, b]) + in_chunk[b]
        y_ref[tok(2 * b), :] = acc[:, :LANE]
        y_ref[tok(2 * b + 1), :] = acc[:, LANE:]


def _ssm_table_specs(tabs, idx, n_t):
    kp, wp, vp, a1, a2 = tabs
    n_pairs, w_rows = n_t // 2, n_t * LANE
    return [pl.BlockSpec((1, n_pairs) + kp.shape[2:], lambda *g: (idx(*g), 0, 0, 0)),
            pl.BlockSpec((1, w_rows, wp.shape[2]), lambda *g: (idx(*g), wp.shape[1] // w_rows - 1, 0)),
            pl.BlockSpec((1, n_pairs) + vp.shape[2:], lambda *g: (idx(*g), 0, 0, 0)),
            pl.BlockSpec((1,) + a1.shape[1:], lambda *g: (idx(*g), 0, 0)),
            pl.BlockSpec((1,) + a2.shape[1:], lambda *g: (idx(*g), 0, 0))]


def _ssm_prompt(u, tabs, n, seq):
    n_rows = seq // CHUNK
    tile = pl.BlockSpec((seq, LANE), lambda j, i: (i, j))
    return pl.pallas_call(
        functools.partial(_ssm_kernel, n_t=CHUNK, n_rows=n_rows, has_h0=False),
        grid=(N_SSM_TILES, n),
        in_specs=[tile] + _ssm_table_specs(tabs, lambda j, i: j, CHUNK),
        out_specs=[tile, pl.BlockSpec((1, 1, 1, TILE_STATE_W), lambda j, i: (i, j, 0, 0))],
        out_shape=[jax.ShapeDtypeStruct((n * seq, D_SSM), F32),
                   jax.ShapeDtypeStruct((n, N_SSM_TILES, 1, TILE_STATE_W), F32)],
        compiler_params=_params(2),
        name="ssm_prompt",
    )(u, *tabs)


def _ssm_sample(u, h0, tabs, n_seq, n_new):
    tile = pl.BlockSpec((n_seq * n_new, LANE), lambda j: (0, j))
    state = pl.BlockSpec((n_seq, TILE_STATE_W), lambda j: (0, j))
    return pl.pallas_call(
        functools.partial(_ssm_kernel, n_t=n_new, n_rows=n_seq, has_h0=True),
        grid=(N_SSM_TILES,),
        in_specs=[tile, state] + _ssm_table_specs(tabs, lambda j: j, n_new),
        out_specs=[tile, state],
        out_shape=[jax.ShapeDtypeStruct((n_seq * n_new, D_SSM), F32),
                   jax.ShapeDtypeStruct((n_seq, N_SSM_TILES * TILE_STATE_W), F32)],
        compiler_params=_params(1),
        name="ssm_sample",
    )(u, h0, *tabs)


def _ssm_prep_kernel(are_ref, aim_ref, ldt_ref, bre_ref, bim_ref, cre_ref, cim_ref, d_ref,
                     kp_ref, wp_ref, vpt_ref, a1_ref, a2_ref, a1s_ref, a2s_ref, *, n_levels, n_new):
    a_re, a_im = are_ref[0], aim_ref[0]
    dt = jnp.exp(ldt_ref[0])
    mag = jnp.exp(a_re * dt)
    ab_re, ab_im = mag * jnp.cos(a_im * dt), mag * jnp.sin(a_im * dt)
    den = a_re * a_re + a_im * a_im
    nr, ni = ab_re - 1.0, ab_im
    f_re = (nr * a_re + ni * a_im) / den
    f_im = (ni * a_re - nr * a_im) / den
    shape = bre_ref.shape[1:]
    same_group = (lax.broadcasted_iota(jnp.int32, shape, 0) // SSM_GROUP
                  == lax.broadcasted_iota(jnp.int32, shape, 1) // SSM_STATE)
    b_re, b_im = bre_ref[0], bim_ref[0]
    x_re = jnp.where(same_group, f_re * b_re - f_im * b_im, 0.0)
    x_im = jnp.where(same_group, f_re * b_im + f_im * b_re, 0.0)
    c_re = jnp.where(same_group, cre_ref[0], 0.0)
    c_im = jnp.where(same_group, cim_ref[0], 0.0)
    pw = [(jnp.ones_like(ab_re), jnp.zeros_like(ab_im))]
    for _ in range(CHUNK):
        r, i = pw[-1]
        pw.append((r * ab_re - i * ab_im, r * ab_im + i * ab_re))
    cat = lambda r, i: jnp.concatenate([r, i], axis=1)
    e_pack = [cat(c_re * r - c_im * i, -(c_re * i + c_im * r)) for r, i in pw]
    split = lambda a: (a.astype(BF16), (a - a.astype(BF16).astype(F32)).astype(BF16))
    x_hi, x_lo = split(cat(x_re, x_im))
    x_3 = jnp.concatenate([x_hi, x_hi, x_lo], axis=1)
    k_lag = []
    for k in range(CHUNK):
        e_hi, e_lo = split(e_pack[k])
        k_lag.append(_dot_nt(x_3, jnp.concatenate([e_hi, e_lo, e_hi], axis=1)))
    diag = (lax.broadcasted_iota(jnp.int32, (LANE, LANE), 0) == lax.broadcasted_iota(jnp.int32, (LANE, LANE), 1))
    k_lag[0] = k_lag[0] + jnp.where(diag, d_ref[0], 0.0)
    zero = jnp.zeros((LANE, LANE), F32)
    for dl in range(N_PAIRS):
        top = cat(k_lag[2 * dl], k_lag[2 * dl + 1])
        bot = cat(k_lag[2 * dl - 1] if dl > 0 else zero, k_lag[2 * dl])
        kp_ref[0, dl] = jnp.concatenate([top, bot], axis=0).astype(BF16)
    for t in range(CHUNK):
        r, i = pw[CHUNK - 1 - t]
        wp_ref[0, t * LANE:(t + 1) * LANE, :] = cat(x_re * r - x_im * i, x_re * i + x_im * r).astype(BF16)
    for t in range(CHUNK):
        vpt_ref[0, t // 2, (t % 2) * LANE:(t % 2 + 1) * LANE, :] = e_pack[t + 1].astype(BF16)
    r, i = step_r, step_i = pw[CHUNK]
    for s in range(n_levels):
        a1_ref[0, s:s + 1, :], a2_ref[0, s:s + 1, :] = cat(r, r), cat(-i, i)
        r, i = r * step_r - i * step_i, r * step_i + i * step_r
    r, i = pw[n_new]
    a1s_ref[0], a2s_ref[0] = cat(r, r), cat(-i, i)


def _ssm_tables(a_re, a_im, log_dt, b_re, b_im, c_re, c_im, d_skip, n_levels, n_new):
    nj, tg = N_SSM_TILES, SSM_TILE_GROUPS
    half = tg * SSM_STATE
    rowv = lambda v: v.reshape(nj, 1, half)
    tiled = lambda m: jnp.tile(m.reshape(nj, LANE, SSM_STATE), (1, 1, tg))
    ins = (rowv(a_re), rowv(a_im), rowv(jnp.repeat(log_dt, SSM_STATE)),
           tiled(b_re.transpose(0, 2, 1)), tiled(b_im.transpose(0, 2, 1)), tiled(c_re), tiled(c_im),
           d_skip.reshape(nj, 1, LANE))
    shapes = ((nj, N_PAIRS, PAIR_W, PAIR_W), (nj, CHUNK * LANE, TILE_STATE_W), (nj, N_PAIRS, PAIR_W, TILE_STATE_W),
              (nj, n_levels, TILE_STATE_W), (nj, n_levels, TILE_STATE_W), (nj, 1, TILE_STATE_W), (nj, 1, TILE_STATE_W))
    dtypes = (BF16, BF16, BF16, F32, F32, F32, F32)
    per_tile = lambda s: pl.BlockSpec((1,) + tuple(s[1:]), lambda j: (j,) + (0,) * (len(s) - 1))
    kp, wp, vpt, a1, a2, a1s, a2s = pl.pallas_call(
        functools.partial(_ssm_prep_kernel, n_levels=n_levels, n_new=n_new),
        grid=(nj,),
        in_specs=[per_tile(x.shape) for x in ins],
        out_specs=[per_tile(s) for s in shapes],
        out_shape=[jax.ShapeDtypeStruct(s, dt) for s, dt in zip(shapes, dtypes)],
        compiler_params=_params(1),
        name="ssm_prep",
    )(*ins)
    return (kp, wp, vpt, a1, a2), (kp, wp, vpt, a1s, a2s)


def _head_of_lane(width):
    return lax.broadcasted_iota(jnp.int32, (1, width), 1) // HEAD_DIM


def _stack_heads(q):
    head = _head_of_lane(q.shape[1])
    q = q.astype(F32)
    return jnp.concatenate([jnp.where(head == h, q, 0.0) for h in range(HEADS)], axis=0).astype(BF16)


def _unstack_heads(x, rows):
    head = _head_of_lane(x.shape[1])
    out = x[(HEADS - 1) * rows:]
    for h in range(HEADS - 2, -1, -1):
        out = jnp.where(head == h, x[h * rows:(h + 1) * rows], out)
    return out


def _attn_prompt_kernel(bias_ref, q0, q1, k0, k1, v0, v1, kt_ref, vt_ref, o0, o1, l0, l1, tail_ref, kprev, vprev, *, d, m):
    qb = Q_BLOCK
    b = pl.program_id(1)
    n_par = min(d, ATTN_PAR)
    rd, wr = b % 2, (b + 1) % 2
    head = _head_of_lane(D_HEADS)

    @pl.when(b == 0)
    def _():
        kprev[0] = jnp.zeros(kprev.shape[1:], kprev.dtype)
        vprev[0] = jnp.zeros(vprev.shape[1:], vprev.dtype)

    def block(r, j, k_prev, v_prev):
        start = r + j * qb * d
        rows = pl.ds(start, qb, stride=d) if d > 1 else pl.ds(pl.multiple_of(start, qb), qb)
        both = lambda lo, hi: jnp.concatenate([lo[rows, :], hi[rows, :]], axis=1)
        qs = _stack_heads(both(q0, q1) * ATTN_SCALE)
        k_own, v_own = both(k0, k1).astype(BF16), both(v0, v1).astype(BF16)
        k = jnp.concatenate([k_prev, k_own], axis=0)
        v = jnp.concatenate([v_prev, v_own], axis=0)
        s = _dot_nt(qs, k) + bias_ref[jnp.minimum(b * m + j, 1)]
        mx = jnp.max(s, axis=-1, keepdims=True)
        pe = jnp.exp(s - mx)
        den = jnp.sum(pe, axis=-1, keepdims=True)
        pn = (pe * (1.0 / den)).astype(BF16)
        o = _dot(pn[:qb], jnp.where(head == 0, v, jnp.zeros_like(v)))
        for h in range(1, HEADS):
            o = o + _dot(pn[h * qb:(h + 1) * qb], jnp.where(head == h, v, jnp.zeros_like(v)))
        lse = _unstack_heads(jnp.broadcast_to(mx + jnp.log(den), (HEADS * qb, D_HEADS)), qb)
        o0[rows, :], o1[rows, :] = o[:, :LANE], o[:, LANE:]
        l0[rows, :], l1[rows, :] = lse[:, :LANE], lse[:, LANE:]
        return k_own, v_own

    def class_group(g, carry):
        classes = [g * n_par + t for t in range(n_par)]

        def step(j, prev):
            return tuple(x for t, r in enumerate(classes) for x in block(r, j, prev[2 * t], prev[2 * t + 1]))

        init = tuple(x for r in classes for x in (kprev[rd, r], vprev[rd, r]))
        last = lax.fori_loop(0, m, step, init, unroll=max(1, min(m, ATTN_PAR // n_par)))
        for t, r in enumerate(classes):
            kprev[wr, r], vprev[wr, r] = last[2 * t], last[2 * t + 1]
        return carry

    lax.fori_loop(0, d // n_par, class_group, 0)
    tail_ref[0, 0] = kt_ref[...].T
    tail_ref[0, 1] = vt_ref[...].T


def _attn_bias():
    qi = np.arange(HEADS * Q_BLOCK)[:, None] % Q_BLOCK
    ki = np.arange(2 * Q_BLOCK)[None, :]
    diff = Q_BLOCK + qi - ki
    valid = (diff >= 0) & (diff <= N_STEPS)
    return jnp.asarray(np.where(np.stack([valid & (ki >= Q_BLOCK), valid]), 0.0, -np.inf), F32)


def _attn_prompt(qkv, n, seq, grp):
    d = DILATIONS[grp]
    tb = min(ATTN_TB, seq)
    assert tb % (Q_BLOCK * d) == 0 and seq % tb == 0
    m, nb = tb // (Q_BLOCK * d), seq // tb
    cur = lambda c: pl.BlockSpec((tb, LANE), lambda i, b: (i * nb + b, c))
    qc, kc, vc = (2 * _qkv_col(which, grp) for which in range(3))
    bias = _attn_bias()
    keep = min(WINDOWS[grp], seq)
    tt = min(keep, 512)
    n_tiles, first = keep // tt, (seq - keep) // tt
    assert keep % tt == 0 and (seq - keep) % tt == 0 and nb >= n_tiles
    tile = lambda b: jnp.minimum(b, n_tiles - 1)
    tail_rows = lambda which: pl.BlockSpec(
        (tt, D_HEADS), lambda i, b: (i * (seq // tt) + first + tile(b), _qkv_col(which, grp)))
    *attn, tail = pl.pallas_call(
        functools.partial(_attn_prompt_kernel, d=d, m=m),
        grid=(n, nb),
        in_specs=[_const_spec(bias.shape), cur(qc), cur(qc + 1), cur(kc), cur(kc + 1), cur(vc), cur(vc + 1),
                  tail_rows(1), tail_rows(2)],
        out_specs=[pl.BlockSpec((tb, LANE), lambda i, b: (i * nb + b, 0))] * 4
        + [pl.BlockSpec((1, 2, D_HEADS, tt), lambda i, b: (i, 0, 0, tile(b)))],
        out_shape=[jax.ShapeDtypeStruct((n * seq, LANE), F32)] * 4
        + [jax.ShapeDtypeStruct((n, 2, D_HEADS, keep), F32)],
        scratch_shapes=[pltpu.VMEM((2, d, Q_BLOCK, D_HEADS), BF16)] * 2,
        compiler_params=_params(2),
        name=f"attn_prompt_d{d}",
    )(bias, *([qkv] * 8))
    return attn, tail.reshape(1, n, 2, HEADS, HEAD_DIM, keep).transpose(0, 1, 5, 2, 3, 4)


def _attn_sample_kernel(q_ref, kn_ref, vn_ref, c_ref, o0_ref, o1_ref, l0_ref, l1_ref, *, d, n_new, seq_blk):
    w = c_ref.shape[-1]
    n_rows = HEADS * n_new
    l_b = lax.broadcasted_iota(jnp.int32, (n_rows, w), 0) % n_new
    pos = lax.broadcasted_iota(jnp.int32, (n_rows, w), 1)
    valid_buf = (((w + l_b - pos) & (d - 1)) == 0) & (pos >= l_b)
    dn = (lax.broadcasted_iota(jnp.int32, (n_rows, LANE), 0) % n_new) - lax.broadcasted_iota(jnp.int32, (n_rows, LANE), 1)
    valid_new = (dn >= 0) & ((dn & (d - 1)) == 0)
    pad = jnp.zeros((LANE - n_new, D_HEADS), F32)

    def one_seq(s, carry):
        qs = _stack_heads(q_ref[s])
        k_t, v_t = c_ref[s, 0].astype(BF16), c_ref[s, 1].astype(BF16)
        k_n = jnp.concatenate([kn_ref[s], pad], axis=0).astype(BF16)
        v_n = jnp.concatenate([vn_ref[s], pad], axis=0).astype(BF16)
        s_b = jnp.where(valid_buf, _dot(qs, k_t) * ATTN_SCALE, -jnp.inf)
        s_n = jnp.where(valid_new, _dot_nt(qs, k_n) * ATTN_SCALE, -jnp.inf)
        mx = jnp.maximum(jnp.max(s_b, axis=-1, keepdims=True), jnp.max(s_n, axis=-1, keepdims=True))
        p_b, p_n = jnp.exp(s_b - mx), jnp.exp(s_n - mx)
        den = jnp.sum(p_b, axis=-1, keepdims=True) + jnp.sum(p_n, axis=-1, keepdims=True)
        o = (_dot_nt(p_b.astype(BF16), v_t) + _dot(p_n.astype(BF16), v_n)) / den
        lse = _unstack_heads(jnp.broadcast_to(mx + jnp.log(den), o.shape), n_new)
        o = _unstack_heads(o, n_new)
        o0_ref[s], o1_ref[s] = o[:, :LANE], o[:, LANE:]
        l0_ref[s], l1_ref[s] = lse[:, :LANE], lse[:, LANE:]
        return carry

    lax.fori_loop(0, seq_blk, one_seq, 0, unroll=True)


def _attn_sample_job(qkv, cache, n_seq, n_new, grp, n_steps):
    d = DILATIONS[grp]
    w = cache.shape[1]
    assert w == N_STEPS * d and n_new <= LANE and n_seq % n_steps == 0
    seq_blk = n_seq // n_steps
    qkv3 = qkv.reshape(n_seq, n_new, D_QKV)
    c_t = cache.transpose(0, 2, 3, 4, 1).reshape(n_seq, 2, D_HEADS, w)
    blk = (seq_blk, n_new, D_HEADS)
    col = lambda which: pl.BlockSpec(blk, lambda i: (i, 0, _qkv_col(which, grp)))
    out_spec = pl.BlockSpec((seq_blk, n_new, LANE), lambda i: (i, 0, 0))
    out_shape = jax.ShapeDtypeStruct((n_seq, n_new, LANE), F32)
    return _SideJob((qkv3, qkv3, qkv3, c_t),
                    (col(0), col(1), col(2), pl.BlockSpec((seq_blk, 2, D_HEADS, w), lambda i: (i, 0, 0, 0))),
                    (out_shape,) * 4, (out_spec,) * 4,
                    functools.partial(_attn_sample_kernel, d=d, n_new=n_new, seq_blk=seq_blk))


def _kv_new_job(qkv, n_seq, n_new, n_steps):
    halves, l_parts = D_HEADS // LANE, 2
    n_units, l_blk = N_GROUPS * 2 * halves * l_parts, n_new // l_parts
    assert n_steps >= n_units and n_new % l_parts == 0
    unit = lambda i: jnp.minimum(i, n_units - 1)
    l_part = lambda i: unit(i) % l_parts
    half = lambda i: unit(i) // l_parts % halves
    kv = lambda i: unit(i) // (l_parts * halves) % 2
    grp = lambda i: unit(i) // (l_parts * halves * 2)
    in_spec = pl.BlockSpec((n_seq * n_new, LANE), lambda i: (0, halves * ((1 + kv(i)) * N_GROUPS + grp(i)) + half(i)))
    out_spec = pl.BlockSpec((1, l_blk, 1, LANE, n_seq), lambda i: (grp(i), l_part(i), kv(i), half(i), 0))
    out_shape = jax.ShapeDtypeStruct((N_GROUPS, n_new, 2, D_HEADS, n_seq), F32)

    def body(x_ref, o_ref):
        l0 = l_part(pl.program_id(0)) * l_blk
        for dl in range(l_blk):
            o_ref[0, dl, 0] = x_ref[pl.ds(l0 + dl, n_seq, stride=n_new), :].T

    return _SideJob((qkv,), (in_spec,), (out_shape,), (out_spec,), body)


def _kv_new_outputs(out, n_seq, n_new):
    out = out.reshape(N_GROUPS, 1, n_new, 2, HEADS, HEAD_DIM, n_seq).transpose(0, 1, 6, 2, 3, 4, 5)
    return [out[g] for g in range(N_GROUPS)]


def _gelu_tanh(x):
    return 0.5 * x * (1.0 + jnp.tanh(math.sqrt(2.0 / math.pi) * (x + 0.044715 * (x * x * x))))


def _stage3_kernel(*refs):
    x1_ref, yp_ref = refs[:2]
    attn = refs[2:2 + 4 * N_GROUPS]
    (p_ref, gm_ref, wgate_ref, wglu_ref, wbs_ref, wba_ref, wout_ref, g2_ref, wg_ref, wu_ref, wd_ref,
     gp_ref, wpg_ref, wpp_ref, gf_ref, y_ref) = refs[2 + 4 * N_GROUPS:]
    x1 = x1_ref[...]
    gates = jax.nn.sigmoid(_dot(_rms(x1, gm_ref[...]).astype(BF16), wgate_ref[...]))
    y = _gelu_tanh(yp_ref[...])
    glu = y * jax.nn.sigmoid(_dot(y.astype(BF16), wglu_ref[...]))
    full = lambda lo, hi: jnp.concatenate([lo[...], hi[...]], axis=1)
    outs = [full(attn[4 * g], attn[4 * g + 1]) for g in range(N_GROUPS)]
    lses = [full(attn[4 * g + 2], attn[4 * g + 3]) for g in range(N_GROUPS)]
    mx = jnp.maximum(jnp.maximum(lses[0], lses[1]), lses[2])
    es = [jnp.exp(l - mx) for l in lses]
    y_attn = (es[0] * outs[0] + es[1] * outs[1] + es[2] * outs[2]) / (es[0] + es[1] + es[2])
    merged = (gates[:, :D_MODEL] * _dot(glu.astype(BF16), wbs_ref[...])
              + gates[:, D_MODEL:] * _dot(y_attn.astype(BF16), wba_ref[...]))
    x = x1 + _dot(merged.astype(BF16), wout_ref[...])
    x = x + 0.5 * _swiglu(_rms(x, g2_ref[...]).astype(BF16), wg_ref, wu_ref, wd_ref)
    gate = jax.nn.sigmoid(_dot(_rms(x, gp_ref[...]).astype(BF16), wpg_ref[...]))
    x = x + gate * _dot(p_ref[...].astype(BF16), wpp_ref[...])
    y_ref[...] = _rms(x, gf_ref[...])


def _stage3(tok_inputs, weights):
    t, tm = tok_inputs[0].shape[0], STAGE_TM
    tok = lambda a: pl.BlockSpec((tm, a.shape[1]), lambda i: (i, 0))
    return pl.pallas_call(
        _stage3_kernel,
        grid=(t // tm,),
        in_specs=[tok(a) for a in tok_inputs] + [_const_spec(w.shape) for w in weights],
        out_specs=pl.BlockSpec((tm, D_MODEL), lambda i: (i, 0)),
        out_shape=jax.ShapeDtypeStruct((t, D_MODEL), F32),
        compiler_params=_params(1),
        name="stage3",
    )(*tok_inputs, *weights)


def kernel(x_prompt, x_sample, p_prompt, p_sample, cache_kv_w128, cache_kv_w512, cache_kv_w2048, state_ssm,
           g_ffn1, ffn1_w_gate, ffn1_w_up, ffn1_w_down, g_mix, w_in, ssm_a_re, ssm_a_im, ssm_log_dt,
           ssm_b_re, ssm_b_im, ssm_c_re, ssm_c_im, ssm_d, ssm_w_glu, w_br_ssm, w_br_attn, w_out,
           g_ffn2, ffn2_w_gate, ffn2_w_up, ffn2_w_down, g_ple, w_ple_gate, w_ple_proj, g_final):
    assert x_prompt.shape[-1] == D_MODEL and g_ffn1.shape[0] == 1
    n_p, seq, _ = x_prompt.shape
    n_s, n_new, _ = x_sample.shape
    caches = (cache_kv_w128, cache_kv_w512, cache_kv_w2048)
    row = lambda g: g.reshape(1, -1)

    wi = w_in[0]
    win = wi[:, :D_SSM + D_QKV].astype(BF16)
    w_gates = wi[:, D_SSM + D_QKV:].astype(BF16)
    s3_f32 = [w[0] for w in (ssm_w_glu, w_br_ssm, w_br_attn, w_out, ffn2_w_gate, ffn2_w_up, ffn2_w_down,
                             w_ple_gate, w_ple_proj)]

    tabs_p, tabs_s = _ssm_tables(
        ssm_a_re[0], ssm_a_im[0], ssm_log_dt[0], ssm_b_re[0], ssm_b_im[0], ssm_c_re[0], ssm_c_im[0],
        ssm_d[0], SCAN_BLOCK, n_new)
    tile_state = (-1, N_SSM_TILES, 2, SSM_TILE_GROUPS, SSM_STATE)
    from_tiles = lambda h: h.reshape(tile_state).transpose(0, 1, 3, 4, 2).reshape(1, -1, N_SSM_GROUPS, SSM_STATE, 2)

    xs = x_sample.reshape(n_s * n_new, D_MODEL)
    x1_s, *ffn1_bf = _ffn1_stream(xs, row(g_ffn1[0]), ffn1_w_gate[0], ffn1_w_up[0], ffn1_w_down[0])
    ffn1_w = (row(g_ffn1[0]), *ffn1_bf)
    u_s, qkv_s, _ = _proj(x1_s, row(g_mix[0]), win, tm=min(STAGE_TM, xs.shape[0]))

    xp = x_prompt.reshape(n_p * seq, D_MODEL)
    attn_job = lambda grp, n_steps: _attn_sample_job(qkv_s, caches[grp][0], n_s, n_new, grp, n_steps)
    steps_a, steps_b = xp.shape[0] // FFN1_HOST_TM, xp.shape[0] // STAGE_TM
    x1, side = _ffn1(xp, *ffn1_w, tm=FFN1_HOST_TM,
                     jobs=[attn_job(2, steps_a), attn_job(0, steps_a)] + [_cast_job(w, steps_a) for w in s3_f32])
    attn_s = {2: side[0], 0: side[1]}
    wglu, wbs, wba, wout, wg2, wu2, wd2, wpg, wpp = (c[0] for c in side[2:])
    s3_w = (row(g_mix[0]), w_gates, wglu, wbs, wba, wout, row(g_ffn2[0]), wg2, wu2, wd2, row(g_ple[0]), wpg, wpp,
            row(g_final))
    u, qkv, side = _proj(x1, row(g_mix[0]), win, tm=STAGE_TM,
                         jobs=[attn_job(1, steps_b), _kv_new_job(qkv_s, n_s, n_new, steps_b)])
    attn_s[1] = side[0]
    kv_sample = _kv_new_outputs(side[1][0], n_s, n_new)
    y_pre, h_last = _ssm_prompt(u, tabs_p, n_p, seq)
    attn_p = [_attn_prompt(qkv, n_p, seq, grp) for grp in range(N_GROUPS)]
    attn, kv_prompt = [a for per_group, _ in attn_p for a in per_group], [tail for _, tail in attn_p]
    tok = (x1, y_pre, *attn, p_prompt[0].reshape(n_p * seq, D_PLE))
    y_prompt = _stage3(tok, s3_w).reshape(n_p, seq, D_MODEL)
    ssm_prompt = from_tiles(h_last)

    h0 = state_ssm[0].reshape(n_s, N_SSM_TILES, SSM_TILE_GROUPS, SSM_STATE, 2)
    h0 = h0.transpose(0, 1, 4, 2, 3).reshape(n_s, N_SSM_TILES * TILE_STATE_W)
    y_pre, h_new = _ssm_sample(u_s, h0, tabs_s, n_s, n_new)
    attn = [a.reshape(-1, LANE) for grp in range(N_GROUPS) for a in attn_s[grp]]
    tok = (x1_s, y_pre, *attn, p_sample[0].reshape(n_s * n_new, D_PLE))
    y_sample = _stage3(tok, s3_w).reshape(n_s, n_new, D_MODEL)
    ssm_sample = from_tiles(h_new)

    return (y_prompt, y_sample, kv_prompt[0], kv_prompt[1], kv_prompt[2], ssm_prompt,
            kv_sample[0], kv_sample[1], kv_sample[2], ssm_sample)
```

```python
import functools
import math
from typing import Callable, NamedTuple

import jax
import jax.numpy as jnp
import numpy as np
from jax import lax
from jax.experimental import pallas as pl
from jax.experimental.pallas import tpu as pltpu

D_MODEL = 1024
D_PLE = 256
SSM_GROUP = 16
SSM_STATE = 64
D_SSM = 512
N_SSM_GROUPS = D_SSM // SSM_GROUP
HEAD_DIM = 64
HEADS = 4
WINDOWS = (128, 512, 2048)
DILATIONS = (1, 4, 16)
N_GROUPS = 3
D_HEADS = HEADS * HEAD_DIM
N_STEPS = 128
ATTN_SCALE = HEAD_DIM ** -0.5
EPS = 1e-6

LANE = 128
V7X_VMEM_LIMIT_BYTES = 56 * 1024 * 1024
STAGE_TM = 512
FFN_STREAM_CHUNK = 256
FFN1_HOST_TM = 256

CHUNK = 16
SCAN_BLOCK = 8
N_PAIRS = CHUNK // 2
PAIR_W = 2 * LANE
SSM_TILE_GROUPS = LANE // SSM_GROUP
N_SSM_TILES = N_SSM_GROUPS // SSM_TILE_GROUPS
TILE_STATE_W = SSM_TILE_GROUPS * 2 * SSM_STATE
Q_BLOCK = 128
ATTN_TB = 2048
ATTN_PAR = 16

D_QKV = 3 * N_GROUPS * D_HEADS


def _qkv_col(which, grp):
    return which * N_GROUPS + grp


BF16 = jnp.bfloat16
F32 = jnp.float32


def _dot(a, b):
    return jnp.dot(a, b, preferred_element_type=F32)


def _dot_nt(a, b):
    return lax.dot_general(a, b, (((1,), (1,)), ((), ())), preferred_element_type=F32)


def _rms(x, g):
    return x * lax.rsqrt(jnp.mean(x * x, axis=-1, keepdims=True) + EPS) * g


def _swiglu(xn, wg_ref, wu_ref, wd_ref):
    gate = _dot(xn, wg_ref[...])
    up = _dot(xn, wu_ref[...])
    act = (gate * jax.nn.sigmoid(gate) * up).astype(BF16)
    return _dot(act, wd_ref[...])


def _const_spec(shape):
    nd = len(shape)
    return pl.BlockSpec(shape, lambda *_: (0,) * nd, pipeline_mode=pl.Buffered(1))


def _params(n_grid_dims):
    return pltpu.CompilerParams(dimension_semantics=("arbitrary",) * n_grid_dims,
                                vmem_limit_bytes=V7X_VMEM_LIMIT_BYTES)


class _SideJob(NamedTuple):
    arrays: tuple
    in_specs: tuple
    out_shapes: tuple
    out_specs: tuple
    body: Callable


def _cast_job(w, n_steps, col0=0, n_cols=None):
    rows = w.shape[0]
    n_cols = w.shape[1] - col0 if n_cols is None else n_cols
    cw = n_cols if col0 == 0 else math.gcd(col0, n_cols)
    c_blks = n_cols // cw
    assert cw % LANE == 0 and n_steps >= c_blks
    r_blks = max(s for s in range(1, n_steps // c_blks + 1) if rows % s == 0 and (rows // s) % 16 == 0)
    unit = lambda i: jnp.minimum(i, r_blks * c_blks - 1)
    blk = (rows // r_blks, cw)
    src_spec = pl.BlockSpec(blk, lambda i: (unit(i) // c_blks, col0 // cw + unit(i) % c_blks))
    dst_spec = pl.BlockSpec(blk, lambda i: (unit(i) // c_blks, unit(i) % c_blks))

    def body(src, dst):
        dst[...] = src[...].astype(BF16)

    return _SideJob((w,), (src_spec,), (jax.ShapeDtypeStruct((rows, n_cols), BF16),), (dst_spec,), body)


def _hosted_call(main_body, n_steps, arrays, in_specs, out_shapes, out_specs, jobs, name):
    n_in, n_out = len(arrays), len(out_shapes)

    def body(*refs):
        pos, job_in, job_out = n_in, [], []
        for job in jobs:
            job_in.append(refs[pos:pos + len(job.arrays)])
            pos += len(job.arrays)
        main_out = refs[pos:pos + n_out]
        pos += n_out
        for job in jobs:
            job_out.append(refs[pos:pos + len(job.out_shapes)])
            pos += len(job.out_shapes)
        for job, ins, outs in zip(jobs, job_in, job_out):
            job.body(*ins, *outs)
        main_body(*refs[:n_in], *main_out)

    outs = pl.pallas_call(
        body,
        grid=(n_steps,),
        in_specs=list(in_specs) + [sp for job in jobs for sp in job.in_specs],
        out_specs=list(out_specs) + [sp for job in jobs for sp in job.out_specs],
        out_shape=list(out_shapes) + [sh for job in jobs for sh in job.out_shapes],
        compiler_params=_params(1),
        name=name,
    )(*arrays, *[a for job in jobs for a in job.arrays])
    main, pos, per_job = outs[:n_out], n_out, []
    for job in jobs:
        per_job.append(outs[pos:pos + len(job.out_shapes)])
        pos += len(job.out_shapes)
    return main, per_job


def _ffn1_kernel(x_ref, g1_ref, wg_ref, wu_ref, wd_ref, x1_ref):
    x = x_ref[...]
    x1_ref[...] = x + 0.5 * _swiglu(_rms(x, g1_ref[...]).astype(BF16), wg_ref, wu_ref, wd_ref)


def _ffn1(x, g1, wg, wu, wd, tm, jobs=()):
    t = x.shape[0]
    tok = pl.BlockSpec((tm, D_MODEL), lambda i: (i, 0))
    (x1,), per_job = _hosted_call(
        _ffn1_kernel, t // tm, (x, g1, wg, wu, wd),
        [tok] + [_const_spec(a.shape) for a in (g1, wg, wu, wd)],
        [jax.ShapeDtypeStruct((t, D_MODEL), F32)], [tok], jobs, "ffn1")
    return x1, per_job


def _ffn1_stream_kernel(x_ref, g1_ref, wg_ref, wu_ref, wd_ref, wi_ref,
                        x1_ref, wg_bf_ref, wu_bf_ref, wd_bf_ref, wi_bf_ref, xn_ref, acc_ref):
    c = pl.program_id(0)

    @pl.when(c == 0)
    def _():
        xn_ref[...] = _rms(x_ref[...], g1_ref[...]).astype(BF16)
        acc_ref[...] = jnp.zeros_like(acc_ref)

    wg, wu, wd = wg_ref[...].astype(BF16), wu_ref[...].astype(BF16), wd_ref[...].astype(BF16)
    wg_bf_ref[...], wu_bf_ref[...], wd_bf_ref[...] = wg, wu, wd
    wi_bf_ref[...] = wi_ref[...].astype(BF16)
    xn = xn_ref[...]
    gate = _dot(xn, wg)
    act = (gate * jax.nn.sigmoid(gate) * _dot(xn, wu)).astype(BF16)
    acc_ref[...] += _dot(act, wd)

    @pl.when(c == pl.num_programs(0) - 1)
    def _():
        x1_ref[...] = x_ref[...] + 0.5 * acc_ref[...]


def _ffn1_stream(x, g1, wg, wu, wd, w_in, n_in_cols):
    t, d_ff = x.shape[0], wg.shape[1]
    ck = FFN_STREAM_CHUNK
    assert d_ff % ck == 0 and n_in_cols <= d_ff and n_in_cols % ck == 0
    in_cols = pl.BlockSpec((D_MODEL, ck), lambda c: (0, jnp.minimum(c, n_in_cols // ck - 1)))
    whole = lambda a: pl.BlockSpec(a.shape, lambda c: (0,) * a.ndim, pipeline_mode=pl.Buffered(1))
    cols = pl.BlockSpec((D_MODEL, ck), lambda c: (0, c))
    rows = pl.BlockSpec((ck, D_MODEL), lambda c: (c, 0))
    return pl.pallas_call(
        _ffn1_stream_kernel,
        grid=(d_ff // ck,),
        in_specs=[whole(x), whole(g1), cols, cols, rows, in_cols],
        out_specs=[pl.BlockSpec(x.shape, lambda c: (0, 0)), cols, cols, rows, in_cols],
        out_shape=[jax.ShapeDtypeStruct(x.shape, F32), jax.ShapeDtypeStruct(wg.shape, BF16),
                   jax.ShapeDtypeStruct(wu.shape, BF16), jax.ShapeDtypeStruct(wd.shape, BF16),
                   jax.ShapeDtypeStruct((D_MODEL, n_in_cols), BF16)],
        scratch_shapes=[pltpu.VMEM(x.shape, BF16), pltpu.VMEM(x.shape, F32)],
        compiler_params=_params(1),
        name="ffn1_stream",
    )(x, g1, wg, wu, wd, w_in)


def _proj_kernel(x1_ref, gm_ref, win_ref, u_ref, qkv_ref):
    z = _dot(_rms(x1_ref[...], gm_ref[...]).astype(BF16), win_ref[...])
    u_ref[...] = z[:, :D_SSM]
    qkv_ref[...] = z[:, D_SSM:]


def _proj(x1, gm, win, tm, jobs=()):
    t = x1.shape[0]
    tok = lambda w: pl.BlockSpec((tm, w), lambda i: (i, 0))
    (u, qkv), per_job = _hosted_call(
        _proj_kernel, t // tm, (x1, gm, win),
        [tok(D_MODEL), _const_spec(gm.shape), _const_spec(win.shape)],
        [jax.ShapeDtypeStruct((t, D_SSM), F32), jax.ShapeDtypeStruct((t, D_QKV), F32)], [tok(D_SSM), tok(D_QKV)],
        jobs, "proj")
    return u, qkv, per_job


def _swap_halves(x):
    half = x.shape[1] // 2
    return jnp.concatenate([x[:, half:], x[:, :half]], axis=1)


def _cmul_split(x, a1, a2):
    return x * a1 + _swap_halves(x) * a2


def _ssm_kernel(*refs, n_t, n_rows, has_h0):
    if has_h0:
        u_ref, h0_ref, kp_ref, wp_ref, vpt_ref, a1_ref, a2_ref, y_ref, hl_ref = refs
    else:
        u_ref, kp_ref, wp_ref, vpt_ref, a1_ref, a2_ref, y_ref, hl_ref = refs
    n_pairs = n_t // 2
    tok = lambda t: pl.ds(t, n_rows, stride=n_t)
    up = [jnp.concatenate([u_ref[tok(2 * a), :], u_ref[tok(2 * a + 1), :]], axis=1).astype(BF16)
          for a in range(n_pairs)]
    s = _dot(jnp.concatenate(up, axis=1), wp_ref[0])
    in_chunk = [None] * n_pairs
    todo = [(b, a) for b in range(n_pairs) for a in range(b + 1)]

    def in_chunk_products(count):
        for b, a in [todo.pop(0) for _ in range(min(count, len(todo)))]:
            prod = _dot(up[a], kp_ref[0, b - a])
            in_chunk[b] = prod if in_chunk[b] is None else in_chunk[b] + prod

    if has_h0:
        hprev = h0_ref[...]
        h_last = _cmul_split(hprev, a1_ref[0], a2_ref[0]) + s
        hl_ref[...] = h_last
    else:
        blk = SCAN_BLOCK
        assert n_rows % blk == 0 and a1_ref.shape[1] == blk
        row = lax.broadcasted_iota(jnp.int32, (blk, s.shape[1]), 0)
        carry = jnp.zeros((1, s.shape[1]), F32)
        entering = []
        for i in range(n_rows // blk):
            h = s[i * blk:(i + 1) * blk]
            for lvl in range(int(math.log2(blk))):
                sh = 1 << lvl
                shifted = jnp.where(row >= sh, pltpu.roll(h, sh, axis=0), 0.0)
                h = h + _cmul_split(shifted, a1_ref[0, sh - 1:sh, :], a2_ref[0, sh - 1:sh, :])
            carry_b = jnp.broadcast_to(carry, h.shape)
            h = h + _cmul_split(carry_b, a1_ref[0], a2_ref[0])
            entering.append(jnp.where(row == 0, carry_b, pltpu.roll(h, 1, axis=0)))
            carry = h[blk - 1:blk]
            in_chunk_products(-(-len(todo) // (n_rows // blk - i)))
        hprev = jnp.concatenate(entering, axis=0)
        hl_ref[0, 0] = carry
    in_chunk_products(len(todo))
    hpb = hprev.astype(BF16)
    for b in range(n_pairs):
        acc = _dot_nt(hpb, vpt_ref[0, b]) + in_chunk[b]
        y_ref[tok(2 * b), :] = acc[:, :LANE]
        y_ref[tok(2 * b + 1), :] = acc[:, LANE:]


def _ssm_table_specs(tabs, idx, n_t):
    kp, wp, vp, a1, a2 = tabs
    n_pairs, w_rows = n_t // 2, n_t * LANE
    return [pl.BlockSpec((1, n_pairs) + kp.shape[2:], lambda *g: (idx(*g), 0, 0, 0)),
            pl.BlockSpec((1, w_rows, wp.shape[2]), lambda *g: (idx(*g), wp.shape[1] // w_rows - 1, 0)),
            pl.BlockSpec((1, n_pairs) + vp.shape[2:], lambda *g: (idx(*g), 0, 0, 0)),
            pl.BlockSpec((1,) + a1.shape[1:], lambda *g: (idx(*g), 0, 0)),
            pl.BlockSpec((1,) + a2.shape[1:], lambda *g: (idx(*g), 0, 0))]


def _ssm_prompt(u, tabs, n, seq):
    n_rows = seq // CHUNK
    tile = pl.BlockSpec((seq, LANE), lambda j, i: (i, j))
    return pl.pallas_call(
        functools.partial(_ssm_kernel, n_t=CHUNK, n_rows=n_rows, has_h0=False),
        grid=(N_SSM_TILES, n),
        in_specs=[tile] + _ssm_table_specs(tabs, lambda j, i: j, CHUNK),
        out_specs=[tile, pl.BlockSpec((1, 1, 1, TILE_STATE_W), lambda j, i: (i, j, 0, 0))],
        out_shape=[jax.ShapeDtypeStruct((n * seq, D_SSM), F32),
                   jax.ShapeDtypeStruct((n, N_SSM_TILES, 1, TILE_STATE_W), F32)],
        compiler_params=_params(2),
        name="ssm_prompt",
    )(u, *tabs)


def _ssm_sample(u, h0, tabs, n_seq, n_new):
    tile = pl.BlockSpec((n_seq * n_new, LANE), lambda j: (0, j))
    state = pl.BlockSpec((n_seq, TILE_STATE_W), lambda j: (0, j))
    return pl.pallas_call(
        functools.partial(_ssm_kernel, n_t=n_new, n_rows=n_seq, has_h0=True),
        grid=(N_SSM_TILES,),
        in_specs=[tile, state] + _ssm_table_specs(tabs, lambda j: j, n_new),
        out_specs=[tile, state],
        out_shape=[jax.ShapeDtypeStruct((n_seq * n_new, D_SSM), F32),
                   jax.ShapeDtypeStruct((n_seq, N_SSM_TILES * TILE_STATE_W), F32)],
        compiler_params=_params(1),
        name="ssm_sample",
    )(u, h0, *tabs)


def _ssm_prep_kernel(are_ref, aim_ref, ldt_ref, bre_ref, bim_ref, cre_ref, cim_ref, d_ref,
                     kp_ref, wp_ref, vpt_ref, a1_ref, a2_ref, a1s_ref, a2s_ref, *, n_powers, n_new):
    a_re, a_im = are_ref[0], aim_ref[0]
    dt = jnp.exp(ldt_ref[0])
    mag = jnp.exp(a_re * dt)
    ab_re, ab_im = mag * jnp.cos(a_im * dt), mag * jnp.sin(a_im * dt)
    den = a_re * a_re + a_im * a_im
    nr, ni = ab_re - 1.0, ab_im
    f_re = (nr * a_re + ni * a_im) / den
    f_im = (ni * a_re - nr * a_im) / den
    shape = bre_ref.shape[1:]
    same_group = (lax.broadcasted_iota(jnp.int32, shape, 0) // SSM_GROUP
                  == lax.broadcasted_iota(jnp.int32, shape, 1) // SSM_STATE)
    b_re, b_im = bre_ref[0], bim_ref[0]
    x_re = jnp.where(same_group, f_re * b_re - f_im * b_im, 0.0)
    x_im = jnp.where(same_group, f_re * b_im + f_im * b_re, 0.0)
    c_re = jnp.where(same_group, cre_ref[0], 0.0)
    c_im = jnp.where(same_group, cim_ref[0], 0.0)
    pw = [(jnp.ones_like(ab_re), jnp.zeros_like(ab_im))]
    for _ in range(CHUNK):
        r, i = pw[-1]
        pw.append((r * ab_re - i * ab_im, r * ab_im + i * ab_re))
    cat = lambda r, i: jnp.concatenate([r, i], axis=1)
    e_pack = [cat(c_re * r - c_im * i, -(c_re * i + c_im * r)) for r, i in pw]
    split = lambda a: (a.astype(BF16), (a - a.astype(BF16).astype(F32)).astype(BF16))
    x_hi, x_lo = split(cat(x_re, x_im))
    x_3 = jnp.concatenate([x_hi, x_hi, x_lo], axis=1)
    k_lag = []
    for k in range(CHUNK):
        e_hi, e_lo = split(e_pack[k])
        k_lag.append(_dot_nt(x_3, jnp.concatenate([e_hi, e_lo, e_hi], axis=1)))
    diag = (lax.broadcasted_iota(jnp.int32, (LANE, LANE), 0) == lax.broadcasted_iota(jnp.int32, (LANE, LANE), 1))
    k_lag[0] = k_lag[0] + jnp.where(diag, d_ref[0], 0.0)
    zero = jnp.zeros((LANE, LANE), F32)
    for dl in range(N_PAIRS):
        top = cat(k_lag[2 * dl], k_lag[2 * dl + 1])
        bot = cat(k_lag[2 * dl - 1] if dl > 0 else zero, k_lag[2 * dl])
        kp_ref[0, dl] = jnp.concatenate([top, bot], axis=0).astype(BF16)
    for t in range(CHUNK):
        r, i = pw[CHUNK - 1 - t]
        wp_ref[0, t * LANE:(t + 1) * LANE, :] = cat(x_re * r - x_im * i, x_re * i + x_im * r).astype(BF16)
    for t in range(CHUNK):
        vpt_ref[0, t // 2, (t % 2) * LANE:(t % 2 + 1) * LANE, :] = e_pack[t + 1].astype(BF16)
    r, i = step_r, step_i = pw[CHUNK]
    for s in range(n_powers):
        a1_ref[0, s:s + 1, :], a2_ref[0, s:s + 1, :] = cat(r, r), cat(-i, i)
        r, i = r * step_r - i * step_i, r * step_i + i * step_r
    r, i = pw[n_new]
    a1s_ref[0], a2s_ref[0] = cat(r, r), cat(-i, i)


def _ssm_tables(a_re, a_im, log_dt, b_re, b_im, c_re, c_im, d_skip, n_powers, n_new):
    nj, tg = N_SSM_TILES, SSM_TILE_GROUPS
    half = tg * SSM_STATE
    rowv = lambda v: v.reshape(nj, 1, half)
    tiled = lambda m: jnp.tile(m.reshape(nj, LANE, SSM_STATE), (1, 1, tg))
    ins = (rowv(a_re), rowv(a_im), rowv(jnp.repeat(log_dt, SSM_STATE)),
           tiled(b_re.transpose(0, 2, 1)), tiled(b_im.transpose(0, 2, 1)), tiled(c_re), tiled(c_im),
           d_skip.reshape(nj, 1, LANE))
    shapes = ((nj, N_PAIRS, PAIR_W, PAIR_W), (nj, CHUNK * LANE, TILE_STATE_W), (nj, N_PAIRS, PAIR_W, TILE_STATE_W),
              (nj, n_powers, TILE_STATE_W), (nj, n_powers, TILE_STATE_W), (nj, 1, TILE_STATE_W), (nj, 1, TILE_STATE_W))
    dtypes = (BF16, BF16, BF16, F32, F32, F32, F32)
    per_tile = lambda s: pl.BlockSpec((1,) + tuple(s[1:]), lambda j: (j,) + (0,) * (len(s) - 1))
    kp, wp, vpt, a1, a2, a1s, a2s = pl.pallas_call(
        functools.partial(_ssm_prep_kernel, n_powers=n_powers, n_new=n_new),
        grid=(nj,),
        in_specs=[per_tile(x.shape) for x in ins],
        out_specs=[per_tile(s) for s in shapes],
        out_shape=[jax.ShapeDtypeStruct(s, dt) for s, dt in zip(shapes, dtypes)],
        compiler_params=_params(1),
        name="ssm_prep",
    )(*ins)
    return (kp, wp, vpt, a1, a2), (kp, wp, vpt, a1s, a2s)


def _head_of_lane(width):
    return lax.broadcasted_iota(jnp.int32, (1, width), 1) // HEAD_DIM


def _stack_heads(q):
    head = _head_of_lane(q.shape[1])
    q = q.astype(F32)
    return jnp.concatenate([jnp.where(head == h, q, 0.0) for h in range(HEADS)], axis=0).astype(BF16)


def _unstack_heads(x, rows):
    head = _head_of_lane(x.shape[1])
    out = x[(HEADS - 1) * rows:]
    for h in range(HEADS - 2, -1, -1):
        out = jnp.where(head == h, x[h * rows:(h + 1) * rows], out)
    return out


def _attn_prompt_kernel(bias_ref, q0, q1, k0, k1, v0, v1, kt_ref, vt_ref, o0, o1, l0, l1, tail_ref, kprev, vprev, *, d, m):
    qb = Q_BLOCK
    b = pl.program_id(1)
    n_par = min(d, ATTN_PAR)
    rd, wr = b % 2, (b + 1) % 2
    head = _head_of_lane(D_HEADS)

    @pl.when(b == 0)
    def _():
        kprev[0] = jnp.zeros(kprev.shape[1:], kprev.dtype)
        vprev[0] = jnp.zeros(vprev.shape[1:], vprev.dtype)

    def block(r, j, k_prev, v_prev):
        start = r + j * qb * d
        rows = pl.ds(start, qb, stride=d) if d > 1 else pl.ds(pl.multiple_of(start, qb), qb)
        both = lambda lo, hi: jnp.concatenate([lo[rows, :], hi[rows, :]], axis=1)
        qs = _stack_heads(both(q0, q1) * ATTN_SCALE)
        k_own, v_own = both(k0, k1).astype(BF16), both(v0, v1).astype(BF16)
        k = jnp.concatenate([k_prev, k_own], axis=0)
        v = jnp.concatenate([v_prev, v_own], axis=0)
        s = _dot_nt(qs, k) + bias_ref[jnp.minimum(b * m + j, 1)]
        mx = jnp.max(s, axis=-1, keepdims=True)
        pe = jnp.exp(s - mx)
        den = jnp.sum(pe, axis=-1, keepdims=True)
        pn = (pe * (1.0 / den)).astype(BF16)
        o = _dot(pn[:qb], jnp.where(head == 0, v, jnp.zeros_like(v)))
        for h in range(1, HEADS):
            o = o + _dot(pn[h * qb:(h + 1) * qb], jnp.where(head == h, v, jnp.zeros_like(v)))
        lse = _unstack_heads(jnp.broadcast_to(mx + jnp.log(den), (HEADS * qb, D_HEADS)), qb)
        o0[rows, :], o1[rows, :] = o[:, :LANE], o[:, LANE:]
        l0[rows, :], l1[rows, :] = lse[:, :LANE], lse[:, LANE:]
        return k_own, v_own

    def class_group(g, carry):
        classes = [g * n_par + t for t in range(n_par)]

        def step(j, prev):
            return tuple(x for t, r in enumerate(classes) for x in block(r, j, prev[2 * t], prev[2 * t + 1]))

        init = tuple(x for r in classes for x in (kprev[rd, r], vprev[rd, r]))
        last = lax.fori_loop(0, m, step, init, unroll=max(1, min(m, ATTN_PAR // n_par)))
        for t, r in enumerate(classes):
            kprev[wr, r], vprev[wr, r] = last[2 * t], last[2 * t + 1]
        return carry

    lax.fori_loop(0, d // n_par, class_group, 0)
    tail_ref[0, 0] = kt_ref[...].T
    tail_ref[0, 1] = vt_ref[...].T


def _attn_bias():
    qi = np.arange(HEADS * Q_BLOCK)[:, None] % Q_BLOCK
    ki = np.arange(2 * Q_BLOCK)[None, :]
    diff = Q_BLOCK + qi - ki
    valid = (diff >= 0) & (diff <= N_STEPS)
    return jnp.asarray(np.where(np.stack([valid & (ki >= Q_BLOCK), valid]), 0.0, -np.inf), F32)


def _attn_prompt(qkv, n, seq, grp):
    d = DILATIONS[grp]
    tb = min(ATTN_TB, seq)
    assert tb % (Q_BLOCK * d) == 0 and seq % tb == 0
    m, nb = tb // (Q_BLOCK * d), seq // tb
    cur = lambda c: pl.BlockSpec((tb, LANE), lambda i, b: (i * nb + b, c))
    qc, kc, vc = (2 * _qkv_col(which, grp) for which in range(3))
    bias = _attn_bias()
    keep = min(WINDOWS[grp], seq)
    tt = min(keep, 512)
    n_tiles, first = keep // tt, (seq - keep) // tt
    assert keep % tt == 0 and (seq - keep) % tt == 0 and nb >= n_tiles
    tile = lambda b: jnp.minimum(b, n_tiles - 1)
    tail_rows = lambda which: pl.BlockSpec(
        (tt, D_HEADS), lambda i, b: (i * (seq // tt) + first + tile(b), _qkv_col(which, grp)))
    *attn, tail = pl.pallas_call(
        functools.partial(_attn_prompt_kernel, d=d, m=m),
        grid=(n, nb),
        in_specs=[_const_spec(bias.shape), cur(qc), cur(qc + 1), cur(kc), cur(kc + 1), cur(vc), cur(vc + 1),
                  tail_rows(1), tail_rows(2)],
        out_specs=[pl.BlockSpec((tb, LANE), lambda i, b: (i * nb + b, 0))] * 4
        + [pl.BlockSpec((1, 2, D_HEADS, tt), lambda i, b: (i, 0, 0, tile(b)))],
        out_shape=[jax.ShapeDtypeStruct((n * seq, LANE), F32)] * 4
        + [jax.ShapeDtypeStruct((n, 2, D_HEADS, keep), F32)],
        scratch_shapes=[pltpu.VMEM((2, d, Q_BLOCK, D_HEADS), BF16)] * 2,
        compiler_params=_params(2),
        name=f"attn_prompt_d{d}",
    )(bias, *([qkv] * 8))
    return attn, tail.reshape(1, n, 2, HEADS, HEAD_DIM, keep).transpose(0, 1, 5, 2, 3, 4)


def _attn_sample_kernel(q_ref, kn_ref, vn_ref, c_ref, o0_ref, o1_ref, l0_ref, l1_ref, *, d, n_new, seq_blk):
    w = c_ref.shape[-1]
    n_rows = HEADS * n_new
    l_b = lax.broadcasted_iota(jnp.int32, (n_rows, w), 0) % n_new
    pos = lax.broadcasted_iota(jnp.int32, (n_rows, w), 1)
    valid_buf = (((w + l_b - pos) & (d - 1)) == 0) & (pos >= l_b)
    dn = (lax.broadcasted_iota(jnp.int32, (n_rows, LANE), 0) % n_new) - lax.broadcasted_iota(jnp.int32, (n_rows, LANE), 1)
    valid_new = (dn >= 0) & ((dn & (d - 1)) == 0)
    pad = jnp.zeros((LANE - n_new, D_HEADS), F32)

    def one_seq(s, carry):
        qs = _stack_heads(q_ref[s])
        k_t, v_t = c_ref[s, 0].astype(BF16), c_ref[s, 1].astype(BF16)
        k_n = jnp.concatenate([kn_ref[s], pad], axis=0).astype(BF16)
        v_n = jnp.concatenate([vn_ref[s], pad], axis=0).astype(BF16)
        s_b = jnp.where(valid_buf, _dot(qs, k_t) * ATTN_SCALE, -jnp.inf)
        s_n = jnp.where(valid_new, _dot_nt(qs, k_n) * ATTN_SCALE, -jnp.inf)
        mx = jnp.maximum(jnp.max(s_b, axis=-1, keepdims=True), jnp.max(s_n, axis=-1, keepdims=True))
        p_b, p_n = jnp.exp(s_b - mx), jnp.exp(s_n - mx)
        den = jnp.sum(p_b, axis=-1, keepdims=True) + jnp.sum(p_n, axis=-1, keepdims=True)
        o = (_dot_nt(p_b.astype(BF16), v_t) + _dot(p_n.astype(BF16), v_n)) / den
        lse = _unstack_heads(jnp.broadcast_to(mx + jnp.log(den), o.shape), n_new)
        o = _unstack_heads(o, n_new)
        o0_ref[s], o1_ref[s] = o[:, :LANE], o[:, LANE:]
        l0_ref[s], l1_ref[s] = lse[:, :LANE], lse[:, LANE:]
        return carry

    lax.fori_loop(0, seq_blk, one_seq, 0, unroll=True)


def _attn_sample_job(qkv, cache, n_seq, n_new, grp, n_steps):
    d = DILATIONS[grp]
    w = cache.shape[1]
    assert w == N_STEPS * d and n_new <= LANE and n_seq % n_steps == 0
    seq_blk = n_seq // n_steps
    qkv3 = qkv.reshape(n_seq, n_new, D_QKV)
    c_t = cache.transpose(0, 2, 3, 4, 1).reshape(n_seq, 2, D_HEADS, w)
    blk = (seq_blk, n_new, D_HEADS)
    col = lambda which: pl.BlockSpec(blk, lambda i: (i, 0, _qkv_col(which, grp)))
    out_spec = pl.BlockSpec((seq_blk, n_new, LANE), lambda i: (i, 0, 0))
    out_shape = jax.ShapeDtypeStruct((n_seq, n_new, LANE), F32)
    return _SideJob((qkv3, qkv3, qkv3, c_t),
                    (col(0), col(1), col(2), pl.BlockSpec((seq_blk, 2, D_HEADS, w), lambda i: (i, 0, 0, 0))),
                    (out_shape,) * 4, (out_spec,) * 4,
                    functools.partial(_attn_sample_kernel, d=d, n_new=n_new, seq_blk=seq_blk))


def _kv_new_job(qkv, n_seq, n_new, n_steps):
    halves, l_parts = D_HEADS // LANE, 2
    n_units, l_blk = N_GROUPS * 2 * halves * l_parts, n_new // l_parts
    assert n_steps >= n_units and n_new % l_parts == 0
    unit = lambda i: jnp.minimum(i, n_units - 1)
    l_part = lambda i: unit(i) % l_parts
    half = lambda i: unit(i) // l_parts % halves
    kv = lambda i: unit(i) // (l_parts * halves) % 2
    grp = lambda i: unit(i) // (l_parts * halves * 2)
    in_spec = pl.BlockSpec((n_seq * n_new, LANE), lambda i: (0, halves * ((1 + kv(i)) * N_GROUPS + grp(i)) + half(i)))
    out_spec = pl.BlockSpec((1, l_blk, 1, LANE, n_seq), lambda i: (grp(i), l_part(i), kv(i), half(i), 0))
    out_shape = jax.ShapeDtypeStruct((N_GROUPS, n_new, 2, D_HEADS, n_seq), F32)

    def body(x_ref, o_ref):
        l0 = l_part(pl.program_id(0)) * l_blk
        for dl in range(l_blk):
            o_ref[0, dl, 0] = x_ref[pl.ds(l0 + dl, n_seq, stride=n_new), :].T

    return _SideJob((qkv,), (in_spec,), (out_shape,), (out_spec,), body)


def _kv_new_outputs(out, n_seq, n_new):
    out = out.reshape(N_GROUPS, 1, n_new, 2, HEADS, HEAD_DIM, n_seq).transpose(0, 1, 6, 2, 3, 4, 5)
    return [out[g] for g in range(N_GROUPS)]


def _gelu_tanh(x):
    return 0.5 * x * (1.0 + jnp.tanh(math.sqrt(2.0 / math.pi) * (x + 0.044715 * (x * x * x))))


def _merge_kernel(*refs):
    x1_ref, yp_ref = refs[:2]
    attn = refs[2:2 + 4 * N_GROUPS]
    gm_ref, wgate_ref, wglu_ref, wbs_ref, wba_ref, wout_ref, x2_ref = refs[2 + 4 * N_GROUPS:]
    x1 = x1_ref[...]
    gates = jax.nn.sigmoid(_dot(_rms(x1, gm_ref[...]).astype(BF16), wgate_ref[...]))
    y = _gelu_tanh(yp_ref[...])
    glu = y * jax.nn.sigmoid(_dot(y.astype(BF16), wglu_ref[...]))
    full = lambda lo, hi: jnp.concatenate([lo[...], hi[...]], axis=1)
    outs = [full(attn[4 * g], attn[4 * g + 1]) for g in range(N_GROUPS)]
    lses = [full(attn[4 * g + 2], attn[4 * g + 3]) for g in range(N_GROUPS)]
    mx = jnp.maximum(jnp.maximum(lses[0], lses[1]), lses[2])
    es = [jnp.exp(l - mx) for l in lses]
    y_attn = (es[0] * outs[0] + es[1] * outs[1] + es[2] * outs[2]) / (es[0] + es[1] + es[2])
    merged = (gates[:, :D_MODEL] * _dot(glu.astype(BF16), wbs_ref[...])
              + gates[:, D_MODEL:] * _dot(y_attn.astype(BF16), wba_ref[...]))
    x2_ref[...] = x1 + _dot(merged.astype(BF16), wout_ref[...])


def _ffn2_kernel(x2_ref, p_ref, g2_ref, wg_ref, wu_ref, wd_ref, gp_ref, wpg_ref, wpp_ref, gf_ref, y_ref):
    x = x2_ref[...]
    x = x + 0.5 * _swiglu(_rms(x, g2_ref[...]).astype(BF16), wg_ref, wu_ref, wd_ref)
    gate = jax.nn.sigmoid(_dot(_rms(x, gp_ref[...]).astype(BF16), wpg_ref[...]))
    x = x + gate * _dot(p_ref[...].astype(BF16), wpp_ref[...])
    y_ref[...] = _rms(x, gf_ref[...])


def _token_call(body, tok_inputs, weights, jobs, name):
    t, tm = tok_inputs[0].shape[0], STAGE_TM
    tok = lambda a: pl.BlockSpec((tm, a.shape[1]), lambda i: (i, 0))
    (out,), per_job = _hosted_call(
        body, t // tm, (*tok_inputs, *weights),
        [tok(a) for a in tok_inputs] + [_const_spec(w.shape) for w in weights],
        [jax.ShapeDtypeStruct((t, D_MODEL), F32)], [pl.BlockSpec((tm, D_MODEL), lambda i: (i, 0))], jobs, name)
    return out, per_job


def _stage3(x1, y_pre, attn, p, merge_w, ffn2_w, merge_jobs=(), ffn2_jobs=()):
    x2, side_m = _token_call(_merge_kernel, (x1, y_pre, *attn), merge_w, merge_jobs, "merge")
    y, side_f = _token_call(_ffn2_kernel, (x2, p), ffn2_w, ffn2_jobs, "ffn2")
    return y, side_m, side_f


def kernel(x_prompt, x_sample, p_prompt, p_sample, cache_kv_w128, cache_kv_w512, cache_kv_w2048, state_ssm,
           g_ffn1, ffn1_w_gate, ffn1_w_up, ffn1_w_down, g_mix, w_in, ssm_a_re, ssm_a_im, ssm_log_dt,
           ssm_b_re, ssm_b_im, ssm_c_re, ssm_c_im, ssm_d, ssm_w_glu, w_br_ssm, w_br_attn, w_out,
           g_ffn2, ffn2_w_gate, ffn2_w_up, ffn2_w_down, g_ple, w_ple_gate, w_ple_proj, g_final):
    assert x_prompt.shape[-1] == D_MODEL and g_ffn1.shape[0] == 1
    n_p, seq, _ = x_prompt.shape
    n_s, n_new, _ = x_sample.shape
    caches = (cache_kv_w128, cache_kv_w512, cache_kv_w2048)
    row = lambda g: g.reshape(1, -1)

    wi = w_in[0]
    s3_f32 = [w[0] for w in (ssm_w_glu, w_br_ssm, w_br_attn, w_out, ffn2_w_gate, ffn2_w_up, ffn2_w_down,
                             w_ple_gate, w_ple_proj)]

    tabs_p, tabs_s = _ssm_tables(
        ssm_a_re[0], ssm_a_im[0], ssm_log_dt[0], ssm_b_re[0], ssm_b_im[0], ssm_c_re[0], ssm_c_im[0],
        ssm_d[0], SCAN_BLOCK, n_new)
    tile_state = (-1, N_SSM_TILES, 2, SSM_TILE_GROUPS, SSM_STATE)
    from_tiles = lambda h: h.reshape(tile_state).transpose(0, 1, 3, 4, 2).reshape(1, -1, N_SSM_GROUPS, SSM_STATE, 2)

    xs = x_sample.reshape(n_s * n_new, D_MODEL)
    x1_s, *ffn1_bf, win = _ffn1_stream(xs, row(g_ffn1[0]), ffn1_w_gate[0], ffn1_w_up[0], ffn1_w_down[0],
                                       wi, D_SSM + D_QKV)
    ffn1_w = (row(g_ffn1[0]), *ffn1_bf)
    u_s, qkv_s, _ = _proj(x1_s, row(g_mix[0]), win, tm=min(STAGE_TM, xs.shape[0]))

    xp = x_prompt.reshape(n_p * seq, D_MODEL)
    attn_job = lambda grp, n_steps: _attn_sample_job(qkv_s, caches[grp][0], n_s, n_new, grp, n_steps)
    steps_a, steps_b = xp.shape[0] // FFN1_HOST_TM, xp.shape[0] // STAGE_TM
    x1, side = _ffn1(xp, *ffn1_w, tm=FFN1_HOST_TM,
                     jobs=[attn_job(2, steps_a)] + [_cast_job(w, steps_a) for w in s3_f32]
                     + [_cast_job(wi, steps_a, col0=D_SSM + D_QKV)])
    attn_s = {2: side[0]}
    wglu, wbs, wba, wout, wg2, wu2, wd2, wpg, wpp, w_gates = (c[0] for c in side[1:])
    merge_w = (row(g_mix[0]), w_gates, wglu, wbs, wba, wout)
    ffn2_w = (row(g_ffn2[0]), wg2, wu2, wd2, row(g_ple[0]), wpg, wpp, row(g_final))
    u, qkv, side = _proj(x1, row(g_mix[0]), win, tm=STAGE_TM, jobs=[_kv_new_job(qkv_s, n_s, n_new, steps_b)])
    kv_sample = _kv_new_outputs(side[0][0], n_s, n_new)
    y_pre, h_last = _ssm_prompt(u, tabs_p, n_p, seq)
    attn_p = [_attn_prompt(qkv, n_p, seq, grp) for grp in range(N_GROUPS)]
    attn, kv_prompt = [a for per_group, _ in attn_p for a in per_group], [tail for _, tail in attn_p]
    y_prompt, side_m, side_f = _stage3(x1, y_pre, attn, p_prompt[0].reshape(n_p * seq, D_PLE), merge_w, ffn2_w,
                                       merge_jobs=[attn_job(0, steps_b)], ffn2_jobs=[attn_job(1, steps_b)])
    attn_s[0], attn_s[1] = side_m[0], side_f[0]
    y_prompt = y_prompt.reshape(n_p, seq, D_MODEL)
    ssm_prompt = from_tiles(h_last)

    h0 = state_ssm[0].reshape(n_s, N_SSM_TILES, SSM_TILE_GROUPS, SSM_STATE, 2)
    h0 = h0.transpose(0, 1, 4, 2, 3).reshape(n_s, N_SSM_TILES * TILE_STATE_W)
    y_pre, h_new = _ssm_sample(u_s, h0, tabs_s, n_s, n_new)
    attn = [a.reshape(-1, LANE) for grp in range(N_GROUPS) for a in attn_s[grp]]
    y_sample, _, _ = _stage3(x1_s, y_pre, attn, p_sample[0].reshape(n_s * n_new, D_PLE), merge_w, ffn2_w)
    y_sample = y_sample.reshape(n_s, n_new, D_MODEL)
    ssm_sample = from_tiles(h_new)

    return (y_prompt, y_sample, kv_prompt[0], kv_prompt[1], kv_prompt[2], ssm_prompt,
            kv_sample[0], kv_sample[1], kv_sample[2], ssm_sample)
```

```python
import functools
import math
from typing import Callable, NamedTuple

import jax
import jax.numpy as jnp
import numpy as np
from jax import lax
from jax.experimental import pallas as pl
from jax.experimental.pallas import tpu as pltpu

D_MODEL = 1024
D_PLE = 256
SSM_GROUP = 16
SSM_STATE = 64
D_SSM = 512
N_SSM_GROUPS = D_SSM // SSM_GROUP
HEAD_DIM = 64
HEADS = 4
WINDOWS = (128, 512, 2048)
DILATIONS = (1, 4, 16)
N_GROUPS = 3
D_HEADS = HEADS * HEAD_DIM
N_STEPS = 128
ATTN_SCALE = HEAD_DIM ** -0.5
EPS = 1e-6

LANE = 128
V7X_VMEM_LIMIT_BYTES = 56 * 1024 * 1024
STAGE_TM = 512
FFN_STREAM_CHUNK = 256
FFN1_HOST_TM = 256

CHUNK = 16
SCAN_BLOCK = 8
N_PAIRS = CHUNK // 2
PAIR_W = 2 * LANE
SSM_TILE_GROUPS = LANE // SSM_GROUP
N_SSM_TILES = N_SSM_GROUPS // SSM_TILE_GROUPS
TILE_STATE_W = SSM_TILE_GROUPS * 2 * SSM_STATE
Q_BLOCK = 128
ATTN_TB = 2048
ATTN_PAR = 32

D_QKV = 3 * N_GROUPS * D_HEADS


def _qkv_col(which, grp):
    return which * N_GROUPS + grp


BF16 = jnp.bfloat16
F32 = jnp.float32


def _dot(a, b):
    return jnp.dot(a, b, preferred_element_type=F32)


def _dot_nt(a, b):
    return lax.dot_general(a, b, (((1,), (1,)), ((), ())), preferred_element_type=F32)


def _rms(x, g):
    return x * lax.rsqrt(jnp.mean(x * x, axis=-1, keepdims=True) + EPS) * g


def _swiglu(xn, wg_ref, wu_ref, wd_ref):
    gate = _dot(xn, wg_ref[...])
    up = _dot(xn, wu_ref[...])
    act = (gate * jax.nn.sigmoid(gate) * up).astype(BF16)
    return _dot(act, wd_ref[...])


def _const_spec(shape):
    nd = len(shape)
    return pl.BlockSpec(shape, lambda *_: (0,) * nd, pipeline_mode=pl.Buffered(1))


def _params(n_grid_dims):
    return pltpu.CompilerParams(dimension_semantics=("arbitrary",) * n_grid_dims,
                                vmem_limit_bytes=V7X_VMEM_LIMIT_BYTES)


class _SideJob(NamedTuple):
    arrays: tuple
    in_specs: tuple
    out_shapes: tuple
    out_specs: tuple
    body: Callable


def _cast_job(w, n_steps, col0=0, n_cols=None):
    rows = w.shape[0]
    n_cols = w.shape[1] - col0 if n_cols is None else n_cols
    cw = n_cols if col0 == 0 else math.gcd(col0, n_cols)
    c_blks = n_cols // cw
    assert cw % LANE == 0 and n_steps >= c_blks
    r_blks = max(s for s in range(1, n_steps // c_blks + 1) if rows % s == 0 and (rows // s) % 16 == 0)
    unit = lambda i: jnp.minimum(i, r_blks * c_blks - 1)
    blk = (rows // r_blks, cw)
    src_spec = pl.BlockSpec(blk, lambda i: (unit(i) // c_blks, col0 // cw + unit(i) % c_blks))
    dst_spec = pl.BlockSpec(blk, lambda i: (unit(i) // c_blks, unit(i) % c_blks))

    def body(src, dst):
        dst[...] = src[...].astype(BF16)

    return _SideJob((w,), (src_spec,), (jax.ShapeDtypeStruct((rows, n_cols), BF16),), (dst_spec,), body)


def _hosted_call(main_body, n_steps, arrays, in_specs, out_shapes, out_specs, jobs, name):
    n_in, n_out = len(arrays), len(out_shapes)

    def body(*refs):
        pos, job_in, job_out = n_in, [], []
        for job in jobs:
            job_in.append(refs[pos:pos + len(job.arrays)])
            pos += len(job.arrays)
        main_out = refs[pos:pos + n_out]
        pos += n_out
        for job in jobs:
            job_out.append(refs[pos:pos + len(job.out_shapes)])
            pos += len(job.out_shapes)
        for job, ins, outs in zip(jobs, job_in, job_out):
            job.body(*ins, *outs)
        main_body(*refs[:n_in], *main_out)

    outs = pl.pallas_call(
        body,
        grid=(n_steps,),
        in_specs=list(in_specs) + [sp for job in jobs for sp in job.in_specs],
        out_specs=list(out_specs) + [sp for job in jobs for sp in job.out_specs],
        out_shape=list(out_shapes) + [sh for job in jobs for sh in job.out_shapes],
        compiler_params=_params(1),
        name=name,
    )(*arrays, *[a for job in jobs for a in job.arrays])
    main, pos, per_job = outs[:n_out], n_out, []
    for job in jobs:
        per_job.append(outs[pos:pos + len(job.out_shapes)])
        pos += len(job.out_shapes)
    return main, per_job


def _ffn1_kernel(x_ref, g1_ref, wg_ref, wu_ref, wd_ref, x1_ref):
    x = x_ref[...]
    x1_ref[...] = x + 0.5 * _swiglu(_rms(x, g1_ref[...]).astype(BF16), wg_ref, wu_ref, wd_ref)


def _ffn1(x, g1, wg, wu, wd, tm, jobs=()):
    t = x.shape[0]
    tok = pl.BlockSpec((tm, D_MODEL), lambda i: (i, 0))
    (x1,), per_job = _hosted_call(
        _ffn1_kernel, t // tm, (x, g1, wg, wu, wd),
        [tok] + [_const_spec(a.shape) for a in (g1, wg, wu, wd)],
        [jax.ShapeDtypeStruct((t, D_MODEL), F32)], [tok], jobs, "ffn1")
    return x1, per_job


def _ffn1_stream_kernel(x_ref, g1_ref, wg_ref, wu_ref, wd_ref, wi_ref,
                        x1_ref, wg_bf_ref, wu_bf_ref, wd_bf_ref, wi_bf_ref, xn_ref, acc_ref):
    c = pl.program_id(0)

    @pl.when(c == 0)
    def _():
        xn_ref[...] = _rms(x_ref[...], g1_ref[...]).astype(BF16)
        acc_ref[...] = jnp.zeros_like(acc_ref)

    wg, wu, wd = wg_ref[...].astype(BF16), wu_ref[...].astype(BF16), wd_ref[...].astype(BF16)
    wg_bf_ref[...], wu_bf_ref[...], wd_bf_ref[...] = wg, wu, wd
    wi_bf_ref[...] = wi_ref[...].astype(BF16)
    xn = xn_ref[...]
    gate = _dot(xn, wg)
    act = (gate * jax.nn.sigmoid(gate) * _dot(xn, wu)).astype(BF16)
    acc_ref[...] += _dot(act, wd)

    @pl.when(c == pl.num_programs(0) - 1)
    def _():
        x1_ref[...] = x_ref[...] + 0.5 * acc_ref[...]


def _ffn1_stream(x, g1, wg, wu, wd, w_in, n_in_cols):
    t, d_ff = x.shape[0], wg.shape[1]
    ck = FFN_STREAM_CHUNK
    assert d_ff % ck == 0 and n_in_cols <= d_ff and n_in_cols % ck == 0
    in_cols = pl.BlockSpec((D_MODEL, ck), lambda c: (0, jnp.minimum(c, n_in_cols // ck - 1)))
    whole = lambda a: pl.BlockSpec(a.shape, lambda c: (0,) * a.ndim, pipeline_mode=pl.Buffered(1))
    cols = pl.BlockSpec((D_MODEL, ck), lambda c: (0, c))
    rows = pl.BlockSpec((ck, D_MODEL), lambda c: (c, 0))
    return pl.pallas_call(
        _ffn1_stream_kernel,
        grid=(d_ff // ck,),
        in_specs=[whole(x), whole(g1), cols, cols, rows, in_cols],
        out_specs=[pl.BlockSpec(x.shape, lambda c: (0, 0)), cols, cols, rows, in_cols],
        out_shape=[jax.ShapeDtypeStruct(x.shape, F32), jax.ShapeDtypeStruct(wg.shape, BF16),
                   jax.ShapeDtypeStruct(wu.shape, BF16), jax.ShapeDtypeStruct(wd.shape, BF16),
                   jax.ShapeDtypeStruct((D_MODEL, n_in_cols), BF16)],
        scratch_shapes=[pltpu.VMEM(x.shape, BF16), pltpu.VMEM(x.shape, F32)],
        compiler_params=_params(1),
        name="ffn1_stream",
    )(x, g1, wg, wu, wd, w_in)


def _proj_kernel(x1_ref, gm_ref, win_ref, u_ref, qkv_ref):
    z = _dot(_rms(x1_ref[...], gm_ref[...]).astype(BF16), win_ref[...])
    u_ref[...] = z[:, :D_SSM]
    qkv_ref[...] = z[:, D_SSM:]


def _proj(x1, gm, win, tm, jobs=()):
    t = x1.shape[0]
    tok = lambda w: pl.BlockSpec((tm, w), lambda i: (i, 0))
    (u, qkv), per_job = _hosted_call(
        _proj_kernel, t // tm, (x1, gm, win),
        [tok(D_MODEL), _const_spec(gm.shape), _const_spec(win.shape)],
        [jax.ShapeDtypeStruct((t, D_SSM), F32), jax.ShapeDtypeStruct((t, D_QKV), F32)], [tok(D_SSM), tok(D_QKV)],
        jobs, "proj")
    return u, qkv, per_job


def _swap_halves(x):
    half = x.shape[1] // 2
    return jnp.concatenate([x[:, half:], x[:, :half]], axis=1)


def _cmul_split(x, a1, a2):
    return x * a1 + _swap_halves(x) * a2


def _ssm_kernel(*refs, n_t, n_rows, has_h0):
    if has_h0:
        u_ref, h0_ref, kp_ref, wp_ref, vpt_ref, a1_ref, a2_ref, y_ref, hl_ref = refs
    else:
        u_ref, kp_ref, wp_ref, vpt_ref, a1_ref, a2_ref, y_ref, hl_ref = refs
    n_pairs = n_t // 2
    tok = lambda t: pl.ds(t, n_rows, stride=n_t)
    up = [jnp.concatenate([u_ref[tok(2 * a), :], u_ref[tok(2 * a + 1), :]], axis=1).astype(BF16)
          for a in range(n_pairs)]
    s = _dot(jnp.concatenate(up, axis=1), wp_ref[0])
    in_chunk = [None] * n_pairs
    todo = [(b, a) for b in range(n_pairs) for a in range(b + 1)]

    def in_chunk_products(count):
        for b, a in [todo.pop(0) for _ in range(min(count, len(todo)))]:
            prod = _dot(up[a], kp_ref[0, b - a])
            in_chunk[b] = prod if in_chunk[b] is None else in_chunk[b] + prod

    if has_h0:
        hprev = h0_ref[...]
        h_last = _cmul_split(hprev, a1_ref[0], a2_ref[0]) + s
        hl_ref[...] = h_last
    else:
        blk = SCAN_BLOCK
        assert n_rows % blk == 0 and a1_ref.shape[1] == blk
        row = lax.broadcasted_iota(jnp.int32, (blk, s.shape[1]), 0)
        carry = jnp.zeros((1, s.shape[1]), F32)
        entering = []
        for i in range(n_rows // blk):
            h = s[i * blk:(i + 1) * blk]
            for lvl in range(int(math.log2(blk))):
                sh = 1 << lvl
                shifted = jnp.where(row >= sh, pltpu.roll(h, sh, axis=0), 0.0)
                h = h + _cmul_split(shifted, a1_ref[0, sh - 1:sh, :], a2_ref[0, sh - 1:sh, :])
            carry_b = jnp.broadcast_to(carry, h.shape)
            h = h + _cmul_split(carry_b, a1_ref[0], a2_ref[0])
            entering.append(jnp.where(row == 0, carry_b, pltpu.roll(h, 1, axis=0)))
            carry = h[blk - 1:blk]
            in_chunk_products(-(-len(todo) // (n_rows // blk - i)))
        hprev = jnp.concatenate(entering, axis=0)
        hl_ref[0, 0] = carry
    in_chunk_products(len(todo))
    hpb = hprev.astype(BF16)
    for b in range(n_pairs):
        acc = _dot_nt(hpb, vpt_ref[0, b]) + in_chunk[b]
        y_ref[tok(2 * b), :] = acc[:, :LANE]
        y_ref[tok(2 * b + 1), :] = acc[:, LANE:]


def _ssm_table_specs(tabs, idx, n_t):
    kp, wp, vp, a1, a2 = tabs
    n_pairs, w_rows = n_t // 2, n_t * LANE
    return [pl.BlockSpec((1, n_pairs) + kp.shape[2:], lambda *g: (idx(*g), 0, 0, 0)),
            pl.BlockSpec((1, w_rows, wp.shape[2]), lambda *g: (idx(*g), wp.shape[1] // w_rows - 1, 0)),
            pl.BlockSpec((1, n_pairs) + vp.shape[2:], lambda *g: (idx(*g), 0, 0, 0)),
            pl.BlockSpec((1,) + a1.shape[1:], lambda *g: (idx(*g), 0, 0)),
            pl.BlockSpec((1,) + a2.shape[1:], lambda *g: (idx(*g), 0, 0))]


def _ssm_prompt(u, tabs, n, seq):
    n_rows = seq // CHUNK
    tile = pl.BlockSpec((seq, LANE), lambda j, i: (i, j))
    return pl.pallas_call(
        functools.partial(_ssm_kernel, n_t=CHUNK, n_rows=n_rows, has_h0=False),
        grid=(N_SSM_TILES, n),
        in_specs=[tile] + _ssm_table_specs(tabs, lambda j, i: j, CHUNK),
        out_specs=[tile, pl.BlockSpec((1, 1, 1, TILE_STATE_W), lambda j, i: (i, j, 0, 0))],
        out_shape=[jax.ShapeDtypeStruct((n * seq, D_SSM), F32),
                   jax.ShapeDtypeStruct((n, N_SSM_TILES, 1, TILE_STATE_W), F32)],
        compiler_params=_params(2),
        name="ssm_prompt",
    )(u, *tabs)


def _ssm_sample(u, h0, tabs, n_seq, n_new):
    tile = pl.BlockSpec((n_seq * n_new, LANE), lambda j: (0, j))
    state = pl.BlockSpec((n_seq, TILE_STATE_W), lambda j: (0, j))
    return pl.pallas_call(
        functools.partial(_ssm_kernel, n_t=n_new, n_rows=n_seq, has_h0=True),
        grid=(N_SSM_TILES,),
        in_specs=[tile, state] + _ssm_table_specs(tabs, lambda j: j, n_new),
        out_specs=[tile, state],
        out_shape=[jax.ShapeDtypeStruct((n_seq * n_new, D_SSM), F32),
                   jax.ShapeDtypeStruct((n_seq, N_SSM_TILES * TILE_STATE_W), F32)],
        compiler_params=_params(1),
        name="ssm_sample",
    )(u, h0, *tabs)


def _ssm_prep_kernel(are_ref, aim_ref, ldt_ref, bre_ref, bim_ref, cre_ref, cim_ref, d_ref,
                     kp_ref, wp_ref, vpt_ref, a1_ref, a2_ref, a1s_ref, a2s_ref, *, n_powers, n_new):
    a_re, a_im = are_ref[0], aim_ref[0]
    dt = jnp.exp(ldt_ref[0])
    mag = jnp.exp(a_re * dt)
    ab_re, ab_im = mag * jnp.cos(a_im * dt), mag * jnp.sin(a_im * dt)
    den = a_re * a_re + a_im * a_im
    nr, ni = ab_re - 1.0, ab_im
    f_re = (nr * a_re + ni * a_im) / den
    f_im = (ni * a_re - nr * a_im) / den
    shape = bre_ref.shape[1:]
    same_group = (lax.broadcasted_iota(jnp.int32, shape, 0) // SSM_GROUP
                  == lax.broadcasted_iota(jnp.int32, shape, 1) // SSM_STATE)
    b_re, b_im = bre_ref[0], bim_ref[0]
    x_re = jnp.where(same_group, f_re * b_re - f_im * b_im, 0.0)
    x_im = jnp.where(same_group, f_re * b_im + f_im * b_re, 0.0)
    c_re = jnp.where(same_group, cre_ref[0], 0.0)
    c_im = jnp.where(same_group, cim_ref[0], 0.0)
    pw = [(jnp.ones_like(ab_re), jnp.zeros_like(ab_im))]
    for _ in range(CHUNK):
        r, i = pw[-1]
        pw.append((r * ab_re - i * ab_im, r * ab_im + i * ab_re))
    cat = lambda r, i: jnp.concatenate([r, i], axis=1)
    e_pack = [cat(c_re * r - c_im * i, -(c_re * i + c_im * r)) for r, i in pw]
    split = lambda a: (a.astype(BF16), (a - a.astype(BF16).astype(F32)).astype(BF16))
    x_hi, x_lo = split(cat(x_re, x_im))
    x_3 = jnp.concatenate([x_hi, x_hi, x_lo], axis=1)
    k_lag = []
    for k in range(CHUNK):
        e_hi, e_lo = split(e_pack[k])
        k_lag.append(_dot_nt(x_3, jnp.concatenate([e_hi, e_lo, e_hi], axis=1)))
    diag = (lax.broadcasted_iota(jnp.int32, (LANE, LANE), 0) == lax.broadcasted_iota(jnp.int32, (LANE, LANE), 1))
    k_lag[0] = k_lag[0] + jnp.where(diag, d_ref[0], 0.0)
    zero = jnp.zeros((LANE, LANE), F32)
    for dl in range(N_PAIRS):
        top = cat(k_lag[2 * dl], k_lag[2 * dl + 1])
        bot = cat(k_lag[2 * dl - 1] if dl > 0 else zero, k_lag[2 * dl])
        kp_ref[0, dl] = jnp.concatenate([top, bot], axis=0).astype(BF16)
    for t in range(CHUNK):
        r, i = pw[CHUNK - 1 - t]
        wp_ref[0, t * LANE:(t + 1) * LANE, :] = cat(x_re * r - x_im * i, x_re * i + x_im * r).astype(BF16)
    for t in range(CHUNK):
        vpt_ref[0, t // 2, (t % 2) * LANE:(t % 2 + 1) * LANE, :] = e_pack[t + 1].astype(BF16)
    r, i = step_r, step_i = pw[CHUNK]
    for s in range(n_powers):
        a1_ref[0, s:s + 1, :], a2_ref[0, s:s + 1, :] = cat(r, r), cat(-i, i)
        r, i = r * step_r - i * step_i, r * step_i + i * step_r
    r, i = pw[n_new]
    a1s_ref[0], a2s_ref[0] = cat(r, r), cat(-i, i)


def _ssm_tables(a_re, a_im, log_dt, b_re, b_im, c_re, c_im, d_skip, n_powers, n_new):
    nj, tg = N_SSM_TILES, SSM_TILE_GROUPS
    half = tg * SSM_STATE
    rowv = lambda v: v.reshape(nj, 1, half)
    tiled = lambda m: jnp.tile(m.reshape(nj, LANE, SSM_STATE), (1, 1, tg))
    ins = (rowv(a_re), rowv(a_im), rowv(jnp.repeat(log_dt, SSM_STATE)),
           tiled(b_re.transpose(0, 2, 1)), tiled(b_im.transpose(0, 2, 1)), tiled(c_re), tiled(c_im),
           d_skip.reshape(nj, 1, LANE))
    shapes = ((nj, N_PAIRS, PAIR_W, PAIR_W), (nj, CHUNK * LANE, TILE_STATE_W), (nj, N_PAIRS, PAIR_W, TILE_STATE_W),
              (nj, n_powers, TILE_STATE_W), (nj, n_powers, TILE_STATE_W), (nj, 1, TILE_STATE_W), (nj, 1, TILE_STATE_W))
    dtypes = (BF16, BF16, BF16, F32, F32, F32, F32)
    per_tile = lambda s: pl.BlockSpec((1,) + tuple(s[1:]), lambda j: (j,) + (0,) * (len(s) - 1))
    kp, wp, vpt, a1, a2, a1s, a2s = pl.pallas_call(
        functools.partial(_ssm_prep_kernel, n_powers=n_powers, n_new=n_new),
        grid=(nj,),
        in_specs=[per_tile(x.shape) for x in ins],
        out_specs=[per_tile(s) for s in shapes],
        out_shape=[jax.ShapeDtypeStruct(s, dt) for s, dt in zip(shapes, dtypes)],
        compiler_params=_params(1),
        name="ssm_prep",
    )(*ins)
    return (kp, wp, vpt, a1, a2), (kp, wp, vpt, a1s, a2s)


def _head_of_lane(width):
    return lax.broadcasted_iota(jnp.int32, (1, width), 1) // HEAD_DIM


def _stack_heads(q):
    head = _head_of_lane(q.shape[1])
    q = q.astype(F32)
    return jnp.concatenate([jnp.where(head == h, q, 0.0) for h in range(HEADS)], axis=0).astype(BF16)


def _unstack_heads(x, rows):
    head = _head_of_lane(x.shape[1])
    out = x[(HEADS - 1) * rows:]
    for h in range(HEADS - 2, -1, -1):
        out = jnp.where(head == h, x[h * rows:(h + 1) * rows], out)
    return out


def _attn_prompt_kernel(bias_ref, q0, q1, k0, k1, v0, v1, kt_ref, vt_ref, o0, o1, l0, l1, tail_ref, kprev, vprev, *, d, m):
    qb = Q_BLOCK
    b = pl.program_id(1)
    n_par = min(d, ATTN_PAR)
    rd, wr = b % 2, (b + 1) % 2
    head = _head_of_lane(D_HEADS)

    @pl.when(b == 0)
    def _():
        kprev[0] = jnp.zeros(kprev.shape[1:], kprev.dtype)
        vprev[0] = jnp.zeros(vprev.shape[1:], vprev.dtype)

    def block(r, j, k_prev, v_prev):
        start = r + j * qb * d
        rows = pl.ds(start, qb, stride=d) if d > 1 else pl.ds(pl.multiple_of(start, qb), qb)
        both = lambda lo, hi: jnp.concatenate([lo[rows, :], hi[rows, :]], axis=1)
        qs = _stack_heads(both(q0, q1) * ATTN_SCALE)
        k_own, v_own = both(k0, k1).astype(BF16), both(v0, v1).astype(BF16)
        k = jnp.concatenate([k_prev, k_own], axis=0)
        v = jnp.concatenate([v_prev, v_own], axis=0)
        s = _dot_nt(qs, k) + bias_ref[jnp.minimum(b * m + j, 1)]
        mx = jnp.max(s, axis=-1, keepdims=True)
        pe = jnp.exp(s - mx)
        den = jnp.sum(pe, axis=-1, keepdims=True)
        pn = (pe * (1.0 / den)).astype(BF16)
        o = _dot(pn[:qb], jnp.where(head == 0, v, jnp.zeros_like(v)))
        for h in range(1, HEADS):
            o = o + _dot(pn[h * qb:(h + 1) * qb], jnp.where(head == h, v, jnp.zeros_like(v)))
        lse = _unstack_heads(jnp.broadcast_to(mx + jnp.log(den), (HEADS * qb, D_HEADS)), qb)
        o0[rows, :], o1[rows, :] = o[:, :LANE], o[:, LANE:]
        l0[rows, :], l1[rows, :] = lse[:, :LANE], lse[:, LANE:]
        return k_own, v_own

    def class_group(g, carry):
        classes = [g * n_par + t for t in range(n_par)]

        def step(j, prev):
            return tuple(x for t, r in enumerate(classes) for x in block(r, j, prev[2 * t], prev[2 * t + 1]))

        init = tuple(x for r in classes for x in (kprev[rd, r], vprev[rd, r]))
        last = lax.fori_loop(0, m, step, init, unroll=max(1, min(m, ATTN_PAR // n_par)))
        for t, r in enumerate(classes):
            kprev[wr, r], vprev[wr, r] = last[2 * t], last[2 * t + 1]
        return carry

    lax.fori_loop(0, d // n_par, class_group, 0)
    tail_ref[0, 0] = kt_ref[...].T
    tail_ref[0, 1] = vt_ref[...].T


def _attn_bias():
    qi = np.arange(HEADS * Q_BLOCK)[:, None] % Q_BLOCK
    ki = np.arange(2 * Q_BLOCK)[None, :]
    diff = Q_BLOCK + qi - ki
    valid = (diff >= 0) & (diff <= N_STEPS)
    return jnp.asarray(np.where(np.stack([valid & (ki >= Q_BLOCK), valid]), 0.0, -np.inf), F32)


def _attn_prompt(qkv, n, seq, grp):
    d = DILATIONS[grp]
    tb = min(ATTN_TB * (2 if d < max(DILATIONS) else 1), seq)
    assert tb % (Q_BLOCK * d) == 0 and seq % tb == 0
    m, nb = tb // (Q_BLOCK * d), seq // tb
    cur = lambda c: pl.BlockSpec((tb, LANE), lambda i, b: (i * nb + b, c))
    qc, kc, vc = (2 * _qkv_col(which, grp) for which in range(3))
    bias = _attn_bias()
    keep = min(WINDOWS[grp], seq)
    tt = min(keep, 512)
    n_tiles, first = keep // tt, (seq - keep) // tt
    assert keep % tt == 0 and (seq - keep) % tt == 0 and nb >= n_tiles
    tile = lambda b: jnp.minimum(b, n_tiles - 1)
    tail_rows = lambda which: pl.BlockSpec(
        (tt, D_HEADS), lambda i, b: (i * (seq // tt) + first + tile(b), _qkv_col(which, grp)))
    *attn, tail = pl.pallas_call(
        functools.partial(_attn_prompt_kernel, d=d, m=m),
        grid=(n, nb),
        in_specs=[_const_spec(bias.shape), cur(qc), cur(qc + 1), cur(kc), cur(kc + 1), cur(vc), cur(vc + 1),
                  tail_rows(1), tail_rows(2)],
        out_specs=[pl.BlockSpec((tb, LANE), lambda i, b: (i * nb + b, 0))] * 4
        + [pl.BlockSpec((1, 2, D_HEADS, tt), lambda i, b: (i, 0, 0, tile(b)))],
        out_shape=[jax.ShapeDtypeStruct((n * seq, LANE), F32)] * 4
        + [jax.ShapeDtypeStruct((n, 2, D_HEADS, keep), F32)],
        scratch_shapes=[pltpu.VMEM((2, d, Q_BLOCK, D_HEADS), BF16)] * 2,
        compiler_params=_params(2),
        name=f"attn_prompt_d{d}",
    )(bias, *([qkv] * 8))
    return attn, tail.reshape(1, n, 2, HEADS, HEAD_DIM, keep).transpose(0, 1, 5, 2, 3, 4)


def _attn_sample_kernel(q_ref, kn_ref, vn_ref, c_ref, o0_ref, o1_ref, l0_ref, l1_ref, *, d, n_new, seq_blk):
    w = c_ref.shape[-1]
    n_rows = HEADS * n_new
    l_b = lax.broadcasted_iota(jnp.int32, (n_rows, w), 0) % n_new
    pos = lax.broadcasted_iota(jnp.int32, (n_rows, w), 1)
    valid_buf = (((w + l_b - pos) & (d - 1)) == 0) & (pos >= l_b)
    dn = (lax.broadcasted_iota(jnp.int32, (n_rows, LANE), 0) % n_new) - lax.broadcasted_iota(jnp.int32, (n_rows, LANE), 1)
    valid_new = (dn >= 0) & ((dn & (d - 1)) == 0)
    pad = jnp.zeros((LANE - n_new, D_HEADS), F32)

    def one_seq(s, carry):
        qs = _stack_heads(q_ref[s])
        k_t, v_t = c_ref[s, 0].astype(BF16), c_ref[s, 1].astype(BF16)
        k_n = jnp.concatenate([kn_ref[s], pad], axis=0).astype(BF16)
        v_n = jnp.concatenate([vn_ref[s], pad], axis=0).astype(BF16)
        s_b = jnp.where(valid_buf, _dot(qs, k_t) * ATTN_SCALE, -jnp.inf)
        s_n = jnp.where(valid_new, _dot_nt(qs, k_n) * ATTN_SCALE, -jnp.inf)
        mx = jnp.maximum(jnp.max(s_b, axis=-1, keepdims=True), jnp.max(s_n, axis=-1, keepdims=True))
        p_b, p_n = jnp.exp(s_b - mx), jnp.exp(s_n - mx)
        den = jnp.sum(p_b, axis=-1, keepdims=True) + jnp.sum(p_n, axis=-1, keepdims=True)
        o = (_dot_nt(p_b.astype(BF16), v_t) + _dot(p_n.astype(BF16), v_n)) / den
        lse = _unstack_heads(jnp.broadcast_to(mx + jnp.log(den), o.shape), n_new)
        o = _unstack_heads(o, n_new)
        o0_ref[s], o1_ref[s] = o[:, :LANE], o[:, LANE:]
        l0_ref[s], l1_ref[s] = lse[:, :LANE], lse[:, LANE:]
        return carry

    lax.fori_loop(0, seq_blk, one_seq, 0, unroll=True)


def _attn_sample_job(qkv, cache, n_seq, n_new, grp, n_steps):
    d = DILATIONS[grp]
    w = cache.shape[1]
    assert w == N_STEPS * d and n_new <= LANE and n_seq % n_steps == 0
    seq_blk = n_seq // n_steps
    qkv3 = qkv.reshape(n_seq, n_new, D_QKV)
    c_t = cache.transpose(0, 2, 3, 4, 1).reshape(n_seq, 2, D_HEADS, w)
    blk = (seq_blk, n_new, D_HEADS)
    col = lambda which: pl.BlockSpec(blk, lambda i: (i, 0, _qkv_col(which, grp)))
    out_spec = pl.BlockSpec((seq_blk, n_new, LANE), lambda i: (i, 0, 0))
    out_shape = jax.ShapeDtypeStruct((n_seq, n_new, LANE), F32)
    return _SideJob((qkv3, qkv3, qkv3, c_t),
                    (col(0), col(1), col(2), pl.BlockSpec((seq_blk, 2, D_HEADS, w), lambda i: (i, 0, 0, 0))),
                    (out_shape,) * 4, (out_spec,) * 4,
                    functools.partial(_attn_sample_kernel, d=d, n_new=n_new, seq_blk=seq_blk))


def _kv_new_job(qkv, n_seq, n_new, n_steps):
    halves, l_parts = D_HEADS // LANE, 2
    n_units, l_blk = N_GROUPS * 2 * halves * l_parts, n_new // l_parts
    assert n_steps >= n_units and n_new % l_parts == 0
    unit = lambda i: jnp.minimum(i, n_units - 1)
    l_part = lambda i: unit(i) % l_parts
    half = lambda i: unit(i) // l_parts % halves
    kv = lambda i: unit(i) // (l_parts * halves) % 2
    grp = lambda i: unit(i) // (l_parts * halves * 2)
    in_spec = pl.BlockSpec((n_seq * n_new, LANE), lambda i: (0, halves * ((1 + kv(i)) * N_GROUPS + grp(i)) + half(i)))
    out_spec = pl.BlockSpec((1, l_blk, 1, LANE, n_seq), lambda i: (grp(i), l_part(i), kv(i), half(i), 0))
    out_shape = jax.ShapeDtypeStruct((N_GROUPS, n_new, 2, D_HEADS, n_seq), F32)

    def body(x_ref, o_ref):
        l0 = l_part(pl.program_id(0)) * l_blk
        for dl in range(l_blk):
            o_ref[0, dl, 0] = x_ref[pl.ds(l0 + dl, n_seq, stride=n_new), :].T

    return _SideJob((qkv,), (in_spec,), (out_shape,), (out_spec,), body)


def _kv_new_outputs(out, n_seq, n_new):
    out = out.reshape(N_GROUPS, 1, n_new, 2, HEADS, HEAD_DIM, n_seq).transpose(0, 1, 6, 2, 3, 4, 5)
    return [out[g] for g in range(N_GROUPS)]


def _gelu_tanh(x):
    return 0.5 * x * (1.0 + jnp.tanh(math.sqrt(2.0 / math.pi) * (x + 0.044715 * (x * x * x))))


def _stage3_kernel(*refs):
    x1_ref, yp_ref = refs[:2]
    attn = refs[2:2 + 4 * N_GROUPS]
    (p_ref, gm_ref, wgate_ref, wglu_ref, wbs_ref, wba_ref, wout_ref, g2_ref, wg_ref, wu_ref, wd_ref,
     gp_ref, wpg_ref, wpp_ref, gf_ref, y_ref) = refs[2 + 4 * N_GROUPS:]
    x1 = x1_ref[...]
    gates = jax.nn.sigmoid(_dot(_rms(x1, gm_ref[...]).astype(BF16), wgate_ref[...]))
    y = _gelu_tanh(yp_ref[...])
    glu = y * jax.nn.sigmoid(_dot(y.astype(BF16), wglu_ref[...]))
    full = lambda lo, hi: jnp.concatenate([lo[...], hi[...]], axis=1)
    outs = [full(attn[4 * g], attn[4 * g + 1]) for g in range(N_GROUPS)]
    lses = [full(attn[4 * g + 2], attn[4 * g + 3]) for g in range(N_GROUPS)]
    mx = jnp.maximum(jnp.maximum(lses[0], lses[1]), lses[2])
    es = [jnp.exp(l - mx) for l in lses]
    y_attn = (es[0] * outs[0] + es[1] * outs[1] + es[2] * outs[2]) / (es[0] + es[1] + es[2])
    merged = (gates[:, :D_MODEL] * _dot(glu.astype(BF16), wbs_ref[...])
              + gates[:, D_MODEL:] * _dot(y_attn.astype(BF16), wba_ref[...]))
    x = x1 + _dot(merged.astype(BF16), wout_ref[...])
    x = x + 0.5 * _swiglu(_rms(x, g2_ref[...]).astype(BF16), wg_ref, wu_ref, wd_ref)
    gate = jax.nn.sigmoid(_dot(_rms(x, gp_ref[...]).astype(BF16), wpg_ref[...]))
    x = x + gate * _dot(p_ref[...].astype(BF16), wpp_ref[...])
    y_ref[...] = _rms(x, gf_ref[...])


def _stage3(tok_inputs, weights):
    t, tm = tok_inputs[0].shape[0], STAGE_TM
    tok = lambda a: pl.BlockSpec((tm, a.shape[1]), lambda i: (i, 0))
    return pl.pallas_call(
        _stage3_kernel,
        grid=(t // tm,),
        in_specs=[tok(a) for a in tok_inputs] + [_const_spec(w.shape) for w in weights],
        out_specs=pl.BlockSpec((tm, D_MODEL), lambda i: (i, 0)),
        out_shape=jax.ShapeDtypeStruct((t, D_MODEL), F32),
        compiler_params=_params(1),
        name="stage3",
    )(*tok_inputs, *weights)


def kernel(x_prompt, x_sample, p_prompt, p_sample, cache_kv_w128, cache_kv_w512, cache_kv_w2048, state_ssm,
           g_ffn1, ffn1_w_gate, ffn1_w_up, ffn1_w_down, g_mix, w_in, ssm_a_re, ssm_a_im, ssm_log_dt,
           ssm_b_re, ssm_b_im, ssm_c_re, ssm_c_im, ssm_d, ssm_w_glu, w_br_ssm, w_br_attn, w_out,
           g_ffn2, ffn2_w_gate, ffn2_w_up, ffn2_w_down, g_ple, w_ple_gate, w_ple_proj, g_final):
    assert x_prompt.shape[-1] == D_MODEL and g_ffn1.shape[0] == 1
    n_p, seq, _ = x_prompt.shape
    n_s, n_new, _ = x_sample.shape
    caches = (cache_kv_w128, cache_kv_w512, cache_kv_w2048)
    row = lambda g: g.reshape(1, -1)

    wi = w_in[0]
    s3_f32 = [w[0] for w in (ssm_w_glu, w_br_ssm, w_br_attn, w_out, ffn2_w_gate, ffn2_w_up, ffn2_w_down,
                             w_ple_gate, w_ple_proj)]

    tabs_p, tabs_s = _ssm_tables(
        ssm_a_re[0], ssm_a_im[0], ssm_log_dt[0], ssm_b_re[0], ssm_b_im[0], ssm_c_re[0], ssm_c_im[0],
        ssm_d[0], SCAN_BLOCK, n_new)
    tile_state = (-1, N_SSM_TILES, 2, SSM_TILE_GROUPS, SSM_STATE)
    from_tiles = lambda h: h.reshape(tile_state).transpose(0, 1, 3, 4, 2).reshape(1, -1, N_SSM_GROUPS, SSM_STATE, 2)

    xs = x_sample.reshape(n_s * n_new, D_MODEL)
    x1_s, *ffn1_bf, win = _ffn1_stream(xs, row(g_ffn1[0]), ffn1_w_gate[0], ffn1_w_up[0], ffn1_w_down[0],
                                       wi, D_SSM + D_QKV)
    ffn1_w = (row(g_ffn1[0]), *ffn1_bf)
    u_s, qkv_s, _ = _proj(x1_s, row(g_mix[0]), win, tm=min(STAGE_TM, xs.shape[0]))

    xp = x_prompt.reshape(n_p * seq, D_MODEL)
    attn_job = lambda grp, n_steps: _attn_sample_job(qkv_s, caches[grp][0], n_s, n_new, grp, n_steps)
    steps_a, steps_b = xp.shape[0] // FFN1_HOST_TM, xp.shape[0] // STAGE_TM
    x1, side = _ffn1(xp, *ffn1_w, tm=FFN1_HOST_TM,
                     jobs=[attn_job(2, steps_a), attn_job(0, steps_a)] + [_cast_job(w, steps_a) for w in s3_f32]
                     + [_cast_job(wi, steps_a, col0=D_SSM + D_QKV)])
    attn_s = {2: side[0], 0: side[1]}
    wglu, wbs, wba, wout, wg2, wu2, wd2, wpg, wpp, w_gates = (c[0] for c in side[2:])
    s3_w = (row(g_mix[0]), w_gates, wglu, wbs, wba, wout, row(g_ffn2[0]), wg2, wu2, wd2, row(g_ple[0]), wpg, wpp,
            row(g_final))
    u, qkv, side = _proj(x1, row(g_mix[0]), win, tm=STAGE_TM,
                         jobs=[attn_job(1, steps_b), _kv_new_job(qkv_s, n_s, n_new, steps_b)])
    attn_s[1] = side[0]
    kv_sample = _kv_new_outputs(side[1][0], n_s, n_new)
    y_pre, h_last = _ssm_prompt(u, tabs_p, n_p, seq)
    attn_p = [_attn_prompt(qkv, n_p, seq, grp) for grp in range(N_GROUPS)]
    attn, kv_prompt = [a for per_group, _ in attn_p for a in per_group], [tail for _, tail in attn_p]
    tok = (x1, y_pre, *attn, p_prompt[0].reshape(n_p * seq, D_PLE))
    y_prompt = _stage3(tok, s3_w).reshape(n_p, seq, D_MODEL)
    ssm_prompt = from_tiles(h_last)

    h0 = state_ssm[0].reshape(n_s, N_SSM_TILES, SSM_TILE_GROUPS, SSM_STATE, 2)
    h0 = h0.transpose(0, 1, 4, 2, 3).reshape(n_s, N_SSM_TILES * TILE_STATE_W)
    y_pre, h_new = _ssm_sample(u_s, h0, tabs_s, n_s, n_new)
    attn = [a.reshape(-1, LANE) for grp in range(N_GROUPS) for a in attn_s[grp]]
    tok = (x1_s, y_pre, *attn, p_sample[0].reshape(n_s * n_new, D_PLE))
    y_sample = _stage3(tok, s3_w).reshape(n_s, n_new, D_MODEL)
    ssm_sample = from_tiles(h_new)

    return (y_prompt, y_sample, kv_prompt[0], kv_prompt[1], kv_prompt[2], ssm_prompt,
            kv_sample[0], kv_sample[1], kv_sample[2], ssm_sample)
```
